```python
import math
import jax, jax.numpy as jnp
from jax import lax
import numpy as np

D_MODEL = 1024
BATCH = 8
SEQ = 8192
DEPTH = 2

CHUNK = 64
SB_BLOCK = 128
RET_HEADS = 4
RET_DIM = 128
SB_HEADS = 4
SB_DIM = 128
SSM_HEADS = 8
SSM_HEAD_DIM = 64
SSM_STATE = 128
SSM_GROUPS = 2
SSM_CONV = 4
D_FF = 2816
ROPE_BASE = 10000.0
NORM_EPS = 1e-6
N_SUB = 3

RET_W = RET_HEADS * RET_DIM
SB_W = SB_HEADS * SB_DIM
SSM_W = SSM_HEADS * SSM_HEAD_DIM
MIX_W = RET_W + SB_W + SSM_W
SSM_XBC = SSM_W + 2 * SSM_GROUPS * SSM_STATE
IN_W = 4 * RET_W + 3 * SB_W + SSM_W + SSM_XBC + SSM_HEADS

kernel_name = "hybrid_ret_sb_ssd_macaron_adaln"


def rmsnorm(x, gain):
    xf = x.astype(jnp.float32)
    y = xf * lax.rsqrt(jnp.mean(xf * xf, axis=-1, keepdims=True) + NORM_EPS)
    return (y * gain.astype(jnp.float32)).astype(x.dtype)


def modulate(h, shift, scale):
    return h * (1.0 + scale) + shift


def swiglu(u, wg, wu, wd):
    return (jax.nn.silu(u @ wg) * (u @ wu)) @ wd


def rope(x, pos):
    half = x.shape[-1] // 2
    inv_freq = ROPE_BASE ** (-jnp.arange(half, dtype=jnp.float32) / half)
    ang = pos[:, None] * inv_freq[None, :]
    cos = jnp.cos(ang)[None, :, None, :]
    sin = jnp.sin(ang)[None, :, None, :]
    x1 = x[..., :half].astype(jnp.float32)
    x2 = x[..., half:].astype(jnp.float32)
    return jnp.concatenate([x1 * cos - x2 * sin, x1 * sin + x2 * cos], axis=-1)


def retention(q, k, v, g, gn_gain):
    B, S, _ = q.shape
    nc = S // CHUNK
    H, Dh = RET_HEADS, RET_DIM
    pos = jnp.arange(S, dtype=jnp.float32)
    q = rope(q.reshape(B, S, H, Dh), pos)
    k = rope(k.reshape(B, S, H, Dh), pos) * (Dh ** -0.5)
    v = v.reshape(B, S, H, Dh).astype(jnp.float32)
    log_gamma = jnp.log1p(-(2.0 ** (-5.0 - jnp.arange(H, dtype=jnp.float32))))
    idx = jnp.arange(CHUNK, dtype=jnp.float32)
    dmat = jnp.exp(log_gamma[:, None, None] * jnp.abs(idx[:, None] - idx[None, :]))
    qc = q.reshape(B, nc, CHUNK, H, Dh)
    kc = k.reshape(B, nc, CHUNK, H, Dh)
    vc = v.reshape(B, nc, CHUNK, H, Dh)
    scores = jnp.einsum('bclhd,bcshd->bchls', qc, kc) * dmat[None, None]
    y_intra = jnp.einsum('bchls,bcshe->bclhe', scores, vc)
    k_decay = jnp.exp(log_gamma[:, None] * (CHUNK - 1 - idx)[None, :])
    kv = jnp.einsum('bcshd,hs,bcshe->bchde', kc, k_decay, vc).astype(jnp.float32)
    chunk_decay = jnp.exp(log_gamma * CHUNK)[None, :, None, None]

    def step(state, kv_c):
        return state * chunk_decay + kv_c, state

    _, s_prev = lax.scan(step, jnp.zeros((B, H, Dh, Dh), jnp.float32), jnp.moveaxis(kv, 1, 0))
    s_prev = jnp.moveaxis(s_prev, 0, 1)
    q_decay = jnp.exp(log_gamma[:, None] * (idx + 1.0)[None, :])
    y_cross = jnp.einsum('bclhd,hl,bchde->bclhe', qc, q_decay, s_prev)
    y = (y_intra + y_cross).reshape(B, S, H, Dh)
    y = rmsnorm(y, gn_gain.reshape(H, Dh)).reshape(B, S, RET_W)
    return y * jax.nn.silu(g.astype(jnp.float32))


def stick_breaking(q, k, v):
    B, S, _ = q.shape
    H, Dh = SB_HEADS, SB_DIM
    q = q.reshape(B, S, H, Dh).transpose(0, 2, 1, 3)
    k = k.reshape(B, S, H, Dh).transpose(0, 2, 1, 3)
    v = v.reshape(B, S, H, Dh).transpose(0, 2, 1, 3)
    scale = Dh ** -0.5
    outs = []
    for i in range(S // SB_BLOCK):
        q0 = i * SB_BLOCK
        kend = q0 + SB_BLOCK
        qb = q[:, :, q0:kend]
        kb = k[:, :, :kend]
        vb = v[:, :, :kend]
        z = jnp.einsum('bhtd,bhsd->bhts', qb, kb).astype(jnp.float32) * scale
        t_pos = q0 + jnp.arange(SB_BLOCK)
        s_pos = jnp.arange(kend)
        visible = s_pos[None, :] < t_pos[:, None]
        log_beta = jax.nn.log_sigmoid(z)
        log_keep = jnp.where(visible, jax.nn.log_sigmoid(-z), 0.0)
        tail = lax.cumsum(log_keep, axis=3, reverse=True) - log_keep
        w = jnp.where(visible, jnp.exp(log_beta + tail), 0.0)
        outs.append(jnp.einsum('bhts,bhsd->bhtd', w, vb.astype(jnp.float32)))
    y = jnp.concatenate(outs, axis=2)
    return y.transpose(0, 2, 1, 3).reshape(B, S, SB_W)


def mamba2(z, xbc, dt_raw, conv_w, conv_b, dt_bias, a_log, d_skip, norm_gain):
    B, S, _ = xbc.shape
    G, Hg, P, N = SSM_GROUPS, SSM_HEADS // SSM_GROUPS, SSM_HEAD_DIM, SSM_STATE
    nc = S // CHUNK
    xbc = lax.conv_general_dilated(
        xbc, conv_w[:, None, :], window_strides=(1,), padding=[(SSM_CONV - 1, 0)],
        dimension_numbers=('NWC', 'WIO', 'NWC'), feature_group_count=SSM_XBC) + conv_b
    xbc = jax.nn.silu(xbc).astype(jnp.float32)
    xs = xbc[..., :SSM_W].reshape(B, nc, CHUNK, G, Hg, P)
    bm = xbc[..., SSM_W:SSM_W + G * N].reshape(B, nc, CHUNK, G, N)
    cm = xbc[..., SSM_W + G * N:].reshape(B, nc, CHUNK, G, N)
    dt = jax.nn.softplus(dt_raw.astype(jnp.float32) + dt_bias).reshape(B, nc, CHUNK, G, Hg)
    a = -jnp.exp(a_log.astype(jnp.float32)).reshape(G, Hg)
    acum = jnp.cumsum(dt * a, axis=2)
    xdt = xs * dt[..., None]
    causal = jnp.tril(jnp.ones((CHUNK, CHUNK), dtype=bool))[None, None, :, :, None, None]
    seg = acum[:, :, :, None] - acum[:, :, None, :]
    decay = jnp.exp(jnp.where(causal, seg, -jnp.inf))
    cb = jnp.einsum('bclgn,bcsgn->bcgls', cm, bm)
    y_intra = jnp.einsum('bcgls,bclsgh,bcsghp->bclghp', cb, decay, xdt)
    decay_end = jnp.exp(acum[:, :, -1:] - acum)
    states = jnp.einsum('bcsgn,bcsgh,bcsghp->bcghpn', bm, decay_end, xdt)
    chunk_decay = jnp.exp(acum[:, :, -1])

    def step(h, inp):
        st, dec = inp
        return h * dec[..., None, None] + st, h

    _, h_prev = lax.scan(step, jnp.zeros((B, G, Hg, P, N), jnp.float32),
                         (jnp.moveaxis(states, 1, 0), jnp.moveaxis(chunk_decay, 1, 0)))
    h_prev = jnp.moveaxis(h_prev, 0, 1)
    y_inter = jnp.einsum('bclgn,bcghpn,bclgh->bclghp', cm, h_prev, jnp.exp(acum))
    y = y_intra + y_inter + xs * d_skip.astype(jnp.float32).reshape(G, Hg)[..., None]
    y = y.reshape(B, S, SSM_W)
    return rmsnorm(y * jax.nn.silu(z.astype(jnp.float32)), norm_gain)


def _fwd_setup_inputs(seed: int = 0) -> dict:
    key = jax.random.key(seed)
    ks = jax.random.split(key, 32)
    f32 = jnp.float32
    nrm = lambda k, shape, s: jax.random.normal(k, shape, f32) * s
    gain = lambda k, shape: 1.0 + 0.01 * jax.random.normal(k, shape, f32)
    dt0 = jnp.exp(jax.random.uniform(ks[11], (DEPTH, SSM_HEADS), f32, math.log(1e-3), math.log(1e-1)))
    return {
        "x": nrm(ks[0], (BATCH, SEQ, D_MODEL), 1.0),
        "c": nrm(ks[1], (BATCH, D_MODEL), 1.0),
        "ada_w": nrm(ks[2], (DEPTH, D_MODEL, 3 * N_SUB * D_MODEL), 0.1 * D_MODEL ** -0.5),
        "ada_b": nrm(ks[3], (DEPTH, 3 * N_SUB * D_MODEL), 0.01),
        "norm_ffn1": gain(ks[4], (DEPTH, D_MODEL)),
        "ffn1_wg": nrm(ks[5], (DEPTH, D_MODEL, D_FF), D_MODEL ** -0.5),
        "ffn1_wu": nrm(ks[6], (DEPTH, D_MODEL, D_FF), D_MODEL ** -0.5),
        "ffn1_wd": nrm(ks[7], (DEPTH, D_FF, D_MODEL), D_FF ** -0.5),
        "norm_mix": gain(ks[8], (DEPTH, D_MODEL)),
        "w_in": nrm(ks[9], (DEPTH, D_MODEL, IN_W), D_MODEL ** -0.5),
        "conv_w": nrm(ks[10], (DEPTH, SSM_CONV, SSM_XBC), SSM_CONV ** -0.5),
        "conv_b": nrm(ks[12], (DEPTH, SSM_XBC), 0.01),
        "dt_bias": dt0 + jnp.log(-jnp.expm1(-dt0)),
        "a_log": jnp.log(jax.random.uniform(ks[13], (DEPTH, SSM_HEADS), f32, 1.0, 16.0)),
        "d_skip": gain(ks[14], (DEPTH, SSM_HEADS)),
        "ret_gn": gain(ks[15], (DEPTH, RET_W)),
        "ssm_norm": gain(ks[16], (DEPTH, SSM_W)),
        "w_out": nrm(ks[17], (DEPTH, MIX_W, D_MODEL), MIX_W ** -0.5),
        "norm_ffn2": gain(ks[18], (DEPTH, D_MODEL)),
        "ffn2_wg": nrm(ks[19], (DEPTH, D_MODEL, D_FF), D_MODEL ** -0.5),
        "ffn2_wu": nrm(ks[20], (DEPTH, D_MODEL, D_FF), D_MODEL ** -0.5),
        "ffn2_wd": nrm(ks[21], (DEPTH, D_FF, D_MODEL), D_FF ** -0.5),
        "final_ada_w": nrm(ks[22], (D_MODEL, 2 * D_MODEL), 0.1 * D_MODEL ** -0.5),
        "final_ada_b": nrm(ks[23], (2 * D_MODEL,), 0.01),
        "final_norm": gain(ks[24], (D_MODEL,)),
    }


def _fwd_reference(x, c, ada_w, ada_b, norm_ffn1, ffn1_wg, ffn1_wu, ffn1_wd, norm_mix, w_in,
              conv_w, conv_b, dt_bias, a_log, d_skip, ret_gn, ssm_norm, w_out,
              norm_ffn2, ffn2_wg, ffn2_wu, ffn2_wd, final_ada_w, final_ada_b, final_norm):
    B, S, _ = x.shape
    cond = jax.nn.silu(c)
    splits = [RET_W, 2 * RET_W, 3 * RET_W, 4 * RET_W,
              4 * RET_W + SB_W, 4 * RET_W + 2 * SB_W, 4 * RET_W + 3 * SB_W,
              4 * RET_W + 3 * SB_W + SSM_W, 4 * RET_W + 3 * SB_W + SSM_W + SSM_XBC]
    h = x
    for l in range(DEPTH):
        mod = (cond @ ada_w[l] + ada_b[l]).reshape(B, 3 * N_SUB, D_MODEL)[:, None]
        u = modulate(rmsnorm(h, norm_ffn1[l]), mod[:, :, 0], mod[:, :, 1])
        h = h + 0.5 * (1.0 + mod[:, :, 2]) * swiglu(u, ffn1_wg[l], ffn1_wu[l], ffn1_wd[l])
        u = modulate(rmsnorm(h, norm_mix[l]), mod[:, :, 3], mod[:, :, 4])
        proj = u @ w_in[l]
        rq, rk, rv, rg, sq, sk, sv, mz, mxbc, mdt = jnp.split(proj, splits, axis=-1)
        y_ret = retention(rq, rk, rv, rg, ret_gn[l]).astype(x.dtype)
        y_sb = stick_breaking(sq, sk, sv).astype(x.dtype)
        y_ssm = mamba2(mz, mxbc, mdt, conv_w[l], conv_b[l], dt_bias[l], a_log[l],
                       d_skip[l], ssm_norm[l]).astype(x.dtype)
        mixed = jnp.concatenate([y_ret, y_sb, y_ssm], axis=-1) @ w_out[l]
        h = h + (1.0 + mod[:, :, 5]) * mixed
        u = modulate(rmsnorm(h, norm_ffn2[l]), mod[:, :, 6], mod[:, :, 7])
        h = h + 0.5 * (1.0 + mod[:, :, 8]) * swiglu(u, ffn2_wg[l], ffn2_wu[l], ffn2_wd[l])
    fmod = (cond @ final_ada_w + final_ada_b).reshape(B, 2, D_MODEL)[:, None]
    return modulate(rmsnorm(h, final_norm), fmod[:, :, 0], fmod[:, :, 1])


import jax as _jax
import jax.numpy as _jnp

TWIN_FORMAT = 'train_step'
FWD_PARAMS = ['x', 'c', 'ada_w', 'ada_b', 'norm_ffn1', 'ffn1_wg', 'ffn1_wu', 'ffn1_wd', 'norm_mix', 'w_in', 'conv_w', 'conv_b', 'dt_bias', 'a_log', 'd_skip', 'ret_gn', 'ssm_norm', 'w_out', 'norm_ffn2', 'ffn2_wg', 'ffn2_wu', 'ffn2_wd', 'final_ada_w', 'final_ada_b', 'final_norm']
TWIN_WEIGHTS = ['ada_w', 'ada_b', 'norm_ffn1', 'ffn1_wg', 'ffn1_wu', 'ffn1_wd', 'norm_mix', 'w_in', 'conv_w', 'conv_b', 'dt_bias', 'a_log', 'd_skip', 'ret_gn', 'ssm_norm', 'w_out', 'norm_ffn2', 'ffn2_wg', 'ffn2_wu', 'ffn2_wd', 'final_ada_w', 'final_ada_b', 'final_norm']
TWIN_DIFF_INPUT = 'x'
TWIN_INPUTS = ['x', 'c', 'ada_w', 'ada_b', 'norm_ffn1', 'ffn1_wg', 'ffn1_wu', 'ffn1_wd', 'norm_mix', 'w_in', 'conv_w', 'conv_b', 'dt_bias', 'a_log', 'd_skip', 'ret_gn', 'ssm_norm', 'w_out', 'norm_ffn2', 'ffn2_wg', 'ffn2_wu', 'ffn2_wd', 'final_ada_w', 'final_ada_b', 'final_norm', 'loss_target', 'm_ada_w', 'm_ada_b', 'm_norm_ffn1', 'm_ffn1_wg', 'm_ffn1_wu', 'm_ffn1_wd', 'm_norm_mix', 'm_w_in', 'm_conv_w', 'm_conv_b', 'm_dt_bias', 'm_a_log', 'm_d_skip', 'm_ret_gn', 'm_ssm_norm', 'm_w_out', 'm_norm_ffn2', 'm_ffn2_wg', 'm_ffn2_wu', 'm_ffn2_wd', 'm_final_ada_w', 'm_final_ada_b', 'm_final_norm', 'v_ada_w', 'v_ada_b', 'v_norm_ffn1', 'v_ffn1_wg', 'v_ffn1_wu', 'v_ffn1_wd', 'v_norm_mix', 'v_w_in', 'v_conv_w', 'v_conv_b', 'v_dt_bias', 'v_a_log', 'v_d_skip', 'v_ret_gn', 'v_ssm_norm', 'v_w_out', 'v_norm_ffn2', 'v_ffn2_wg', 'v_ffn2_wu', 'v_ffn2_wd', 'v_final_ada_w', 'v_final_ada_b', 'v_final_norm']
TWIN_OUTPUTS = ['loss', 'grad_x', 'grad_ada_w', 'grad_ada_b', 'grad_norm_ffn1', 'grad_ffn1_wg', 'grad_ffn1_wu', 'grad_ffn1_wd', 'grad_norm_mix', 'grad_w_in', 'grad_conv_w', 'grad_conv_b', 'grad_dt_bias', 'grad_a_log', 'grad_d_skip', 'grad_ret_gn', 'grad_ssm_norm', 'grad_w_out', 'grad_norm_ffn2', 'grad_ffn2_wg', 'grad_ffn2_wu', 'grad_ffn2_wd', 'grad_final_ada_w', 'grad_final_ada_b', 'grad_final_norm', 'delta_ada_w', 'delta_ada_b', 'delta_norm_ffn1', 'delta_ffn1_wg', 'delta_ffn1_wu', 'delta_ffn1_wd', 'delta_norm_mix', 'delta_w_in', 'delta_conv_w', 'delta_conv_b', 'delta_dt_bias', 'delta_a_log', 'delta_d_skip', 'delta_ret_gn', 'delta_ssm_norm', 'delta_w_out', 'delta_norm_ffn2', 'delta_ffn2_wg', 'delta_ffn2_wu', 'delta_ffn2_wd', 'delta_final_ada_w', 'delta_final_ada_b', 'delta_final_norm', 'new_m_ada_w', 'new_m_ada_b', 'new_m_norm_ffn1', 'new_m_ffn1_wg', 'new_m_ffn1_wu', 'new_m_ffn1_wd', 'new_m_norm_mix', 'new_m_w_in', 'new_m_conv_w', 'new_m_conv_b', 'new_m_dt_bias', 'new_m_a_log', 'new_m_d_skip', 'new_m_ret_gn', 'new_m_ssm_norm', 'new_m_w_out', 'new_m_norm_ffn2', 'new_m_ffn2_wg', 'new_m_ffn2_wu', 'new_m_ffn2_wd', 'new_m_final_ada_w', 'new_m_final_ada_b', 'new_m_final_norm', 'new_v_ada_w', 'new_v_ada_b', 'new_v_norm_ffn1', 'new_v_ffn1_wg', 'new_v_ffn1_wu', 'new_v_ffn1_wd', 'new_v_norm_mix', 'new_v_w_in', 'new_v_conv_w', 'new_v_conv_b', 'new_v_dt_bias', 'new_v_a_log', 'new_v_d_skip', 'new_v_ret_gn', 'new_v_ssm_norm', 'new_v_w_out', 'new_v_norm_ffn2', 'new_v_ffn2_wg', 'new_v_ffn2_wu', 'new_v_ffn2_wd', 'new_v_final_ada_w', 'new_v_final_ada_b', 'new_v_final_norm']
TWIN_LEAF_KINDS = {'loss': 'loss', 'grad_x': 'grad_x', 'grad_ada_w': 'grad_w', 'grad_ada_b': 'grad_w', 'grad_norm_ffn1': 'grad_w', 'grad_ffn1_wg': 'grad_w', 'grad_ffn1_wu': 'grad_w', 'grad_ffn1_wd': 'grad_w', 'grad_norm_mix': 'grad_w', 'grad_w_in': 'grad_w', 'grad_conv_w': 'grad_w', 'grad_conv_b': 'grad_w', 'grad_dt_bias': 'grad_w', 'grad_a_log': 'grad_w', 'grad_d_skip': 'grad_w', 'grad_ret_gn': 'grad_w', 'grad_ssm_norm': 'grad_w', 'grad_w_out': 'grad_w', 'grad_norm_ffn2': 'grad_w', 'grad_ffn2_wg': 'grad_w', 'grad_ffn2_wu': 'grad_w', 'grad_ffn2_wd': 'grad_w', 'grad_final_ada_w': 'grad_w', 'grad_final_ada_b': 'grad_w', 'grad_final_norm': 'grad_w', 'delta_ada_w': 'delta_w', 'delta_ada_b': 'delta_w', 'delta_norm_ffn1': 'delta_w', 'delta_ffn1_wg': 'delta_w', 'delta_ffn1_wu': 'delta_w', 'delta_ffn1_wd': 'delta_w', 'delta_norm_mix': 'delta_w', 'delta_w_in': 'delta_w', 'delta_conv_w': 'delta_w', 'delta_conv_b': 'delta_w', 'delta_dt_bias': 'delta_w', 'delta_a_log': 'delta_w', 'delta_d_skip': 'delta_w', 'delta_ret_gn': 'delta_w', 'delta_ssm_norm': 'delta_w', 'delta_w_out': 'delta_w', 'delta_norm_ffn2': 'delta_w', 'delta_ffn2_wg': 'delta_w', 'delta_ffn2_wu': 'delta_w', 'delta_ffn2_wd': 'delta_w', 'delta_final_ada_w': 'delta_w', 'delta_final_ada_b': 'delta_w', 'delta_final_norm': 'delta_w', 'new_m_ada_w': 'new_m', 'new_m_ada_b': 'new_m', 'new_m_norm_ffn1': 'new_m', 'new_m_ffn1_wg': 'new_m', 'new_m_ffn1_wu': 'new_m', 'new_m_ffn1_wd': 'new_m', 'new_m_norm_mix': 'new_m', 'new_m_w_in': 'new_m', 'new_m_conv_w': 'new_m', 'new_m_conv_b': 'new_m', 'new_m_dt_bias': 'new_m', 'new_m_a_log': 'new_m', 'new_m_d_skip': 'new_m', 'new_m_ret_gn': 'new_m', 'new_m_ssm_norm': 'new_m', 'new_m_w_out': 'new_m', 'new_m_norm_ffn2': 'new_m', 'new_m_ffn2_wg': 'new_m', 'new_m_ffn2_wu': 'new_m', 'new_m_ffn2_wd': 'new_m', 'new_m_final_ada_w': 'new_m', 'new_m_final_ada_b': 'new_m', 'new_m_final_norm': 'new_m', 'new_v_ada_w': 'new_v', 'new_v_ada_b': 'new_v', 'new_v_norm_ffn1': 'new_v', 'new_v_ffn1_wg': 'new_v', 'new_v_ffn1_wu': 'new_v', 'new_v_ffn1_wd': 'new_v', 'new_v_norm_mix': 'new_v', 'new_v_w_in': 'new_v', 'new_v_conv_w': 'new_v', 'new_v_conv_b': 'new_v', 'new_v_dt_bias': 'new_v', 'new_v_a_log': 'new_v', 'new_v_d_skip': 'new_v', 'new_v_ret_gn': 'new_v', 'new_v_ssm_norm': 'new_v', 'new_v_w_out': 'new_v', 'new_v_norm_ffn2': 'new_v', 'new_v_ffn2_wg': 'new_v', 'new_v_ffn2_wu': 'new_v', 'new_v_ffn2_wd': 'new_v', 'new_v_final_ada_w': 'new_v', 'new_v_final_ada_b': 'new_v', 'new_v_final_norm': 'new_v'}


def _forward(args):
    return _fwd_reference(*[args[k] for k in FWD_PARAMS])


def _output_shape():
    def fwd():
        inp = _fwd_setup_inputs(0)
        return _fwd_reference(*[inp[k] for k in FWD_PARAMS])
    out = _jax.eval_shape(fwd)
    return out.shape, out.dtype

N_MICROBATCH = 1
ADAM_LR = 0.001
ADAM_B1 = 0.9
ADAM_B2 = 0.999
ADAM_EPS = 1e-08
ADAM_WD = 0.01
ADAM_STEP = 10
PER_EXAMPLE_BATCH_AXIS = {'x': 0, 'c': 0, 'loss_target': 0}
SHARED_INPUTS = []
_WEIGHT_DTYPES = {'ada_w': _jnp.float32, 'ada_b': _jnp.float32, 'norm_ffn1': _jnp.float32, 'ffn1_wg': _jnp.float32, 'ffn1_wu': _jnp.float32, 'ffn1_wd': _jnp.float32, 'norm_mix': _jnp.float32, 'w_in': _jnp.float32, 'conv_w': _jnp.float32, 'conv_b': _jnp.float32, 'dt_bias': _jnp.float32, 'a_log': _jnp.float32, 'd_skip': _jnp.float32, 'ret_gn': _jnp.float32, 'ssm_norm': _jnp.float32, 'w_out': _jnp.float32, 'norm_ffn2': _jnp.float32, 'ffn2_wg': _jnp.float32, 'ffn2_wu': _jnp.float32, 'ffn2_wd': _jnp.float32, 'final_ada_w': _jnp.float32, 'final_ada_b': _jnp.float32, 'final_norm': _jnp.float32}
MOMENT_SCALE = {'ada_w': 1.516960e-01, 'ada_b': 3.157585e-01, 'norm_ffn1': 1.342363e-01, 'ffn1_wg': 5.707013e-02, 'ffn1_wu': 5.553888e-02, 'ffn1_wd': 9.181635e-02, 'norm_mix': 2.531394e-01, 'w_in': 1.144054e-01, 'conv_w': 1.464822e-01, 'conv_b': 2.269695e-01, 'dt_bias': 5.588587e-01, 'a_log': 5.644225e-01, 'd_skip': 9.410043e-01, 'ret_gn': 1.019478e-01, 'ssm_norm': 2.033193e-01, 'w_out': 1.780059e-01, 'norm_ffn2': 8.742808e-02, 'ffn2_wg': 3.734502e-02, 'ffn2_wu': 3.649768e-02, 'ffn2_wd': 6.042039e-02, 'final_ada_w': 1.309096e+01, 'final_ada_b': 4.546760e+01, 'final_norm': 6.427895e+01}


def _to_microbatches(a, axis):
    t = _jnp.moveaxis(a, axis, 0)
    t = t.reshape((N_MICROBATCH, t.shape[0] // N_MICROBATCH) + t.shape[1:])
    return _jnp.moveaxis(t, 1, axis + 1)


def setup_inputs(seed: int = 0) -> dict:
    inp = _fwd_setup_inputs(seed)
    key = _jax.random.fold_in(_jax.random.key(seed), 7919)
    shape, _ = _output_shape()
    out = dict(inp)
    out["loss_target"] = _jax.random.normal(_jax.random.fold_in(key, 0), shape, _jnp.float32)
    for i, name in enumerate(TWIN_WEIGHTS):
        w = inp[name].astype(_jnp.float32)
        if MOMENT_SCALE is None:
            s = _jnp.sqrt(_jnp.mean(_jnp.square(w)) + 1e-30)
        else:
            s = MOMENT_SCALE[name]
        km, kv = _jax.random.split(_jax.random.fold_in(key, i + 1))
        out[name] = w
        out["m_" + name] = s * _jax.random.normal(km, w.shape, _jnp.float32)
        out["v_" + name] = (s * s) * _jax.random.uniform(kv, w.shape, _jnp.float32, 0.5, 1.5)
    if N_MICROBATCH > 1:
        for name, axis in PER_EXAMPLE_BATCH_AXIS.items():
            out[name] = _to_microbatches(out[name], axis)
    return {'x': out['x'], 'c': out['c'], 'ada_w': out['ada_w'], 'ada_b': out['ada_b'], 'norm_ffn1': out['norm_ffn1'], 'ffn1_wg': out['ffn1_wg'], 'ffn1_wu': out['ffn1_wu'], 'ffn1_wd': out['ffn1_wd'], 'norm_mix': out['norm_mix'], 'w_in': out['w_in'], 'conv_w': out['conv_w'], 'conv_b': out['conv_b'], 'dt_bias': out['dt_bias'], 'a_log': out['a_log'], 'd_skip': out['d_skip'], 'ret_gn': out['ret_gn'], 'ssm_norm': out['ssm_norm'], 'w_out': out['w_out'], 'norm_ffn2': out['norm_ffn2'], 'ffn2_wg': out['ffn2_wg'], 'ffn2_wu': out['ffn2_wu'], 'ffn2_wd': out['ffn2_wd'], 'final_ada_w': out['final_ada_w'], 'final_ada_b': out['final_ada_b'], 'final_norm': out['final_norm'], 'loss_target': out['loss_target'], 'm_ada_w': out['m_ada_w'], 'm_ada_b': out['m_ada_b'], 'm_norm_ffn1': out['m_norm_ffn1'], 'm_ffn1_wg': out['m_ffn1_wg'], 'm_ffn1_wu': out['m_ffn1_wu'], 'm_ffn1_wd': out['m_ffn1_wd'], 'm_norm_mix': out['m_norm_mix'], 'm_w_in': out['m_w_in'], 'm_conv_w': out['m_conv_w'], 'm_conv_b': out['m_conv_b'], 'm_dt_bias': out['m_dt_bias'], 'm_a_log': out['m_a_log'], 'm_d_skip': out['m_d_skip'], 'm_ret_gn': out['m_ret_gn'], 'm_ssm_norm': out['m_ssm_norm'], 'm_w_out': out['m_w_out'], 'm_norm_ffn2': out['m_norm_ffn2'], 'm_ffn2_wg': out['m_ffn2_wg'], 'm_ffn2_wu': out['m_ffn2_wu'], 'm_ffn2_wd': out['m_ffn2_wd'], 'm_final_ada_w': out['m_final_ada_w'], 'm_final_ada_b': out['m_final_ada_b'], 'm_final_norm': out['m_final_norm'], 'v_ada_w': out['v_ada_w'], 'v_ada_b': out['v_ada_b'], 'v_norm_ffn1': out['v_norm_ffn1'], 'v_ffn1_wg': out['v_ffn1_wg'], 'v_ffn1_wu': out['v_ffn1_wu'], 'v_ffn1_wd': out['v_ffn1_wd'], 'v_norm_mix': out['v_norm_mix'], 'v_w_in': out['v_w_in'], 'v_conv_w': out['v_conv_w'], 'v_conv_b': out['v_conv_b'], 'v_dt_bias': out['v_dt_bias'], 'v_a_log': out['v_a_log'], 'v_d_skip': out['v_d_skip'], 'v_ret_gn': out['v_ret_gn'], 'v_ssm_norm': out['v_ssm_norm'], 'v_w_out': out['v_w_out'], 'v_norm_ffn2': out['v_norm_ffn2'], 'v_ffn2_wg': out['v_ffn2_wg'], 'v_ffn2_wu': out['v_ffn2_wu'], 'v_ffn2_wd': out['v_ffn2_wd'], 'v_final_ada_w': out['v_final_ada_w'], 'v_final_ada_b': out['v_final_ada_b'], 'v_final_norm': out['v_final_norm']}


def _loss(weights, diff, rest, loss_target):
    with _jax.named_scope("forward"):
        args = {**rest, TWIN_DIFF_INPUT: diff, **{k: w.astype(_WEIGHT_DTYPES[k]) for k, w in weights.items()}}
        y = _forward(args)
    with _jax.named_scope("loss_head"):
        err = _jnp.square(y.astype(_jnp.float32) - loss_target)
        return 0.5 * _jnp.sum(_jnp.mean(err, axis=-1)) if err.ndim else 0.5 * err


def _adamw(w, g, m, v):
    m = ADAM_B1 * m + (1.0 - ADAM_B1) * g
    v = ADAM_B2 * v + (1.0 - ADAM_B2) * _jnp.square(g)
    m_hat = m / (1.0 - ADAM_B1 ** ADAM_STEP)
    v_hat = v / (1.0 - ADAM_B2 ** ADAM_STEP)
    delta = -ADAM_LR * (m_hat / (_jnp.sqrt(v_hat) + ADAM_EPS) + ADAM_WD * w)
    return delta, m, v


def reference(x, c, ada_w, ada_b, norm_ffn1, ffn1_wg, ffn1_wu, ffn1_wd, norm_mix, w_in, conv_w, conv_b, dt_bias, a_log, d_skip, ret_gn, ssm_norm, w_out, norm_ffn2, ffn2_wg, ffn2_wu, ffn2_wd, final_ada_w, final_ada_b, final_norm, loss_target, m_ada_w, m_ada_b, m_norm_ffn1, m_ffn1_wg, m_ffn1_wu, m_ffn1_wd, m_norm_mix, m_w_in, m_conv_w, m_conv_b, m_dt_bias, m_a_log, m_d_skip, m_ret_gn, m_ssm_norm, m_w_out, m_norm_ffn2, m_ffn2_wg, m_ffn2_wu, m_ffn2_wd, m_final_ada_w, m_final_ada_b, m_final_norm, v_ada_w, v_ada_b, v_norm_ffn1, v_ffn1_wg, v_ffn1_wu, v_ffn1_wd, v_norm_mix, v_w_in, v_conv_w, v_conv_b, v_dt_bias, v_a_log, v_d_skip, v_ret_gn, v_ssm_norm, v_w_out, v_norm_ffn2, v_ffn2_wg, v_ffn2_wu, v_ffn2_wd, v_final_ada_w, v_final_ada_b, v_final_norm):
    given = dict(x=x, c=c, ada_w=ada_w, ada_b=ada_b, norm_ffn1=norm_ffn1, ffn1_wg=ffn1_wg, ffn1_wu=ffn1_wu, ffn1_wd=ffn1_wd, norm_mix=norm_mix, w_in=w_in, conv_w=conv_w, conv_b=conv_b, dt_bias=dt_bias, a_log=a_log, d_skip=d_skip, ret_gn=ret_gn, ssm_norm=ssm_norm, w_out=w_out, norm_ffn2=norm_ffn2, ffn2_wg=ffn2_wg, ffn2_wu=ffn2_wu, ffn2_wd=ffn2_wd, final_ada_w=final_ada_w, final_ada_b=final_ada_b, final_norm=final_norm, loss_target=loss_target, m_ada_w=m_ada_w, m_ada_b=m_ada_b, m_norm_ffn1=m_norm_ffn1, m_ffn1_wg=m_ffn1_wg, m_ffn1_wu=m_ffn1_wu, m_ffn1_wd=m_ffn1_wd, m_norm_mix=m_norm_mix, m_w_in=m_w_in, m_conv_w=m_conv_w, m_conv_b=m_conv_b, m_dt_bias=m_dt_bias, m_a_log=m_a_log, m_d_skip=m_d_skip, m_ret_gn=m_ret_gn, m_ssm_norm=m_ssm_norm, m_w_out=m_w_out, m_norm_ffn2=m_norm_ffn2, m_ffn2_wg=m_ffn2_wg, m_ffn2_wu=m_ffn2_wu, m_ffn2_wd=m_ffn2_wd, m_final_ada_w=m_final_ada_w, m_final_ada_b=m_final_ada_b, m_final_norm=m_final_norm, v_ada_w=v_ada_w, v_ada_b=v_ada_b, v_norm_ffn1=v_norm_ffn1, v_ffn1_wg=v_ffn1_wg, v_ffn1_wu=v_ffn1_wu, v_ffn1_wd=v_ffn1_wd, v_norm_mix=v_norm_mix, v_w_in=v_w_in, v_conv_w=v_conv_w, v_conv_b=v_conv_b, v_dt_bias=v_dt_bias, v_a_log=v_a_log, v_d_skip=v_d_skip, v_ret_gn=v_ret_gn, v_ssm_norm=v_ssm_norm, v_w_out=v_w_out, v_norm_ffn2=v_norm_ffn2, v_ffn2_wg=v_ffn2_wg, v_ffn2_wu=v_ffn2_wu, v_ffn2_wd=v_ffn2_wd, v_final_ada_w=v_final_ada_w, v_final_ada_b=v_final_ada_b, v_final_norm=v_final_norm)
    weights = {n: given[n] for n in TWIN_WEIGHTS}
    shared = {n: given[n] for n in SHARED_INPUTS}
    per_example = {n: given[n] for n in ['x', 'c']}
    grad_fn = _jax.value_and_grad(_loss, argnums=(0, 1))

    def one_microbatch(ex, loss_target):
        ex = dict(ex)
        diff = ex.pop(TWIN_DIFF_INPUT)
        return grad_fn(weights, diff, {**shared, **ex}, loss_target)

    if N_MICROBATCH == 1:
        loss, (grad_w, grad_x) = one_microbatch(per_example, given["loss_target"])
    else:
        def body(carry, xs):
            loss_sum, grad_sum = carry
            l_k, (gw_k, gx_k) = one_microbatch(xs[0], xs[1])
            with _jax.named_scope("update"):
                return (loss_sum + l_k, _jax.tree.map(_jnp.add, grad_sum, gw_k)), gx_k

        init = (_jnp.zeros((), _jnp.float32), _jax.tree.map(_jnp.zeros_like, weights))
        (loss, grad_w), grad_x = _jax.lax.scan(body, init, (per_example, given["loss_target"]))
    with _jax.named_scope("update"):
        delta_w, new_m, new_v = {}, {}, {}
        for n in TWIN_WEIGHTS:
            delta_w[n], new_m[n], new_v[n] = _adamw(weights[n], grad_w[n], given["m_" + n], given["v_" + n])
    return (loss, grad_x, *[grad_w[n] for n in TWIN_WEIGHTS], *[delta_w[n] for n in TWIN_WEIGHTS],
            *[new_m[n] for n in TWIN_WEIGHTS], *[new_v[n] for n in TWIN_WEIGHTS])
```

```python
import functools
import math

import numpy as np
import jax
import jax.numpy as jnp
from jax import lax
from jax.experimental import pallas as pl
from jax.experimental.pallas import tpu as pltpu

F32 = jnp.float32
BF16 = jnp.bfloat16

D_MODEL = 1024
DEPTH = 2
RET_HEADS = 4
HEAD_DIM = 128
SSM_HEADS = 8
SSM_HEAD_DIM = 64
SSM_STATE = 128
SSM_CONV = 4
D_FF = 2816
ROPE_BASE = 10000.0
NORM_EPS = 1e-6
MIX_W = 1536
IN_W = 5128
IN_MAIN = 5120
IN_PAD = 5632
N_DEV = 8
LANE = 128

ADAM_LR = 0.001
ADAM_B1 = 0.9
ADAM_B2 = 0.999
ADAM_EPS = 1e-08
ADAM_WD = 0.01
ADAM_STEP = 10

TOKEN_TILE = 512
SEQ_BLOCK = 256
VMEM_LIMIT = 56 << 20

CB_RQ, CB_RK, CB_RV, CB_RG = 0, 4, 8, 12
CB_SQ, CB_SK, CB_SV = 16, 20, 24
CB_MZ, CB_XS, CB_BM, CB_CM, CB_DT = 28, 32, 36, 38, 40


def _cparams(*sem):
    return pltpu.CompilerParams(dimension_semantics=sem, vmem_limit_bytes=VMEM_LIMIT)


def _sds(shape, dtype):
    return jax.ShapeDtypeStruct(tuple(shape), dtype)


def _tile(n, *prefs):
    for p in prefs:
        if n % p == 0:
            return p
    return n


def _dot(a, b, dims):
    return lax.dot_general(a, b, (dims, ((), ())), preferred_element_type=F32)


NN = ((1,), (0,))
NT = ((1,), (1,))
TN = ((0,), (0,))


def _bf(x):
    return x.astype(BF16)


def _sigmoid(x):
    return jax.nn.sigmoid(x)


def _split2(x):
    hi = x.astype(BF16)
    lo = (x - hi.astype(F32)).astype(BF16)
    return hi, lo


def _split3(x):
    hi = x.astype(BF16)
    r = x - hi.astype(F32)
    mid = r.astype(BF16)
    lo = (r - mid.astype(F32)).astype(BF16)
    return hi, mid, lo


def matmul(a, b, *, ta=False, tb=False, tm=512, tn=512, tk=512, out_dtype=F32, name):
    M, K = (a.shape[1], a.shape[0]) if ta else a.shape
    N = b.shape[0] if tb else b.shape[1]
    tm, tn, tk = min(tm, M), min(tn, N), min(tk, K)
    assert M % tm == 0 and N % tn == 0 and K % tk == 0, (name, M, N, K, tm, tn, tk)
    nk = K // tk
    a_spec = pl.BlockSpec((tk, tm), lambda i, j, k: (k, i)) if ta else pl.BlockSpec((tm, tk), lambda i, j, k: (i, k))
    b_spec = pl.BlockSpec((tn, tk), lambda i, j, k: (j, k)) if tb else pl.BlockSpec((tk, tn), lambda i, j, k: (k, j))
    dims = ((0 if ta else 1,), (1 if tb else 0,))

    def body(a_ref, b_ref, o_ref, acc_ref):
        k = pl.program_id(2)
        p = _dot(_bf(a_ref[...]), _bf(b_ref[...]), dims)

        @pl.when(k == 0)
        def _():
            acc_ref[...] = p

        @pl.when(k > 0)
        def _():
            acc_ref[...] += p

        @pl.when(k == nk - 1)
        def _():
            o_ref[...] = acc_ref[...].astype(out_dtype)

    return pl.pallas_call(
        body, grid=(M // tm, N // tn, nk), in_specs=[a_spec, b_spec],
        out_specs=pl.BlockSpec((tm, tn), lambda i, j, k: (i, j)), out_shape=_sds((M, N), out_dtype),
        scratch_shapes=[pltpu.VMEM((tm, tn), F32)], name=name,
        compiler_params=_cparams("parallel", "parallel", "arbitrary"))(a, b)


def matmul_resid(a, w, h, gate, factor, *, name):
    M, K = a.shape
    N = w.shape[1]
    tm, tn = min(TOKEN_TILE, M), _tile(N, 512)

    def body(a_ref, w_ref, h_ref, g_ref, hn_ref, o_ref):
        out = _dot(a_ref[...], w_ref[...], NN)
        o_ref[...] = out
        hn_ref[...] = h_ref[...] + (factor * (1.0 + g_ref[...])) * out

    mn = pl.BlockSpec((tm, tn), lambda i, j: (i, j))
    return pl.pallas_call(
        body, grid=(M // tm, N // tn),
        in_specs=[pl.BlockSpec((tm, K), lambda i, j: (i, 0)), pl.BlockSpec((K, tn), lambda i, j: (0, j)), mn,
                  pl.BlockSpec((1, tn), lambda i, j: (0, j))],
        out_specs=[mn, mn], out_shape=[_sds((M, N), F32), _sds((M, N), F32)], name=name,
        compiler_params=_cparams("parallel", "parallel"))(a, w, h, gate)


def ffn_up(u, wg, wu, *, name):
    M, K = u.shape
    N = wg.shape[1]
    tm, tn = min(TOKEN_TILE, M), _tile(N, 1408, 256)

    def body(u_ref, wg_ref, wu_ref, a_ref, b_ref, act_ref):
        uu = u_ref[...]
        a = _dot(uu, wg_ref[...], NN)
        b = _dot(uu, wu_ref[...], NN)
        a_ref[...] = a
        b_ref[...] = b
        act_ref[...] = _bf(a * _sigmoid(a) * b)

    mn = pl.BlockSpec((tm, tn), lambda i, j: (i, j))
    wspec = pl.BlockSpec((K, tn), lambda i, j: (0, j))
    return pl.pallas_call(
        body, grid=(M // tm, N // tn), in_specs=[pl.BlockSpec((tm, K), lambda i, j: (i, 0)), wspec, wspec],
        out_specs=[mn, mn, mn], out_shape=[_sds((M, N), F32), _sds((M, N), F32), _sds((M, N), BF16)], name=name,
        compiler_params=_cparams("parallel", "parallel"))(u, wg, wu)


def ffn_dact(dout, wd, a, b, *, name):
    M, K = dout.shape
    N = wd.shape[0]
    tm, tn = min(TOKEN_TILE, M), _tile(N, 1408, 256)

    def body(d_ref, w_ref, a_ref, b_ref, da_ref, db_ref):
        dact = _dot(d_ref[...], w_ref[...], NT)
        av = a_ref[...]
        sg = _sigmoid(av)
        db_ref[...] = _bf(dact * av * sg)
        da_ref[...] = _bf(dact * b_ref[...] * (sg * (1.0 + av * (1.0 - sg))))

    mn = pl.BlockSpec((tm, tn), lambda i, j: (i, j))
    return pl.pallas_call(
        body, grid=(M // tm, N // tn),
        in_specs=[pl.BlockSpec((tm, K), lambda i, j: (i, 0)), pl.BlockSpec((tn, K), lambda i, j: (j, 0)), mn, mn],
        out_specs=[mn, mn], out_shape=[_sds((M, N), BF16), _sds((M, N), BF16)], name=name,
        compiler_params=_cparams("parallel", "parallel"))(dout, wd, a, b)


def norm_mod(h, gain, shift, scale, *, name):
    S, D = h.shape
    tm = min(TOKEN_TILE, S)

    def body(h_ref, g_ref, sh_ref, sc_ref, u_ref):
        x = h_ref[...]
        r = lax.rsqrt(jnp.mean(x * x, axis=-1, keepdims=True) + NORM_EPS)
        n = x * r * g_ref[...]
        u_ref[...] = _bf(n * (1.0 + sc_ref[...]) + sh_ref[...])

    row = pl.BlockSpec((1, D), lambda i: (0, 0))
    tile = pl.BlockSpec((tm, D), lambda i: (i, 0))
    return pl.pallas_call(body, grid=(S // tm,), in_specs=[tile, row, row, row], out_specs=tile,
                          out_shape=_sds((S, D), BF16), name=name, compiler_params=_cparams("parallel"))(h, gain, shift, scale)


def dgrad_norm_bwd(lhs, ws, h, gain, scale, dres, *, name):
    S, D = h.shape
    tm = min(256, S)
    n = len(lhs)

    def body(*refs):
        l_refs, w_refs = refs[:n], refs[n:2 * n]
        h_ref, g_ref, sc_ref, dres_ref, dh_ref, st_ref = refs[2 * n:]
        du = _dot(l_refs[0][...], w_refs[0][...], NT)
        for lr, wr in zip(l_refs[1:], w_refs[1:]):
            du = du + _dot(lr[...], wr[...], NT)
        x = h_ref[...]
        g = g_ref[...]
        r = lax.rsqrt(jnp.mean(x * x, axis=-1, keepdims=True) + NORM_EPS)
        xhat = x * r
        dn = du * (1.0 + sc_ref[...])
        dxhat = dn * g
        dh_ref[...] = dres_ref[...] + r * (dxhat - xhat * jnp.mean(dxhat * xhat, axis=-1, keepdims=True))

        @pl.when(pl.program_id(0) == 0)
        def _():
            st_ref[...] = jnp.zeros_like(st_ref)

        st_ref[0:1, :] += jnp.sum(dn * xhat, axis=0, keepdims=True)
        st_ref[1:2, :] += jnp.sum(du, axis=0, keepdims=True)
        st_ref[2:3, :] += jnp.sum(du * (xhat * g), axis=0, keepdims=True)

    row = pl.BlockSpec((1, D), lambda i: (0, 0))
    tile = pl.BlockSpec((tm, D), lambda i: (i, 0))
    in_specs = [pl.BlockSpec((tm, l.shape[1]), lambda i: (i, 0)) for l in lhs]
    in_specs += [pl.BlockSpec(w.shape, lambda i: (0, 0)) for w in ws]
    in_specs += [tile, row, row, tile]
    return pl.pallas_call(
        body, grid=(S // tm,), in_specs=in_specs, out_specs=[tile, pl.BlockSpec((8, D), lambda i: (0, 0))],
        out_shape=[_sds((S, D), F32), _sds((8, D), F32)], name=name,
        compiler_params=_cparams("arbitrary"))(*lhs, *ws, h, gain, scale, dres)


def gate_bwd(dh, out, gate, factor, *, name):
    S, D = dh.shape
    tm = min(TOKEN_TILE, S)

    def body(dh_ref, o_ref, g_ref, do_ref, st_ref):
        d = dh_ref[...]
        do_ref[...] = _bf(d * (factor * (1.0 + g_ref[...])))

        @pl.when(pl.program_id(0) == 0)
        def _():
            st_ref[...] = jnp.zeros_like(st_ref)

        st_ref[0:1, :] += factor * jnp.sum(d * o_ref[...], axis=0, keepdims=True)

    tile = pl.BlockSpec((tm, D), lambda i: (i, 0))
    return pl.pallas_call(
        body, grid=(S // tm,), in_specs=[tile, tile, pl.BlockSpec((1, D), lambda i: (0, 0))],
        out_specs=[tile, pl.BlockSpec((8, D), lambda i: (0, 0))], out_shape=[_sds((S, D), BF16), _sds((8, D), F32)],
        name=name, compiler_params=_cparams("arbitrary"))(dh, out, gate)


def final_loss_bwd(h, gain, shift, scale, target, *, name):
    S, D = h.shape
    tm = min(TOKEN_TILE, S)

    def body(h_ref, g_ref, sh_ref, sc_ref, t_ref, dh_ref, st_ref):
        x = h_ref[...]
        g = g_ref[...]
        r = lax.rsqrt(jnp.mean(x * x, axis=-1, keepdims=True) + NORM_EPS)
        xhat = x * r
        n = xhat * g
        err = n * (1.0 + sc_ref[...]) + sh_ref[...] - t_ref[...]
        dy = err * (1.0 / D)
        dn = dy * (1.0 + sc_ref[...])
        dxhat = dn * g
        dh_ref[...] = r * (dxhat - xhat * jnp.mean(dxhat * xhat, axis=-1, keepdims=True))

        @pl.when(pl.program_id(0) == 0)
        def _():
            st_ref[...] = jnp.zeros_like(st_ref)

        st_ref[0:1, :] += jnp.sum(dn * xhat, axis=0, keepdims=True)
        st_ref[1:2, :] += jnp.sum(dy, axis=0, keepdims=True)
        st_ref[2:3, :] += jnp.sum(dy * n, axis=0, keepdims=True)
        tok = jnp.mean(err * err, axis=-1, keepdims=True)
        st_ref[3:4, :] += 0.5 * jnp.sum(tok, axis=0, keepdims=True)

    row = pl.BlockSpec((1, D), lambda i: (0, 0))
    tile = pl.BlockSpec((tm, D), lambda i: (i, 0))
    return pl.pallas_call(
        body, grid=(S // tm,), in_specs=[tile, row, row, row, tile], out_specs=[tile, pl.BlockSpec((8, D), lambda i: (0, 0))],
        out_shape=[_sds((S, D), F32), _sds((8, D), F32)], name=name,
        compiler_params=_cparams("arbitrary"))(h, gain, shift, scale, target)


def _ret_tables(T):
    heads = np.arange(RET_HEADS, dtype=np.float64)
    lg = np.log1p(-(2.0 ** (-5.0 - heads)))
    t = np.arange(T)
    same = (t[:, None] // 64) == (t[None, :] // 64)
    earlier = (t[None, :] // 64) < (t[:, None] // 64)
    dist = np.abs(t[:, None] - t[None, :]).astype(np.float64)
    dmat = np.where(same | earlier, np.exp(lg[:, None, None] * dist[None]), 0.0)
    qdec = np.exp(lg[:, None] * (t + 1.0)[None, :])
    kdec = np.exp(lg[:, None] * (T - 1.0 - t)[None, :])
    cdec = np.exp(lg * T)
    bc = lambda v: jnp.asarray(np.broadcast_to(v[:, :, None], (RET_HEADS, T, LANE)), F32)
    cd = jnp.asarray(np.broadcast_to(cdec[:, None, None], (RET_HEADS, LANE, LANE)), F32)
    return jnp.asarray(dmat, F32), bc(qdec), bc(kdec), cd


def _rope_tables(S):
    half = HEAD_DIM // 2
    inv_freq = ROPE_BASE ** (-jnp.arange(half, dtype=F32) / half)
    ang = jnp.arange(S, dtype=F32)[:, None] * inv_freq[None, :]
    cos, sin = jnp.cos(ang), jnp.sin(ang)
    return jnp.concatenate([cos, cos], axis=-1), jnp.concatenate([-sin, sin], axis=-1)


def _rope(x, c, s):
    return x * c + pltpu.roll(x, HEAD_DIM // 2, 1) * s


def _rope_t(dx, c, s):
    return dx * c + pltpu.roll(dx * s, HEAD_DIM // 2, 1)


def ret_fwd(proj, gn, cos, sin, *, name):
    S = proj.shape[0]
    T = min(SEQ_BLOCK, S)
    nb = S // T
    dmat, qdec, kdec, cdec = _ret_tables(T)

    def body(q_ref, k_ref, v_ref, g_ref, c_ref, s_ref, dm_ref, qd_ref, kd_ref, cd_ref, gn_ref, yo_ref, yp_ref, st_ref, state):
        @pl.when(pl.program_id(1) == 0)
        def _():
            state[...] = jnp.zeros_like(state)

        c, s = c_ref[...], s_ref[...]
        qr = _rope(q_ref[...], c, s)
        kr = _rope(k_ref[...], c, s) * (HEAD_DIM ** -0.5)
        v = _bf(v_ref[...])
        sp = state[...]
        st_ref[...] = sp
        a = _dot(_bf(qr), _bf(kr), NT) * dm_ref[...]
        y = _dot(_bf(a), v, NN) + _dot(_bf(qr * qd_ref[...]), _bf(sp), NN)
        state[...] = cd_ref[...] * sp + _dot(_bf(kr * kd_ref[...]), v, TN)
        yp_ref[...] = y
        yn = y * lax.rsqrt(jnp.mean(y * y, axis=-1, keepdims=True) + NORM_EPS) * gn_ref[...]
        g = g_ref[...]
        yo_ref[...] = _bf(yn * (g * _sigmoid(g)))

    col = lambda cb: pl.BlockSpec((T, LANE), lambda h, b: (b, cb + h))
    tok = pl.BlockSpec((T, LANE), lambda h, b: (b, 0))
    per_head = lambda r: pl.BlockSpec((None, r, LANE), lambda h, b: (h, 0, 0))
    out_tok = pl.BlockSpec((T, LANE), lambda h, b: (b, h))
    return pl.pallas_call(
        body, grid=(RET_HEADS, nb),
        in_specs=[col(CB_RQ), col(CB_RK), col(CB_RV), col(CB_RG), tok, tok,
                  pl.BlockSpec((None, T, T), lambda h, b: (h, 0, 0)), per_head(T), per_head(T), per_head(LANE),
                  pl.BlockSpec((1, LANE), lambda h, b: (0, h))],
        out_specs=[out_tok, out_tok, pl.BlockSpec((None, None, LANE, LANE), lambda h, b: (h, b, 0, 0))],
        out_shape=[_sds((S, 512), BF16), _sds((S, 512), F32), _sds((RET_HEADS, nb, LANE, LANE), F32)],
        scratch_shapes=[pltpu.VMEM((LANE, LANE), F32)], name=name,
        compiler_params=_cparams("parallel", "arbitrary"))(proj, proj, proj, proj, cos, sin, dmat, qdec, kdec, cdec, gn)


def ret_bwd(proj, gn, cos, sin, ypre, states, dycat, *, name):
    S = proj.shape[0]
    T = min(SEQ_BLOCK, S)
    nb = S // T
    dmat, qdec, kdec, cdec = _ret_tables(T)

    def body(q_ref, k_ref, v_ref, g_ref, c_ref, s_ref, dm_ref, qd_ref, kd_ref, cd_ref, gn_ref, yp_ref, st_ref, dy_ref,
             dq_ref, dk_ref, dv_ref, dg_ref, stat_ref, gstate):
        @pl.when(pl.program_id(1) == 0)
        def _():
            gstate[...] = jnp.zeros_like(gstate)
            stat_ref[...] = jnp.zeros_like(stat_ref)

        c, s = c_ref[...], s_ref[...]
        scale = HEAD_DIM ** -0.5
        qr = _rope(q_ref[...], c, s)
        kr = _rope(k_ref[...], c, s) * scale
        v = _bf(v_ref[...])
        qd, kd, dm = qd_ref[...], kd_ref[...], dm_ref[...]
        sp = _bf(st_ref[...])
        gs = gstate[...]
        gsb = _bf(gs)
        g = g_ref[...]
        sg = _sigmoid(g)
        y = yp_ref[...]
        gn_row = gn_ref[...]
        r = lax.rsqrt(jnp.mean(y * y, axis=-1, keepdims=True) + NORM_EPS)
        yhat = y * r
        dyo = dy_ref[...]
        dg_ref[...] = _bf(dyo * (yhat * gn_row) * (sg * (1.0 + g * (1.0 - sg))))
        dyn = dyo * (g * sg)
        stat_ref[0:1, :] += jnp.sum(dyn * yhat, axis=0, keepdims=True)
        dyhat = dyn * gn_row
        dy = _bf(r * (dyhat - yhat * jnp.mean(dyhat * yhat, axis=-1, keepdims=True)))
        qrb, krb = _bf(qr), _bf(kr)
        qdb = _bf(qr * qd)
        kdb = _bf(kr * kd)
        a = _bf(_dot(qrb, krb, NT) * dm)
        da = _bf(_dot(dy, v, NT) * dm)
        dv_ref[...] = _bf(_dot(a, dy, TN) + _dot(kdb, gsb, NN))
        dqr = _dot(da, krb, NN) + qd * _dot(dy, sp, NT)
        dkr = _dot(da, qrb, TN) + kd * _dot(v, gsb, NT)
        gstate[...] = cd_ref[...] * gs + _dot(qdb, dy, TN)
        dq_ref[...] = _bf(_rope_t(dqr, c, s))
        dk_ref[...] = _bf(_rope_t(dkr * scale, c, s))

    rb = lambda b: nb - 1 - b
    col = lambda cb: pl.BlockSpec((T, LANE), lambda h, b: (rb(b), cb + h))
    tok = pl.BlockSpec((T, LANE), lambda h, b: (rb(b), 0))
    per_head = lambda r: pl.BlockSpec((None, r, LANE), lambda h, b: (h, 0, 0))
    out_tok = pl.BlockSpec((T, LANE), lambda h, b: (rb(b), h))
    return pl.pallas_call(
        body, grid=(RET_HEADS, nb),
        in_specs=[col(CB_RQ), col(CB_RK), col(CB_RV), col(CB_RG), tok, tok,
                  pl.BlockSpec((None, T, T), lambda h, b: (h, 0, 0)), per_head(T), per_head(T), per_head(LANE),
                  pl.BlockSpec((1, LANE), lambda h, b: (0, h)), out_tok,
                  pl.BlockSpec((None, None, LANE, LANE), lambda h, b: (h, rb(b), 0, 0)), out_tok],
        out_specs=[out_tok, out_tok, out_tok, out_tok, pl.BlockSpec((8, LANE), lambda h, b: (0, h))],
        out_shape=[_sds((S, 512), BF16)] * 4 + [_sds((8, 512), F32)],
        scratch_shapes=[pltpu.VMEM((LANE, LANE), F32)], name=name,
        compiler_params=_cparams("parallel", "arbitrary"))(proj, proj, proj, proj, cos, sin, dmat, qdec, kdec, cdec, gn, ypre, states, dycat)


def _sb_cast_kv(k_ref, v_ref, kb, vb, S):
    step = min(TOKEN_TILE, S)
    for r in range(0, S, step):
        kb[r:r + step, :] = _bf(k_ref[r:r + step, :])
        vb[r:r + step, :] = _bf(v_ref[r:r + step, :])


def _sb_logits(q, kblk, vis):
    z = _dot(q, kblk, NT) * (HEAD_DIM ** -0.5)
    l = jnp.log1p(jnp.exp(-jnp.abs(z)))
    lb = jnp.minimum(z, 0.0) - l
    lk = jnp.minimum(-z, 0.0) - l
    if vis is not None:
        lk = jnp.where(vis, lk, 0.0)
    return lb, lk


def _tri(T, cmp):
    r = lax.broadcasted_iota(jnp.int32, (T, T), 0)
    c = lax.broadcasted_iota(jnp.int32, (T, T), 1)
    return cmp(r, c)


def _dot_split2(x, m):
    hi, lo = _split2(x)
    return _dot(hi, m, NN) + _dot(lo, m, NN)


def sb_fwd(proj, *, name):
    S = proj.shape[0]
    T = min(SEQ_BLOCK, S)
    nq = S // T

    assert nq <= LANE

    def body(q_ref, k_ref, v_ref, o_ref, cin_ref, kb, vb):
        qi = pl.program_id(1)

        @pl.when(qi == 0)
        def _():
            _sb_cast_kv(k_ref, v_ref, kb, vb, S)

        q = _bf(q_ref[...])
        vis = _tri(T, lambda t, s: s < t)
        after = _tri(T, lambda j, s: j > s).astype(BF16)
        lane = lax.broadcasted_iota(jnp.int32, (T, LANE), 1)

        def block(jb, carry, acc, cin, mask):
            rows = pl.ds(pl.multiple_of(jb * T, T), T)
            lb, lk = _sb_logits(q, kb[rows, :], mask)
            tail = _dot_split2(lk, after) + carry
            w = jnp.exp(lb + tail)
            if mask is not None:
                w = jnp.where(mask, w, 0.0)
            return (carry + jnp.sum(lk, axis=1, keepdims=True), acc + _dot(_bf(w), vb[rows, :], NN),
                    jnp.where(lane == jb, carry, cin))

        zeros = jnp.zeros((T, LANE), F32)
        st = block(qi, jnp.zeros((T, 1), F32), zeros, zeros, vis)
        st = lax.fori_loop(0, qi, lambda it, c: block(qi - 1 - it, c[0], c[1], c[2], None), st)
        o_ref[...] = st[1]
        cin_ref[...] = st[2]

    whole = lambda cb: pl.BlockSpec((S, LANE), lambda h, i: (0, cb + h))
    tok = pl.BlockSpec((T, LANE), lambda h, i: (i, h))
    return pl.pallas_call(
        body, grid=(RET_HEADS, nq),
        in_specs=[pl.BlockSpec((T, LANE), lambda h, i: (i, CB_SQ + h)), whole(CB_SK), whole(CB_SV)],
        out_specs=[tok, tok], out_shape=[_sds((S, 512), F32), _sds((S, 512), F32)],
        scratch_shapes=[pltpu.VMEM((S, LANE), BF16), pltpu.VMEM((S, LANE), BF16)], name=name,
        compiler_params=_cparams("parallel", "arbitrary"))(proj, proj, proj)


def sb_bwd(proj, cin, dycat, *, name):
    S = proj.shape[0]
    T = min(SEQ_BLOCK, S)
    nq = S // T
    scale = HEAD_DIM ** -0.5

    def body(q_ref, k_ref, v_ref, cin_ref, do_ref, dq_ref, dk_ref, dv_ref, kb, vb):
        qi = pl.program_id(1)

        @pl.when(qi == 0)
        def _():
            _sb_cast_kv(k_ref, v_ref, kb, vb, S)
            dk_ref[...] = jnp.zeros_like(dk_ref)
            dv_ref[...] = jnp.zeros_like(dv_ref)

        q = _bf(q_ref[...])
        dob = _bf(do_ref[...])
        cin = cin_ref[...]
        vis = _tri(T, lambda t, s: s < t)
        after = _tri(T, lambda j, s: j > s).astype(BF16)
        before = _tri(T, lambda s, j: s < j).astype(BF16)
        lane = lax.broadcasted_iota(jnp.int32, (T, LANE), 1)

        def block(jb, ecarry, dq, mask):
            rows = pl.ds(pl.multiple_of(jb * T, T), T)
            kblk, vblk = kb[rows, :], vb[rows, :]
            lb, lk = _sb_logits(q, kblk, mask)
            carry = jnp.sum(jnp.where(lane == jb, cin, 0.0), axis=1, keepdims=True)
            w = jnp.exp(lb + _dot_split2(lk, after) + carry)
            if mask is not None:
                w = jnp.where(mask, w, 0.0)
            e = w * _dot(dob, vblk, NT)
            dv_ref[rows, :] += _dot(_bf(w), dob, TN)
            dlk = _dot_split2(e, before) + ecarry
            beta = jnp.exp(lb)
            dz = e * (1.0 - beta) - beta * dlk
            if mask is not None:
                dz = jnp.where(mask, dz, 0.0)
            dzb = _bf(dz * scale)
            dk_ref[rows, :] += _dot(dzb, q, TN)
            return ecarry + jnp.sum(e, axis=1, keepdims=True), dq + _dot(dzb, kblk, NN)

        st = lax.fori_loop(0, qi, lambda jb, c: block(jb, c[0], c[1], None), (jnp.zeros((T, 1), F32), jnp.zeros((T, LANE), F32)))
        st = block(qi, st[0], st[1], vis)
        dq_ref[...] = _bf(st[1])

    whole = lambda cb: pl.BlockSpec((S, LANE), lambda h, i: (0, cb + h))
    tok = pl.BlockSpec((T, LANE), lambda h, i: (i, h))
    acc = pl.BlockSpec((S, LANE), lambda h, i: (0, h))
    return pl.pallas_call(
        body, grid=(RET_HEADS, nq),
        in_specs=[pl.BlockSpec((T, LANE), lambda h, i: (i, CB_SQ + h)), whole(CB_SK), whole(CB_SV), tok,
                  pl.BlockSpec((T, LANE), lambda h, i: (i, 4 + h))],
        out_specs=[tok, acc, acc], out_shape=[_sds((S, 512), BF16), _sds((S, 512), F32), _sds((S, 512), F32)],
        scratch_shapes=[pltpu.VMEM((S, LANE), BF16), pltpu.VMEM((S, LANE), BF16)], name=name,
        compiler_params=_cparams("parallel", "arbitrary"))(proj, proj, proj, cin, dycat)


def _shift_down(x, d, row):
    return jnp.where(row >= d, pltpu.roll(x, d, 0), 0.0)


def _shift_up(x, d, row, S):
    return jnp.where(row < S - d, pltpu.roll(x, S - d, 0), 0.0)


def conv_fwd(proj, conv_w, conv_b, *, name):
    S = proj.shape[0]

    def body(x_ref, w_ref, b_ref, pre_ref, act_ref):
        x = x_ref[...]
        row = lax.broadcasted_iota(jnp.int32, x.shape, 0)
        pre = b_ref[...] + w_ref[3:4, :] * x
        for d in range(1, SSM_CONV):
            pre = pre + w_ref[3 - d:4 - d, :] * _shift_down(x, d, row)
        pre_ref[...] = pre
        act_ref[...] = pre * _sigmoid(pre)

    blk = pl.BlockSpec((S, LANE), lambda c: (0, c))
    return pl.pallas_call(
        body, grid=(8,),
        in_specs=[pl.BlockSpec((S, LANE), lambda c: (0, CB_XS + c)), pl.BlockSpec((SSM_CONV, LANE), lambda c: (0, c)),
                  pl.BlockSpec((1, LANE), lambda c: (0, c))],
        out_specs=[blk, blk], out_shape=[_sds((S, 1024), F32), _sds((S, 1024), F32)], name=name,
        compiler_params=_cparams("parallel"))(proj, conv_w, conv_b)


def conv_bwd(proj, pre, dact, conv_w, *, name):
    S = proj.shape[0]

    def body(x_ref, pre_ref, da_ref, w_ref, dx_ref, st_ref):
        x = x_ref[...]
        p = pre_ref[...]
        row = lax.broadcasted_iota(jnp.int32, x.shape, 0)
        sg = _sigmoid(p)
        dpre = da_ref[...] * (sg * (1.0 + p * (1.0 - sg)))
        dx = w_ref[3:4, :] * dpre
        st_ref[3:4, :] = jnp.sum(dpre * x, axis=0, keepdims=True)
        for d in range(1, SSM_CONV):
            dx = dx + w_ref[3 - d:4 - d, :] * _shift_up(dpre, d, row, S)
            st_ref[3 - d:4 - d, :] = jnp.sum(dpre * _shift_down(x, d, row), axis=0, keepdims=True)
        st_ref[4:5, :] = jnp.sum(dpre, axis=0, keepdims=True)
        st_ref[5:8, :] = jnp.zeros((3, LANE), F32)
        dx_ref[...] = _bf(dx)

    blk = pl.BlockSpec((S, LANE), lambda c: (0, c))
    return pl.pallas_call(
        body, grid=(8,),
        in_specs=[pl.BlockSpec((S, LANE), lambda c: (0, CB_XS + c)), blk, blk, pl.BlockSpec((SSM_CONV, LANE), lambda c: (0, c))],
        out_specs=[blk, pl.BlockSpec((8, LANE), lambda c: (0, c))],
        out_shape=[_sds((S, 1024), BF16), _sds((8, 1024), F32)], name=name,
        compiler_params=_cparams("parallel"))(proj, pre, dact, conv_w)


def _softplus(x):
    return jnp.maximum(x, 0.0) + jnp.log1p(jnp.exp(-jnp.abs(x)))


def _pair(lane, v0, v1):
    return jnp.where(lane < SSM_HEAD_DIM, v0, v1)


def _ssd_pair_common(raw, dtb, alog, xs, cm, hprev, T):
    lane = lax.broadcasted_iota(jnp.int32, (T, LANE), 1)
    dt = _softplus(raw + dtb)
    a = -jnp.exp(alog)
    incl = _tri(T, lambda l, s: s <= l).astype(BF16)
    h1, h2, h3 = _split3(dt * a)
    acum = _dot(incl, h1, NN) + _dot(incl, h2, NN) + _dot(incl, h3, NN)
    acum_t = acum.T
    causal = _tri(T, lambda l, s: s <= l)
    decay = [jnp.where(causal, jnp.exp(jnp.minimum(acum[:, j:j + 1] - acum_t[j:j + 1, :], 0.0)), 0.0) for j in (0, 1)]
    dtc = _pair(lane, dt[:, 0:1], dt[:, 1:2])
    ac = _pair(lane, acum[:, 0:1], acum[:, 1:2])
    xdt = xs * dtc
    ea = jnp.exp(ac)
    e_end = jnp.exp(ac[T - 1:T, :] - ac)
    sub = lax.broadcasted_iota(jnp.int32, (LANE, LANE), 0)
    cd = jnp.where(sub < SSM_HEAD_DIM, jnp.exp(acum[T - 1:T, 0:1]), jnp.exp(acum[T - 1:T, 1:2]))
    r = _dot(cm, _bf(hprev), NT)
    return lane, dt, a, acum, decay, dtc, xdt, ea, e_end, cd, r


def ssd_fwd(xact, proj, dtb, alog, dskip, *, name):
    S = xact.shape[0]
    T = min(SEQ_BLOCK, S)
    nb = S // T

    def body(xs_ref, bm_ref, cm_ref, dt0_ref, dt1_ref, dtb_ref, al_ref, ds_ref, y_ref, st_ref, state):
        @pl.when(pl.program_id(1) == 0)
        def _():
            state[...] = jnp.zeros_like(state)

        bm, cm = _bf(bm_ref[...]), _bf(cm_ref[...])
        gm = _dot(cm, bm, NT)
        for i, dt_ref in enumerate((dt0_ref, dt1_ref)):
            xs = xs_ref[:, LANE * i:LANE * (i + 1)]
            hprev = state[i]
            st_ref[i] = hprev
            lane, dt, a, acum, decay, dtc, xdt, ea, e_end, cd, r = _ssd_pair_common(
                dt_ref[...], dtb_ref[i], al_ref[i], xs, cm, hprev, T)
            xdtb = _bf(xdt)
            y_intra = _pair(lane, _dot(_bf(gm * decay[0]), xdtb, NN), _dot(_bf(gm * decay[1]), xdtb, NN))
            state[i] = cd * hprev + _dot(_bf(xdt * e_end), bm, TN)
            dsk = ds_ref[i]
            lane1 = lane[0:1, :]
            y_ref[:, LANE * i:LANE * (i + 1)] = y_intra + ea * r + _pair(lane1, dsk[:, 0:1], dsk[:, 1:2]) * xs

    rows3 = pl.BlockSpec((2, 1, LANE), lambda g, b: (g, 0, 0))
    return pl.pallas_call(
        body, grid=(2, nb),
        in_specs=[pl.BlockSpec((T, 256), lambda g, b: (b, g)), pl.BlockSpec((T, LANE), lambda g, b: (b, 4 + g)),
                  pl.BlockSpec((T, LANE), lambda g, b: (b, 6 + g)),
                  pl.BlockSpec((T, LANE), lambda g, b: (b, CB_DT + 2 * g)), pl.BlockSpec((T, LANE), lambda g, b: (b, CB_DT + 2 * g + 1)),
                  rows3, rows3, rows3],
        out_specs=[pl.BlockSpec((T, 256), lambda g, b: (b, g)),
                   pl.BlockSpec((None, None, 2, LANE, LANE), lambda g, b: (g, b, 0, 0, 0))],
        out_shape=[_sds((S, 512), F32), _sds((2, nb, 2, LANE, LANE), F32)],
        scratch_shapes=[pltpu.VMEM((2, LANE, LANE), F32)], name=name,
        compiler_params=_cparams("parallel", "arbitrary"))(xact, xact, xact, proj, proj, dtb, alog, dskip)


def ssd_bwd(xact, proj, dtb, alog, dskip, states, dy, *, name):
    S = xact.shape[0]
    T = min(SEQ_BLOCK, S)
    nb = S // T

    def body(xs_ref, bm_ref, cm_ref, dt0_ref, dt1_ref, dtb_ref, al_ref, ds_ref, st_ref, dy_ref,
             dxs_ref, dbm_ref, dcm_ref, ddt_ref, stat_ref, dstate):
        @pl.when(pl.program_id(1) == 0)
        def _():
            dstate[...] = jnp.zeros_like(dstate)
            stat_ref[...] = jnp.zeros_like(stat_ref)

        bm, cm = _bf(bm_ref[...]), _bf(cm_ref[...])
        gm = _dot(cm, bm, NT)
        dbm = jnp.zeros((T, LANE), F32)
        dcm = jnp.zeros((T, LANE), F32)
        after_eq = _tri(T, lambda i, l: l >= i).astype(BF16)
        rowi = lax.broadcasted_iota(jnp.int32, (T, 1), 0)
        for i, dt_ref in enumerate((dt0_ref, dt1_ref)):
            xs = xs_ref[:, LANE * i:LANE * (i + 1)]
            dyp = dy_ref[:, LANE * i:LANE * (i + 1)]
            hprev = st_ref[i]
            dh = dstate[i]
            raw = dt_ref[...]
            lane, dt, a, acum, decay, dtc, xdt, ea, e_end, cd, r = _ssd_pair_common(
                raw, dtb_ref[i], al_ref[i], xs, cm, hprev, T)
            lane1 = lane[0:1, :]
            dsk = ds_ref[i]
            dskp = _pair(lane1, dsk[:, 0:1], dsk[:, 1:2])
            head = [lane < SSM_HEAD_DIM, lane >= SSM_HEAD_DIM]
            hsum = lambda v, j: jnp.sum(jnp.where(head[j], v, 0.0), axis=1, keepdims=True)
            dhb = _bf(dh)
            xdtb = _bf(xdt)
            dyb = _bf(dyp)
            z = xdt * e_end
            dz = _dot(bm, dhb, NT)
            dbm = dbm + _dot(_bf(z), dhb, NN)
            dxdt = dz * e_end
            de_e = dz * z
            drr = dyp * ea
            dea_ea = drr * r
            dcm = dcm + _dot(_bf(drr), _bf(hprev), NN)
            dstate[i] = cd * dh + _dot(_bf(drr), cm, TN)
            dcd_cd = cd * dh * hprev
            dgs = jnp.zeros((T, T), F32)
            da_cols = []
            for j in (0, 1):
                w = gm * decay[j]
                dw = _dot(_bf(jnp.where(head[j], dyp, 0.0)), xdtb, NT)
                dxdt = dxdt + jnp.where(head[j], _dot(_bf(w), dyb, TN), 0.0)
                dgs = dgs + dw * decay[j]
                dseg = dw * w
                col = jnp.sum(dseg, axis=1, keepdims=True) - jnp.sum(dseg.T, axis=1, keepdims=True)
                col = col + hsum(dea_ea, j) - hsum(de_e, j)
                sub = lax.broadcasted_iota(jnp.int32, (LANE, LANE), 0)
                in_head = (sub < SSM_HEAD_DIM) if j == 0 else (sub >= SSM_HEAD_DIM)
                end = jnp.sum(hsum(de_e, j), axis=0, keepdims=True) + jnp.sum(
                    jnp.sum(jnp.where(in_head, dcd_cd, 0.0), axis=1, keepdims=True), axis=0, keepdims=True)
                da_cols.append(col + jnp.where(rowi == T - 1, end, 0.0))
            dgb = _bf(dgs)
            dcm = dcm + _dot(dgb, bm, NN)
            dbm = dbm + _dot(dgb, cm, TN)
            dacum = jnp.where(lane == 0, da_cols[0], jnp.where(lane == 1, da_cols[1], 0.0))
            h1, h2, h3 = _split3(dacum)
            ddta = _dot(after_eq, h1, NN) + _dot(after_eq, h2, NN) + _dot(after_eq, h3, NN)
            dxs_ref[:, LANE * i:LANE * (i + 1)] = dskp * dyp + dxdt * dtc
            dx_x = dxdt * xs
            ddt = ddta * a + jnp.where(lane == 0, hsum(dx_x, 0), jnp.where(lane == 1, hsum(dx_x, 1), 0.0))
            ddraw = jnp.where(lane < 2, ddt * _sigmoid(raw + dtb_ref[i]), 0.0)
            ddt_ref[:, LANE * i:LANE * (i + 1)] = _bf(ddraw)
            dsum = jnp.sum(dyp * xs, axis=0, keepdims=True)
            d0 = jnp.sum(jnp.where(lane1 < SSM_HEAD_DIM, dsum, 0.0), axis=1, keepdims=True)
            d1 = jnp.sum(jnp.where(lane1 >= SSM_HEAD_DIM, dsum, 0.0), axis=1, keepdims=True)
            dd = jnp.where(lane1 == 0, d0, jnp.where(lane1 == 1, d1, 0.0))
            stat_ref[i, 0:1, :] += jnp.sum(ddraw, axis=0, keepdims=True)
            stat_ref[i, 1:2, :] += jnp.where(lane1 < 2, jnp.sum(ddta * dt, axis=0, keepdims=True) * a, 0.0)
            stat_ref[i, 2:3, :] += dd
        dbm_ref[...] = dbm
        dcm_ref[...] = dcm

    rb = lambda b: nb - 1 - b
    rows3 = pl.BlockSpec((2, 1, LANE), lambda g, b: (g, 0, 0))
    tok256 = pl.BlockSpec((T, 256), lambda g, b: (rb(b), g))
    tok128 = pl.BlockSpec((T, LANE), lambda g, b: (rb(b), g))
    return pl.pallas_call(
        body, grid=(2, nb),
        in_specs=[tok256, pl.BlockSpec((T, LANE), lambda g, b: (rb(b), 4 + g)), pl.BlockSpec((T, LANE), lambda g, b: (rb(b), 6 + g)),
                  pl.BlockSpec((T, LANE), lambda g, b: (rb(b), CB_DT + 2 * g)),
                  pl.BlockSpec((T, LANE), lambda g, b: (rb(b), CB_DT + 2 * g + 1)),
                  rows3, rows3, rows3,
                  pl.BlockSpec((None, None, 2, LANE, LANE), lambda g, b: (g, rb(b), 0, 0, 0)), tok256],
        out_specs=[tok256, tok128, tok128, tok256, pl.BlockSpec((2, 8, LANE), lambda g, b: (g, 0, 0))],
        out_shape=[_sds((S, 512), F32), _sds((S, 256), F32), _sds((S, 256), F32), _sds((S, 512), BF16), _sds((4, 8, LANE), F32)],
        scratch_shapes=[pltpu.VMEM((2, LANE, LANE), F32)], name=name,
        compiler_params=_cparams("parallel", "arbitrary"))(xact, xact, xact, proj, proj, dtb, alog, dskip, states, dy)


def gated_norm(ypre, proj, gain, *, name):
    S, W = ypre.shape
    tm = min(TOKEN_TILE, S)

    def body(y_ref, z_ref, g_ref, o_ref):
        z = z_ref[...]
        yg = y_ref[...] * (z * _sigmoid(z))
        o_ref[...] = _bf(yg * lax.rsqrt(jnp.mean(yg * yg, axis=-1, keepdims=True) + NORM_EPS) * g_ref[...])

    tile = pl.BlockSpec((tm, W), lambda i: (i, 0))
    return pl.pallas_call(
        body, grid=(S // tm,), in_specs=[tile, pl.BlockSpec((tm, W), lambda i: (i, CB_MZ // 4)), pl.BlockSpec((1, W), lambda i: (0, 0))],
        out_specs=tile, out_shape=_sds((S, W), BF16), name=name, compiler_params=_cparams("parallel"))(ypre, proj, gain)


def gated_norm_bwd(ypre, proj, gain, dycat, *, name):
    S, W = ypre.shape
    tm = min(TOKEN_TILE, S)

    def body(y_ref, z_ref, g_ref, dy_ref, dyp_ref, dz_ref, st_ref):
        z = z_ref[...]
        y = y_ref[...]
        sg = _sigmoid(z)
        sz = z * sg
        yg = y * sz
        r = lax.rsqrt(jnp.mean(yg * yg, axis=-1, keepdims=True) + NORM_EPS)
        yhat = yg * r
        dyo = dy_ref[...]

        @pl.when(pl.program_id(0) == 0)
        def _():
            st_ref[...] = jnp.zeros_like(st_ref)

        st_ref[0:1, :] += jnp.sum(dyo * yhat, axis=0, keepdims=True)
        dyhat = dyo * g_ref[...]
        dyg = r * (dyhat - yhat * jnp.mean(dyhat * yhat, axis=-1, keepdims=True))
        dyp_ref[...] = dyg * sz
        dz_ref[...] = _bf(dyg * y * (sg * (1.0 + z * (1.0 - sg))))

    tile = pl.BlockSpec((tm, W), lambda i: (i, 0))
    return pl.pallas_call(
        body, grid=(S // tm,),
        in_specs=[tile, pl.BlockSpec((tm, W), lambda i: (i, CB_MZ // 4)), pl.BlockSpec((1, W), lambda i: (0, 0)),
                  pl.BlockSpec((tm, W), lambda i: (i, 2))],
        out_specs=[tile, tile, pl.BlockSpec((8, W), lambda i: (0, 0))],
        out_shape=[_sds((S, W), F32), _sds((S, W), BF16), _sds((8, W), F32)], name=name,
        compiler_params=_cparams("arbitrary"))(ypre, proj, gain, dycat)


def ada_mod(c_all, w, bias, *, name):
    M, K = c_all.shape
    N = w.shape[1]
    tn = _tile(N, 512)

    def body(c_ref, w_ref, b_ref, o_ref, cond_ref):
        cv = c_ref[...]
        cond = cv * _sigmoid(cv)
        cond_ref[...] = cond
        o_ref[...] = _dot(_bf(cond), _bf(w_ref[...]), NN) + b_ref[...]

    return pl.pallas_call(
        body, grid=(N // tn,),
        in_specs=[pl.BlockSpec((M, K), lambda j: (0, 0)), pl.BlockSpec((K, tn), lambda j: (0, j)), pl.BlockSpec((1, tn), lambda j: (0, j))],
        out_specs=[pl.BlockSpec((M, tn), lambda j: (0, j)), pl.BlockSpec((M, K), lambda j: (0, 0))],
        out_shape=[_sds((M, N), F32), _sds((M, K), F32)], name=name, compiler_params=_cparams("arbitrary"))(c_all, w, bias)


def _adamw(g, w, m, v):
    m = ADAM_B1 * m + (1.0 - ADAM_B1) * g
    v = ADAM_B2 * v + (1.0 - ADAM_B2) * (g * g)
    m_hat = m / (1.0 - ADAM_B1 ** ADAM_STEP)
    v_hat = v / (1.0 - ADAM_B2 ** ADAM_STEP)
    return -ADAM_LR * (m_hat / (jnp.sqrt(v_hat) + ADAM_EPS) + ADAM_WD * w), m, v


def adamw_parts(parts, w, m, v, *, name):
    P, R, C = parts.shape
    tr = _tile(R, 512, 256, 128, 64, 32, 16)

    def body(p_ref, w_ref, m_ref, v_ref, g_ref, d_ref, mo_ref, vo_ref):
        g = p_ref[0].astype(F32)
        for j in range(1, P):
            g = g + p_ref[j].astype(F32)
        g_ref[...] = g
        d_ref[...], mo_ref[...], vo_ref[...] = _adamw(g, w_ref[...], m_ref[...], v_ref[...])

    tile = pl.BlockSpec((tr, C), lambda i: (i, 0))
    return pl.pallas_call(
        body, grid=(R // tr,), in_specs=[pl.BlockSpec((P, tr, C), lambda i: (0, i, 0)), tile, tile, tile],
        out_specs=[tile] * 4, out_shape=[_sds((R, C), F32)] * 4, name=name, compiler_params=_cparams("parallel"))(parts, w, m, v)


def ada_adamw(cond_t, dmod, w, m, v, *, name):
    D, N = w.shape
    tr = _tile(D, 256)

    def body(c_ref, d_ref, w_ref, m_ref, v_ref, g_ref, dl_ref, mo_ref, vo_ref):
        cc = c_ref[...]
        dd = d_ref[...]
        g = cc[:, 0:1] * dd[0:1, :]
        for b in range(1, N_DEV):
            g = g + cc[:, b:b + 1] * dd[b:b + 1, :]
        g_ref[...] = g
        dl_ref[...], mo_ref[...], vo_ref[...] = _adamw(g, w_ref[...], m_ref[...], v_ref[...])

    tile = pl.BlockSpec((tr, N), lambda i: (i, 0))
    return pl.pallas_call(
        body, grid=(D // tr,), in_specs=[pl.BlockSpec((tr, N_DEV), lambda i: (i, 0)), pl.BlockSpec((N_DEV, N), lambda i: (0, 0)), tile, tile, tile],
        out_specs=[tile] * 4, out_shape=[_sds((D, N), F32)] * 4, name=name, compiler_params=_cparams("parallel"))(cond_t, dmod, w, m, v)


def _my_place():
    mx, my, mc = lax.axis_index("x"), lax.axis_index("y"), lax.axis_index("c")
    return mx, my, mc, 4 * mx + 2 * my + mc


def _peer(mx, my, mc, k):
    px = 1 - mx if (k >> 2) & 1 else mx
    py = 1 - my if (k >> 1) & 1 else my
    pc = 1 - mc if k & 1 else mc
    return (px, py, pc), 4 * px + 2 * py + pc


def _comm_call(body, x, out_shape, space, name):
    spec = pl.BlockSpec(memory_space=space)
    return pl.pallas_call(
        body, in_specs=[spec], out_specs=spec, out_shape=out_shape,
        scratch_shapes=[pltpu.SemaphoreType.DMA((N_DEV - 1,)), pltpu.SemaphoreType.DMA((N_DEV - 1,)), pltpu.SemaphoreType.DMA(())],
        name=name, compiler_params=pltpu.CompilerParams(has_side_effects=True, vmem_limit_bytes=VMEM_LIMIT))(x)


def allgather(x, *, in_vmem, name):
    def body(x_ref, out_ref, send_sems, recv_sems, local_sem):
        mx, my, mc, me = _my_place()
        mine = pltpu.make_async_copy(x_ref, out_ref.at[me], local_sem)
        mine.start()
        copies = []
        for k in range(1, N_DEV):
            peer, _ = _peer(mx, my, mc, k)
            cp = pltpu.make_async_remote_copy(src_ref=x_ref, dst_ref=out_ref.at[me], send_sem=send_sems.at[k - 1],
                                              recv_sem=recv_sems.at[k - 1], device_id=peer, device_id_type=pl.DeviceIdType.MESH)
            cp.start()
            copies.append(cp)
        for cp in copies:
            cp.wait()
        mine.wait()

    return _comm_call(body, x, _sds((N_DEV,) + x.shape, x.dtype), pltpu.VMEM if in_vmem else pltpu.HBM, name)


def alltoall(send, *, name):
    def body(s_ref, r_ref, send_sems, recv_sems, local_sem):
        mx, my, mc, me = _my_place()
        mine = pltpu.make_async_copy(s_ref.at[me], r_ref.at[me], local_sem)
        mine.start()
        copies = []
        for k in range(1, N_DEV):
            peer, pidx = _peer(mx, my, mc, k)
            cp = pltpu.make_async_remote_copy(src_ref=s_ref.at[pidx], dst_ref=r_ref.at[me], send_sem=send_sems.at[k - 1],
                                              recv_sem=recv_sems.at[k - 1], device_id=peer, device_id_type=pl.DeviceIdType.MESH)
            cp.start()
            copies.append(cp)
        for cp in copies:
            cp.wait()
        mine.wait()

    return _comm_call(body, send, _sds(send.shape, send.dtype), pltpu.HBM, name)


def _rows128(a):
    f = a.reshape(-1)
    n = -(-f.shape[0] // (16 * LANE)) * (16 * LANE)
    return jnp.pad(f, (0, n - f.shape[0])).reshape(-1, LANE)


PACK_ROWS = 512


def _pad_rows(buf):
    r = buf.shape[-2]
    pad = -r % PACK_ROWS
    return jnp.pad(buf, [(0, 0)] * (buf.ndim - 2) + [(0, pad), (0, 0)])


def _pack(arrays):
    parts = [_rows128(a) for a in arrays]
    offs = np.cumsum([0] + [p.shape[0] for p in parts])
    return _pad_rows(jnp.concatenate(parts, axis=0)), [int(o) for o in offs]


def _unpack(buf, offs, shapes):
    lead = buf.shape[:-2]
    out = []
    for o, shp in zip(offs, shapes):
        n = int(np.prod(shp))
        rows = -(-n // LANE)
        seg = buf[..., o:o + rows, :].reshape(lead + (rows * LANE,))[..., :n]
        out.append(seg.reshape(lead + tuple(shp)))
    return out


def _col_full(g):
    n, L, R, c = g.shape
    return g.transpose(1, 2, 0, 3).reshape(L, R, n * c)


def _row_full(g):
    n, L, r, C = g.shape
    return g.transpose(1, 0, 2, 3).reshape(L, n * r, C)


def _col_shards(w):
    L, R, C = w.shape
    return w.reshape(L, R, N_DEV, C // N_DEV).transpose(2, 0, 1, 3)


def _row_shards(w):
    L, R, C = w.shape
    return w.reshape(L, N_DEV, R // N_DEV, C).transpose(1, 0, 2, 3)


def _pad_w_in(w):
    L, D, _ = w.shape
    dt = jnp.pad(w[:, :, IN_MAIN:].reshape(L, D, 4, 2), ((0, 0), (0, 0), (0, 0), (0, LANE - 2)))
    return jnp.concatenate([w[:, :, :IN_MAIN], dt.reshape(L, D, 4 * LANE)], axis=-1)


def _unpad_w_in(g):
    L, D, _ = g.shape
    dt = g[:, :, IN_MAIN:].reshape(L, D, 4, LANE)[:, :, :, :2].reshape(L, D, SSM_HEADS)
    return jnp.concatenate([g[:, :, :IN_MAIN], dt], axis=-1)


def _pair_rows(p):
    return jnp.pad(p.reshape(4, 1, 2), ((0, 0), (0, 0), (0, LANE - 2)))


def _row(v):
    return v.reshape(1, -1)


def _ffn_fwd(h, gain, mod3, wg, wu, wd, tag):
    shift, scale, gate = mod3
    u = norm_mod(h, gain, shift, scale, name=tag + "_norm")
    a, b, act = ffn_up(u, wg, wu, name=tag + "_up")
    hn, out = matmul_resid(act, wd, h, gate, 0.5, name=tag + "_down")
    return hn, (h, u, a, b, act, out)


def _ffn_bwd(dh, saved, gain, mod3, wg, wu, wd, tag):
    h, u, a, b, act, out = saved
    _, scale, gate = mod3
    D, Fd = wg.shape
    dout, gst = gate_bwd(dh, out, gate, 0.5, name=tag + "_gate_bwd")
    da, db = ffn_dact(dout, wd, a, b, name=tag + "_dact")
    dh_prev, nst = dgrad_norm_bwd([da, db], [wg, wu], h, gain, scale, dh, name=tag + "_dgrad")
    tf = _tile(Fd, 1408, 512)
    dwg = matmul(u, da, ta=True, tm=D, tn=tf, tk=512, name=tag + "_dwg")
    dwu = matmul(u, db, ta=True, tm=D, tn=tf, tk=512, name=tag + "_dwu")
    dwd = matmul(act, dout, ta=True, tm=tf, tn=D, tk=512, name=tag + "_dwd")
    return dh_prev, (dwg, dwu, dwd), nst[0], [nst[1], nst[2], gst[0]]


def _mix_fwd(h, p, mod3, w_in, w_out, cos, sin, tag):
    shift, scale, gate = mod3
    u = norm_mod(h, p["norm_mix"], shift, scale, name=tag + "_norm")
    proj = matmul(u, w_in, tm=TOKEN_TILE, tn=512, tk=D_MODEL, name=tag + "_proj")
    y_ret, ypre_ret, st_ret = ret_fwd(proj, p["ret_gn"], cos, sin, name=tag + "_ret")
    y_sb, sb_cin = sb_fwd(proj, name=tag + "_sb")
    pre, xact = conv_fwd(proj, p["conv_w"], p["conv_b"], name=tag + "_conv")
    ypre_ssm, st_ssm = ssd_fwd(xact, proj, p["dt_bias"], p["a_log"], p["d_skip"], name=tag + "_ssd")
    y_ssm = gated_norm(ypre_ssm, proj, p["ssm_norm"], name=tag + "_gnorm")
    ycat = jnp.concatenate([y_ret, y_sb.astype(BF16), y_ssm], axis=1)
    hn, mixed = matmul_resid(ycat, w_out, h, gate, 1.0, name=tag + "_out")
    return hn, (h, u, proj, ypre_ret, st_ret, sb_cin, pre, xact, ypre_ssm, st_ssm, ycat, mixed)


def _mix_bwd(dh, saved, p, mod3, w_in, w_out, cos, sin, tag):
    h, u, proj, ypre_ret, st_ret, sb_cin, pre, xact, ypre_ssm, st_ssm, ycat, mixed = saved
    _, scale, gate = mod3
    dmixed, gst = gate_bwd(dh, mixed, gate, 1.0, name=tag + "_gate_bwd")
    dycat = matmul(dmixed, w_out, tb=True, tm=TOKEN_TILE, tn=512, tk=D_MODEL, name=tag + "_dycat")
    dw_out = matmul(ycat, dmixed, ta=True, tm=512, tn=D_MODEL, tk=512, name=tag + "_dw_out")
    dq, dk, dv, dg, rst = ret_bwd(proj, p["ret_gn"], cos, sin, ypre_ret, st_ret, dycat, name=tag + "_ret_bwd")
    dsq, dsk, dsv = sb_bwd(proj, sb_cin, dycat, name=tag + "_sb_bwd")
    dypre, dz, nst2 = gated_norm_bwd(ypre_ssm, proj, p["ssm_norm"], dycat, name=tag + "_gnorm_bwd")
    dxs, dbm, dcm, ddt, sst = ssd_bwd(xact, proj, p["dt_bias"], p["a_log"], p["d_skip"], st_ssm, dypre, name=tag + "_ssd_bwd")
    dact = jnp.concatenate([dxs, dbm, dcm], axis=1)
    dxbc, cst = conv_bwd(proj, pre, dact, p["conv_w"], name=tag + "_conv_bwd")
    dproj = jnp.concatenate([dq, dk, dv, dg, dsq, dsk.astype(BF16), dsv.astype(BF16), dz, dxbc, ddt], axis=1)
    dh_prev, nst = dgrad_norm_bwd([dproj], [w_in], h, p["norm_mix"], scale, dh, name=tag + "_dgrad")
    dw_in = matmul(u, dproj, ta=True, tm=D_MODEL, tn=512, tk=512, name=tag + "_dw_in")
    small = dict(norm_mix=nst[0], ret_gn=rst[0], ssm_norm=nst2[0], conv_w=cst[0:4], conv_b=cst[4],
                 dt_bias=sst[:, 0, :2].reshape(SSM_HEADS), a_log=sst[:, 1, :2].reshape(SSM_HEADS), d_skip=sst[:, 2, :2].reshape(SSM_HEADS))
    return dh_prev, dw_in, dw_out, small, [nst[1], nst[2], gst[0]]


BIG = ("ffn1_wg", "ffn1_wu", "ffn1_wd", "w_in", "w_out", "ffn2_wg", "ffn2_wu", "ffn2_wd")
ROW_SHARDED = ("ffn1_wd", "w_out", "ffn2_wd")
SMALL = ("ada_b", "norm_ffn1", "norm_mix", "conv_b", "dt_bias", "a_log", "d_skip", "ret_gn", "ssm_norm", "norm_ffn2",
         "final_ada_b", "final_norm")
NAMES = ("ada_w", "ada_b", "norm_ffn1", "ffn1_wg", "ffn1_wu", "ffn1_wd", "norm_mix", "w_in", "conv_w", "conv_b", "dt_bias", "a_log",
         "d_skip", "ret_gn", "ssm_norm", "w_out", "norm_ffn2", "ffn2_wg", "ffn2_wu", "ffn2_wd", "final_ada_w", "final_ada_b", "final_norm")


def kernel(x, c, ada_w, ada_b, norm_ffn1, ffn1_wg, ffn1_wu, ffn1_wd, norm_mix, w_in, conv_w, conv_b, dt_bias, a_log, d_skip, ret_gn, ssm_norm, w_out, norm_ffn2, ffn2_wg, ffn2_wu, ffn2_wd, final_ada_w, final_ada_b, final_norm, loss_target, m_ada_w, m_ada_b, m_norm_ffn1, m_ffn1_wg, m_ffn1_wu, m_ffn1_wd, m_norm_mix, m_w_in, m_conv_w, m_conv_b, m_dt_bias, m_a_log, m_d_skip, m_ret_gn, m_ssm_norm, m_w_out, m_norm_ffn2, m_ffn2_wg, m_ffn2_wu, m_ffn2_wd, m_final_ada_w, m_final_ada_b, m_final_norm, v_ada_w, v_ada_b, v_norm_ffn1, v_ffn1_wg, v_ffn1_wu, v_ffn1_wd, v_norm_mix, v_w_in, v_conv_w, v_conv_b, v_dt_bias, v_a_log, v_d_skip, v_ret_gn, v_ssm_norm, v_w_out, v_norm_ffn2, v_ffn2_wg, v_ffn2_wu, v_ffn2_wd, v_final_ada_w, v_final_ada_b, v_final_norm):
    W = dict(ada_w=ada_w, ada_b=ada_b, norm_ffn1=norm_ffn1, ffn1_wg=ffn1_wg, ffn1_wu=ffn1_wu, ffn1_wd=ffn1_wd, norm_mix=norm_mix,
             w_in=w_in, conv_w=conv_w, conv_b=conv_b, dt_bias=dt_bias, a_log=a_log, d_skip=d_skip, ret_gn=ret_gn, ssm_norm=ssm_norm,
             w_out=w_out, norm_ffn2=norm_ffn2, ffn2_wg=ffn2_wg, ffn2_wu=ffn2_wu, ffn2_wd=ffn2_wd, final_ada_w=final_ada_w,
             final_ada_b=final_ada_b, final_norm=final_norm)
    M1 = dict(ada_w=m_ada_w, ada_b=m_ada_b, norm_ffn1=m_norm_ffn1, ffn1_wg=m_ffn1_wg, ffn1_wu=m_ffn1_wu, ffn1_wd=m_ffn1_wd,
              norm_mix=m_norm_mix, w_in=m_w_in, conv_w=m_conv_w, conv_b=m_conv_b, dt_bias=m_dt_bias, a_log=m_a_log, d_skip=m_d_skip,
              ret_gn=m_ret_gn, ssm_norm=m_ssm_norm, w_out=m_w_out, norm_ffn2=m_norm_ffn2, ffn2_wg=m_ffn2_wg, ffn2_wu=m_ffn2_wu,
              ffn2_wd=m_ffn2_wd, final_ada_w=m_final_ada_w, final_ada_b=m_final_ada_b, final_norm=m_final_norm)
    V2 = dict(ada_w=v_ada_w, ada_b=v_ada_b, norm_ffn1=v_norm_ffn1, ffn1_wg=v_ffn1_wg, ffn1_wu=v_ffn1_wu, ffn1_wd=v_ffn1_wd,
              norm_mix=v_norm_mix, w_in=v_w_in, conv_w=v_conv_w, conv_b=v_conv_b, dt_bias=v_dt_bias, a_log=v_a_log, d_skip=v_d_skip,
              ret_gn=v_ret_gn, ssm_norm=v_ssm_norm, w_out=v_w_out, norm_ffn2=v_norm_ffn2, ffn2_wg=v_ffn2_wg, ffn2_wu=v_ffn2_wu,
              ffn2_wd=v_ffn2_wd, final_ada_w=v_final_ada_w, final_ada_b=v_final_ada_b, final_norm=v_final_norm)
    D = D_MODEL
    S = x.shape[1]
    me = 4 * lax.axis_index("x") + 2 * lax.axis_index("y") + lax.axis_index("c")
    n_mod = ada_w.shape[2]
    n_fmod = final_ada_w.shape[1]

    c_all = allgather(jnp.broadcast_to(c, (8, D)), in_vmem=True, name="gather_c")[:, 0, :]
    ada_cols = jnp.concatenate([ada_w[0], ada_w[1], final_ada_w], axis=1)
    ada_bias = jnp.concatenate([lax.dynamic_slice(ada_b, (0, me * n_mod), (DEPTH, n_mod)).reshape(1, -1),
                                lax.dynamic_slice(final_ada_b, (me * n_fmod,), (n_fmod,)).reshape(1, -1)], axis=1)
    mod_sh, cond = ada_mod(jnp.pad(c_all, ((0, 8), (0, 0))), ada_cols, ada_bias, name="ada_mod")
    n_cols = mod_sh.shape[1]
    small_in = jnp.concatenate([mod_sh[:8], jnp.pad(conv_w.reshape(8, LANE), ((0, 0), (0, n_cols - LANE)))], axis=0)
    small_g = allgather(small_in, in_vmem=True, name="gather_mod")
    mod_rows = lax.dynamic_index_in_dim(small_g[:, :8, :], me, axis=1, keepdims=False)
    mod = [mod_rows[:, l * n_mod:(l + 1) * n_mod].reshape(9, D) for l in range(DEPTH)]
    fmod = mod_rows[:, DEPTH * n_mod:].reshape(2, D)
    conv_w_full = small_g[:, 8:, :LANE].reshape(N_DEV, DEPTH, SSM_CONV, LANE).transpose(1, 2, 0, 3).reshape(DEPTH, SSM_CONV, 8 * LANE)

    wpack, woffs = _pack([W[n].astype(BF16) for n in BIG])
    wgath = allgather(wpack, in_vmem=False, name="gather_weights")
    gshards = _unpack(wgath, woffs, [W[n].shape for n in BIG])
    full = {n: (_row_full(g) if n in ROW_SHARDED else _col_full(g)) for n, g in zip(BIG, gshards)}
    full["w_in"] = _pad_w_in(full["w_in"])

    cos, sin = _rope_tables(S)
    h = x[0]
    target = loss_target[0]
    layer_p = []
    for l in range(DEPTH):
        layer_p.append(dict(norm_ffn1=_row(norm_ffn1[l]), norm_mix=_row(norm_mix[l]), norm_ffn2=_row(norm_ffn2[l]),
                            ret_gn=_row(ret_gn[l]), ssm_norm=_row(ssm_norm[l]), conv_w=conv_w_full[l], conv_b=_row(conv_b[l]),
                            dt_bias=_pair_rows(dt_bias[l]), a_log=_pair_rows(a_log[l]), d_skip=_pair_rows(d_skip[l])))
    mods = [[[_row(mod[l][3 * s + k]) for k in range(3)] for s in range(3)] for l in range(DEPTH)]

    saved = []
    for l in range(DEPTH):
        p = layer_p[l]
        h, s1 = _ffn_fwd(h, p["norm_ffn1"], mods[l][0], full["ffn1_wg"][l], full["ffn1_wu"][l], full["ffn1_wd"][l], f"l{l}_ffn1")
        h, s2 = _mix_fwd(h, p, mods[l][1], full["w_in"][l], full["w_out"][l], cos, sin, f"l{l}_mix")
        h, s3 = _ffn_fwd(h, p["norm_ffn2"], mods[l][2], full["ffn2_wg"][l], full["ffn2_wu"][l], full["ffn2_wd"][l], f"l{l}_ffn2")
        saved.append((s1, s2, s3))

    dh, fst = final_loss_bwd(h, _row(final_norm), _row(fmod[0]), _row(fmod[1]), target, name="final")
    big_g = {n: [None] * DEPTH for n in BIG}
    small_g_l = [None] * DEPTH
    dmod = [None] * DEPTH
    for l in reversed(range(DEPTH)):
        p = layer_p[l]
        s1, s2, s3 = saved[l]
        dh, (g2g, g2u, g2d), gn2, dm2 = _ffn_bwd(dh, s3, p["norm_ffn2"], mods[l][2], full["ffn2_wg"][l], full["ffn2_wu"][l],
                                                 full["ffn2_wd"][l], f"l{l}_ffn2")
        dh, gw_in, gw_out, sm, dm1 = _mix_bwd(dh, s2, p, mods[l][1], full["w_in"][l], full["w_out"][l], cos, sin, f"l{l}_mix")
        dh, (g1g, g1u, g1d), gn1, dm0 = _ffn_bwd(dh, s1, p["norm_ffn1"], mods[l][0], full["ffn1_wg"][l], full["ffn1_wu"][l],
                                                 full["ffn1_wd"][l], f"l{l}_ffn1")
        for n, g in zip(BIG, (g1g, g1u, g1d, gw_in, gw_out, g2g, g2u, g2d)):
            big_g[n][l] = g
        sm["norm_ffn1"], sm["norm_ffn2"] = gn1, gn2
        small_g_l[l] = sm
        dmod[l] = jnp.concatenate(dm0 + dm1 + dm2, axis=0)
    grad_x = dh[None]

    gfull = {n: jnp.stack(big_g[n]) for n in BIG}
    gfull["w_in"] = _unpad_w_in(gfull["w_in"])
    gsh = [(_row_shards(gfull[n]) if n in ROW_SHARDED else _col_shards(gfull[n])).astype(BF16) for n in BIG]
    spack = _pad_rows(jnp.concatenate(
        [jnp.pad(g.reshape(N_DEV, -1), ((0, 0), (0, (woffs[i + 1] - woffs[i]) * LANE - g[0].size))).reshape(N_DEV, -1, LANE)
         for i, g in enumerate(gsh)], axis=1))
    rpack = alltoall(spack, name="exchange_grads")
    packs = [_pack([src[n] for n in BIG])[0] for src in (W, M1, V2)]
    outs = adamw_parts(rpack, *packs, name="adamw_big")
    big_out = [dict(zip(BIG, _unpack(o, woffs, [W[n].shape for n in BIG]))) for o in outs]

    stack2 = lambda key: jnp.stack([small_g_l[l][key] for l in range(DEPTH)])
    pieces = [("loss", fst[3, 0:1]), ("ada_b", jnp.stack(dmod)), ("final_ada_b", jnp.concatenate([fst[1], fst[2]])),
              ("norm_ffn1", stack2("norm_ffn1")), ("norm_mix", stack2("norm_mix")), ("norm_ffn2", stack2("norm_ffn2")),
              ("conv_w", stack2("conv_w")), ("conv_b", stack2("conv_b")), ("dt_bias", stack2("dt_bias")), ("a_log", stack2("a_log")),
              ("d_skip", stack2("d_skip")), ("ret_gn", stack2("ret_gn")), ("ssm_norm", stack2("ssm_norm")), ("final_norm", fst[0])]
    names = [n for n, _ in pieces]
    shapes = [a.shape for _, a in pieces]
    ppack, poffs = _pack([a for _, a in pieces])
    pg = allgather(ppack, in_vmem=True, name="gather_small")
    zero_like = lambda n, a: jnp.zeros(a.shape, F32)
    spacks = [_pack([(src[n] if n in SMALL else zero_like(n, a)) for n, a in pieces])[0] for src in (W, M1, V2)]
    souts = adamw_parts(pg, *spacks, name="adamw_small")
    small_out = [dict(zip(names, _unpack(o, poffs, shapes))) for o in souts]
    loss = small_out[0]["loss"][0]

    gathered = dict(zip(names, _unpack(pg, poffs, shapes)))
    conv_parts = lax.dynamic_slice_in_dim(gathered["conv_w"], me * LANE, LANE, axis=3).reshape(N_DEV, DEPTH * SSM_CONV, LANE)
    conv_out = [o.reshape(conv_w.shape) for o in adamw_parts(conv_parts, conv_w.reshape(-1, LANE), m_conv_w.reshape(-1, LANE),
                                                              v_conv_w.reshape(-1, LANE), name="adamw_conv_w")]
    cond_t = cond[:8].T
    ada_out = []
    for l in range(DEPTH):
        dsel = lax.dynamic_slice_in_dim(gathered["ada_b"][:, l, :], me * n_mod, n_mod, axis=1)
        ada_out.append(ada_adamw(cond_t, dsel, ada_w[l], m_ada_w[l], v_ada_w[l], name=f"adamw_ada_w{l}"))
    ada_out = [jnp.stack([ada_out[l][k] for l in range(DEPTH)]) for k in range(4)]
    fsel = lax.dynamic_slice_in_dim(gathered["final_ada_b"].reshape(N_DEV, 2 * D), me * n_fmod, n_fmod, axis=1)
    fada_out = ada_adamw(cond_t, fsel, final_ada_w, m_final_ada_w, v_final_ada_w, name="adamw_final_ada_w")

    def pick(k, n):
        if n in BIG:
            return big_out[k][n]
        if n == "ada_w":
            return ada_out[k]
        if n == "final_ada_w":
            return fada_out[k]
        if n == "conv_w":
            return conv_out[k]
        return small_out[k][n]

    return (loss, grad_x) + tuple(pick(k, n) for k in range(4) for n in NAMES)
```

```python
import functools
import math

import numpy as np
import jax
import jax.numpy as jnp
from jax import lax
from jax.experimental import pallas as pl
from jax.experimental.pallas import tpu as pltpu

F32 = jnp.float32
BF16 = jnp.bfloat16

D_MODEL = 1024
DEPTH = 2
RET_HEADS = 4
HEAD_DIM = 128
SSM_HEADS = 8
SSM_HEAD_DIM = 64
SSM_STATE = 128
SSM_CONV = 4
D_FF = 2816
ROPE_BASE = 10000.0
NORM_EPS = 1e-6
MIX_W = 1536
IN_W = 5128
IN_MAIN = 5120
IN_PAD = 5632
N_DEV = 8
LANE = 128

ADAM_LR = 0.001
ADAM_B1 = 0.9
ADAM_B2 = 0.999
ADAM_EPS = 1e-08
ADAM_WD = 0.01
ADAM_STEP = 10

TOKEN_TILE = 512
SEQ_BLOCK = 256
VMEM_LIMIT = 56 << 20
SB_SKIP = 120.0
SB_UNVISITED = -1e30

CB_RQ, CB_RK, CB_RV, CB_RG = 0, 4, 8, 12
CB_SQ, CB_SK, CB_SV = 16, 20, 24
CB_MZ, CB_XS, CB_BM, CB_CM, CB_DT = 28, 32, 36, 38, 40


def _cparams(*sem):
    return pltpu.CompilerParams(dimension_semantics=sem, vmem_limit_bytes=VMEM_LIMIT)


def _sds(shape, dtype):
    return jax.ShapeDtypeStruct(tuple(shape), dtype)


def _tile(n, *prefs):
    for p in prefs:
        if n % p == 0:
            return p
    return n


def _dot(a, b, dims):
    return lax.dot_general(a, b, (dims, ((), ())), preferred_element_type=F32)


NN = ((1,), (0,))
NT = ((1,), (1,))
TN = ((0,), (0,))


def _bf(x):
    return x.astype(BF16)


def _sigmoid(x):
    return jax.nn.sigmoid(x)


def _split2(x):
    hi = x.astype(BF16)
    lo = (x - hi.astype(F32)).astype(BF16)
    return hi, lo


def _split3(x):
    hi = x.astype(BF16)
    r = x - hi.astype(F32)
    mid = r.astype(BF16)
    lo = (r - mid.astype(F32)).astype(BF16)
    return hi, mid, lo


def matmul(a, b, *, ta=False, tb=False, tm=512, tn=512, tk=512, out_dtype=F32, name):
    M, K = (a.shape[1], a.shape[0]) if ta else a.shape
    N = b.shape[0] if tb else b.shape[1]
    tm, tn, tk = min(tm, M), min(tn, N), min(tk, K)
    assert M % tm == 0 and N % tn == 0 and K % tk == 0, (name, M, N, K, tm, tn, tk)
    nk = K // tk
    a_spec = pl.BlockSpec((tk, tm), lambda i, j, k: (k, i)) if ta else pl.BlockSpec((tm, tk), lambda i, j, k: (i, k))
    b_spec = pl.BlockSpec((tn, tk), lambda i, j, k: (j, k)) if tb else pl.BlockSpec((tk, tn), lambda i, j, k: (k, j))
    dims = ((0 if ta else 1,), (1 if tb else 0,))

    def body(a_ref, b_ref, o_ref, acc_ref):
        k = pl.program_id(2)
        p = _dot(_bf(a_ref[...]), _bf(b_ref[...]), dims)

        @pl.when(k == 0)
        def _():
            acc_ref[...] = p

        @pl.when(k > 0)
        def _():
            acc_ref[...] += p

        @pl.when(k == nk - 1)
        def _():
            o_ref[...] = acc_ref[...].astype(out_dtype)

    return pl.pallas_call(
        body, grid=(M // tm, N // tn, nk), in_specs=[a_spec, b_spec],
        out_specs=pl.BlockSpec((tm, tn), lambda i, j, k: (i, j)), out_shape=_sds((M, N), out_dtype),
        scratch_shapes=[pltpu.VMEM((tm, tn), F32)], name=name,
        compiler_params=_cparams("parallel", "parallel", "arbitrary"))(a, b)


def matmul_resid(a, w, h, gate, factor, *, name):
    M, K = a.shape
    N = w.shape[1]
    tm, tn = min(TOKEN_TILE, M), _tile(N, 512)

    def body(a_ref, w_ref, h_ref, g_ref, hn_ref, o_ref):
        out = _dot(a_ref[...], w_ref[...], NN)
        o_ref[...] = out
        hn_ref[...] = h_ref[...] + (factor * (1.0 + g_ref[...])) * out

    mn = pl.BlockSpec((tm, tn), lambda i, j: (i, j))
    return pl.pallas_call(
        body, grid=(M // tm, N // tn),
        in_specs=[pl.BlockSpec((tm, K), lambda i, j: (i, 0)), pl.BlockSpec((K, tn), lambda i, j: (0, j)), mn,
                  pl.BlockSpec((1, tn), lambda i, j: (0, j))],
        out_specs=[mn, mn], out_shape=[_sds((M, N), F32), _sds((M, N), F32)], name=name,
        compiler_params=_cparams("parallel", "parallel"))(a, w, h, gate)


def ffn_up(u, wg, wu, *, name):
    M, K = u.shape
    N = wg.shape[1]
    tm, tn = min(TOKEN_TILE, M), _tile(N, 1408, 256)

    def body(u_ref, wg_ref, wu_ref, a_ref, b_ref, act_ref):
        uu = u_ref[...]
        a = _dot(uu, wg_ref[...], NN)
        b = _dot(uu, wu_ref[...], NN)
        a_ref[...] = a
        b_ref[...] = b
        act_ref[...] = _bf(a * _sigmoid(a) * b)

    mn = pl.BlockSpec((tm, tn), lambda i, j: (i, j))
    wspec = pl.BlockSpec((K, tn), lambda i, j: (0, j))
    return pl.pallas_call(
        body, grid=(M // tm, N // tn), in_specs=[pl.BlockSpec((tm, K), lambda i, j: (i, 0)), wspec, wspec],
        out_specs=[mn, mn, mn], out_shape=[_sds((M, N), F32), _sds((M, N), F32), _sds((M, N), BF16)], name=name,
        compiler_params=_cparams("parallel", "parallel"))(u, wg, wu)


def ffn_dact(dout, wd, a, b, *, name):
    M, K = dout.shape
    N = wd.shape[0]
    tm, tn = min(TOKEN_TILE, M), _tile(N, 1408, 256)

    def body(d_ref, w_ref, a_ref, b_ref, da_ref, db_ref):
        dact = _dot(d_ref[...], w_ref[...], NT)
        av = a_ref[...]
        sg = _sigmoid(av)
        db_ref[...] = _bf(dact * av * sg)
        da_ref[...] = _bf(dact * b_ref[...] * (sg * (1.0 + av * (1.0 - sg))))

    mn = pl.BlockSpec((tm, tn), lambda i, j: (i, j))
    return pl.pallas_call(
        body, grid=(M // tm, N // tn),
        in_specs=[pl.BlockSpec((tm, K), lambda i, j: (i, 0)), pl.BlockSpec((tn, K), lambda i, j: (j, 0)), mn, mn],
        out_specs=[mn, mn], out_shape=[_sds((M, N), BF16), _sds((M, N), BF16)], name=name,
        compiler_params=_cparams("parallel", "parallel"))(dout, wd, a, b)


def norm_mod(h, gain, shift, scale, *, name):
    S, D = h.shape
    tm = min(TOKEN_TILE, S)

    def body(h_ref, g_ref, sh_ref, sc_ref, u_ref):
        x = h_ref[...]
        r = lax.rsqrt(jnp.mean(x * x, axis=-1, keepdims=True) + NORM_EPS)
        n = x * r * g_ref[...]
        u_ref[...] = _bf(n * (1.0 + sc_ref[...]) + sh_ref[...])

    row = pl.BlockSpec((1, D), lambda i: (0, 0))
    tile = pl.BlockSpec((tm, D), lambda i: (i, 0))
    return pl.pallas_call(body, grid=(S // tm,), in_specs=[tile, row, row, row], out_specs=tile,
                          out_shape=_sds((S, D), BF16), name=name, compiler_params=_cparams("parallel"))(h, gain, shift, scale)


def dgrad_norm_bwd(lhs, ws, h, gain, scale, dres, *, name):
    S, D = h.shape
    tm = min(256, S)
    n = len(lhs)

    def body(*refs):
        l_refs, w_refs = refs[:n], refs[n:2 * n]
        h_ref, g_ref, sc_ref, dres_ref, dh_ref, st_ref = refs[2 * n:]
        du = _dot(l_refs[0][...], w_refs[0][...], NT)
        for lr, wr in zip(l_refs[1:], w_refs[1:]):
            du = du + _dot(lr[...], wr[...], NT)
        x = h_ref[...]
        g = g_ref[...]
        r = lax.rsqrt(jnp.mean(x * x, axis=-1, keepdims=True) + NORM_EPS)
        xhat = x * r
        dn = du * (1.0 + sc_ref[...])
        dxhat = dn * g
        dh_ref[...] = dres_ref[...] + r * (dxhat - xhat * jnp.mean(dxhat * xhat, axis=-1, keepdims=True))

        @pl.when(pl.program_id(0) == 0)
        def _():
            st_ref[...] = jnp.zeros_like(st_ref)

        st_ref[0:1, :] += jnp.sum(dn * xhat, axis=0, keepdims=True)
        st_ref[1:2, :] += jnp.sum(du, axis=0, keepdims=True)
        st_ref[2:3, :] += jnp.sum(du * (xhat * g), axis=0, keepdims=True)

    row = pl.BlockSpec((1, D), lambda i: (0, 0))
    tile = pl.BlockSpec((tm, D), lambda i: (i, 0))
    in_specs = [pl.BlockSpec((tm, l.shape[1]), lambda i: (i, 0)) for l in lhs]
    in_specs += [pl.BlockSpec(w.shape, lambda i: (0, 0)) for w in ws]
    in_specs += [tile, row, row, tile]
    return pl.pallas_call(
        body, grid=(S // tm,), in_specs=in_specs, out_specs=[tile, pl.BlockSpec((8, D), lambda i: (0, 0))],
        out_shape=[_sds((S, D), F32), _sds((8, D), F32)], name=name,
        compiler_params=_cparams("arbitrary"))(*lhs, *ws, h, gain, scale, dres)


def gate_bwd(dh, out, gate, factor, *, name):
    S, D = dh.shape
    tm = min(TOKEN_TILE, S)

    def body(dh_ref, o_ref, g_ref, do_ref, st_ref):
        d = dh_ref[...]
        do_ref[...] = _bf(d * (factor * (1.0 + g_ref[...])))

        @pl.when(pl.program_id(0) == 0)
        def _():
            st_ref[...] = jnp.zeros_like(st_ref)

        st_ref[0:1, :] += factor * jnp.sum(d * o_ref[...], axis=0, keepdims=True)

    tile = pl.BlockSpec((tm, D), lambda i: (i, 0))
    return pl.pallas_call(
        body, grid=(S // tm,), in_specs=[tile, tile, pl.BlockSpec((1, D), lambda i: (0, 0))],
        out_specs=[tile, pl.BlockSpec((8, D), lambda i: (0, 0))], out_shape=[_sds((S, D), BF16), _sds((8, D), F32)],
        name=name, compiler_params=_cparams("arbitrary"))(dh, out, gate)


def final_loss_bwd(h, gain, shift, scale, target, *, name):
    S, D = h.shape
    tm = min(TOKEN_TILE, S)

    def body(h_ref, g_ref, sh_ref, sc_ref, t_ref, dh_ref, st_ref):
        x = h_ref[...]
        g = g_ref[...]
        r = lax.rsqrt(jnp.mean(x * x, axis=-1, keepdims=True) + NORM_EPS)
        xhat = x * r
        n = xhat * g
        err = n * (1.0 + sc_ref[...]) + sh_ref[...] - t_ref[...]
        dy = err * (1.0 / D)
        dn = dy * (1.0 + sc_ref[...])
        dxhat = dn * g
        dh_ref[...] = r * (dxhat - xhat * jnp.mean(dxhat * xhat, axis=-1, keepdims=True))

        @pl.when(pl.program_id(0) == 0)
        def _():
            st_ref[...] = jnp.zeros_like(st_ref)

        st_ref[0:1, :] += jnp.sum(dn * xhat, axis=0, keepdims=True)
        st_ref[1:2, :] += jnp.sum(dy, axis=0, keepdims=True)
        st_ref[2:3, :] += jnp.sum(dy * n, axis=0, keepdims=True)
        tok = jnp.mean(err * err, axis=-1, keepdims=True)
        st_ref[3:4, :] += 0.5 * jnp.sum(tok, axis=0, keepdims=True)

    row = pl.BlockSpec((1, D), lambda i: (0, 0))
    tile = pl.BlockSpec((tm, D), lambda i: (i, 0))
    return pl.pallas_call(
        body, grid=(S // tm,), in_specs=[tile, row, row, row, tile], out_specs=[tile, pl.BlockSpec((8, D), lambda i: (0, 0))],
        out_shape=[_sds((S, D), F32), _sds((8, D), F32)], name=name,
        compiler_params=_cparams("arbitrary"))(h, gain, shift, scale, target)


def _ret_tables(T):
    heads = np.arange(RET_HEADS, dtype=np.float64)
    lg = np.log1p(-(2.0 ** (-5.0 - heads)))
    t = np.arange(T)
    same = (t[:, None] // 64) == (t[None, :] // 64)
    earlier = (t[None, :] // 64) < (t[:, None] // 64)
    dist = np.abs(t[:, None] - t[None, :]).astype(np.float64)
    dmat = np.where(same | earlier, np.exp(lg[:, None, None] * dist[None]), 0.0)
    qdec = np.exp(lg[:, None] * (t + 1.0)[None, :])
    kdec = np.exp(lg[:, None] * (T - 1.0 - t)[None, :])
    cdec = np.exp(lg * T)
    bc = lambda v: jnp.asarray(np.broadcast_to(v[:, :, None], (RET_HEADS, T, LANE)), F32)
    cd = jnp.asarray(np.broadcast_to(cdec[:, None, None], (RET_HEADS, LANE, LANE)), F32)
    return jnp.asarray(dmat, F32), bc(qdec), bc(kdec), cd


def _rope_tables(S):
    half = HEAD_DIM // 2
    inv_freq = ROPE_BASE ** (-jnp.arange(half, dtype=F32) / half)
    ang = jnp.arange(S, dtype=F32)[:, None] * inv_freq[None, :]
    cos, sin = jnp.cos(ang), jnp.sin(ang)
    return jnp.concatenate([cos, cos], axis=-1), jnp.concatenate([-sin, sin], axis=-1)


def _rope(x, c, s):
    return x * c + pltpu.roll(x, HEAD_DIM // 2, 1) * s


def _rope_t(dx, c, s):
    return dx * c + pltpu.roll(dx * s, HEAD_DIM // 2, 1)


def ret_fwd(proj, gn, cos, sin, *, name):
    S = proj.shape[0]
    T = min(SEQ_BLOCK, S)
    nb = S // T
    dmat, qdec, kdec, cdec = _ret_tables(T)

    def body(q_ref, k_ref, v_ref, g_ref, c_ref, s_ref, dm_ref, qd_ref, kd_ref, cd_ref, gn_ref, yo_ref, yp_ref, st_ref, state):
        @pl.when(pl.program_id(1) == 0)
        def _():
            state[...] = jnp.zeros_like(state)

        c, s = c_ref[...], s_ref[...]
        qr = _rope(q_ref[...], c, s)
        kr = _rope(k_ref[...], c, s) * (HEAD_DIM ** -0.5)
        v = _bf(v_ref[...])
        sp = state[...]
        st_ref[...] = sp
        a = _dot(_bf(qr), _bf(kr), NT) * dm_ref[...]
        y = _dot(_bf(a), v, NN) + _dot(_bf(qr * qd_ref[...]), _bf(sp), NN)
        state[...] = cd_ref[...] * sp + _dot(_bf(kr * kd_ref[...]), v, TN)
        yp_ref[...] = y
        yn = y * lax.rsqrt(jnp.mean(y * y, axis=-1, keepdims=True) + NORM_EPS) * gn_ref[...]
        g = g_ref[...]
        yo_ref[...] = _bf(yn * (g * _sigmoid(g)))

    col = lambda cb: pl.BlockSpec((T, LANE), lambda h, b: (b, cb + h))
    tok = pl.BlockSpec((T, LANE), lambda h, b: (b, 0))
    per_head = lambda r: pl.BlockSpec((None, r, LANE), lambda h, b: (h, 0, 0))
    out_tok = pl.BlockSpec((T, LANE), lambda h, b: (b, h))
    return pl.pallas_call(
        body, grid=(RET_HEADS, nb),
        in_specs=[col(CB_RQ), col(CB_RK), col(CB_RV), col(CB_RG), tok, tok,
                  pl.BlockSpec((None, T, T), lambda h, b: (h, 0, 0)), per_head(T), per_head(T), per_head(LANE),
                  pl.BlockSpec((1, LANE), lambda h, b: (0, h))],
        out_specs=[out_tok, out_tok, pl.BlockSpec((None, None, LANE, LANE), lambda h, b: (h, b, 0, 0))],
        out_shape=[_sds((S, 512), BF16), _sds((S, 512), F32), _sds((RET_HEADS, nb, LANE, LANE), F32)],
        scratch_shapes=[pltpu.VMEM((LANE, LANE), F32)], name=name,
        compiler_params=_cparams("parallel", "arbitrary"))(proj, proj, proj, proj, cos, sin, dmat, qdec, kdec, cdec, gn)


def ret_bwd(proj, gn, cos, sin, ypre, states, dycat, *, name):
    S = proj.shape[0]
    T = min(SEQ_BLOCK, S)
    nb = S // T
    dmat, qdec, kdec, cdec = _ret_tables(T)

    def body(q_ref, k_ref, v_ref, g_ref, c_ref, s_ref, dm_ref, qd_ref, kd_ref, cd_ref, gn_ref, yp_ref, st_ref, dy_ref,
             dq_ref, dk_ref, dv_ref, dg_ref, stat_ref, gstate):
        @pl.when(pl.program_id(1) == 0)
        def _():
            gstate[...] = jnp.zeros_like(gstate)
            stat_ref[...] = jnp.zeros_like(stat_ref)

        c, s = c_ref[...], s_ref[...]
        scale = HEAD_DIM ** -0.5
        qr = _rope(q_ref[...], c, s)
        kr = _rope(k_ref[...], c, s) * scale
        v = _bf(v_ref[...])
        qd, kd, dm = qd_ref[...], kd_ref[...], dm_ref[...]
        sp = _bf(st_ref[...])
        gs = gstate[...]
        gsb = _bf(gs)
        g = g_ref[...]
        sg = _sigmoid(g)
        y = yp_ref[...]
        gn_row = gn_ref[...]
        r = lax.rsqrt(jnp.mean(y * y, axis=-1, keepdims=True) + NORM_EPS)
        yhat = y * r
        dyo = dy_ref[...]
        dg_ref[...] = _bf(dyo * (yhat * gn_row) * (sg * (1.0 + g * (1.0 - sg))))
        dyn = dyo * (g * sg)
        stat_ref[0:1, :] += jnp.sum(dyn * yhat, axis=0, keepdims=True)
        dyhat = dyn * gn_row
        dy = _bf(r * (dyhat - yhat * jnp.mean(dyhat * yhat, axis=-1, keepdims=True)))
        qrb, krb = _bf(qr), _bf(kr)
        qdb = _bf(qr * qd)
        kdb = _bf(kr * kd)
        a = _bf(_dot(qrb, krb, NT) * dm)
        da = _bf(_dot(dy, v, NT) * dm)
        dv_ref[...] = _bf(_dot(a, dy, TN) + _dot(kdb, gsb, NN))
        dqr = _dot(da, krb, NN) + qd * _dot(dy, sp, NT)
        dkr = _dot(da, qrb, TN) + kd * _dot(v, gsb, NT)
        gstate[...] = cd_ref[...] * gs + _dot(qdb, dy, TN)
        dq_ref[...] = _bf(_rope_t(dqr, c, s))
        dk_ref[...] = _bf(_rope_t(dkr * scale, c, s))

    rb = lambda b: nb - 1 - b
    col = lambda cb: pl.BlockSpec((T, LANE), lambda h, b: (rb(b), cb + h))
    tok = pl.BlockSpec((T, LANE), lambda h, b: (rb(b), 0))
    per_head = lambda r: pl.BlockSpec((None, r, LANE), lambda h, b: (h, 0, 0))
    out_tok = pl.BlockSpec((T, LANE), lambda h, b: (rb(b), h))
    return pl.pallas_call(
        body, grid=(RET_HEADS, nb),
        in_specs=[col(CB_RQ), col(CB_RK), col(CB_RV), col(CB_RG), tok, tok,
                  pl.BlockSpec((None, T, T), lambda h, b: (h, 0, 0)), per_head(T), per_head(T), per_head(LANE),
                  pl.BlockSpec((1, LANE), lambda h, b: (0, h)), out_tok,
                  pl.BlockSpec((None, None, LANE, LANE), lambda h, b: (h, rb(b), 0, 0)), out_tok],
        out_specs=[out_tok, out_tok, out_tok, out_tok, pl.BlockSpec((8, LANE), lambda h, b: (0, h))],
        out_shape=[_sds((S, 512), BF16)] * 4 + [_sds((8, 512), F32)],
        scratch_shapes=[pltpu.VMEM((LANE, LANE), F32)], name=name,
        compiler_params=_cparams("parallel", "arbitrary"))(proj, proj, proj, proj, cos, sin, dmat, qdec, kdec, cdec, gn, ypre, states, dycat)


def _sb_cast_kv(k_ref, v_ref, kb, vb, S):
    step = min(TOKEN_TILE, S)
    for r in range(0, S, step):
        kb[r:r + step, :] = _bf(k_ref[r:r + step, :])
        vb[r:r + step, :] = _bf(v_ref[r:r + step, :])


def _sb_logits(q, kblk, vis):
    z = _dot(q, kblk, NT) * (HEAD_DIM ** -0.5)
    l = jnp.log1p(jnp.exp(-jnp.abs(z)))
    lb = jnp.minimum(z, 0.0) - l
    lk = jnp.minimum(-z, 0.0) - l
    if vis is not None:
        lk = jnp.where(vis, lk, 0.0)
    return lb, lk


def _tri(T, cmp):
    r = lax.broadcasted_iota(jnp.int32, (T, T), 0)
    c = lax.broadcasted_iota(jnp.int32, (T, T), 1)
    return cmp(r, c)


def _dot_split2(x, m):
    hi, lo = _split2(x)
    return _dot(hi, m, NN) + _dot(lo, m, NN)


def sb_fwd(proj, *, name):
    S = proj.shape[0]
    T = min(SEQ_BLOCK, S)
    nq = S // T

    assert nq <= LANE

    def body(q_ref, k_ref, v_ref, o_ref, cin_ref, kb, vb):
        qi = pl.program_id(1)

        @pl.when(qi == 0)
        def _():
            _sb_cast_kv(k_ref, v_ref, kb, vb, S)

        q = _bf(q_ref[...])
        vis = _tri(T, lambda t, s: s < t)
        after = _tri(T, lambda j, s: j > s).astype(BF16)
        lane = lax.broadcasted_iota(jnp.int32, (T, LANE), 1)

        def block(jb, carry, acc, cin, mask):
            rows = pl.ds(pl.multiple_of(jb * T, T), T)
            lb, lk = _sb_logits(q, kb[rows, :], mask)
            tail = _dot_split2(lk, after) + carry
            w = jnp.exp(lb + tail)
            if mask is not None:
                w = jnp.where(mask, w, 0.0)
            return (carry + jnp.sum(lk, axis=1, keepdims=True), acc + _dot(_bf(w), vb[rows, :], NN),
                    jnp.where(lane == jb, carry, cin))

        st = block(qi, jnp.zeros((T, 1), F32), jnp.zeros((T, LANE), F32), jnp.full((T, LANE), SB_UNVISITED, F32), vis)

        def more(c):
            return (c[0] < qi) & (jnp.max(c[1]) > -SB_SKIP)

        def step(c):
            return (c[0] + 1,) + block(qi - 1 - c[0], c[1], c[2], c[3], None)

        st = lax.while_loop(more, step, (jnp.int32(0),) + st)
        o_ref[...] = st[2]
        cin_ref[...] = st[3]

    whole = lambda cb: pl.BlockSpec((S, LANE), lambda h, i: (0, cb + h))
    tok = pl.BlockSpec((T, LANE), lambda h, i: (i, h))
    return pl.pallas_call(
        body, grid=(RET_HEADS, nq),
        in_specs=[pl.BlockSpec((T, LANE), lambda h, i: (i, CB_SQ + h)), whole(CB_SK), whole(CB_SV)],
        out_specs=[tok, tok], out_shape=[_sds((S, 512), F32), _sds((S, 512), F32)],
        scratch_shapes=[pltpu.VMEM((S, LANE), BF16), pltpu.VMEM((S, LANE), BF16)], name=name,
        compiler_params=_cparams("parallel", "arbitrary"))(proj, proj, proj)


def sb_bwd(proj, cin, dycat, *, name):
    S = proj.shape[0]
    T = min(SEQ_BLOCK, S)
    nq = S // T
    scale = HEAD_DIM ** -0.5

    def body(q_ref, k_ref, v_ref, cin_ref, do_ref, dq_ref, dk_ref, dv_ref, kb, vb):
        qi = pl.program_id(1)

        @pl.when(qi == 0)
        def _():
            _sb_cast_kv(k_ref, v_ref, kb, vb, S)
            dk_ref[...] = jnp.zeros_like(dk_ref)
            dv_ref[...] = jnp.zeros_like(dv_ref)

        q = _bf(q_ref[...])
        dob = _bf(do_ref[...])
        cin = cin_ref[...]
        vis = _tri(T, lambda t, s: s < t)
        after = _tri(T, lambda j, s: j > s).astype(BF16)
        before = _tri(T, lambda s, j: s < j).astype(BF16)
        lane = lax.broadcasted_iota(jnp.int32, (T, LANE), 1)

        def block(jb, ecarry, dq, mask):
            rows = pl.ds(pl.multiple_of(jb * T, T), T)
            kblk, vblk = kb[rows, :], vb[rows, :]
            lb, lk = _sb_logits(q, kblk, mask)
            carry = jnp.sum(jnp.where(lane == jb, cin, 0.0), axis=1, keepdims=True)
            w = jnp.exp(lb + _dot_split2(lk, after) + carry)
            if mask is not None:
                w = jnp.where(mask, w, 0.0)
            e = w * _dot(dob, vblk, NT)
            dv_ref[rows, :] += _dot(_bf(w), dob, TN)
            dlk = _dot_split2(e, before) + ecarry
            beta = jnp.exp(lb)
            dz = e * (1.0 - beta) - beta * dlk
            if mask is not None:
                dz = jnp.where(mask, dz, 0.0)
            dzb = _bf(dz * scale)
            dk_ref[rows, :] += _dot(dzb, q, TN)
            return ecarry + jnp.sum(e, axis=1, keepdims=True), dq + _dot(dzb, kblk, NN)

        lane1 = lane[0:1, :]
        skipped = (jnp.max(cin, axis=0, keepdims=True) <= -SB_SKIP) & (lane1 < qi)
        first = jnp.sum(jnp.where(skipped, 1, 0))
        st = lax.fori_loop(first, qi, lambda jb, c: block(jb, c[0], c[1], None), (jnp.zeros((T, 1), F32), jnp.zeros((T, LANE), F32)))
        st = block(qi, st[0], st[1], vis)
        dq_ref[...] = _bf(st[1])

    whole = lambda cb: pl.BlockSpec((S, LANE), lambda h, i: (0, cb + h))
    tok = pl.BlockSpec((T, LANE), lambda h, i: (i, h))
    acc = pl.BlockSpec((S, LANE), lambda h, i: (0, h))
    return pl.pallas_call(
        body, grid=(RET_HEADS, nq),
        in_specs=[pl.BlockSpec((T, LANE), lambda h, i: (i, CB_SQ + h)), whole(CB_SK), whole(CB_SV), tok,
                  pl.BlockSpec((T, LANE), lambda h, i: (i, 4 + h))],
        out_specs=[tok, acc, acc], out_shape=[_sds((S, 512), BF16), _sds((S, 512), F32), _sds((S, 512), F32)],
        scratch_shapes=[pltpu.VMEM((S, LANE), BF16), pltpu.VMEM((S, LANE), BF16)], name=name,
        compiler_params=_cparams("parallel", "arbitrary"))(proj, proj, proj, cin, dycat)


def _shift_down(x, d, row):
    return jnp.where(row >= d, pltpu.roll(x, d, 0), 0.0)


def _shift_up(x, d, row, S):
    return jnp.where(row < S - d, pltpu.roll(x, S - d, 0), 0.0)


def conv_fwd(proj, conv_w, conv_b, *, name):
    S = proj.shape[0]

    def body(x_ref, w_ref, b_ref, pre_ref, act_ref):
        x = x_ref[...]
        row = lax.broadcasted_iota(jnp.int32, x.shape, 0)
        pre = b_ref[...] + w_ref[3:4, :] * x
        for d in range(1, SSM_CONV):
            pre = pre + w_ref[3 - d:4 - d, :] * _shift_down(x, d, row)
        pre_ref[...] = pre
        act_ref[...] = pre * _sigmoid(pre)

    blk = pl.BlockSpec((S, LANE), lambda c: (0, c))
    return pl.pallas_call(
        body, grid=(8,),
        in_specs=[pl.BlockSpec((S, LANE), lambda c: (0, CB_XS + c)), pl.BlockSpec((SSM_CONV, LANE), lambda c: (0, c)),
                  pl.BlockSpec((1, LANE), lambda c: (0, c))],
        out_specs=[blk, blk], out_shape=[_sds((S, 1024), F32), _sds((S, 1024), F32)], name=name,
        compiler_params=_cparams("parallel"))(proj, conv_w, conv_b)


def conv_bwd(proj, pre, dact, conv_w, *, name):
    S = proj.shape[0]

    def body(x_ref, pre_ref, da_ref, w_ref, dx_ref, st_ref):
        x = x_ref[...]
        p = pre_ref[...]
        row = lax.broadcasted_iota(jnp.int32, x.shape, 0)
        sg = _sigmoid(p)
        dpre = da_ref[...] * (sg * (1.0 + p * (1.0 - sg)))
        dx = w_ref[3:4, :] * dpre
        st_ref[3:4, :] = jnp.sum(dpre * x, axis=0, keepdims=True)
        for d in range(1, SSM_CONV):
            dx = dx + w_ref[3 - d:4 - d, :] * _shift_up(dpre, d, row, S)
            st_ref[3 - d:4 - d, :] = jnp.sum(dpre * _shift_down(x, d, row), axis=0, keepdims=True)
        st_ref[4:5, :] = jnp.sum(dpre, axis=0, keepdims=True)
        st_ref[5:8, :] = jnp.zeros((3, LANE), F32)
        dx_ref[...] = _bf(dx)

    blk = pl.BlockSpec((S, LANE), lambda c: (0, c))
    return pl.pallas_call(
        body, grid=(8,),
        in_specs=[pl.BlockSpec((S, LANE), lambda c: (0, CB_XS + c)), blk, blk, pl.BlockSpec((SSM_CONV, LANE), lambda c: (0, c))],
        out_specs=[blk, pl.BlockSpec((8, LANE), lambda c: (0, c))],
        out_shape=[_sds((S, 1024), BF16), _sds((8, 1024), F32)], name=name,
        compiler_params=_cparams("parallel"))(proj, pre, dact, conv_w)


def _softplus(x):
    return jnp.maximum(x, 0.0) + jnp.log1p(jnp.exp(-jnp.abs(x)))


def _pair(lane, v0, v1):
    return jnp.where(lane < SSM_HEAD_DIM, v0, v1)


def _ssd_pair_common(raw, dtb, alog, xs, cm, hprev, T):
    lane = lax.broadcasted_iota(jnp.int32, (T, LANE), 1)
    dt = _softplus(raw + dtb)
    a = -jnp.exp(alog)
    incl = _tri(T, lambda l, s: s <= l).astype(BF16)
    h1, h2, h3 = _split3(dt * a)
    acum = _dot(incl, h1, NN) + _dot(incl, h2, NN) + _dot(incl, h3, NN)
    acum_t = acum.T
    causal = _tri(T, lambda l, s: s <= l)
    decay = [jnp.where(causal, jnp.exp(jnp.minimum(acum[:, j:j + 1] - acum_t[j:j + 1, :], 0.0)), 0.0) for j in (0, 1)]
    dtc = _pair(lane, dt[:, 0:1], dt[:, 1:2])
    ac = _pair(lane, acum[:, 0:1], acum[:, 1:2])
    xdt = xs * dtc
    ea = jnp.exp(ac)
    e_end = jnp.exp(ac[T - 1:T, :] - ac)
    sub = lax.broadcasted_iota(jnp.int32, (LANE, LANE), 0)
    cd = jnp.where(sub < SSM_HEAD_DIM, jnp.exp(acum[T - 1:T, 0:1]), jnp.exp(acum[T - 1:T, 1:2]))
    r = _dot(cm, _bf(hprev), NT)
    return lane, dt, a, acum, decay, dtc, xdt, ea, e_end, cd, r


def ssd_fwd(xact, proj, dtb, alog, dskip, *, name):
    S = xact.shape[0]
    T = min(SEQ_BLOCK, S)
    nb = S // T

    def body(xs_ref, bm_ref, cm_ref, dt0_ref, dt1_ref, dtb_ref, al_ref, ds_ref, y_ref, st_ref, state):
        @pl.when(pl.program_id(1) == 0)
        def _():
            state[...] = jnp.zeros_like(state)

        bm, cm = _bf(bm_ref[...]), _bf(cm_ref[...])
        gm = _dot(cm, bm, NT)
        for i, dt_ref in enumerate((dt0_ref, dt1_ref)):
            xs = xs_ref[:, LANE * i:LANE * (i + 1)]
            hprev = state[i]
            st_ref[i] = hprev
            lane, dt, a, acum, decay, dtc, xdt, ea, e_end, cd, r = _ssd_pair_common(
                dt_ref[...], dtb_ref[i], al_ref[i], xs, cm, hprev, T)
            xdtb = _bf(xdt)
            y_intra = _pair(lane, _dot(_bf(gm * decay[0]), xdtb, NN), _dot(_bf(gm * decay[1]), xdtb, NN))
            state[i] = cd * hprev + _dot(_bf(xdt * e_end), bm, TN)
            dsk = ds_ref[i]
            lane1 = lane[0:1, :]
            y_ref[:, LANE * i:LANE * (i + 1)] = y_intra + ea * r + _pair(lane1, dsk[:, 0:1], dsk[:, 1:2]) * xs

    rows3 = pl.BlockSpec((2, 1, LANE), lambda g, b: (g, 0, 0))
    return pl.pallas_call(
        body, grid=(2, nb),
        in_specs=[pl.BlockSpec((T, 256), lambda g, b: (b, g)), pl.BlockSpec((T, LANE), lambda g, b: (b, 4 + g)),
                  pl.BlockSpec((T, LANE), lambda g, b: (b, 6 + g)),
                  pl.BlockSpec((T, LANE), lambda g, b: (b, CB_DT + 2 * g)), pl.BlockSpec((T, LANE), lambda g, b: (b, CB_DT + 2 * g + 1)),
                  rows3, rows3, rows3],
        out_specs=[pl.BlockSpec((T, 256), lambda g, b: (b, g)),
                   pl.BlockSpec((None, None, 2, LANE, LANE), lambda g, b: (g, b, 0, 0, 0))],
        out_shape=[_sds((S, 512), F32), _sds((2, nb, 2, LANE, LANE), F32)],
        scratch_shapes=[pltpu.VMEM((2, LANE, LANE), F32)], name=name,
        compiler_params=_cparams("parallel", "arbitrary"))(xact, xact, xact, proj, proj, dtb, alog, dskip)


def ssd_bwd(xact, proj, dtb, alog, dskip, states, dy, *, name):
    S = xact.shape[0]
    T = min(SEQ_BLOCK, S)
    nb = S // T

    def body(xs_ref, bm_ref, cm_ref, dt0_ref, dt1_ref, dtb_ref, al_ref, ds_ref, st_ref, dy_ref,
             dxs_ref, dbm_ref, dcm_ref, ddt_ref, stat_ref, dstate):
        @pl.when(pl.program_id(1) == 0)
        def _():
            dstate[...] = jnp.zeros_like(dstate)
            stat_ref[...] = jnp.zeros_like(stat_ref)

        bm, cm = _bf(bm_ref[...]), _bf(cm_ref[...])
        gm = _dot(cm, bm, NT)
        dbm = jnp.zeros((T, LANE), F32)
        dcm = jnp.zeros((T, LANE), F32)
        after_eq = _tri(T, lambda i, l: l >= i).astype(BF16)
        rowi = lax.broadcasted_iota(jnp.int32, (T, 1), 0)
        for i, dt_ref in enumerate((dt0_ref, dt1_ref)):
            xs = xs_ref[:, LANE * i:LANE * (i + 1)]
            dyp = dy_ref[:, LANE * i:LANE * (i + 1)]
            hprev = st_ref[i]
            dh = dstate[i]
            raw = dt_ref[...]
            lane, dt, a, acum, decay, dtc, xdt, ea, e_end, cd, r = _ssd_pair_common(
                raw, dtb_ref[i], al_ref[i], xs, cm, hprev, T)
            lane1 = lane[0:1, :]
            dsk = ds_ref[i]
            dskp = _pair(lane1, dsk[:, 0:1], dsk[:, 1:2])
            head = [lane < SSM_HEAD_DIM, lane >= SSM_HEAD_DIM]
            hsum = lambda v, j: jnp.sum(jnp.where(head[j], v, 0.0), axis=1, keepdims=True)
            dhb = _bf(dh)
            xdtb = _bf(xdt)
            dyb = _bf(dyp)
            z = xdt * e_end
            dz = _dot(bm, dhb, NT)
            dbm = dbm + _dot(_bf(z), dhb, NN)
            dxdt = dz * e_end
            de_e = dz * z
            drr = dyp * ea
            dea_ea = drr * r
            dcm = dcm + _dot(_bf(drr), _bf(hprev), NN)
            dstate[i] = cd * dh + _dot(_bf(drr), cm, TN)
            dcd_cd = cd * dh * hprev
            dgs = jnp.zeros((T, T), F32)
            da_cols = []
            for j in (0, 1):
                w = gm * decay[j]
                dw = _dot(_bf(jnp.where(head[j], dyp, 0.0)), xdtb, NT)
                dxdt = dxdt + jnp.where(head[j], _dot(_bf(w), dyb, TN), 0.0)
                dgs = dgs + dw * decay[j]
                dseg = dw * w
                col = jnp.sum(dseg, axis=1, keepdims=True) - jnp.sum(dseg.T, axis=1, keepdims=True)
                col = col + hsum(dea_ea, j) - hsum(de_e, j)
                sub = lax.broadcasted_iota(jnp.int32, (LANE, LANE), 0)
                in_head = (sub < SSM_HEAD_DIM) if j == 0 else (sub >= SSM_HEAD_DIM)
                end = jnp.sum(hsum(de_e, j), axis=0, keepdims=True) + jnp.sum(
                    jnp.sum(jnp.where(in_head, dcd_cd, 0.0), axis=1, keepdims=True), axis=0, keepdims=True)
                da_cols.append(col + jnp.where(rowi == T - 1, end, 0.0))
            dgb = _bf(dgs)
            dcm = dcm + _dot(dgb, bm, NN)
            dbm = dbm + _dot(dgb, cm, TN)
            dacum = jnp.where(lane == 0, da_cols[0], jnp.where(lane == 1, da_cols[1], 0.0))
            h1, h2, h3 = _split3(dacum)
            ddta = _dot(after_eq, h1, NN) + _dot(after_eq, h2, NN) + _dot(after_eq, h3, NN)
            dxs_ref[:, LANE * i:LANE * (i + 1)] = dskp * dyp + dxdt * dtc
            dx_x = dxdt * xs
            ddt = ddta * a + jnp.where(lane == 0, hsum(dx_x, 0), jnp.where(lane == 1, hsum(dx_x, 1), 0.0))
            ddraw = jnp.where(lane < 2, ddt * _sigmoid(raw + dtb_ref[i]), 0.0)
            ddt_ref[:, LANE * i:LANE * (i + 1)] = _bf(ddraw)
            dsum = jnp.sum(dyp * xs, axis=0, keepdims=True)
            d0 = jnp.sum(jnp.where(lane1 < SSM_HEAD_DIM, dsum, 0.0), axis=1, keepdims=True)
            d1 = jnp.sum(jnp.where(lane1 >= SSM_HEAD_DIM, dsum, 0.0), axis=1, keepdims=True)
            dd = jnp.where(lane1 == 0, d0, jnp.where(lane1 == 1, d1, 0.0))
            stat_ref[i, 0:1, :] += jnp.sum(ddraw, axis=0, keepdims=True)
            stat_ref[i, 1:2, :] += jnp.where(lane1 < 2, jnp.sum(ddta * dt, axis=0, keepdims=True) * a, 0.0)
            stat_ref[i, 2:3, :] += dd
        dbm_ref[...] = dbm
        dcm_ref[...] = dcm

    rb = lambda b: nb - 1 - b
    rows3 = pl.BlockSpec((2, 1, LANE), lambda g, b: (g, 0, 0))
    tok256 = pl.BlockSpec((T, 256), lambda g, b: (rb(b), g))
    tok128 = pl.BlockSpec((T, LANE), lambda g, b: (rb(b), g))
    return pl.pallas_call(
        body, grid=(2, nb),
        in_specs=[tok256, pl.BlockSpec((T, LANE), lambda g, b: (rb(b), 4 + g)), pl.BlockSpec((T, LANE), lambda g, b: (rb(b), 6 + g)),
                  pl.BlockSpec((T, LANE), lambda g, b: (rb(b), CB_DT + 2 * g)),
                  pl.BlockSpec((T, LANE), lambda g, b: (rb(b), CB_DT + 2 * g + 1)),
                  rows3, rows3, rows3,
                  pl.BlockSpec((None, None, 2, LANE, LANE), lambda g, b: (g, rb(b), 0, 0, 0)), tok256],
        out_specs=[tok256, tok128, tok128, tok256, pl.BlockSpec((2, 8, LANE), lambda g, b: (g, 0, 0))],
        out_shape=[_sds((S, 512), F32), _sds((S, 256), F32), _sds((S, 256), F32), _sds((S, 512), BF16), _sds((4, 8, LANE), F32)],
        scratch_shapes=[pltpu.VMEM((2, LANE, LANE), F32)], name=name,
        compiler_params=_cparams("parallel", "arbitrary"))(xact, xact, xact, proj, proj, dtb, alog, dskip, states, dy)


def gated_norm(ypre, proj, gain, *, name):
    S, W = ypre.shape
    tm = min(TOKEN_TILE, S)

    def body(y_ref, z_ref, g_ref, o_ref):
        z = z_ref[...]
        yg = y_ref[...] * (z * _sigmoid(z))
        o_ref[...] = _bf(yg * lax.rsqrt(jnp.mean(yg * yg, axis=-1, keepdims=True) + NORM_EPS) * g_ref[...])

    tile = pl.BlockSpec((tm, W), lambda i: (i, 0))
    return pl.pallas_call(
        body, grid=(S // tm,), in_specs=[tile, pl.BlockSpec((tm, W), lambda i: (i, CB_MZ // 4)), pl.BlockSpec((1, W), lambda i: (0, 0))],
        out_specs=tile, out_shape=_sds((S, W), BF16), name=name, compiler_params=_cparams("parallel"))(ypre, proj, gain)


def gated_norm_bwd(ypre, proj, gain, dycat, *, name):
    S, W = ypre.shape
    tm = min(TOKEN_TILE, S)

    def body(y_ref, z_ref, g_ref, dy_ref, dyp_ref, dz_ref, st_ref):
        z = z_ref[...]
        y = y_ref[...]
        sg = _sigmoid(z)
        sz = z * sg
        yg = y * sz
        r = lax.rsqrt(jnp.mean(yg * yg, axis=-1, keepdims=True) + NORM_EPS)
        yhat = yg * r
        dyo = dy_ref[...]

        @pl.when(pl.program_id(0) == 0)
        def _():
            st_ref[...] = jnp.zeros_like(st_ref)

        st_ref[0:1, :] += jnp.sum(dyo * yhat, axis=0, keepdims=True)
        dyhat = dyo * g_ref[...]
        dyg = r * (dyhat - yhat * jnp.mean(dyhat * yhat, axis=-1, keepdims=True))
        dyp_ref[...] = dyg * sz
        dz_ref[...] = _bf(dyg * y * (sg * (1.0 + z * (1.0 - sg))))

    tile = pl.BlockSpec((tm, W), lambda i: (i, 0))
    return pl.pallas_call(
        body, grid=(S // tm,),
        in_specs=[tile, pl.BlockSpec((tm, W), lambda i: (i, CB_MZ // 4)), pl.BlockSpec((1, W), lambda i: (0, 0)),
                  pl.BlockSpec((tm, W), lambda i: (i, 2))],
        out_specs=[tile, tile, pl.BlockSpec((8, W), lambda i: (0, 0))],
        out_shape=[_sds((S, W), F32), _sds((S, W), BF16), _sds((8, W), F32)], name=name,
        compiler_params=_cparams("arbitrary"))(ypre, proj, gain, dycat)


def ada_mod(c_all, w, bias, *, name):
    M, K = c_all.shape
    N = w.shape[1]
    tn = _tile(N, 512)

    def body(c_ref, w_ref, b_ref, o_ref, cond_ref):
        cv = c_ref[...]
        cond = cv * _sigmoid(cv)
        cond_ref[...] = cond
        o_ref[...] = _dot(_bf(cond), _bf(w_ref[...]), NN) + b_ref[...]

    return pl.pallas_call(
        body, grid=(N // tn,),
        in_specs=[pl.BlockSpec((M, K), lambda j: (0, 0)), pl.BlockSpec((K, tn), lambda j: (0, j)), pl.BlockSpec((1, tn), lambda j: (0, j))],
        out_specs=[pl.BlockSpec((M, tn), lambda j: (0, j)), pl.BlockSpec((M, K), lambda j: (0, 0))],
        out_shape=[_sds((M, N), F32), _sds((M, K), F32)], name=name, compiler_params=_cparams("arbitrary"))(c_all, w, bias)


def _adamw(g, w, m, v):
    m = ADAM_B1 * m + (1.0 - ADAM_B1) * g
    v = ADAM_B2 * v + (1.0 - ADAM_B2) * (g * g)
    m_hat = m / (1.0 - ADAM_B1 ** ADAM_STEP)
    v_hat = v / (1.0 - ADAM_B2 ** ADAM_STEP)
    return -ADAM_LR * (m_hat / (jnp.sqrt(v_hat) + ADAM_EPS) + ADAM_WD * w), m, v


def adamw_parts(parts, w, m, v, *, name):
    P, R, C = parts.shape
    tr = _tile(R, 512, 256, 128, 64, 32, 16)

    def body(p_ref, w_ref, m_ref, v_ref, g_ref, d_ref, mo_ref, vo_ref):
        g = p_ref[0].astype(F32)
        for j in range(1, P):
            g = g + p_ref[j].astype(F32)
        g_ref[...] = g
        d_ref[...], mo_ref[...], vo_ref[...] = _adamw(g, w_ref[...], m_ref[...], v_ref[...])

    tile = pl.BlockSpec((tr, C), lambda i: (i, 0))
    return pl.pallas_call(
        body, grid=(R // tr,), in_specs=[pl.BlockSpec((P, tr, C), lambda i: (0, i, 0)), tile, tile, tile],
        out_specs=[tile] * 4, out_shape=[_sds((R, C), F32)] * 4, name=name, compiler_params=_cparams("parallel"))(parts, w, m, v)


def ada_adamw(cond_t, dmod, w, m, v, *, name):
    D, N = w.shape
    tr = _tile(D, 256)

    def body(c_ref, d_ref, w_ref, m_ref, v_ref, g_ref, dl_ref, mo_ref, vo_ref):
        cc = c_ref[...]
        dd = d_ref[...]
        g = cc[:, 0:1] * dd[0:1, :]
        for b in range(1, N_DEV):
            g = g + cc[:, b:b + 1] * dd[b:b + 1, :]
        g_ref[...] = g
        dl_ref[...], mo_ref[...], vo_ref[...] = _adamw(g, w_ref[...], m_ref[...], v_ref[...])

    tile = pl.BlockSpec((tr, N), lambda i: (i, 0))
    return pl.pallas_call(
        body, grid=(D // tr,), in_specs=[pl.BlockSpec((tr, N_DEV), lambda i: (i, 0)), pl.BlockSpec((N_DEV, N), lambda i: (0, 0)), tile, tile, tile],
        out_specs=[tile] * 4, out_shape=[_sds((D, N), F32)] * 4, name=name, compiler_params=_cparams("parallel"))(cond_t, dmod, w, m, v)


def _my_place():
    mx, my, mc = lax.axis_index("x"), lax.axis_index("y"), lax.axis_index("c")
    return mx, my, mc, 4 * mx + 2 * my + mc


def _peer(mx, my, mc, k):
    px = 1 - mx if (k >> 2) & 1 else mx
    py = 1 - my if (k >> 1) & 1 else my
    pc = 1 - mc if k & 1 else mc
    return (px, py, pc), 4 * px + 2 * py + pc


def _comm_call(body, x, out_shape, space, name):
    spec = pl.BlockSpec(memory_space=space)
    return pl.pallas_call(
        body, in_specs=[spec], out_specs=spec, out_shape=out_shape,
        scratch_shapes=[pltpu.SemaphoreType.DMA((N_DEV - 1,)), pltpu.SemaphoreType.DMA((N_DEV - 1,)), pltpu.SemaphoreType.DMA(())],
        name=name, compiler_params=pltpu.CompilerParams(has_side_effects=True, vmem_limit_bytes=VMEM_LIMIT))(x)


def allgather(x, *, in_vmem, name):
    def body(x_ref, out_ref, send_sems, recv_sems, local_sem):
        mx, my, mc, me = _my_place()
        mine = pltpu.make_async_copy(x_ref, out_ref.at[me], local_sem)
        mine.start()
        copies = []
        for k in range(1, N_DEV):
            peer, _ = _peer(mx, my, mc, k)
            cp = pltpu.make_async_remote_copy(src_ref=x_ref, dst_ref=out_ref.at[me], send_sem=send_sems.at[k - 1],
                                              recv_sem=recv_sems.at[k - 1], device_id=peer, device_id_type=pl.DeviceIdType.MESH)
            cp.start()
            copies.append(cp)
        for cp in copies:
            cp.wait()
        mine.wait()

    return _comm_call(body, x, _sds((N_DEV,) + x.shape, x.dtype), pltpu.VMEM if in_vmem else pltpu.HBM, name)


def allgather_two_level(x, *, name):
    def body(x_ref, out_ref, send_sems, recv_sems, local_sem):
        mx, my, mc, _ = _my_place()
        me, sibling = (mx, my, mc), (mx, my, 1 - mc)
        chips = [(1 - mx, my), (mx, 1 - my), (1 - mx, 1 - my)]

        def copy(k, block, to, src=None):
            slot = out_ref.at[4 * block[0] + 2 * block[1] + block[2]]
            return pltpu.make_async_remote_copy(src_ref=slot if src is None else src, dst_ref=slot, send_sem=send_sems.at[k],
                                                recv_sem=recv_sems.at[k], device_id=to, device_id_type=pl.DeviceIdType.MESH)

        mine = pltpu.make_async_copy(x_ref, out_ref.at[4 * mx + 2 * my + mc], local_sem)
        mine.start()
        first = [copy(0, me, sibling, src=x_ref)] + [copy(1 + j, me, (*chip, mc), src=x_ref) for j, chip in enumerate(chips)]
        for cp in first:
            cp.start()
        passed = [copy(4 + j, (*chip, mc), sibling) for j, chip in enumerate(chips)]
        for j, chip in enumerate(chips):
            copy(1 + j, (*chip, mc), me).wait_recv()
            passed[j].start()
        copy(0, sibling, me).wait_recv()
        for j, chip in enumerate(chips):
            copy(4 + j, (*chip, 1 - mc), me).wait_recv()
        for cp in first + passed:
            cp.wait_send()
        mine.wait()

    return _comm_call(body, x, _sds((N_DEV,) + x.shape, x.dtype), pltpu.HBM, name)


def sibling_exchange(send, *, name):
    n_chip, _, R, C = send.shape

    def body(s_ref, r_ref, send_sems, recv_sems, local_sem):
        mx, my, mc, _ = _my_place()
        copies = []
        for q in range(n_chip):
            cp = pltpu.make_async_remote_copy(src_ref=s_ref.at[q, 1 - mc], dst_ref=r_ref.at[q], send_sem=send_sems.at[q],
                                              recv_sem=recv_sems.at[q], device_id=(mx, my, 1 - mc), device_id_type=pl.DeviceIdType.MESH)
            cp.start()
            copies.append(cp)
        for cp in copies:
            cp.wait()

    return _comm_call(body, send, _sds((n_chip, R, C), send.dtype), pltpu.HBM, name)


def chip_exchange(send, *, name):
    def body(s_ref, r_ref, send_sems, recv_sems, local_sem):
        mx, my, mc, _ = _my_place()
        mine = pltpu.make_async_copy(s_ref.at[2 * mx + my], r_ref.at[2 * mx + my], local_sem)
        mine.start()
        copies = []
        for k in range(1, 4):
            px = 1 - mx if (k >> 1) & 1 else mx
            py = 1 - my if k & 1 else my
            cp = pltpu.make_async_remote_copy(src_ref=s_ref.at[2 * px + py], dst_ref=r_ref.at[2 * mx + my], send_sem=send_sems.at[k - 1],
                                              recv_sem=recv_sems.at[k - 1], device_id=(px, py, mc), device_id_type=pl.DeviceIdType.MESH)
            cp.start()
            copies.append(cp)
        for cp in copies:
            cp.wait()
        mine.wait()

    return _comm_call(body, send, _sds(send.shape, send.dtype), pltpu.HBM, name)


def add_partials(a, b, *, name):
    P, R, C = a.shape
    tr = _tile(R, PACK_ROWS)

    def body(a_ref, b_ref, o_ref):
        o_ref[...] = _bf(a_ref[...].astype(F32) + b_ref[...].astype(F32))

    tile = pl.BlockSpec((P, tr, C), lambda i: (0, i, 0))
    return pl.pallas_call(body, grid=(R // tr,), in_specs=[tile, tile], out_specs=tile, out_shape=_sds((P, R, C), BF16), name=name,
                          compiler_params=_cparams("parallel"))(a, b)


def _rows128(a):
    f = a.reshape(-1)
    n = -(-f.shape[0] // (16 * LANE)) * (16 * LANE)
    return jnp.pad(f, (0, n - f.shape[0])).reshape(-1, LANE)


PACK_ROWS = 512


def _pad_rows(buf):
    r = buf.shape[-2]
    pad = -r % PACK_ROWS
    return jnp.pad(buf, [(0, 0)] * (buf.ndim - 2) + [(0, pad), (0, 0)])


def _pack(arrays):
    parts = [_rows128(a) for a in arrays]
    offs = np.cumsum([0] + [p.shape[0] for p in parts])
    return _pad_rows(jnp.concatenate(parts, axis=0)), [int(o) for o in offs]


def _unpack(buf, offs, shapes):
    lead = buf.shape[:-2]
    out = []
    for o, shp in zip(offs, shapes):
        n = int(np.prod(shp))
        rows = -(-n // LANE)
        seg = buf[..., o:o + rows, :].reshape(lead + (rows * LANE,))[..., :n]
        out.append(seg.reshape(lead + tuple(shp)))
    return out


def _col_full(g):
    n, L, R, c = g.shape
    return g.transpose(1, 2, 0, 3).reshape(L, R, n * c)


def _row_full(g):
    n, L, r, C = g.shape
    return g.transpose(1, 0, 2, 3).reshape(L, n * r, C)


def _col_shards(w):
    L, R, C = w.shape
    return w.reshape(L, R, N_DEV, C // N_DEV).transpose(2, 0, 1, 3)


def _row_shards(w):
    L, R, C = w.shape
    return w.reshape(L, N_DEV, R // N_DEV, C).transpose(1, 0, 2, 3)


def _pad_w_in(w):
    L, D, _ = w.shape
    dt = jnp.pad(w[:, :, IN_MAIN:].reshape(L, D, 4, 2), ((0, 0), (0, 0), (0, 0), (0, LANE - 2)))
    return jnp.concatenate([w[:, :, :IN_MAIN], dt.reshape(L, D, 4 * LANE)], axis=-1)


def _unpad_w_in(g):
    L, D, _ = g.shape
    dt = g[:, :, IN_MAIN:].reshape(L, D, 4, LANE)[:, :, :, :2].reshape(L, D, SSM_HEADS)
    return jnp.concatenate([g[:, :, :IN_MAIN], dt], axis=-1)


def _pair_rows(p):
    return jnp.pad(p.reshape(4, 1, 2), ((0, 0), (0, 0), (0, LANE - 2)))


def _row(v):
    return v.reshape(1, -1)


def _ffn_fwd(h, gain, mod3, wg, wu, wd, tag):
    shift, scale, gate = mod3
    u = norm_mod(h, gain, shift, scale, name=tag + "_norm")
    a, b, act = ffn_up(u, wg, wu, name=tag + "_up")
    hn, out = matmul_resid(act, wd, h, gate, 0.5, name=tag + "_down")
    return hn, (h, u, a, b, act, out)


def _ffn_bwd(dh, saved, gain, mod3, wg, wu, wd, tag):
    h, u, a, b, act, out = saved
    _, scale, gate = mod3
    D, Fd = wg.shape
    dout, gst = gate_bwd(dh, out, gate, 0.5, name=tag + "_gate_bwd")
    da, db = ffn_dact(dout, wd, a, b, name=tag + "_dact")
    dh_prev, nst = dgrad_norm_bwd([da, db], [wg, wu], h, gain, scale, dh, name=tag + "_dgrad")
    tf = _tile(Fd, 1408, 512)
    dwg = matmul(u, da, ta=True, tm=D, tn=tf, tk=512, name=tag + "_dwg")
    dwu = matmul(u, db, ta=True, tm=D, tn=tf, tk=512, name=tag + "_dwu")
    dwd = matmul(act, dout, ta=True, tm=tf, tn=D, tk=512, name=tag + "_dwd")
    return dh_prev, (dwg, dwu, dwd), nst[0], [nst[1], nst[2], gst[0]]


def _mix_fwd(h, p, mod3, w_in, w_out, cos, sin, tag):
    shift, scale, gate = mod3
    u = norm_mod(h, p["norm_mix"], shift, scale, name=tag + "_norm")
    proj = matmul(u, w_in, tm=TOKEN_TILE, tn=512, tk=D_MODEL, name=tag + "_proj")
    y_ret, ypre_ret, st_ret = ret_fwd(proj, p["ret_gn"], cos, sin, name=tag + "_ret")
    y_sb, sb_cin = sb_fwd(proj, name=tag + "_sb")
    pre, xact = conv_fwd(proj, p["conv_w"], p["conv_b"], name=tag + "_conv")
    ypre_ssm, st_ssm = ssd_fwd(xact, proj, p["dt_bias"], p["a_log"], p["d_skip"], name=tag + "_ssd")
    y_ssm = gated_norm(ypre_ssm, proj, p["ssm_norm"], name=tag + "_gnorm")
    ycat = jnp.concatenate([y_ret, y_sb.astype(BF16), y_ssm], axis=1)
    hn, mixed = matmul_resid(ycat, w_out, h, gate, 1.0, name=tag + "_out")
    return hn, (h, u, proj, ypre_ret, st_ret, sb_cin, pre, xact, ypre_ssm, st_ssm, ycat, mixed)


def _mix_bwd(dh, saved, p, mod3, w_in, w_out, cos, sin, tag):
    h, u, proj, ypre_ret, st_ret, sb_cin, pre, xact, ypre_ssm, st_ssm, ycat, mixed = saved
    _, scale, gate = mod3
    dmixed, gst = gate_bwd(dh, mixed, gate, 1.0, name=tag + "_gate_bwd")
    dycat = matmul(dmixed, w_out, tb=True, tm=TOKEN_TILE, tn=512, tk=D_MODEL, name=tag + "_dycat")
    dw_out = matmul(ycat, dmixed, ta=True, tm=512, tn=D_MODEL, tk=512, name=tag + "_dw_out")
    dq, dk, dv, dg, rst = ret_bwd(proj, p["ret_gn"], cos, sin, ypre_ret, st_ret, dycat, name=tag + "_ret_bwd")
    dsq, dsk, dsv = sb_bwd(proj, sb_cin, dycat, name=tag + "_sb_bwd")
    dypre, dz, nst2 = gated_norm_bwd(ypre_ssm, proj, p["ssm_norm"], dycat, name=tag + "_gnorm_bwd")
    dxs, dbm, dcm, ddt, sst = ssd_bwd(xact, proj, p["dt_bias"], p["a_log"], p["d_skip"], st_ssm, dypre, name=tag + "_ssd_bwd")
    dact = jnp.concatenate([dxs, dbm, dcm], axis=1)
    dxbc, cst = conv_bwd(proj, pre, dact, p["conv_w"], name=tag + "_conv_bwd")
    dproj = jnp.concatenate([dq, dk, dv, dg, dsq, dsk.astype(BF16), dsv.astype(BF16), dz, dxbc, ddt], axis=1)
    dh_prev, nst = dgrad_norm_bwd([dproj], [w_in], h, p["norm_mix"], scale, dh, name=tag + "_dgrad")
    dw_in = matmul(u, dproj, ta=True, tm=D_MODEL, tn=512, tk=512, name=tag + "_dw_in")
    small = dict(norm_mix=nst[0], ret_gn=rst[0], ssm_norm=nst2[0], conv_w=cst[0:4], conv_b=cst[4],
                 dt_bias=sst[:, 0, :2].reshape(SSM_HEADS), a_log=sst[:, 1, :2].reshape(SSM_HEADS), d_skip=sst[:, 2, :2].reshape(SSM_HEADS))
    return dh_prev, dw_in, dw_out, small, [nst[1], nst[2], gst[0]]


BIG = ("ffn1_wg", "ffn1_wu", "ffn1_wd", "w_in", "w_out", "ffn2_wg", "ffn2_wu", "ffn2_wd")
ROW_SHARDED = ("ffn1_wd", "w_out", "ffn2_wd")
SMALL = ("ada_b", "norm_ffn1", "norm_mix", "conv_b", "dt_bias", "a_log", "d_skip", "ret_gn", "ssm_norm", "norm_ffn2",
         "final_ada_b", "final_norm")
NAMES = ("ada_w", "ada_b", "norm_ffn1", "ffn1_wg", "ffn1_wu", "ffn1_wd", "norm_mix", "w_in", "conv_w", "conv_b", "dt_bias", "a_log",
         "d_skip", "ret_gn", "ssm_norm", "w_out", "norm_ffn2", "ffn2_wg", "ffn2_wu", "ffn2_wd", "final_ada_w", "final_ada_b", "final_norm")


def kernel(x, c, ada_w, ada_b, norm_ffn1, ffn1_wg, ffn1_wu, ffn1_wd, norm_mix, w_in, conv_w, conv_b, dt_bias, a_log, d_skip, ret_gn, ssm_norm, w_out, norm_ffn2, ffn2_wg, ffn2_wu, ffn2_wd, final_ada_w, final_ada_b, final_norm, loss_target, m_ada_w, m_ada_b, m_norm_ffn1, m_ffn1_wg, m_ffn1_wu, m_ffn1_wd, m_norm_mix, m_w_in, m_conv_w, m_conv_b, m_dt_bias, m_a_log, m_d_skip, m_ret_gn, m_ssm_norm, m_w_out, m_norm_ffn2, m_ffn2_wg, m_ffn2_wu, m_ffn2_wd, m_final_ada_w, m_final_ada_b, m_final_norm, v_ada_w, v_ada_b, v_norm_ffn1, v_ffn1_wg, v_ffn1_wu, v_ffn1_wd, v_norm_mix, v_w_in, v_conv_w, v_conv_b, v_dt_bias, v_a_log, v_d_skip, v_ret_gn, v_ssm_norm, v_w_out, v_norm_ffn2, v_ffn2_wg, v_ffn2_wu, v_ffn2_wd, v_final_ada_w, v_final_ada_b, v_final_norm):
    W = dict(ada_w=ada_w, ada_b=ada_b, norm_ffn1=norm_ffn1, ffn1_wg=ffn1_wg, ffn1_wu=ffn1_wu, ffn1_wd=ffn1_wd, norm_mix=norm_mix,
             w_in=w_in, conv_w=conv_w, conv_b=conv_b, dt_bias=dt_bias, a_log=a_log, d_skip=d_skip, ret_gn=ret_gn, ssm_norm=ssm_norm,
             w_out=w_out, norm_ffn2=norm_ffn2, ffn2_wg=ffn2_wg, ffn2_wu=ffn2_wu, ffn2_wd=ffn2_wd, final_ada_w=final_ada_w,
             final_ada_b=final_ada_b, final_norm=final_norm)
    M1 = dict(ada_w=m_ada_w, ada_b=m_ada_b, norm_ffn1=m_norm_ffn1, ffn1_wg=m_ffn1_wg, ffn1_wu=m_ffn1_wu, ffn1_wd=m_ffn1_wd,
              norm_mix=m_norm_mix, w_in=m_w_in, conv_w=m_conv_w, conv_b=m_conv_b, dt_bias=m_dt_bias, a_log=m_a_log, d_skip=m_d_skip,
              ret_gn=m_ret_gn, ssm_norm=m_ssm_norm, w_out=m_w_out, norm_ffn2=m_norm_ffn2, ffn2_wg=m_ffn2_wg, ffn2_wu=m_ffn2_wu,
              ffn2_wd=m_ffn2_wd, final_ada_w=m_final_ada_w, final_ada_b=m_final_ada_b, final_norm=m_final_norm)
    V2 = dict(ada_w=v_ada_w, ada_b=v_ada_b, norm_ffn1=v_norm_ffn1, ffn1_wg=v_ffn1_wg, ffn1_wu=v_ffn1_wu, ffn1_wd=v_ffn1_wd,
              norm_mix=v_norm_mix, w_in=v_w_in, conv_w=v_conv_w, conv_b=v_conv_b, dt_bias=v_dt_bias, a_log=v_a_log, d_skip=v_d_skip,
              ret_gn=v_ret_gn, ssm_norm=v_ssm_norm, w_out=v_w_out, norm_ffn2=v_norm_ffn2, ffn2_wg=v_ffn2_wg, ffn2_wu=v_ffn2_wu,
              ffn2_wd=v_ffn2_wd, final_ada_w=v_final_ada_w, final_ada_b=v_final_ada_b, final_norm=v_final_norm)
    D = D_MODEL
    S = x.shape[1]
    me = 4 * lax.axis_index("x") + 2 * lax.axis_index("y") + lax.axis_index("c")
    n_mod = ada_w.shape[2]
    n_fmod = final_ada_w.shape[1]

    c_all = allgather(jnp.broadcast_to(c, (8, D)), in_vmem=True, name="gather_c")[:, 0, :]
    ada_cols = jnp.concatenate([ada_w[0], ada_w[1], final_ada_w], axis=1)
    ada_bias = jnp.concatenate([lax.dynamic_slice(ada_b, (0, me * n_mod), (DEPTH, n_mod)).reshape(1, -1),
                                lax.dynamic_slice(final_ada_b, (me * n_fmod,), (n_fmod,)).reshape(1, -1)], axis=1)
    mod_sh, cond = ada_mod(jnp.pad(c_all, ((0, 8), (0, 0))), ada_cols, ada_bias, name="ada_mod")
    n_cols = mod_sh.shape[1]
    small_in = jnp.concatenate([mod_sh[:8], jnp.pad(conv_w.reshape(8, LANE), ((0, 0), (0, n_cols - LANE)))], axis=0)
    small_g = allgather(small_in, in_vmem=True, name="gather_mod")
    mod_rows = lax.dynamic_index_in_dim(small_g[:, :8, :], me, axis=1, keepdims=False)
    mod = [mod_rows[:, l * n_mod:(l + 1) * n_mod].reshape(9, D) for l in range(DEPTH)]
    fmod = mod_rows[:, DEPTH * n_mod:].reshape(2, D)
    conv_w_full = small_g[:, 8:, :LANE].reshape(N_DEV, DEPTH, SSM_CONV, LANE).transpose(1, 2, 0, 3).reshape(DEPTH, SSM_CONV, 8 * LANE)

    wpack, woffs = _pack([W[n].astype(BF16) for n in BIG])
    wgath = allgather_two_level(wpack, name="gather_weights")
    gshards = _unpack(wgath, woffs, [W[n].shape for n in BIG])
    full = {n: (_row_full(g) if n in ROW_SHARDED else _col_full(g)) for n, g in zip(BIG, gshards)}
    full["w_in"] = _pad_w_in(full["w_in"])

    cos, sin = _rope_tables(S)
    h = x[0]
    target = loss_target[0]
    layer_p = []
    for l in range(DEPTH):
        layer_p.append(dict(norm_ffn1=_row(norm_ffn1[l]), norm_mix=_row(norm_mix[l]), norm_ffn2=_row(norm_ffn2[l]),
                            ret_gn=_row(ret_gn[l]), ssm_norm=_row(ssm_norm[l]), conv_w=conv_w_full[l], conv_b=_row(conv_b[l]),
                            dt_bias=_pair_rows(dt_bias[l]), a_log=_pair_rows(a_log[l]), d_skip=_pair_rows(d_skip[l])))
    mods = [[[_row(mod[l][3 * s + k]) for k in range(3)] for s in range(3)] for l in range(DEPTH)]

    saved = []
    for l in range(DEPTH):
        p = layer_p[l]
        h, s1 = _ffn_fwd(h, p["norm_ffn1"], mods[l][0], full["ffn1_wg"][l], full["ffn1_wu"][l], full["ffn1_wd"][l], f"l{l}_ffn1")
        h, s2 = _mix_fwd(h, p, mods[l][1], full["w_in"][l], full["w_out"][l], cos, sin, f"l{l}_mix")
        h, s3 = _ffn_fwd(h, p["norm_ffn2"], mods[l][2], full["ffn2_wg"][l], full["ffn2_wu"][l], full["ffn2_wd"][l], f"l{l}_ffn2")
        saved.append((s1, s2, s3))

    dh, fst = final_loss_bwd(h, _row(final_norm), _row(fmod[0]), _row(fmod[1]), target, name="final")
    big_g = {n: [None] * DEPTH for n in BIG}
    small_g_l = [None] * DEPTH
    dmod = [None] * DEPTH
    for l in reversed(range(DEPTH)):
        p = layer_p[l]
        s1, s2, s3 = saved[l]
        dh, (g2g, g2u, g2d), gn2, dm2 = _ffn_bwd(dh, s3, p["norm_ffn2"], mods[l][2], full["ffn2_wg"][l], full["ffn2_wu"][l],
                                                 full["ffn2_wd"][l], f"l{l}_ffn2")
        dh, gw_in, gw_out, sm, dm1 = _mix_bwd(dh, s2, p, mods[l][1], full["w_in"][l], full["w_out"][l], cos, sin, f"l{l}_mix")
        dh, (g1g, g1u, g1d), gn1, dm0 = _ffn_bwd(dh, s1, p["norm_ffn1"], mods[l][0], full["ffn1_wg"][l], full["ffn1_wu"][l],
                                                 full["ffn1_wd"][l], f"l{l}_ffn1")
        for n, g in zip(BIG, (g1g, g1u, g1d, gw_in, gw_out, g2g, g2u, g2d)):
            big_g[n][l] = g
        sm["norm_ffn1"], sm["norm_ffn2"] = gn1, gn2
        small_g_l[l] = sm
        dmod[l] = jnp.concatenate(dm0 + dm1 + dm2, axis=0)
    grad_x = dh[None]

    gfull = {n: jnp.stack(big_g[n]) for n in BIG}
    gfull["w_in"] = _unpad_w_in(gfull["w_in"])
    gsh = [(_row_shards(gfull[n]) if n in ROW_SHARDED else _col_shards(gfull[n])).astype(BF16) for n in BIG]
    spack = _pad_rows(jnp.concatenate(
        [jnp.pad(g.reshape(N_DEV, -1), ((0, 0), (0, (woffs[i + 1] - woffs[i]) * LANE - g[0].size))).reshape(N_DEV, -1, LANE)
         for i, g in enumerate(gsh)], axis=1))
    by_core = spack.reshape((N_DEV // 2, 2) + spack.shape[1:])
    from_sibling = sibling_exchange(by_core, name="exchange_sibling")
    own = lax.dynamic_index_in_dim(by_core, lax.axis_index("c"), axis=1, keepdims=False)
    rpack = chip_exchange(add_partials(own, from_sibling, name="add_sibling"), name="exchange_chips")
    packs = [_pack([src[n] for n in BIG])[0] for src in (W, M1, V2)]
    outs = adamw_parts(rpack, *packs, name="adamw_big")
    big_out = [dict(zip(BIG, _unpack(o, woffs, [W[n].shape for n in BIG]))) for o in outs]

    stack2 = lambda key: jnp.stack([small_g_l[l][key] for l in range(DEPTH)])
    pieces = [("loss", fst[3, 0:1]), ("ada_b", jnp.stack(dmod)), ("final_ada_b", jnp.concatenate([fst[1], fst[2]])),
              ("norm_ffn1", stack2("norm_ffn1")), ("norm_mix", stack2("norm_mix")), ("norm_ffn2", stack2("norm_ffn2")),
              ("conv_w", stack2("conv_w")), ("conv_b", stack2("conv_b")), ("dt_bias", stack2("dt_bias")), ("a_log", stack2("a_log")),
              ("d_skip", stack2("d_skip")), ("ret_gn", stack2("ret_gn")), ("ssm_norm", stack2("ssm_norm")), ("final_norm", fst[0])]
    names = [n for n, _ in pieces]
    shapes = [a.shape for _, a in pieces]
    ppack, poffs = _pack([a for _, a in pieces])
    pg = allgather(ppack, in_vmem=True, name="gather_small")
    zero_like = lambda n, a: jnp.zeros(a.shape, F32)
    spacks = [_pack([(src[n] if n in SMALL else zero_like(n, a)) for n, a in pieces])[0] for src in (W, M1, V2)]
    souts = adamw_parts(pg, *spacks, name="adamw_small")
    small_out = [dict(zip(names, _unpack(o, poffs, shapes))) for o in souts]
    loss = small_out[0]["loss"][0]

    gathered = dict(zip(names, _unpack(pg, poffs, shapes)))
    conv_parts = lax.dynamic_slice_in_dim(gathered["conv_w"], me * LANE, LANE, axis=3).reshape(N_DEV, DEPTH * SSM_CONV, LANE)
    conv_out = [o.reshape(conv_w.shape) for o in adamw_parts(conv_parts, conv_w.reshape(-1, LANE), m_conv_w.reshape(-1, LANE),
                                                              v_conv_w.reshape(-1, LANE), name="adamw_conv_w")]
    cond_t = cond[:8].T
    ada_out = []
    for l in range(DEPTH):
        dsel = lax.dynamic_slice_in_dim(gathered["ada_b"][:, l, :], me * n_mod, n_mod, axis=1)
        ada_out.append(ada_adamw(cond_t, dsel, ada_w[l], m_ada_w[l], v_ada_w[l], name=f"adamw_ada_w{l}"))
    ada_out = [jnp.stack([ada_out[l][k] for l in range(DEPTH)]) for k in range(4)]
    fsel = lax.dynamic_slice_in_dim(gathered["final_ada_b"].reshape(N_DEV, 2 * D), me * n_fmod, n_fmod, axis=1)
    fada_out = ada_adamw(cond_t, fsel, final_ada_w, m_final_ada_w, v_final_ada_w, name="adamw_final_ada_w")

    def pick(k, n):
        if n in BIG:
            return big_out[k][n]
        if n == "ada_w":
            return ada_out[k]
        if n == "final_ada_w":
            return fada_out[k]
        if n == "conv_w":
            return conv_out[k]
        return small_out[k][n]

    return (loss, grad_x) + tuple(pick(k, n) for k in range(4) for n in NAMES)
```

```python
import functools
import math

import numpy as np
import jax
import jax.numpy as jnp
from jax import lax
from jax.experimental import pallas as pl
from jax.experimental.pallas import tpu as pltpu

F32 = jnp.float32
BF16 = jnp.bfloat16

D_MODEL = 1024
DEPTH = 2
RET_HEADS = 4
HEAD_DIM = 128
SSM_HEADS = 8
SSM_HEAD_DIM = 64
SSM_STATE = 128
SSM_CONV = 4
D_FF = 2816
ROPE_BASE = 10000.0
NORM_EPS = 1e-6
MIX_W = 1536
IN_W = 5128
IN_MAIN = 5120
IN_PAD = 5632
N_DEV = 8
LANE = 128

ADAM_LR = 0.001
ADAM_B1 = 0.9
ADAM_B2 = 0.999
ADAM_EPS = 1e-08
ADAM_WD = 0.01
ADAM_STEP = 10

TOKEN_TILE = 512
WGRAD_TOKENS = 2048
BIG_TOKEN_TILE = 2048
SEQ_BLOCK = 256
VMEM_LIMIT = 56 << 20
SB_SKIP = 120.0
SB_UNVISITED = -1e30

CB_RQ, CB_RK, CB_RV, CB_RG = 0, 4, 8, 12
CB_SQ, CB_SK, CB_SV = 16, 20, 24
CB_MZ, CB_XS, CB_BM, CB_CM, CB_DT = 28, 32, 36, 38, 40


def _cparams(*sem):
    return pltpu.CompilerParams(dimension_semantics=sem, vmem_limit_bytes=VMEM_LIMIT)


def _sds(shape, dtype):
    return jax.ShapeDtypeStruct(tuple(shape), dtype)


def _tile(n, *prefs):
    for p in prefs:
        if n % p == 0:
            return p
    return n


def _dot(a, b, dims):
    return lax.dot_general(a, b, (dims, ((), ())), preferred_element_type=F32)


NN = ((1,), (0,))
NT = ((1,), (1,))
TN = ((0,), (0,))


def _bf(x):
    return x.astype(BF16)


def _sigmoid(x):
    return jax.nn.sigmoid(x)


def _split2(x):
    hi = x.astype(BF16)
    lo = (x - hi.astype(F32)).astype(BF16)
    return hi, lo


def _split3(x):
    hi = x.astype(BF16)
    r = x - hi.astype(F32)
    mid = r.astype(BF16)
    lo = (r - mid.astype(F32)).astype(BF16)
    return hi, mid, lo


def matmul(a, b, *, ta=False, tb=False, tm=512, tn=512, tk=512, out_dtype=F32, name):
    M, K = (a.shape[1], a.shape[0]) if ta else a.shape
    N = b.shape[0] if tb else b.shape[1]
    tm, tn, tk = min(tm, M), min(tn, N), min(tk, K)
    assert M % tm == 0 and N % tn == 0 and K % tk == 0, (name, M, N, K, tm, tn, tk)
    nk = K // tk
    a_spec = pl.BlockSpec((tk, tm), lambda i, j, k: (k, i)) if ta else pl.BlockSpec((tm, tk), lambda i, j, k: (i, k))
    b_spec = pl.BlockSpec((tn, tk), lambda i, j, k: (j, k)) if tb else pl.BlockSpec((tk, tn), lambda i, j, k: (k, j))
    dims = ((0 if ta else 1,), (1 if tb else 0,))

    def body(a_ref, b_ref, o_ref, acc_ref):
        k = pl.program_id(2)
        p = _dot(_bf(a_ref[...]), _bf(b_ref[...]), dims)

        @pl.when(k == 0)
        def _():
            acc_ref[...] = p

        @pl.when(k > 0)
        def _():
            acc_ref[...] += p

        @pl.when(k == nk - 1)
        def _():
            o_ref[...] = acc_ref[...].astype(out_dtype)

    return pl.pallas_call(
        body, grid=(M // tm, N // tn, nk), in_specs=[a_spec, b_spec],
        out_specs=pl.BlockSpec((tm, tn), lambda i, j, k: (i, j)), out_shape=_sds((M, N), out_dtype),
        scratch_shapes=[pltpu.VMEM((tm, tn), F32)], name=name,
        compiler_params=_cparams("parallel", "parallel", "arbitrary"))(a, b)


def matmul_resid(a, w, h, gate, factor, *, name):
    M, K = a.shape
    N = w.shape[1]
    tm = min(TOKEN_TILE, M)

    def body(a_ref, w_ref, h_ref, g_ref, hn_ref, o_ref):
        out = _dot(a_ref[...], w_ref[...], NN)
        o_ref[...] = out
        hn_ref[...] = h_ref[...] + (factor * (1.0 + g_ref[...])) * out

    mn = pl.BlockSpec((tm, N), lambda i: (i, 0))
    return pl.pallas_call(
        body, grid=(M // tm,),
        in_specs=[pl.BlockSpec((tm, K), lambda i: (i, 0)), pl.BlockSpec((K, N), lambda i: (0, 0)), mn,
                  pl.BlockSpec((1, N), lambda i: (0, 0))],
        out_specs=[mn, mn], out_shape=[_sds((M, N), F32), _sds((M, N), F32)], name=name,
        compiler_params=_cparams("parallel"))(a, w, h, gate)


def ffn_up(u, wg_t, wu_t, *, name):
    M, K = u.shape
    N = wg_t.shape[0]
    tm, tn = min(BIG_TOKEN_TILE, M), _tile(N, 256)

    def body(u_ref, wg_ref, wu_ref, a_ref, b_ref, act_ref):
        uu = u_ref[...]
        a = _dot(uu, wg_ref[...], NT)
        b = _dot(uu, wu_ref[...], NT)
        a_ref[...] = a
        b_ref[...] = b
        act_ref[...] = _bf(a * _sigmoid(a) * b)

    mn = pl.BlockSpec((tm, tn), lambda i, j: (i, j))
    wspec = pl.BlockSpec((tn, K), lambda i, j: (j, 0))
    return pl.pallas_call(
        body, grid=(M // tm, N // tn), in_specs=[pl.BlockSpec((tm, K), lambda i, j: (i, 0)), wspec, wspec],
        out_specs=[mn, mn, mn], out_shape=[_sds((M, N), F32), _sds((M, N), F32), _sds((M, N), BF16)], name=name,
        compiler_params=_cparams("parallel", "parallel"))(u, wg_t, wu_t)


def ffn_dact(dout, wd, a, b, *, name):
    M, K = dout.shape
    N = wd.shape[0]
    tm, tn = min(BIG_TOKEN_TILE, M), _tile(N, 256)

    def body(d_ref, w_ref, a_ref, b_ref, da_ref, db_ref):
        dact = _dot(d_ref[...], w_ref[...], NT)
        av = a_ref[...]
        sg = _sigmoid(av)
        db_ref[...] = _bf(dact * av * sg)
        da_ref[...] = _bf(dact * b_ref[...] * (sg * (1.0 + av * (1.0 - sg))))

    mn = pl.BlockSpec((tm, tn), lambda i, j: (i, j))
    return pl.pallas_call(
        body, grid=(M // tm, N // tn),
        in_specs=[pl.BlockSpec((tm, K), lambda i, j: (i, 0)), pl.BlockSpec((tn, K), lambda i, j: (j, 0)), mn, mn],
        out_specs=[mn, mn], out_shape=[_sds((M, N), BF16), _sds((M, N), BF16)], name=name,
        compiler_params=_cparams("parallel", "parallel"))(dout, wd, a, b)


def norm_mod(h, gain, shift, scale, *, name):
    S, D = h.shape
    tm = min(TOKEN_TILE, S)

    def body(h_ref, g_ref, sh_ref, sc_ref, u_ref):
        x = h_ref[...]
        r = lax.rsqrt(jnp.mean(x * x, axis=-1, keepdims=True) + NORM_EPS)
        n = x * r * g_ref[...]
        u_ref[...] = _bf(n * (1.0 + sc_ref[...]) + sh_ref[...])

    row = pl.BlockSpec((1, D), lambda i: (0, 0))
    tile = pl.BlockSpec((tm, D), lambda i: (i, 0))
    return pl.pallas_call(body, grid=(S // tm,), in_specs=[tile, row, row, row], out_specs=tile,
                          out_shape=_sds((S, D), BF16), name=name, compiler_params=_cparams("parallel"))(h, gain, shift, scale)


def dgrad_norm_bwd(lhs, ws, h, gain, scale, dres, *, name):
    S, D = h.shape
    tm = min(TOKEN_TILE, S)
    n = len(lhs)

    def body(*refs):
        l_refs, w_refs = refs[:n], refs[n:2 * n]
        h_ref, g_ref, sc_ref, dres_ref, dh_ref, st_ref = refs[2 * n:]
        du = _dot(l_refs[0][...], w_refs[0][...], NN)
        for lr, wr in zip(l_refs[1:], w_refs[1:]):
            du = du + _dot(lr[...], wr[...], NN)
        x = h_ref[...]
        g = g_ref[...]
        r = lax.rsqrt(jnp.mean(x * x, axis=-1, keepdims=True) + NORM_EPS)
        xhat = x * r
        dn = du * (1.0 + sc_ref[...])
        dxhat = dn * g
        dh_ref[...] = dres_ref[...] + r * (dxhat - xhat * jnp.mean(dxhat * xhat, axis=-1, keepdims=True))

        @pl.when(pl.program_id(0) == 0)
        def _():
            st_ref[...] = jnp.zeros_like(st_ref)

        st_ref[0:1, :] += jnp.sum(dn * xhat, axis=0, keepdims=True)
        st_ref[1:2, :] += jnp.sum(du, axis=0, keepdims=True)
        st_ref[2:3, :] += jnp.sum(du * (xhat * g), axis=0, keepdims=True)

    row = pl.BlockSpec((1, D), lambda i: (0, 0))
    tile = pl.BlockSpec((tm, D), lambda i: (i, 0))
    in_specs = [pl.BlockSpec((tm, l.shape[1]), lambda i: (i, 0)) for l in lhs]
    in_specs += [pl.BlockSpec(w.shape, lambda i: (0, 0)) for w in ws]
    in_specs += [tile, row, row, tile]
    return pl.pallas_call(
        body, grid=(S // tm,), in_specs=in_specs, out_specs=[tile, pl.BlockSpec((8, D), lambda i: (0, 0))],
        out_shape=[_sds((S, D), F32), _sds((8, D), F32)], name=name,
        compiler_params=_cparams("arbitrary"))(*lhs, *ws, h, gain, scale, dres)


def gate_bwd(dh, out, gate, factor, *, name):
    S, D = dh.shape
    tm = min(TOKEN_TILE, S)

    def body(dh_ref, o_ref, g_ref, do_ref, st_ref):
        d = dh_ref[...]
        do_ref[...] = _bf(d * (factor * (1.0 + g_ref[...])))

        @pl.when(pl.program_id(0) == 0)
        def _():
            st_ref[...] = jnp.zeros_like(st_ref)

        st_ref[0:1, :] += factor * jnp.sum(d * o_ref[...], axis=0, keepdims=True)

    tile = pl.BlockSpec((tm, D), lambda i: (i, 0))
    return pl.pallas_call(
        body, grid=(S // tm,), in_specs=[tile, tile, pl.BlockSpec((1, D), lambda i: (0, 0))],
        out_specs=[tile, pl.BlockSpec((8, D), lambda i: (0, 0))], out_shape=[_sds((S, D), BF16), _sds((8, D), F32)],
        name=name, compiler_params=_cparams("arbitrary"))(dh, out, gate)


def final_loss_bwd(h, gain, shift, scale, target, *, name):
    S, D = h.shape
    tm = min(TOKEN_TILE, S)

    def body(h_ref, g_ref, sh_ref, sc_ref, t_ref, dh_ref, st_ref):
        x = h_ref[...]
        g = g_ref[...]
        r = lax.rsqrt(jnp.mean(x * x, axis=-1, keepdims=True) + NORM_EPS)
        xhat = x * r
        n = xhat * g
        err = n * (1.0 + sc_ref[...]) + sh_ref[...] - t_ref[...]
        dy = err * (1.0 / D)
        dn = dy * (1.0 + sc_ref[...])
        dxhat = dn * g
        dh_ref[...] = r * (dxhat - xhat * jnp.mean(dxhat * xhat, axis=-1, keepdims=True))

        @pl.when(pl.program_id(0) == 0)
        def _():
            st_ref[...] = jnp.zeros_like(st_ref)

        st_ref[0:1, :] += jnp.sum(dn * xhat, axis=0, keepdims=True)
        st_ref[1:2, :] += jnp.sum(dy, axis=0, keepdims=True)
        st_ref[2:3, :] += jnp.sum(dy * n, axis=0, keepdims=True)
        tok = jnp.mean(err * err, axis=-1, keepdims=True)
        st_ref[3:4, :] += 0.5 * jnp.sum(tok, axis=0, keepdims=True)

    row = pl.BlockSpec((1, D), lambda i: (0, 0))
    tile = pl.BlockSpec((tm, D), lambda i: (i, 0))
    return pl.pallas_call(
        body, grid=(S // tm,), in_specs=[tile, row, row, row, tile], out_specs=[tile, pl.BlockSpec((8, D), lambda i: (0, 0))],
        out_shape=[_sds((S, D), F32), _sds((8, D), F32)], name=name,
        compiler_params=_cparams("arbitrary"))(h, gain, shift, scale, target)


def _ret_tables(T):
    heads = np.arange(RET_HEADS, dtype=np.float64)
    lg = np.log1p(-(2.0 ** (-5.0 - heads)))
    t = np.arange(T)
    same = (t[:, None] // 64) == (t[None, :] // 64)
    earlier = (t[None, :] // 64) < (t[:, None] // 64)
    dist = np.abs(t[:, None] - t[None, :]).astype(np.float64)
    dmat = np.where(same | earlier, np.exp(lg[:, None, None] * dist[None]), 0.0)
    qdec = np.exp(lg[:, None] * (t + 1.0)[None, :])
    kdec = np.exp(lg[:, None] * (T - 1.0 - t)[None, :])
    cdec = np.exp(lg * T)
    bc = lambda v: jnp.asarray(np.broadcast_to(v[:, :, None], (RET_HEADS, T, LANE)), F32)
    cd = jnp.asarray(np.broadcast_to(cdec[:, None, None], (RET_HEADS, LANE, LANE)), F32)
    return jnp.asarray(dmat, F32), bc(qdec), bc(kdec), cd


def _rope_tables(S):
    half = HEAD_DIM // 2
    inv_freq = ROPE_BASE ** (-jnp.arange(half, dtype=F32) / half)
    ang = jnp.arange(S, dtype=F32)[:, None] * inv_freq[None, :]
    cos, sin = jnp.cos(ang), jnp.sin(ang)
    return jnp.concatenate([cos, cos], axis=-1), jnp.concatenate([-sin, sin], axis=-1)


def _rope(x, c, s):
    return x * c + pltpu.roll(x, HEAD_DIM // 2, 1) * s


def _rope_t(dx, c, s):
    return dx * c + pltpu.roll(dx * s, HEAD_DIM // 2, 1)


def ret_fwd(proj, gn, cos, sin, *, name):
    S = proj.shape[0]
    T = min(SEQ_BLOCK, S)
    nb = S // T
    dmat, qdec, kdec, cdec = _ret_tables(T)

    def body(q_ref, k_ref, v_ref, g_ref, c_ref, s_ref, dm_ref, qd_ref, kd_ref, cd_ref, gn_ref, yo_ref, yp_ref, st_ref, state):
        @pl.when(pl.program_id(1) == 0)
        def _():
            state[...] = jnp.zeros_like(state)

        c, s = c_ref[...], s_ref[...]
        qr = _rope(q_ref[...], c, s)
        kr = _rope(k_ref[...], c, s) * (HEAD_DIM ** -0.5)
        v = _bf(v_ref[...])
        sp = state[...]
        st_ref[...] = sp
        a = _dot(_bf(qr), _bf(kr), NT) * dm_ref[...]
        y = _dot(_bf(a), v, NN) + _dot(_bf(qr * qd_ref[...]), _bf(sp), NN)
        state[...] = cd_ref[...] * sp + _dot(_bf(kr * kd_ref[...]), v, TN)
        yp_ref[...] = y
        yn = y * lax.rsqrt(jnp.mean(y * y, axis=-1, keepdims=True) + NORM_EPS) * gn_ref[...]
        g = g_ref[...]
        yo_ref[...] = _bf(yn * (g * _sigmoid(g)))

    col = lambda cb: pl.BlockSpec((T, LANE), lambda h, b: (b, cb + h))
    tok = pl.BlockSpec((T, LANE), lambda h, b: (b, 0))
    per_head = lambda r: pl.BlockSpec((None, r, LANE), lambda h, b: (h, 0, 0))
    out_tok = pl.BlockSpec((T, LANE), lambda h, b: (b, h))
    return pl.pallas_call(
        body, grid=(RET_HEADS, nb),
        in_specs=[col(CB_RQ), col(CB_RK), col(CB_RV), col(CB_RG), tok, tok,
                  pl.BlockSpec((None, T, T), lambda h, b: (h, 0, 0)), per_head(T), per_head(T), per_head(LANE),
                  pl.BlockSpec((1, LANE), lambda h, b: (0, h))],
        out_specs=[out_tok, out_tok, pl.BlockSpec((None, None, LANE, LANE), lambda h, b: (h, b, 0, 0))],
        out_shape=[_sds((S, 512), BF16), _sds((S, 512), F32), _sds((RET_HEADS, nb, LANE, LANE), F32)],
        scratch_shapes=[pltpu.VMEM((LANE, LANE), F32)], name=name,
        compiler_params=_cparams("parallel", "arbitrary"))(proj, proj, proj, proj, cos, sin, dmat, qdec, kdec, cdec, gn)


def ret_bwd(proj, gn, cos, sin, ypre, states, dycat, *, name):
    S = proj.shape[0]
    T = min(SEQ_BLOCK, S)
    nb = S // T
    dmat, qdec, kdec, cdec = _ret_tables(T)

    def body(q_ref, k_ref, v_ref, g_ref, c_ref, s_ref, dm_ref, qd_ref, kd_ref, cd_ref, gn_ref, yp_ref, st_ref, dy_ref,
             dq_ref, dk_ref, dv_ref, dg_ref, stat_ref, gstate):
        @pl.when(pl.program_id(1) == 0)
        def _():
            gstate[...] = jnp.zeros_like(gstate)
            stat_ref[...] = jnp.zeros_like(stat_ref)

        c, s = c_ref[...], s_ref[...]
        scale = HEAD_DIM ** -0.5
        qr = _rope(q_ref[...], c, s)
        kr = _rope(k_ref[...], c, s) * scale
        v = _bf(v_ref[...])
        qd, kd, dm = qd_ref[...], kd_ref[...], dm_ref[...]
        sp = _bf(st_ref[...])
        gs = gstate[...]
        gsb = _bf(gs)
        g = g_ref[...]
        sg = _sigmoid(g)
        y = yp_ref[...]
        gn_row = gn_ref[...]
        r = lax.rsqrt(jnp.mean(y * y, axis=-1, keepdims=True) + NORM_EPS)
        yhat = y * r
        dyo = dy_ref[...]
        dg_ref[...] = _bf(dyo * (yhat * gn_row) * (sg * (1.0 + g * (1.0 - sg))))
        dyn = dyo * (g * sg)
        stat_ref[0:1, :] += jnp.sum(dyn * yhat, axis=0, keepdims=True)
        dyhat = dyn * gn_row
        dy = _bf(r * (dyhat - yhat * jnp.mean(dyhat * yhat, axis=-1, keepdims=True)))
        qrb, krb = _bf(qr), _bf(kr)
        qdb = _bf(qr * qd)
        kdb = _bf(kr * kd)
        a = _bf(_dot(qrb, krb, NT) * dm)
        da = _bf(_dot(dy, v, NT) * dm)
        dv_ref[...] = _bf(_dot(a, dy, TN) + _dot(kdb, gsb, NN))
        dqr = _dot(da, krb, NN) + qd * _dot(dy, sp, NT)
        dkr = _dot(da, qrb, TN) + kd * _dot(v, gsb, NT)
        gstate[...] = cd_ref[...] * gs + _dot(qdb, dy, TN)
        dq_ref[...] = _bf(_rope_t(dqr, c, s))
        dk_ref[...] = _bf(_rope_t(dkr * scale, c, s))

    rb = lambda b: nb - 1 - b
    col = lambda cb: pl.BlockSpec((T, LANE), lambda h, b: (rb(b), cb + h))
    tok = pl.BlockSpec((T, LANE), lambda h, b: (rb(b), 0))
    per_head = lambda r: pl.BlockSpec((None, r, LANE), lambda h, b: (h, 0, 0))
    out_tok = pl.BlockSpec((T, LANE), lambda h, b: (rb(b), h))
    return pl.pallas_call(
        body, grid=(RET_HEADS, nb),
        in_specs=[col(CB_RQ), col(CB_RK), col(CB_RV), col(CB_RG), tok, tok,
                  pl.BlockSpec((None, T, T), lambda h, b: (h, 0, 0)), per_head(T), per_head(T), per_head(LANE),
                  pl.BlockSpec((1, LANE), lambda h, b: (0, h)), out_tok,
                  pl.BlockSpec((None, None, LANE, LANE), lambda h, b: (h, rb(b), 0, 0)), out_tok],
        out_specs=[out_tok, out_tok, out_tok, out_tok, pl.BlockSpec((8, LANE), lambda h, b: (0, h))],
        out_shape=[_sds((S, 512), BF16)] * 4 + [_sds((8, 512), F32)],
        scratch_shapes=[pltpu.VMEM((LANE, LANE), F32)], name=name,
        compiler_params=_cparams("parallel", "arbitrary"))(proj, proj, proj, proj, cos, sin, dmat, qdec, kdec, cdec, gn, ypre, states, dycat)


def _sb_cast_kv(k_ref, v_ref, kb, vb, S):
    step = min(TOKEN_TILE, S)
    for r in range(0, S, step):
        kb[r:r + step, :] = _bf(k_ref[r:r + step, :])
        vb[r:r + step, :] = _bf(v_ref[r:r + step, :])


def _sb_logits(q, kblk, vis):
    z = _dot(q, kblk, NT) * (HEAD_DIM ** -0.5)
    l = jnp.log1p(jnp.exp(-jnp.abs(z)))
    lb = jnp.minimum(z, 0.0) - l
    lk = jnp.minimum(-z, 0.0) - l
    if vis is not None:
        lk = jnp.where(vis, lk, 0.0)
    return lb, lk


def _tri(T, cmp):
    r = lax.broadcasted_iota(jnp.int32, (T, T), 0)
    c = lax.broadcasted_iota(jnp.int32, (T, T), 1)
    return cmp(r, c)


def _dot_split2(x, m):
    hi, lo = _split2(x)
    return _dot(hi, m, NN) + _dot(lo, m, NN)


def sb_fwd(proj, *, name):
    S = proj.shape[0]
    T = min(SEQ_BLOCK, S)
    nq = S // T

    assert nq <= LANE

    def body(q_ref, k_ref, v_ref, o_ref, cin_ref, kb, vb):
        qi = pl.program_id(1)

        @pl.when(qi == 0)
        def _():
            _sb_cast_kv(k_ref, v_ref, kb, vb, S)

        q = _bf(q_ref[...])
        vis = _tri(T, lambda t, s: s < t)
        after = _tri(T, lambda j, s: j > s).astype(BF16)
        lane = lax.broadcasted_iota(jnp.int32, (T, LANE), 1)

        def block(jb, carry, acc, cin, mask):
            rows = pl.ds(pl.multiple_of(jb * T, T), T)
            lb, lk = _sb_logits(q, kb[rows, :], mask)
            tail = _dot_split2(lk, after) + carry
            w = jnp.exp(lb + tail)
            if mask is not None:
                w = jnp.where(mask, w, 0.0)
            return (carry + jnp.sum(lk, axis=1, keepdims=True), acc + _dot(_bf(w), vb[rows, :], NN),
                    jnp.where(lane == jb, carry, cin))

        st = block(qi, jnp.zeros((T, 1), F32), jnp.zeros((T, LANE), F32), jnp.full((T, LANE), SB_UNVISITED, F32), vis)

        def more(c):
            return (c[0] < qi) & (jnp.max(c[1]) > -SB_SKIP)

        def step(c):
            return (c[0] + 1,) + block(qi - 1 - c[0], c[1], c[2], c[3], None)

        st = lax.while_loop(more, step, (jnp.int32(0),) + st)
        o_ref[...] = st[2]
        cin_ref[...] = st[3]

    whole = lambda cb: pl.BlockSpec((S, LANE), lambda h, i: (0, cb + h))
    tok = pl.BlockSpec((T, LANE), lambda h, i: (i, h))
    return pl.pallas_call(
        body, grid=(RET_HEADS, nq),
        in_specs=[pl.BlockSpec((T, LANE), lambda h, i: (i, CB_SQ + h)), whole(CB_SK), whole(CB_SV)],
        out_specs=[tok, tok], out_shape=[_sds((S, 512), F32), _sds((S, 512), F32)],
        scratch_shapes=[pltpu.VMEM((S, LANE), BF16), pltpu.VMEM((S, LANE), BF16)], name=name,
        compiler_params=_cparams("parallel", "arbitrary"))(proj, proj, proj)


def sb_bwd(proj, cin, dycat, *, name):
    S = proj.shape[0]
    T = min(SEQ_BLOCK, S)
    nq = S // T
    scale = HEAD_DIM ** -0.5

    def body(q_ref, k_ref, v_ref, cin_ref, do_ref, dq_ref, dk_ref, dv_ref, kb, vb):
        qi = pl.program_id(1)

        @pl.when(qi == 0)
        def _():
            _sb_cast_kv(k_ref, v_ref, kb, vb, S)
            dk_ref[...] = jnp.zeros_like(dk_ref)
            dv_ref[...] = jnp.zeros_like(dv_ref)

        q = _bf(q_ref[...])
        dob = _bf(do_ref[...])
        cin = cin_ref[...]
        vis = _tri(T, lambda t, s: s < t)
        after = _tri(T, lambda j, s: j > s).astype(BF16)
        before = _tri(T, lambda s, j: s < j).astype(BF16)
        lane = lax.broadcasted_iota(jnp.int32, (T, LANE), 1)

        def block(jb, ecarry, dq, mask):
            rows = pl.ds(pl.multiple_of(jb * T, T), T)
            kblk, vblk = kb[rows, :], vb[rows, :]
            lb, lk = _sb_logits(q, kblk, mask)
            carry = jnp.sum(jnp.where(lane == jb, cin, 0.0), axis=1, keepdims=True)
            w = jnp.exp(lb + _dot_split2(lk, after) + carry)
            if mask is not None:
                w = jnp.where(mask, w, 0.0)
            e = w * _dot(dob, vblk, NT)
            dv_ref[rows, :] += _dot(_bf(w), dob, TN)
            dlk = _dot_split2(e, before) + ecarry
            beta = jnp.exp(lb)
            dz = e * (1.0 - beta) - beta * dlk
            if mask is not None:
                dz = jnp.where(mask, dz, 0.0)
            dzb = _bf(dz * scale)
            dk_ref[rows, :] += _dot(dzb, q, TN)
            return ecarry + jnp.sum(e, axis=1, keepdims=True), dq + _dot(dzb, kblk, NN)

        lane1 = lane[0:1, :]
        skipped = (jnp.max(cin, axis=0, keepdims=True) <= -SB_SKIP) & (lane1 < qi)
        first = jnp.sum(jnp.where(skipped, 1, 0))
        st = lax.fori_loop(first, qi, lambda jb, c: block(jb, c[0], c[1], None), (jnp.zeros((T, 1), F32), jnp.zeros((T, LANE), F32)))
        st = block(qi, st[0], st[1], vis)
        dq_ref[...] = _bf(st[1])

    whole = lambda cb: pl.BlockSpec((S, LANE), lambda h, i: (0, cb + h))
    tok = pl.BlockSpec((T, LANE), lambda h, i: (i, h))
    acc = pl.BlockSpec((S, LANE), lambda h, i: (0, h))
    return pl.pallas_call(
        body, grid=(RET_HEADS, nq),
        in_specs=[pl.BlockSpec((T, LANE), lambda h, i: (i, CB_SQ + h)), whole(CB_SK), whole(CB_SV), tok,
                  pl.BlockSpec((T, LANE), lambda h, i: (i, 4 + h))],
        out_specs=[tok, acc, acc], out_shape=[_sds((S, 512), BF16), _sds((S, 512), F32), _sds((S, 512), F32)],
        scratch_shapes=[pltpu.VMEM((S, LANE), BF16), pltpu.VMEM((S, LANE), BF16)], name=name,
        compiler_params=_cparams("parallel", "arbitrary"))(proj, proj, proj, cin, dycat)


def _shift_down(x, d, row):
    return jnp.where(row >= d, pltpu.roll(x, d, 0), 0.0)


def _shift_up(x, d, row, S):
    return jnp.where(row < S - d, pltpu.roll(x, S - d, 0), 0.0)


def conv_fwd(proj, conv_w, conv_b, *, name):
    S = proj.shape[0]

    def body(x_ref, w_ref, b_ref, pre_ref, act_ref):
        x = x_ref[...]
        row = lax.broadcasted_iota(jnp.int32, x.shape, 0)
        pre = b_ref[...] + w_ref[3:4, :] * x
        for d in range(1, SSM_CONV):
            pre = pre + w_ref[3 - d:4 - d, :] * _shift_down(x, d, row)
        pre_ref[...] = pre
        act_ref[...] = pre * _sigmoid(pre)

    blk = pl.BlockSpec((S, LANE), lambda c: (0, c))
    return pl.pallas_call(
        body, grid=(8,),
        in_specs=[pl.BlockSpec((S, LANE), lambda c: (0, CB_XS + c)), pl.BlockSpec((SSM_CONV, LANE), lambda c: (0, c)),
                  pl.BlockSpec((1, LANE), lambda c: (0, c))],
        out_specs=[blk, blk], out_shape=[_sds((S, 1024), F32), _sds((S, 1024), F32)], name=name,
        compiler_params=_cparams("parallel"))(proj, conv_w, conv_b)


def conv_bwd(proj, pre, dact, conv_w, *, name):
    S = proj.shape[0]

    def body(x_ref, pre_ref, da_ref, w_ref, dx_ref, st_ref):
        x = x_ref[...]
        p = pre_ref[...]
        row = lax.broadcasted_iota(jnp.int32, x.shape, 0)
        sg = _sigmoid(p)
        dpre = da_ref[...] * (sg * (1.0 + p * (1.0 - sg)))
        dx = w_ref[3:4, :] * dpre
        st_ref[3:4, :] = jnp.sum(dpre * x, axis=0, keepdims=True)
        for d in range(1, SSM_CONV):
            dx = dx + w_ref[3 - d:4 - d, :] * _shift_up(dpre, d, row, S)
            st_ref[3 - d:4 - d, :] = jnp.sum(dpre * _shift_down(x, d, row), axis=0, keepdims=True)
        st_ref[4:5, :] = jnp.sum(dpre, axis=0, keepdims=True)
        st_ref[5:8, :] = jnp.zeros((3, LANE), F32)
        dx_ref[...] = _bf(dx)

    blk = pl.BlockSpec((S, LANE), lambda c: (0, c))
    return pl.pallas_call(
        body, grid=(8,),
        in_specs=[pl.BlockSpec((S, LANE), lambda c: (0, CB_XS + c)), blk, blk, pl.BlockSpec((SSM_CONV, LANE), lambda c: (0, c))],
        out_specs=[blk, pl.BlockSpec((8, LANE), lambda c: (0, c))],
        out_shape=[_sds((S, 1024), BF16), _sds((8, 1024), F32)], name=name,
        compiler_params=_cparams("parallel"))(proj, pre, dact, conv_w)


def _softplus(x):
    return jnp.maximum(x, 0.0) + jnp.log1p(jnp.exp(-jnp.abs(x)))


def _pair(lane, v0, v1):
    return jnp.where(lane < SSM_HEAD_DIM, v0, v1)


def _ssd_pair_common(raw, dtb, alog, xs, cm, hprev, T):
    lane = lax.broadcasted_iota(jnp.int32, (T, LANE), 1)
    dt = _softplus(raw + dtb)
    a = -jnp.exp(alog)
    incl = _tri(T, lambda l, s: s <= l).astype(BF16)
    h1, h2, h3 = _split3(dt * a)
    acum = _dot(incl, h1, NN) + _dot(incl, h2, NN) + _dot(incl, h3, NN)
    acum_t = acum.T
    causal = _tri(T, lambda l, s: s <= l)
    decay = [jnp.where(causal, jnp.exp(jnp.minimum(acum[:, j:j + 1] - acum_t[j:j + 1, :], 0.0)), 0.0) for j in (0, 1)]
    dtc = _pair(lane, dt[:, 0:1], dt[:, 1:2])
    ac = _pair(lane, acum[:, 0:1], acum[:, 1:2])
    xdt = xs * dtc
    ea = jnp.exp(ac)
    e_end = jnp.exp(ac[T - 1:T, :] - ac)
    sub = lax.broadcasted_iota(jnp.int32, (LANE, LANE), 0)
    cd = jnp.where(sub < SSM_HEAD_DIM, jnp.exp(acum[T - 1:T, 0:1]), jnp.exp(acum[T - 1:T, 1:2]))
    r = _dot(cm, _bf(hprev), NT)
    return lane, dt, a, acum, decay, dtc, xdt, ea, e_end, cd, r


def ssd_fwd(xact, proj, dtb, alog, dskip, *, name):
    S = xact.shape[0]
    T = min(SEQ_BLOCK, S)
    nb = S // T

    def body(xs_ref, bm_ref, cm_ref, dt0_ref, dt1_ref, dtb_ref, al_ref, ds_ref, y_ref, st_ref, state):
        @pl.when(pl.program_id(1) == 0)
        def _():
            state[...] = jnp.zeros_like(state)

        bm, cm = _bf(bm_ref[...]), _bf(cm_ref[...])
        gm = _dot(cm, bm, NT)
        for i, dt_ref in enumerate((dt0_ref, dt1_ref)):
            xs = xs_ref[:, LANE * i:LANE * (i + 1)]
            hprev = state[i]
            st_ref[i] = hprev
            lane, dt, a, acum, decay, dtc, xdt, ea, e_end, cd, r = _ssd_pair_common(
                dt_ref[...], dtb_ref[i], al_ref[i], xs, cm, hprev, T)
            xdtb = _bf(xdt)
            y_intra = _pair(lane, _dot(_bf(gm * decay[0]), xdtb, NN), _dot(_bf(gm * decay[1]), xdtb, NN))
            state[i] = cd * hprev + _dot(_bf(xdt * e_end), bm, TN)
            dsk = ds_ref[i]
            lane1 = lane[0:1, :]
            y_ref[:, LANE * i:LANE * (i + 1)] = y_intra + ea * r + _pair(lane1, dsk[:, 0:1], dsk[:, 1:2]) * xs

    rows3 = pl.BlockSpec((2, 1, LANE), lambda g, b: (g, 0, 0))
    return pl.pallas_call(
        body, grid=(2, nb),
        in_specs=[pl.BlockSpec((T, 256), lambda g, b: (b, g)), pl.BlockSpec((T, LANE), lambda g, b: (b, 4 + g)),
                  pl.BlockSpec((T, LANE), lambda g, b: (b, 6 + g)),
                  pl.BlockSpec((T, LANE), lambda g, b: (b, CB_DT + 2 * g)), pl.BlockSpec((T, LANE), lambda g, b: (b, CB_DT + 2 * g + 1)),
                  rows3, rows3, rows3],
        out_specs=[pl.BlockSpec((T, 256), lambda g, b: (b, g)),
                   pl.BlockSpec((None, None, 2, LANE, LANE), lambda g, b: (g, b, 0, 0, 0))],
        out_shape=[_sds((S, 512), F32), _sds((2, nb, 2, LANE, LANE), F32)],
        scratch_shapes=[pltpu.VMEM((2, LANE, LANE), F32)], name=name,
        compiler_params=_cparams("parallel", "arbitrary"))(xact, xact, xact, proj, proj, dtb, alog, dskip)


def ssd_bwd(xact, proj, dtb, alog, dskip, states, dy, *, name):
    S = xact.shape[0]
    T = min(SEQ_BLOCK, S)
    nb = S // T

    def body(xs_ref, bm_ref, cm_ref, dt0_ref, dt1_ref, dtb_ref, al_ref, ds_ref, st_ref, dy_ref,
             dxs_ref, dbm_ref, dcm_ref, ddt_ref, stat_ref, dstate):
        @pl.when(pl.program_id(1) == 0)
        def _():
            dstate[...] = jnp.zeros_like(dstate)
            stat_ref[...] = jnp.zeros_like(stat_ref)

        bm, cm = _bf(bm_ref[...]), _bf(cm_ref[...])
        gm = _dot(cm, bm, NT)
        dbm = jnp.zeros((T, LANE), F32)
        dcm = jnp.zeros((T, LANE), F32)
        after_eq = _tri(T, lambda i, l: l >= i).astype(BF16)
        rowi = lax.broadcasted_iota(jnp.int32, (T, 1), 0)
        for i, dt_ref in enumerate((dt0_ref, dt1_ref)):
            xs = xs_ref[:, LANE * i:LANE * (i + 1)]
            dyp = dy_ref[:, LANE * i:LANE * (i + 1)]
            hprev = st_ref[i]
            dh = dstate[i]
            raw = dt_ref[...]
            lane, dt, a, acum, decay, dtc, xdt, ea, e_end, cd, r = _ssd_pair_common(
                raw, dtb_ref[i], al_ref[i], xs, cm, hprev, T)
            lane1 = lane[0:1, :]
            dsk = ds_ref[i]
            dskp = _pair(lane1, dsk[:, 0:1], dsk[:, 1:2])
            head = [lane < SSM_HEAD_DIM, lane >= SSM_HEAD_DIM]
            hsum = lambda v, j: jnp.sum(jnp.where(head[j], v, 0.0), axis=1, keepdims=True)
            dhb = _bf(dh)
            xdtb = _bf(xdt)
            dyb = _bf(dyp)
            z = xdt * e_end
            dz = _dot(bm, dhb, NT)
            dbm = dbm + _dot(_bf(z), dhb, NN)
            dxdt = dz * e_end
            de_e = dz * z
            drr = dyp * ea
            dea_ea = drr * r
            dcm = dcm + _dot(_bf(drr), _bf(hprev), NN)
            dstate[i] = cd * dh + _dot(_bf(drr), cm, TN)
            dcd_cd = cd * dh * hprev
            dgs = jnp.zeros((T, T), F32)
            da_cols = []
            for j in (0, 1):
                w = gm * decay[j]
                dw = _dot(_bf(jnp.where(head[j], dyp, 0.0)), xdtb, NT)
                dxdt = dxdt + jnp.where(head[j], _dot(_bf(w), dyb, TN), 0.0)
                dgs = dgs + dw * decay[j]
                dseg = dw * w
                col = jnp.sum(dseg, axis=1, keepdims=True) - jnp.sum(dseg.T, axis=1, keepdims=True)
                col = col + hsum(dea_ea, j) - hsum(de_e, j)
                sub = lax.broadcasted_iota(jnp.int32, (LANE, LANE), 0)
                in_head = (sub < SSM_HEAD_DIM) if j == 0 else (sub >= SSM_HEAD_DIM)
                end = jnp.sum(hsum(de_e, j), axis=0, keepdims=True) + jnp.sum(
                    jnp.sum(jnp.where(in_head, dcd_cd, 0.0), axis=1, keepdims=True), axis=0, keepdims=True)
                da_cols.append(col + jnp.where(rowi == T - 1, end, 0.0))
            dgb = _bf(dgs)
            dcm = dcm + _dot(dgb, bm, NN)
            dbm = dbm + _dot(dgb, cm, TN)
            dacum = jnp.where(lane == 0, da_cols[0], jnp.where(lane == 1, da_cols[1], 0.0))
            h1, h2, h3 = _split3(dacum)
            ddta = _dot(after_eq, h1, NN) + _dot(after_eq, h2, NN) + _dot(after_eq, h3, NN)
            dxs_ref[:, LANE * i:LANE * (i + 1)] = dskp * dyp + dxdt * dtc
            dx_x = dxdt * xs
            ddt = ddta * a + jnp.where(lane == 0, hsum(dx_x, 0), jnp.where(lane == 1, hsum(dx_x, 1), 0.0))
            ddraw = jnp.where(lane < 2, ddt * _sigmoid(raw + dtb_ref[i]), 0.0)
            ddt_ref[:, LANE * i:LANE * (i + 1)] = _bf(ddraw)
            dsum = jnp.sum(dyp * xs, axis=0, keepdims=True)
            d0 = jnp.sum(jnp.where(lane1 < SSM_HEAD_DIM, dsum, 0.0), axis=1, keepdims=True)
            d1 = jnp.sum(jnp.where(lane1 >= SSM_HEAD_DIM, dsum, 0.0), axis=1, keepdims=True)
            dd = jnp.where(lane1 == 0, d0, jnp.where(lane1 == 1, d1, 0.0))
            stat_ref[i, 0:1, :] += jnp.sum(ddraw, axis=0, keepdims=True)
            stat_ref[i, 1:2, :] += jnp.where(lane1 < 2, jnp.sum(ddta * dt, axis=0, keepdims=True) * a, 0.0)
            stat_ref[i, 2:3, :] += dd
        dbm_ref[...] = dbm
        dcm_ref[...] = dcm

    rb = lambda b: nb - 1 - b
    rows3 = pl.BlockSpec((2, 1, LANE), lambda g, b: (g, 0, 0))
    tok256 = pl.BlockSpec((T, 256), lambda g, b: (rb(b), g))
    tok128 = pl.BlockSpec((T, LANE), lambda g, b: (rb(b), g))
    return pl.pallas_call(
        body, grid=(2, nb),
        in_specs=[tok256, pl.BlockSpec((T, LANE), lambda g, b: (rb(b), 4 + g)), pl.BlockSpec((T, LANE), lambda g, b: (rb(b), 6 + g)),
                  pl.BlockSpec((T, LANE), lambda g, b: (rb(b), CB_DT + 2 * g)),
                  pl.BlockSpec((T, LANE), lambda g, b: (rb(b), CB_DT + 2 * g + 1)),
                  rows3, rows3, rows3,
                  pl.BlockSpec((None, None, 2, LANE, LANE), lambda g, b: (g, rb(b), 0, 0, 0)), tok256],
        out_specs=[tok256, tok128, tok128, tok256, pl.BlockSpec((2, 8, LANE), lambda g, b: (g, 0, 0))],
        out_shape=[_sds((S, 512), F32), _sds((S, 256), F32), _sds((S, 256), F32), _sds((S, 512), BF16), _sds((4, 8, LANE), F32)],
        scratch_shapes=[pltpu.VMEM((2, LANE, LANE), F32)], name=name,
        compiler_params=_cparams("parallel", "arbitrary"))(xact, xact, xact, proj, proj, dtb, alog, dskip, states, dy)


def gated_norm(ypre, proj, gain, *, name):
    S, W = ypre.shape
    tm = min(TOKEN_TILE, S)

    def body(y_ref, z_ref, g_ref, o_ref):
        z = z_ref[...]
        yg = y_ref[...] * (z * _sigmoid(z))
        o_ref[...] = _bf(yg * lax.rsqrt(jnp.mean(yg * yg, axis=-1, keepdims=True) + NORM_EPS) * g_ref[...])

    tile = pl.BlockSpec((tm, W), lambda i: (i, 0))
    return pl.pallas_call(
        body, grid=(S // tm,), in_specs=[tile, pl.BlockSpec((tm, W), lambda i: (i, CB_MZ // 4)), pl.BlockSpec((1, W), lambda i: (0, 0))],
        out_specs=tile, out_shape=_sds((S, W), BF16), name=name, compiler_params=_cparams("parallel"))(ypre, proj, gain)


def gated_norm_bwd(ypre, proj, gain, dycat, *, name):
    S, W = ypre.shape
    tm = min(TOKEN_TILE, S)

    def body(y_ref, z_ref, g_ref, dy_ref, dyp_ref, dz_ref, st_ref):
        z = z_ref[...]
        y = y_ref[...]
        sg = _sigmoid(z)
        sz = z * sg
        yg = y * sz
        r = lax.rsqrt(jnp.mean(yg * yg, axis=-1, keepdims=True) + NORM_EPS)
        yhat = yg * r
        dyo = dy_ref[...]

        @pl.when(pl.program_id(0) == 0)
        def _():
            st_ref[...] = jnp.zeros_like(st_ref)

        st_ref[0:1, :] += jnp.sum(dyo * yhat, axis=0, keepdims=True)
        dyhat = dyo * g_ref[...]
        dyg = r * (dyhat - yhat * jnp.mean(dyhat * yhat, axis=-1, keepdims=True))
        dyp_ref[...] = dyg * sz
        dz_ref[...] = _bf(dyg * y * (sg * (1.0 + z * (1.0 - sg))))

    tile = pl.BlockSpec((tm, W), lambda i: (i, 0))
    return pl.pallas_call(
        body, grid=(S // tm,),
        in_specs=[tile, pl.BlockSpec((tm, W), lambda i: (i, CB_MZ // 4)), pl.BlockSpec((1, W), lambda i: (0, 0)),
                  pl.BlockSpec((tm, W), lambda i: (i, 2))],
        out_specs=[tile, tile, pl.BlockSpec((8, W), lambda i: (0, 0))],
        out_shape=[_sds((S, W), F32), _sds((S, W), BF16), _sds((8, W), F32)], name=name,
        compiler_params=_cparams("arbitrary"))(ypre, proj, gain, dycat)


def ada_mod(c_all, w, bias, *, name):
    M, K = c_all.shape
    N = w.shape[1]
    tn = _tile(N, 512)

    def body(c_ref, w_ref, b_ref, o_ref, cond_ref):
        cv = c_ref[...]
        cond = cv * _sigmoid(cv)
        cond_ref[...] = cond
        o_ref[...] = _dot(_bf(cond), _bf(w_ref[...]), NN) + b_ref[...]

    return pl.pallas_call(
        body, grid=(N // tn,),
        in_specs=[pl.BlockSpec((M, K), lambda j: (0, 0)), pl.BlockSpec((K, tn), lambda j: (0, j)), pl.BlockSpec((1, tn), lambda j: (0, j))],
        out_specs=[pl.BlockSpec((M, tn), lambda j: (0, j)), pl.BlockSpec((M, K), lambda j: (0, 0))],
        out_shape=[_sds((M, N), F32), _sds((M, K), F32)], name=name, compiler_params=_cparams("arbitrary"))(c_all, w, bias)


def _adamw(g, w, m, v):
    m = ADAM_B1 * m + (1.0 - ADAM_B1) * g
    v = ADAM_B2 * v + (1.0 - ADAM_B2) * (g * g)
    m_hat = m / (1.0 - ADAM_B1 ** ADAM_STEP)
    v_hat = v / (1.0 - ADAM_B2 ** ADAM_STEP)
    return -ADAM_LR * (m_hat / (jnp.sqrt(v_hat) + ADAM_EPS) + ADAM_WD * w), m, v


def adamw_parts(parts, w, m, v, *, name):
    P, R, C = parts.shape
    tr = _tile(R, 512, 256, 160, 128, 64, 32, 16)

    def body(p_ref, w_ref, m_ref, v_ref, g_ref, d_ref, mo_ref, vo_ref):
        g = p_ref[0].astype(F32)
        for j in range(1, P):
            g = g + p_ref[j].astype(F32)
        g_ref[...] = g
        d_ref[...], mo_ref[...], vo_ref[...] = _adamw(g, w_ref[...], m_ref[...], v_ref[...])

    tile = pl.BlockSpec((tr, C), lambda i: (i, 0))
    return pl.pallas_call(
        body, grid=(R // tr,), in_specs=[pl.BlockSpec((P, tr, C), lambda i: (0, i, 0)), tile, tile, tile],
        out_specs=[tile] * 4, out_shape=[_sds((R, C), F32)] * 4, name=name, compiler_params=_cparams("parallel"))(parts, w, m, v)


def ada_adamw(cond_t, dmod, w, m, v, *, name):
    D, N = w.shape
    tr = _tile(D, 256)

    def body(c_ref, d_ref, w_ref, m_ref, v_ref, g_ref, dl_ref, mo_ref, vo_ref):
        cc = c_ref[...]
        dd = d_ref[...]
        g = cc[:, 0:1] * dd[0:1, :]
        for b in range(1, N_DEV):
            g = g + cc[:, b:b + 1] * dd[b:b + 1, :]
        g_ref[...] = g
        dl_ref[...], mo_ref[...], vo_ref[...] = _adamw(g, w_ref[...], m_ref[...], v_ref[...])

    tile = pl.BlockSpec((tr, N), lambda i: (i, 0))
    return pl.pallas_call(
        body, grid=(D // tr,), in_specs=[pl.BlockSpec((tr, N_DEV), lambda i: (i, 0)), pl.BlockSpec((N_DEV, N), lambda i: (0, 0)), tile, tile, tile],
        out_specs=[tile] * 4, out_shape=[_sds((D, N), F32)] * 4, name=name, compiler_params=_cparams("parallel"))(cond_t, dmod, w, m, v)


def _my_place():
    mx, my, mc = lax.axis_index("x"), lax.axis_index("y"), lax.axis_index("c")
    return mx, my, mc, 4 * mx + 2 * my + mc


def _peer(mx, my, mc, k):
    px = 1 - mx if (k >> 2) & 1 else mx
    py = 1 - my if (k >> 1) & 1 else my
    pc = 1 - mc if k & 1 else mc
    return (px, py, pc), 4 * px + 2 * py + pc


def _comm_call(body, x, out_shape, space, name):
    spec = pl.BlockSpec(memory_space=space)
    return pl.pallas_call(
        body, in_specs=[spec], out_specs=spec, out_shape=out_shape,
        scratch_shapes=[pltpu.SemaphoreType.DMA((N_DEV - 1,)), pltpu.SemaphoreType.DMA((N_DEV - 1,)), pltpu.SemaphoreType.DMA(())],
        name=name, compiler_params=pltpu.CompilerParams(has_side_effects=True, vmem_limit_bytes=VMEM_LIMIT))(x)


def allgather(x, *, in_vmem, name):
    def body(x_ref, out_ref, send_sems, recv_sems, local_sem):
        mx, my, mc, me = _my_place()
        mine = pltpu.make_async_copy(x_ref, out_ref.at[me], local_sem)
        mine.start()
        copies = []
        for k in range(1, N_DEV):
            peer, _ = _peer(mx, my, mc, k)
            cp = pltpu.make_async_remote_copy(src_ref=x_ref, dst_ref=out_ref.at[me], send_sem=send_sems.at[k - 1],
                                              recv_sem=recv_sems.at[k - 1], device_id=peer, device_id_type=pl.DeviceIdType.MESH)
            cp.start()
            copies.append(cp)
        for cp in copies:
            cp.wait()
        mine.wait()

    return _comm_call(body, x, _sds((N_DEV,) + x.shape, x.dtype), pltpu.VMEM if in_vmem else pltpu.HBM, name)


def allgather_two_level(x, *, name):
    def body(x_ref, out_ref, send_sems, recv_sems, local_sem):
        mx, my, mc, _ = _my_place()
        me, sibling = (mx, my, mc), (mx, my, 1 - mc)
        chips = [(1 - mx, my), (mx, 1 - my), (1 - mx, 1 - my)]

        def copy(k, block, to, src=None):
            slot = out_ref.at[4 * block[0] + 2 * block[1] + block[2]]
            return pltpu.make_async_remote_copy(src_ref=slot if src is None else src, dst_ref=slot, send_sem=send_sems.at[k],
                                                recv_sem=recv_sems.at[k], device_id=to, device_id_type=pl.DeviceIdType.MESH)

        mine = pltpu.make_async_copy(x_ref, out_ref.at[4 * mx + 2 * my + mc], local_sem)
        mine.start()
        first = [copy(0, me, sibling, src=x_ref)] + [copy(1 + j, me, (*chip, mc), src=x_ref) for j, chip in enumerate(chips)]
        for cp in first:
            cp.start()
        passed = [copy(4 + j, (*chip, mc), sibling) for j, chip in enumerate(chips)]
        for j, chip in enumerate(chips):
            copy(1 + j, (*chip, mc), me).wait_recv()
            passed[j].start()
        copy(0, sibling, me).wait_recv()
        for j, chip in enumerate(chips):
            copy(4 + j, (*chip, 1 - mc), me).wait_recv()
        for cp in first + passed:
            cp.wait_send()
        mine.wait()

    return _comm_call(body, x, _sds((N_DEV,) + x.shape, x.dtype), pltpu.HBM, name)


def sibling_exchange(send, *, name):
    n_chip, _, R, C = send.shape

    def body(s_ref, r_ref, send_sems, recv_sems, local_sem):
        mx, my, mc, _ = _my_place()
        copies = []
        for q in range(n_chip):
            cp = pltpu.make_async_remote_copy(src_ref=s_ref.at[q, 1 - mc], dst_ref=r_ref.at[q], send_sem=send_sems.at[q],
                                              recv_sem=recv_sems.at[q], device_id=(mx, my, 1 - mc), device_id_type=pl.DeviceIdType.MESH)
            cp.start()
            copies.append(cp)
        for cp in copies:
            cp.wait()

    return _comm_call(body, send, _sds((n_chip, R, C), send.dtype), pltpu.HBM, name)


def chip_exchange(send, *, name):
    def body(s_ref, r_ref, send_sems, recv_sems, local_sem):
        mx, my, mc, _ = _my_place()
        mine = pltpu.make_async_copy(s_ref.at[2 * mx + my], r_ref.at[2 * mx + my], local_sem)
        mine.start()
        copies = []
        for k in range(1, 4):
            px = 1 - mx if (k >> 1) & 1 else mx
            py = 1 - my if k & 1 else my
            cp = pltpu.make_async_remote_copy(src_ref=s_ref.at[2 * px + py], dst_ref=r_ref.at[2 * mx + my], send_sem=send_sems.at[k - 1],
                                              recv_sem=recv_sems.at[k - 1], device_id=(px, py, mc), device_id_type=pl.DeviceIdType.MESH)
            cp.start()
            copies.append(cp)
        for cp in copies:
            cp.wait()
        mine.wait()

    return _comm_call(body, send, _sds(send.shape, send.dtype), pltpu.HBM, name)


def add_partials(a, b, *, name):
    P, R, C = a.shape
    tr = _tile(R, 512, 256, 160, 128, 64, 32, 16)

    def body(a_ref, b_ref, o_ref):
        o_ref[...] = _bf(a_ref[...].astype(F32) + b_ref[...].astype(F32))

    tile = pl.BlockSpec((P, tr, C), lambda i: (0, i, 0))
    return pl.pallas_call(body, grid=(R // tr,), in_specs=[tile, tile], out_specs=tile, out_shape=_sds((P, R, C), BF16), name=name,
                          compiler_params=_cparams("parallel"))(a, b)


def _rows128(a):
    f = a.reshape(-1)
    n = -(-f.shape[0] // (16 * LANE)) * (16 * LANE)
    return jnp.pad(f, (0, n - f.shape[0])).reshape(-1, LANE)


PACK_ROWS = 512


def _pad_rows(buf):
    r = buf.shape[-2]
    pad = -r % PACK_ROWS
    return jnp.pad(buf, [(0, 0)] * (buf.ndim - 2) + [(0, pad), (0, 0)])


def _pack(arrays):
    parts = [_rows128(a) for a in arrays]
    offs = np.cumsum([0] + [p.shape[0] for p in parts])
    return _pad_rows(jnp.concatenate(parts, axis=0)), [int(o) for o in offs]


def _unpack(buf, offs, shapes):
    lead = buf.shape[:-2]
    out = []
    for o, shp in zip(offs, shapes):
        n = int(np.prod(shp))
        rows = -(-n // LANE)
        seg = buf[..., o:o + rows, :].reshape(lead + (rows * LANE,))[..., :n]
        out.append(seg.reshape(lead + tuple(shp)))
    return out


def _pad_w_in_t(w_t):
    D = w_t.shape[1]
    dt = jnp.pad(w_t[IN_MAIN:].reshape(4, 2, D), ((0, 0), (0, LANE - 2), (0, 0)))
    return jnp.concatenate([w_t[:IN_MAIN], dt.reshape(4 * LANE, D)], axis=0)


def _unpad_w_in_t(g_t):
    D = g_t.shape[1]
    dt = g_t[IN_MAIN:].reshape(4, LANE, D)[:, :2].reshape(SSM_HEADS, D)
    return jnp.concatenate([g_t[:IN_MAIN], dt], axis=0)


def _piece_rows(n, shard_shape):
    r = shard_shape[0] if n in ROW_SHARDED else shard_shape[1]
    return r, -(-r // 16) * 16


def _to_piece(n, shard):
    t = shard if n in ROW_SHARDED else shard.T
    return jnp.pad(t, ((0, -t.shape[0] % 16), (0, 0)))


def _from_piece(n, piece, shard_shape):
    r, _ = _piece_rows(n, shard_shape)
    return piece[:r] if n in ROW_SHARDED else piece[:r].T


def _pair_rows(p):
    return jnp.pad(p.reshape(4, 1, 2), ((0, 0), (0, 0), (0, LANE - 2)))


def _row(v):
    return v.reshape(1, -1)


def _ffn_fwd(h, gain, mod3, wg, wu, wd, tag):
    shift, scale, gate = mod3
    u = norm_mod(h, gain, shift, scale, name=tag + "_norm")
    a, b, act = ffn_up(u, wg, wu, name=tag + "_up")
    hn, out = matmul_resid(act, wd, h, gate, 0.5, name=tag + "_down")
    return hn, (h, u, a, b, act, out)


def _wgrad(a, b, name):
    return matmul(a, b, ta=True, tm=_tile(a.shape[1], 1408, 1536, 512), tn=b.shape[1], tk=WGRAD_TOKENS, out_dtype=BF16, name=name)


def _ffn_bwd(dh, saved, gain, mod3, wg, wu, wd, tag):
    h, u, a, b, act, out = saved
    _, scale, gate = mod3
    dout, gst = gate_bwd(dh, out, gate, 0.5, name=tag + "_gate_bwd")
    da, db = ffn_dact(dout, wd, a, b, name=tag + "_dact")
    dh_prev, nst = dgrad_norm_bwd([da, db], [wg, wu], h, gain, scale, dh, name=tag + "_dgrad")
    grads = (_wgrad(da, u, tag + "_dwg"), _wgrad(db, u, tag + "_dwu"), _wgrad(act, dout, tag + "_dwd"))
    return dh_prev, grads, nst[0], [nst[1], nst[2], gst[0]]


def _mix_fwd(h, p, mod3, w_in, w_out, cos, sin, tag):
    shift, scale, gate = mod3
    u = norm_mod(h, p["norm_mix"], shift, scale, name=tag + "_norm")
    proj = matmul(u, w_in, tb=True, tm=BIG_TOKEN_TILE, tn=512, tk=D_MODEL, name=tag + "_proj")
    y_ret, ypre_ret, st_ret = ret_fwd(proj, p["ret_gn"], cos, sin, name=tag + "_ret")
    y_sb, sb_cin = sb_fwd(proj, name=tag + "_sb")
    pre, xact = conv_fwd(proj, p["conv_w"], p["conv_b"], name=tag + "_conv")
    ypre_ssm, st_ssm = ssd_fwd(xact, proj, p["dt_bias"], p["a_log"], p["d_skip"], name=tag + "_ssd")
    y_ssm = gated_norm(ypre_ssm, proj, p["ssm_norm"], name=tag + "_gnorm")
    ycat = jnp.concatenate([y_ret, y_sb.astype(BF16), y_ssm], axis=1)
    hn, mixed = matmul_resid(ycat, w_out, h, gate, 1.0, name=tag + "_out")
    return hn, (h, u, proj, ypre_ret, st_ret, sb_cin, pre, xact, ypre_ssm, st_ssm, ycat, mixed)


def _mix_bwd(dh, saved, p, mod3, w_in, w_out, cos, sin, tag):
    h, u, proj, ypre_ret, st_ret, sb_cin, pre, xact, ypre_ssm, st_ssm, ycat, mixed = saved
    _, scale, gate = mod3
    dmixed, gst = gate_bwd(dh, mixed, gate, 1.0, name=tag + "_gate_bwd")
    dycat = matmul(dmixed, w_out, tb=True, tm=BIG_TOKEN_TILE, tn=512, tk=D_MODEL, name=tag + "_dycat")
    dw_out = _wgrad(ycat, dmixed, tag + "_dw_out")
    dq, dk, dv, dg, rst = ret_bwd(proj, p["ret_gn"], cos, sin, ypre_ret, st_ret, dycat, name=tag + "_ret_bwd")
    dsq, dsk, dsv = sb_bwd(proj, sb_cin, dycat, name=tag + "_sb_bwd")
    dypre, dz, nst2 = gated_norm_bwd(ypre_ssm, proj, p["ssm_norm"], dycat, name=tag + "_gnorm_bwd")
    dxs, dbm, dcm, ddt, sst = ssd_bwd(xact, proj, p["dt_bias"], p["a_log"], p["d_skip"], st_ssm, dypre, name=tag + "_ssd_bwd")
    dact = jnp.concatenate([dxs, dbm, dcm], axis=1)
    dxbc, cst = conv_bwd(proj, pre, dact, p["conv_w"], name=tag + "_conv_bwd")
    dproj = jnp.concatenate([dq, dk, dv, dg, dsq, dsk.astype(BF16), dsv.astype(BF16), dz, dxbc, ddt], axis=1)
    dh_prev, nst = dgrad_norm_bwd([dproj], [w_in], h, p["norm_mix"], scale, dh, name=tag + "_dgrad")
    dw_in = _wgrad(dproj, u, tag + "_dw_in")
    small = dict(norm_mix=nst[0], ret_gn=rst[0], ssm_norm=nst2[0], conv_w=cst[0:4], conv_b=cst[4],
                 dt_bias=sst[:, 0, :2].reshape(SSM_HEADS), a_log=sst[:, 1, :2].reshape(SSM_HEADS), d_skip=sst[:, 2, :2].reshape(SSM_HEADS))
    return dh_prev, dw_in, dw_out, small, [nst[1], nst[2], gst[0]]


BIG = ("ffn1_wg", "ffn1_wu", "ffn1_wd", "w_in", "w_out", "ffn2_wg", "ffn2_wu", "ffn2_wd")
ROW_SHARDED = ("ffn1_wd", "w_out", "ffn2_wd")
SMALL = ("ada_b", "norm_ffn1", "norm_mix", "conv_b", "dt_bias", "a_log", "d_skip", "ret_gn", "ssm_norm", "norm_ffn2",
         "final_ada_b", "final_norm")
NAMES = ("ada_w", "ada_b", "norm_ffn1", "ffn1_wg", "ffn1_wu", "ffn1_wd", "norm_mix", "w_in", "conv_w", "conv_b", "dt_bias", "a_log",
         "d_skip", "ret_gn", "ssm_norm", "w_out", "norm_ffn2", "ffn2_wg", "ffn2_wu", "ffn2_wd", "final_ada_w", "final_ada_b", "final_norm")


def kernel(x, c, ada_w, ada_b, norm_ffn1, ffn1_wg, ffn1_wu, ffn1_wd, norm_mix, w_in, conv_w, conv_b, dt_bias, a_log, d_skip, ret_gn, ssm_norm, w_out, norm_ffn2, ffn2_wg, ffn2_wu, ffn2_wd, final_ada_w, final_ada_b, final_norm, loss_target, m_ada_w, m_ada_b, m_norm_ffn1, m_ffn1_wg, m_ffn1_wu, m_ffn1_wd, m_norm_mix, m_w_in, m_conv_w, m_conv_b, m_dt_bias, m_a_log, m_d_skip, m_ret_gn, m_ssm_norm, m_w_out, m_norm_ffn2, m_ffn2_wg, m_ffn2_wu, m_ffn2_wd, m_final_ada_w, m_final_ada_b, m_final_norm, v_ada_w, v_ada_b, v_norm_ffn1, v_ffn1_wg, v_ffn1_wu, v_ffn1_wd, v_norm_mix, v_w_in, v_conv_w, v_conv_b, v_dt_bias, v_a_log, v_d_skip, v_ret_gn, v_ssm_norm, v_w_out, v_norm_ffn2, v_ffn2_wg, v_ffn2_wu, v_ffn2_wd, v_final_ada_w, v_final_ada_b, v_final_norm):
    W = dict(ada_w=ada_w, ada_b=ada_b, norm_ffn1=norm_ffn1, ffn1_wg=ffn1_wg, ffn1_wu=ffn1_wu, ffn1_wd=ffn1_wd, norm_mix=norm_mix,
             w_in=w_in, conv_w=conv_w, conv_b=conv_b, dt_bias=dt_bias, a_log=a_log, d_skip=d_skip, ret_gn=ret_gn, ssm_norm=ssm_norm,
             w_out=w_out, norm_ffn2=norm_ffn2, ffn2_wg=ffn2_wg, ffn2_wu=ffn2_wu, ffn2_wd=ffn2_wd, final_ada_w=final_ada_w,
             final_ada_b=final_ada_b, final_norm=final_norm)
    M1 = dict(ada_w=m_ada_w, ada_b=m_ada_b, norm_ffn1=m_norm_ffn1, ffn1_wg=m_ffn1_wg, ffn1_wu=m_ffn1_wu, ffn1_wd=m_ffn1_wd,
              norm_mix=m_norm_mix, w_in=m_w_in, conv_w=m_conv_w, conv_b=m_conv_b, dt_bias=m_dt_bias, a_log=m_a_log, d_skip=m_d_skip,
              ret_gn=m_ret_gn, ssm_norm=m_ssm_norm, w_out=m_w_out, norm_ffn2=m_norm_ffn2, ffn2_wg=m_ffn2_wg, ffn2_wu=m_ffn2_wu,
              ffn2_wd=m_ffn2_wd, final_ada_w=m_final_ada_w, final_ada_b=m_final_ada_b, final_norm=m_final_norm)
    V2 = dict(ada_w=v_ada_w, ada_b=v_ada_b, norm_ffn1=v_norm_ffn1, ffn1_wg=v_ffn1_wg, ffn1_wu=v_ffn1_wu, ffn1_wd=v_ffn1_wd,
              norm_mix=v_norm_mix, w_in=v_w_in, conv_w=v_conv_w, conv_b=v_conv_b, dt_bias=v_dt_bias, a_log=v_a_log, d_skip=v_d_skip,
              ret_gn=v_ret_gn, ssm_norm=v_ssm_norm, w_out=v_w_out, norm_ffn2=v_norm_ffn2, ffn2_wg=v_ffn2_wg, ffn2_wu=v_ffn2_wu,
              ffn2_wd=v_ffn2_wd, final_ada_w=v_final_ada_w, final_ada_b=v_final_ada_b, final_norm=v_final_norm)
    D = D_MODEL
    S = x.shape[1]
    me = 4 * lax.axis_index("x") + 2 * lax.axis_index("y") + lax.axis_index("c")
    n_mod = ada_w.shape[2]
    n_fmod = final_ada_w.shape[1]

    c_all = allgather(jnp.broadcast_to(c, (8, D)), in_vmem=True, name="gather_c")[:, 0, :]
    ada_cols = jnp.concatenate([ada_w[0], ada_w[1], final_ada_w], axis=1)
    ada_bias = jnp.concatenate([lax.dynamic_slice(ada_b, (0, me * n_mod), (DEPTH, n_mod)).reshape(1, -1),
                                lax.dynamic_slice(final_ada_b, (me * n_fmod,), (n_fmod,)).reshape(1, -1)], axis=1)
    mod_sh, cond = ada_mod(jnp.pad(c_all, ((0, 8), (0, 0))), ada_cols, ada_bias, name="ada_mod")
    n_cols = mod_sh.shape[1]
    small_in = jnp.concatenate([mod_sh[:8], jnp.pad(conv_w.reshape(8, LANE), ((0, 0), (0, n_cols - LANE)))], axis=0)
    small_g = allgather(small_in, in_vmem=True, name="gather_mod")
    mod_rows = lax.dynamic_index_in_dim(small_g[:, :8, :], me, axis=1, keepdims=False)
    mod = [mod_rows[:, l * n_mod:(l + 1) * n_mod].reshape(9, D) for l in range(DEPTH)]
    fmod = mod_rows[:, DEPTH * n_mod:].reshape(2, D)
    conv_w_full = small_g[:, 8:, :LANE].reshape(N_DEV, DEPTH, SSM_CONV, LANE).transpose(1, 2, 0, 3).reshape(DEPTH, SSM_CONV, 8 * LANE)

    order = [(l, n) for l in range(DEPTH) for n in BIG]
    rows = {n: _piece_rows(n, W[n].shape[1:]) for n in BIG}
    offs, o = {}, 0
    for l, n in order:
        offs[l, n] = o
        o += rows[n][1]
    pack_of = lambda src, dtype: jnp.concatenate([_to_piece(n, src[n][l]).astype(dtype) for l, n in order], axis=0)
    wgath = allgather_two_level(pack_of(W, BF16), name="gather_weights")
    full = {n: [None] * DEPTH for n in BIG}
    for l, n in order:
        r = rows[n][0]
        full[n][l] = wgath[:, offs[l, n]:offs[l, n] + r, :].reshape(N_DEV * r, D)
    full["w_in"] = [_pad_w_in_t(w) for w in full["w_in"]]

    cos, sin = _rope_tables(S)
    h = x[0]
    target = loss_target[0]
    layer_p = []
    for l in range(DEPTH):
        layer_p.append(dict(norm_ffn1=_row(norm_ffn1[l]), norm_mix=_row(norm_mix[l]), norm_ffn2=_row(norm_ffn2[l]),
                            ret_gn=_row(ret_gn[l]), ssm_norm=_row(ssm_norm[l]), conv_w=conv_w_full[l], conv_b=_row(conv_b[l]),
                            dt_bias=_pair_rows(dt_bias[l]), a_log=_pair_rows(a_log[l]), d_skip=_pair_rows(d_skip[l])))
    mods = [[[_row(mod[l][3 * s + k]) for k in range(3)] for s in range(3)] for l in range(DEPTH)]

    saved = []
    for l in range(DEPTH):
        p = layer_p[l]
        h, s1 = _ffn_fwd(h, p["norm_ffn1"], mods[l][0], full["ffn1_wg"][l], full["ffn1_wu"][l], full["ffn1_wd"][l], f"l{l}_ffn1")
        h, s2 = _mix_fwd(h, p, mods[l][1], full["w_in"][l], full["w_out"][l], cos, sin, f"l{l}_mix")
        h, s3 = _ffn_fwd(h, p["norm_ffn2"], mods[l][2], full["ffn2_wg"][l], full["ffn2_wu"][l], full["ffn2_wd"][l], f"l{l}_ffn2")
        saved.append((s1, s2, s3))

    dh, fst = final_loss_bwd(h, _row(final_norm), _row(fmod[0]), _row(fmod[1]), target, name="final")
    big_g = {n: [None] * DEPTH for n in BIG}
    small_g_l = [None] * DEPTH
    dmod = [None] * DEPTH
    for l in reversed(range(DEPTH)):
        p = layer_p[l]
        s1, s2, s3 = saved[l]
        dh, (g2g, g2u, g2d), gn2, dm2 = _ffn_bwd(dh, s3, p["norm_ffn2"], mods[l][2], full["ffn2_wg"][l], full["ffn2_wu"][l],
                                                 full["ffn2_wd"][l], f"l{l}_ffn2")
        dh, gw_in, gw_out, sm, dm1 = _mix_bwd(dh, s2, p, mods[l][1], full["w_in"][l], full["w_out"][l], cos, sin, f"l{l}_mix")
        dh, (g1g, g1u, g1d), gn1, dm0 = _ffn_bwd(dh, s1, p["norm_ffn1"], mods[l][0], full["ffn1_wg"][l], full["ffn1_wu"][l],
                                                 full["ffn1_wd"][l], f"l{l}_ffn1")
        for n, g in zip(BIG, (g1g, g1u, g1d, gw_in, gw_out, g2g, g2u, g2d)):
            big_g[n][l] = g
        sm["norm_ffn1"], sm["norm_ffn2"] = gn1, gn2
        small_g_l[l] = sm
        dmod[l] = jnp.concatenate(dm0 + dm1 + dm2, axis=0)
    grad_x = dh[None]

    def send_piece(l, n):
        g = _unpad_w_in_t(big_g[n][l]) if n == "w_in" else big_g[n][l]
        r, rp = rows[n]
        return jnp.pad(g.reshape(N_DEV, r, D), ((0, 0), (0, rp - r), (0, 0)))

    spack = jnp.concatenate([send_piece(l, n) for l, n in order], axis=1)
    by_core = spack.reshape((N_DEV // 2, 2) + spack.shape[1:])
    from_sibling = sibling_exchange(by_core, name="exchange_sibling")
    own = lax.dynamic_index_in_dim(by_core, lax.axis_index("c"), axis=1, keepdims=False)
    rpack = chip_exchange(add_partials(own, from_sibling, name="add_sibling"), name="exchange_chips")
    outs = adamw_parts(rpack, *[pack_of(src, F32) for src in (W, M1, V2)], name="adamw_big")
    big_out = [{n: jnp.stack([_from_piece(n, o[offs[l, n]:offs[l, n] + rows[n][1]], W[n].shape[1:]) for l in range(DEPTH)])
                for n in BIG} for o in outs]

    stack2 = lambda key: jnp.stack([small_g_l[l][key] for l in range(DEPTH)])
    pieces = [("loss", fst[3, 0:1]), ("ada_b", jnp.stack(dmod)), ("final_ada_b", jnp.concatenate([fst[1], fst[2]])),
              ("norm_ffn1", stack2("norm_ffn1")), ("norm_mix", stack2("norm_mix")), ("norm_ffn2", stack2("norm_ffn2")),
              ("conv_w", stack2("conv_w")), ("conv_b", stack2("conv_b")), ("dt_bias", stack2("dt_bias")), ("a_log", stack2("a_log")),
              ("d_skip", stack2("d_skip")), ("ret_gn", stack2("ret_gn")), ("ssm_norm", stack2("ssm_norm")), ("final_norm", fst[0])]
    names = [n for n, _ in pieces]
    shapes = [a.shape for _, a in pieces]
    ppack, poffs = _pack([a for _, a in pieces])
    pg = allgather(ppack, in_vmem=True, name="gather_small")
    zero_like = lambda n, a: jnp.zeros(a.shape, F32)
    spacks = [_pack([(src[n] if n in SMALL else zero_like(n, a)) for n, a in pieces])[0] for src in (W, M1, V2)]
    souts = adamw_parts(pg, *spacks, name="adamw_small")
    small_out = [dict(zip(names, _unpack(o, poffs, shapes))) for o in souts]
    loss = small_out[0]["loss"][0]

    gathered = dict(zip(names, _unpack(pg, poffs, shapes)))
    conv_parts = lax.dynamic_slice_in_dim(gathered["conv_w"], me * LANE, LANE, axis=3).reshape(N_DEV, DEPTH * SSM_CONV, LANE)
    conv_out = [o.reshape(conv_w.shape) for o in adamw_parts(conv_parts, conv_w.reshape(-1, LANE), m_conv_w.reshape(-1, LANE),
                                                              v_conv_w.reshape(-1, LANE), name="adamw_conv_w")]
    cond_t = cond[:8].T
    ada_out = []
    for l in range(DEPTH):
        dsel = lax.dynamic_slice_in_dim(gathered["ada_b"][:, l, :], me * n_mod, n_mod, axis=1)
        ada_out.append(ada_adamw(cond_t, dsel, ada_w[l], m_ada_w[l], v_ada_w[l], name=f"adamw_ada_w{l}"))
    ada_out = [jnp.stack([ada_out[l][k] for l in range(DEPTH)]) for k in range(4)]
    fsel = lax.dynamic_slice_in_dim(gathered["final_ada_b"].reshape(N_DEV, 2 * D), me * n_fmod, n_fmod, axis=1)
    fada_out = ada_adamw(cond_t, fsel, final_ada_w, m_final_ada_w, v_final_ada_w, name="adamw_final_ada_w")

    def pick(k, n):
        if n in BIG:
            return big_out[k][n]
        if n == "ada_w":
            return ada_out[k]
        if n == "final_ada_w":
            return fada_out[k]
        if n == "conv_w":
            return conv_out[k]
        return small_out[k][n]

    return (loss, grad_x) + tuple(pick(k, n) for k in range(4) for n in NAMES)
```

```python
import functools
import math

import numpy as np
import jax
import jax.numpy as jnp
from jax import lax
from jax.experimental import pallas as pl
from jax.experimental.pallas import tpu as pltpu

F32 = jnp.float32
BF16 = jnp.bfloat16

D_MODEL = 1024
DEPTH = 2
RET_HEADS = 4
HEAD_DIM = 128
SSM_HEADS = 8
SSM_HEAD_DIM = 64
SSM_STATE = 128
SSM_CONV = 4
D_FF = 2816
ROPE_BASE = 10000.0
NORM_EPS = 1e-6
MIX_W = 1536
IN_W = 5128
IN_MAIN = 5120
IN_PAD = 5632
N_DEV = 8
LANE = 128

ADAM_LR = 0.001
ADAM_B1 = 0.9
ADAM_B2 = 0.999
ADAM_EPS = 1e-08
ADAM_WD = 0.01
ADAM_STEP = 10

TOKEN_TILE = 512
WGRAD_TOKENS = 2048
BIG_TOKEN_TILE = 2048
SEQ_BLOCK = 256
VMEM_LIMIT = 56 << 20
SB_SKIP = 120.0
SB_UNVISITED = -1e30

CB_RQ, CB_RK, CB_RV, CB_RG = 0, 4, 8, 12
CB_SQ, CB_SK, CB_SV = 16, 20, 24
CB_MZ, CB_XS, CB_BM, CB_CM, CB_DT = 28, 32, 36, 38, 40


def _cparams(*sem):
    return pltpu.CompilerParams(dimension_semantics=sem, vmem_limit_bytes=VMEM_LIMIT)


def _sds(shape, dtype):
    return jax.ShapeDtypeStruct(tuple(shape), dtype)


def _tile(n, *prefs):
    for p in prefs:
        if n % p == 0:
            return p
    return n


def _dot(a, b, dims):
    return lax.dot_general(a, b, (dims, ((), ())), preferred_element_type=F32)


NN = ((1,), (0,))
NT = ((1,), (1,))
TN = ((0,), (0,))


def _bf(x):
    return x.astype(BF16)


def _sigmoid(x):
    return jax.nn.sigmoid(x)


def _split2(x):
    hi = x.astype(BF16)
    lo = (x - hi.astype(F32)).astype(BF16)
    return hi, lo


def _split3(x):
    hi = x.astype(BF16)
    r = x - hi.astype(F32)
    mid = r.astype(BF16)
    lo = (r - mid.astype(F32)).astype(BF16)
    return hi, mid, lo


def matmul(a, b, *, ta=False, tb=False, tm=512, tn=512, tk=512, out_dtype=F32, name):
    M, K = (a.shape[1], a.shape[0]) if ta else a.shape
    N = b.shape[0] if tb else b.shape[1]
    tm, tn, tk = min(tm, M), min(tn, N), min(tk, K)
    assert M % tm == 0 and N % tn == 0 and K % tk == 0, (name, M, N, K, tm, tn, tk)
    nk = K // tk
    a_spec = pl.BlockSpec((tk, tm), lambda i, j, k: (k, i)) if ta else pl.BlockSpec((tm, tk), lambda i, j, k: (i, k))
    b_spec = pl.BlockSpec((tn, tk), lambda i, j, k: (j, k)) if tb else pl.BlockSpec((tk, tn), lambda i, j, k: (k, j))
    dims = ((0 if ta else 1,), (1 if tb else 0,))

    def body(a_ref, b_ref, o_ref, acc_ref):
        k = pl.program_id(2)
        p = _dot(_bf(a_ref[...]), _bf(b_ref[...]), dims)

        @pl.when(k == 0)
        def _():
            acc_ref[...] = p

        @pl.when(k > 0)
        def _():
            acc_ref[...] += p

        @pl.when(k == nk - 1)
        def _():
            o_ref[...] = acc_ref[...].astype(out_dtype)

    return pl.pallas_call(
        body, grid=(M // tm, N // tn, nk), in_specs=[a_spec, b_spec],
        out_specs=pl.BlockSpec((tm, tn), lambda i, j, k: (i, j)), out_shape=_sds((M, N), out_dtype),
        scratch_shapes=[pltpu.VMEM((tm, tn), F32)], name=name,
        compiler_params=_cparams("parallel", "parallel", "arbitrary"))(a, b)


def matmul_resid(a, w, h, gate, factor, *, name):
    M, K = a.shape
    N = w.shape[1]
    tm = min(TOKEN_TILE, M)

    def body(a_ref, w_ref, h_ref, g_ref, hn_ref, o_ref):
        out = _dot(a_ref[...], w_ref[...], NN)
        o_ref[...] = out
        hn_ref[...] = h_ref[...] + (factor * (1.0 + g_ref[...])) * out

    mn = pl.BlockSpec((tm, N), lambda i: (i, 0))
    return pl.pallas_call(
        body, grid=(M // tm,),
        in_specs=[pl.BlockSpec((tm, K), lambda i: (i, 0)), pl.BlockSpec((K, N), lambda i: (0, 0)), mn,
                  pl.BlockSpec((1, N), lambda i: (0, 0))],
        out_specs=[mn, mn], out_shape=[_sds((M, N), F32), _sds((M, N), F32)], name=name,
        compiler_params=_cparams("parallel"))(a, w, h, gate)


def ffn_up(u, wg_t, wu_t, *, name):
    M, K = u.shape
    N = wg_t.shape[0]
    tm, tn = min(BIG_TOKEN_TILE, M), _tile(N, 256)

    def body(u_ref, wg_ref, wu_ref, a_ref, b_ref, act_ref):
        uu = u_ref[...]
        a = _dot(uu, wg_ref[...], NT)
        b = _dot(uu, wu_ref[...], NT)
        a_ref[...] = a
        b_ref[...] = b
        act_ref[...] = _bf(a * _sigmoid(a) * b)

    mn = pl.BlockSpec((tm, tn), lambda i, j: (i, j))
    wspec = pl.BlockSpec((tn, K), lambda i, j: (j, 0))
    return pl.pallas_call(
        body, grid=(M // tm, N // tn), in_specs=[pl.BlockSpec((tm, K), lambda i, j: (i, 0)), wspec, wspec],
        out_specs=[mn, mn, mn], out_shape=[_sds((M, N), F32), _sds((M, N), F32), _sds((M, N), BF16)], name=name,
        compiler_params=_cparams("parallel", "parallel"))(u, wg_t, wu_t)


def ffn_dact(dout, wd, a, b, *, name):
    M, K = dout.shape
    N = wd.shape[0]
    tm, tn = min(BIG_TOKEN_TILE, M), _tile(N, 256)

    def body(d_ref, w_ref, a_ref, b_ref, da_ref, db_ref):
        dact = _dot(d_ref[...], w_ref[...], NT)
        av = a_ref[...]
        sg = _sigmoid(av)
        db_ref[...] = _bf(dact * av * sg)
        da_ref[...] = _bf(dact * b_ref[...] * (sg * (1.0 + av * (1.0 - sg))))

    mn = pl.BlockSpec((tm, tn), lambda i, j: (i, j))
    return pl.pallas_call(
        body, grid=(M // tm, N // tn),
        in_specs=[pl.BlockSpec((tm, K), lambda i, j: (i, 0)), pl.BlockSpec((tn, K), lambda i, j: (j, 0)), mn, mn],
        out_specs=[mn, mn], out_shape=[_sds((M, N), BF16), _sds((M, N), BF16)], name=name,
        compiler_params=_cparams("parallel", "parallel"))(dout, wd, a, b)


def norm_mod(h, gain, shift, scale, *, name):
    S, D = h.shape
    tm = min(TOKEN_TILE, S)

    def body(h_ref, g_ref, sh_ref, sc_ref, u_ref):
        x = h_ref[...]
        r = lax.rsqrt(jnp.mean(x * x, axis=-1, keepdims=True) + NORM_EPS)
        n = x * r * g_ref[...]
        u_ref[...] = _bf(n * (1.0 + sc_ref[...]) + sh_ref[...])

    row = pl.BlockSpec((1, D), lambda i: (0, 0))
    tile = pl.BlockSpec((tm, D), lambda i: (i, 0))
    return pl.pallas_call(body, grid=(S // tm,), in_specs=[tile, row, row, row], out_specs=tile,
                          out_shape=_sds((S, D), BF16), name=name, compiler_params=_cparams("parallel"))(h, gain, shift, scale)


def dgrad_norm_bwd(lhs, ws, spans, h, gain, scale, dres, *, name):
    S, D = h.shape
    tm = min(TOKEN_TILE, S)
    n, nw = len(lhs), len(ws)

    def body(*refs):
        l_refs, w_refs = refs[:n], refs[n:n + nw]
        h_ref, g_ref, sc_ref, dres_ref, dh_ref, st_ref = refs[n + nw:]
        du = None
        for lr, (k, r0) in zip(l_refs, spans):
            part = _dot(_bf(lr[...]), w_refs[k][r0:r0 + lr.shape[1], :], NN)
            du = part if du is None else du + part
        x = h_ref[...]
        g = g_ref[...]
        r = lax.rsqrt(jnp.mean(x * x, axis=-1, keepdims=True) + NORM_EPS)
        xhat = x * r
        dn = du * (1.0 + sc_ref[...])
        dxhat = dn * g
        dh_ref[...] = dres_ref[...] + r * (dxhat - xhat * jnp.mean(dxhat * xhat, axis=-1, keepdims=True))

        @pl.when(pl.program_id(0) == 0)
        def _():
            st_ref[...] = jnp.zeros_like(st_ref)

        st_ref[0:1, :] += jnp.sum(dn * xhat, axis=0, keepdims=True)
        st_ref[1:2, :] += jnp.sum(du, axis=0, keepdims=True)
        st_ref[2:3, :] += jnp.sum(du * (xhat * g), axis=0, keepdims=True)

    row = pl.BlockSpec((1, D), lambda i: (0, 0))
    tile = pl.BlockSpec((tm, D), lambda i: (i, 0))
    in_specs = [pl.BlockSpec((tm, l.shape[1]), lambda i: (i, 0)) for l in lhs]
    in_specs += [pl.BlockSpec(w.shape, lambda i: (0, 0)) for w in ws]
    in_specs += [tile, row, row, tile]
    return pl.pallas_call(
        body, grid=(S // tm,), in_specs=in_specs, out_specs=[tile, pl.BlockSpec((8, D), lambda i: (0, 0))],
        out_shape=[_sds((S, D), F32), _sds((8, D), F32)], name=name,
        compiler_params=_cparams("arbitrary"))(*lhs, *ws, h, gain, scale, dres)


def gate_bwd(dh, out, gate, factor, *, name):
    S, D = dh.shape
    tm = min(TOKEN_TILE, S)

    def body(dh_ref, o_ref, g_ref, do_ref, st_ref):
        d = dh_ref[...]
        do_ref[...] = _bf(d * (factor * (1.0 + g_ref[...])))

        @pl.when(pl.program_id(0) == 0)
        def _():
            st_ref[...] = jnp.zeros_like(st_ref)

        st_ref[0:1, :] += factor * jnp.sum(d * o_ref[...], axis=0, keepdims=True)

    tile = pl.BlockSpec((tm, D), lambda i: (i, 0))
    return pl.pallas_call(
        body, grid=(S // tm,), in_specs=[tile, tile, pl.BlockSpec((1, D), lambda i: (0, 0))],
        out_specs=[tile, pl.BlockSpec((8, D), lambda i: (0, 0))], out_shape=[_sds((S, D), BF16), _sds((8, D), F32)],
        name=name, compiler_params=_cparams("arbitrary"))(dh, out, gate)


def final_loss_bwd(h, gain, shift, scale, target, *, name):
    S, D = h.shape
    tm = min(TOKEN_TILE, S)

    def body(h_ref, g_ref, sh_ref, sc_ref, t_ref, dh_ref, st_ref):
        x = h_ref[...]
        g = g_ref[...]
        r = lax.rsqrt(jnp.mean(x * x, axis=-1, keepdims=True) + NORM_EPS)
        xhat = x * r
        n = xhat * g
        err = n * (1.0 + sc_ref[...]) + sh_ref[...] - t_ref[...]
        dy = err * (1.0 / D)
        dn = dy * (1.0 + sc_ref[...])
        dxhat = dn * g
        dh_ref[...] = r * (dxhat - xhat * jnp.mean(dxhat * xhat, axis=-1, keepdims=True))

        @pl.when(pl.program_id(0) == 0)
        def _():
            st_ref[...] = jnp.zeros_like(st_ref)

        st_ref[0:1, :] += jnp.sum(dn * xhat, axis=0, keepdims=True)
        st_ref[1:2, :] += jnp.sum(dy, axis=0, keepdims=True)
        st_ref[2:3, :] += jnp.sum(dy * n, axis=0, keepdims=True)
        tok = jnp.mean(err * err, axis=-1, keepdims=True)
        st_ref[3:4, :] += 0.5 * jnp.sum(tok, axis=0, keepdims=True)

    row = pl.BlockSpec((1, D), lambda i: (0, 0))
    tile = pl.BlockSpec((tm, D), lambda i: (i, 0))
    return pl.pallas_call(
        body, grid=(S // tm,), in_specs=[tile, row, row, row, tile], out_specs=[tile, pl.BlockSpec((8, D), lambda i: (0, 0))],
        out_shape=[_sds((S, D), F32), _sds((8, D), F32)], name=name,
        compiler_params=_cparams("arbitrary"))(h, gain, shift, scale, target)


def _ret_tables(T):
    heads = np.arange(RET_HEADS, dtype=np.float64)
    lg = np.log1p(-(2.0 ** (-5.0 - heads)))
    t = np.arange(T)
    same = (t[:, None] // 64) == (t[None, :] // 64)
    earlier = (t[None, :] // 64) < (t[:, None] // 64)
    dist = np.abs(t[:, None] - t[None, :]).astype(np.float64)
    dmat = np.where(same | earlier, np.exp(lg[:, None, None] * dist[None]), 0.0)
    qdec = np.exp(lg[:, None] * (t + 1.0)[None, :])
    kdec = np.exp(lg[:, None] * (T - 1.0 - t)[None, :])
    cdec = np.exp(lg * T)
    bc = lambda v: jnp.asarray(np.broadcast_to(v[:, :, None], (RET_HEADS, T, LANE)), F32)
    cd = jnp.asarray(np.broadcast_to(cdec[:, None, None], (RET_HEADS, LANE, LANE)), F32)
    return jnp.asarray(dmat, F32), bc(qdec), bc(kdec), cd


def _rope_tables(S):
    half = HEAD_DIM // 2
    inv_freq = ROPE_BASE ** (-jnp.arange(half, dtype=F32) / half)
    ang = jnp.arange(S, dtype=F32)[:, None] * inv_freq[None, :]
    cos, sin = jnp.cos(ang), jnp.sin(ang)
    return jnp.concatenate([cos, cos], axis=-1), jnp.concatenate([-sin, sin], axis=-1)


def _rope(x, c, s):
    return x * c + pltpu.roll(x, HEAD_DIM // 2, 1) * s


def _rope_t(dx, c, s):
    return dx * c + pltpu.roll(dx * s, HEAD_DIM // 2, 1)


def ret_fwd(proj, gn, cos, sin, *, name):
    S = proj.shape[0]
    T = min(SEQ_BLOCK, S)
    nb = S // T
    dmat, qdec, kdec, cdec = _ret_tables(T)

    def body(q_ref, k_ref, v_ref, g_ref, c_ref, s_ref, dm_ref, qd_ref, kd_ref, cd_ref, gn_ref, yo_ref, yp_ref, st_ref, state):
        @pl.when(pl.program_id(0) == 0)
        def _():
            state[...] = jnp.zeros_like(state)

        c, s = c_ref[...], s_ref[...]
        for h in range(RET_HEADS):
            cols = slice(LANE * h, LANE * (h + 1))
            qr = _rope(q_ref[:, cols], c, s)
            kr = _rope(k_ref[:, cols], c, s) * (HEAD_DIM ** -0.5)
            v = _bf(v_ref[:, cols])
            sp = state[h]
            st_ref[h] = sp
            a = _dot(_bf(qr), _bf(kr), NT) * dm_ref[h]
            y = _dot(_bf(a), v, NN) + _dot(_bf(qr * qd_ref[h]), _bf(sp), NN)
            state[h] = cd_ref[h] * sp + _dot(_bf(kr * kd_ref[h]), v, TN)
            yp_ref[:, cols] = y
            yn = y * lax.rsqrt(jnp.mean(y * y, axis=-1, keepdims=True) + NORM_EPS) * gn_ref[:, cols]
            g = g_ref[:, cols]
            yo_ref[:, cols] = _bf(yn * (g * _sigmoid(g)))

    col = lambda cb: pl.BlockSpec((T, 512), lambda b: (b, cb // 4))
    tok = pl.BlockSpec((T, LANE), lambda b: (b, 0))
    const = lambda shape: pl.BlockSpec(shape, lambda b: (0,) * len(shape))
    out_tok = pl.BlockSpec((T, 512), lambda b: (b, 0))
    return pl.pallas_call(
        body, grid=(nb,),
        in_specs=[col(CB_RQ), col(CB_RK), col(CB_RV), col(CB_RG), tok, tok, const(dmat.shape), const(qdec.shape), const(kdec.shape),
                  const(cdec.shape), const((1, 512))],
        out_specs=[out_tok, out_tok, pl.BlockSpec((None, RET_HEADS, LANE, LANE), lambda b: (b, 0, 0, 0))],
        out_shape=[_sds((S, 512), BF16), _sds((S, 512), F32), _sds((nb, RET_HEADS, LANE, LANE), F32)],
        scratch_shapes=[pltpu.VMEM((RET_HEADS, LANE, LANE), F32)], name=name,
        compiler_params=_cparams("arbitrary"))(proj, proj, proj, proj, cos, sin, dmat, qdec, kdec, cdec, gn)


def ret_bwd(proj, gn, cos, sin, ypre, states, dycat, *, name):
    S = proj.shape[0]
    T = min(SEQ_BLOCK, S)
    nb = S // T
    dmat, qdec, kdec, cdec = _ret_tables(T)

    def body(q_ref, k_ref, v_ref, g_ref, c_ref, s_ref, dm_ref, qd_ref, kd_ref, cd_ref, gn_ref, yp_ref, st_ref, dy_ref,
             d_ref, stat_ref, gstate):
        @pl.when(pl.program_id(0) == 0)
        def _():
            gstate[...] = jnp.zeros_like(gstate)
            stat_ref[...] = jnp.zeros_like(stat_ref)

        c, s = c_ref[...], s_ref[...]
        scale = HEAD_DIM ** -0.5
        for h in range(RET_HEADS):
            cols = slice(LANE * h, LANE * (h + 1))
            qr = _rope(q_ref[:, cols], c, s)
            kr = _rope(k_ref[:, cols], c, s) * scale
            v = _bf(v_ref[:, cols])
            qd, kd, dm = qd_ref[h], kd_ref[h], dm_ref[h]
            sp = _bf(st_ref[h])
            gs = gstate[h]
            gsb = _bf(gs)
            g = g_ref[:, cols]
            sg = _sigmoid(g)
            y = yp_ref[:, cols]
            gn_row = gn_ref[:, cols]
            r = lax.rsqrt(jnp.mean(y * y, axis=-1, keepdims=True) + NORM_EPS)
            yhat = y * r
            dyo = dy_ref[:, cols]
            d_ref[:, 1536 + LANE * h:1536 + LANE * (h + 1)] = _bf(dyo * (yhat * gn_row) * (sg * (1.0 + g * (1.0 - sg))))
            dyn = dyo * (g * sg)
            stat_ref[0:1, cols] += jnp.sum(dyn * yhat, axis=0, keepdims=True)
            dyhat = dyn * gn_row
            dy = _bf(r * (dyhat - yhat * jnp.mean(dyhat * yhat, axis=-1, keepdims=True)))
            qrb, krb = _bf(qr), _bf(kr)
            qdb = _bf(qr * qd)
            kdb = _bf(kr * kd)
            a = _bf(_dot(qrb, krb, NT) * dm)
            da = _bf(_dot(dy, v, NT) * dm)
            d_ref[:, 1024 + LANE * h:1024 + LANE * (h + 1)] = _bf(_dot(a, dy, TN) + _dot(kdb, gsb, NN))
            dqr = _dot(da, krb, NN) + qd * _dot(dy, sp, NT)
            dkr = _dot(da, qrb, TN) + kd * _dot(v, gsb, NT)
            gstate[h] = cd_ref[h] * gs + _dot(qdb, dy, TN)
            d_ref[:, cols] = _bf(_rope_t(dqr, c, s))
            d_ref[:, 512 + LANE * h:512 + LANE * (h + 1)] = _bf(_rope_t(dkr * scale, c, s))

    rb = lambda b: nb - 1 - b
    col = lambda cb: pl.BlockSpec((T, 512), lambda b: (rb(b), cb // 4))
    tok = pl.BlockSpec((T, LANE), lambda b: (rb(b), 0))
    const = lambda shape: pl.BlockSpec(shape, lambda b: (0,) * len(shape))
    tok512 = pl.BlockSpec((T, 512), lambda b: (rb(b), 0))
    return pl.pallas_call(
        body, grid=(nb,),
        in_specs=[col(CB_RQ), col(CB_RK), col(CB_RV), col(CB_RG), tok, tok, const(dmat.shape), const(qdec.shape), const(kdec.shape),
                  const(cdec.shape), const((1, 512)), tok512,
                  pl.BlockSpec((None, RET_HEADS, LANE, LANE), lambda b: (rb(b), 0, 0, 0)), tok512],
        out_specs=[pl.BlockSpec((T, 2048), lambda b: (rb(b), 0)), const((8, 512))],
        out_shape=[_sds((S, 2048), BF16), _sds((8, 512), F32)],
        scratch_shapes=[pltpu.VMEM((RET_HEADS, LANE, LANE), F32)], name=name,
        compiler_params=_cparams("arbitrary"))(proj, proj, proj, proj, cos, sin, dmat, qdec, kdec, cdec, gn, ypre, states, dycat)


def _sb_cast_kv(k_ref, v_ref, kb, vb, S):
    step = min(TOKEN_TILE, S)
    for r in range(0, S, step):
        kb[r:r + step, :] = _bf(k_ref[r:r + step, :])
        vb[r:r + step, :] = _bf(v_ref[r:r + step, :])


def _sb_logits(q, kblk, vis):
    z = _dot(q, kblk, NT) * (HEAD_DIM ** -0.5)
    l = jnp.log1p(jnp.exp(-jnp.abs(z)))
    lb = jnp.minimum(z, 0.0) - l
    lk = jnp.minimum(-z, 0.0) - l
    if vis is not None:
        lk = jnp.where(vis, lk, 0.0)
    return lb, lk


def _tri(T, cmp):
    r = lax.broadcasted_iota(jnp.int32, (T, T), 0)
    c = lax.broadcasted_iota(jnp.int32, (T, T), 1)
    return cmp(r, c)


def _dot_split2(x, m):
    hi, lo = _split2(x)
    return _dot(hi, m, NN) + _dot(lo, m, NN)


def sb_fwd(proj, *, name):
    S = proj.shape[0]
    T = min(SEQ_BLOCK, S)
    nq = S // T

    assert nq <= LANE

    def body(q_ref, k_ref, v_ref, o_ref, cin_ref, kb, vb):
        qi = pl.program_id(1)

        @pl.when(qi == 0)
        def _():
            _sb_cast_kv(k_ref, v_ref, kb, vb, S)

        q = _bf(q_ref[...])
        vis = _tri(T, lambda t, s: s < t)
        after = _tri(T, lambda j, s: j > s).astype(BF16)
        lane = lax.broadcasted_iota(jnp.int32, (T, LANE), 1)

        def block(jb, carry, acc, cin, mask):
            rows = pl.ds(pl.multiple_of(jb * T, T), T)
            lb, lk = _sb_logits(q, kb[rows, :], mask)
            tail = _dot_split2(lk, after) + carry
            w = jnp.exp(lb + tail)
            if mask is not None:
                w = jnp.where(mask, w, 0.0)
            return (carry + jnp.sum(lk, axis=1, keepdims=True), acc + _dot(_bf(w), vb[rows, :], NN),
                    jnp.where(lane == jb, carry, cin))

        st = block(qi, jnp.zeros((T, 1), F32), jnp.zeros((T, LANE), F32), jnp.full((T, LANE), SB_UNVISITED, F32), vis)

        def more(c):
            return (c[0] < qi) & (jnp.max(c[1]) > -SB_SKIP)

        def step(c):
            return (c[0] + 1,) + block(qi - 1 - c[0], c[1], c[2], c[3], None)

        st = lax.while_loop(more, step, (jnp.int32(0),) + st)
        o_ref[...] = st[2]
        cin_ref[...] = st[3]

    whole = lambda cb: pl.BlockSpec((S, LANE), lambda h, i: (0, cb + h))
    tok = pl.BlockSpec((T, LANE), lambda h, i: (i, h))
    return pl.pallas_call(
        body, grid=(RET_HEADS, nq),
        in_specs=[pl.BlockSpec((T, LANE), lambda h, i: (i, CB_SQ + h)), whole(CB_SK), whole(CB_SV)],
        out_specs=[tok, tok], out_shape=[_sds((S, 512), F32), _sds((S, 512), F32)],
        scratch_shapes=[pltpu.VMEM((S, LANE), BF16), pltpu.VMEM((S, LANE), BF16)], name=name,
        compiler_params=_cparams("parallel", "arbitrary"))(proj, proj, proj)


def sb_bwd(proj, cin, dycat, *, name):
    S = proj.shape[0]
    T = min(SEQ_BLOCK, S)
    nq = S // T
    scale = HEAD_DIM ** -0.5

    def body(q_ref, k_ref, v_ref, cin_ref, do_ref, dq_ref, dk_ref, dv_ref, kb, vb):
        qi = pl.program_id(1)

        @pl.when(qi == 0)
        def _():
            _sb_cast_kv(k_ref, v_ref, kb, vb, S)
            dk_ref[...] = jnp.zeros_like(dk_ref)
            dv_ref[...] = jnp.zeros_like(dv_ref)

        q = _bf(q_ref[...])
        dob = _bf(do_ref[...])
        cin = cin_ref[...]
        vis = _tri(T, lambda t, s: s < t)
        after = _tri(T, lambda j, s: j > s).astype(BF16)
        before = _tri(T, lambda s, j: s < j).astype(BF16)
        lane = lax.broadcasted_iota(jnp.int32, (T, LANE), 1)

        def block(jb, ecarry, dq, mask):
            rows = pl.ds(pl.multiple_of(jb * T, T), T)
            kblk, vblk = kb[rows, :], vb[rows, :]
            lb, lk = _sb_logits(q, kblk, mask)
            carry = jnp.sum(jnp.where(lane == jb, cin, 0.0), axis=1, keepdims=True)
            w = jnp.exp(lb + _dot_split2(lk, after) + carry)
            if mask is not None:
                w = jnp.where(mask, w, 0.0)
            e = w * _dot(dob, vblk, NT)
            dv_ref[rows, :] += _dot(_bf(w), dob, TN)
            dlk = _dot_split2(e, before) + ecarry
            beta = jnp.exp(lb)
            dz = e * (1.0 - beta) - beta * dlk
            if mask is not None:
                dz = jnp.where(mask, dz, 0.0)
            dzb = _bf(dz * scale)
            dk_ref[rows, :] += _dot(dzb, q, TN)
            return ecarry + jnp.sum(e, axis=1, keepdims=True), dq + _dot(dzb, kblk, NN)

        lane1 = lane[0:1, :]
        skipped = (jnp.max(cin, axis=0, keepdims=True) <= -SB_SKIP) & (lane1 < qi)
        first = jnp.sum(jnp.where(skipped, 1, 0))
        st = lax.fori_loop(first, qi, lambda jb, c: block(jb, c[0], c[1], None), (jnp.zeros((T, 1), F32), jnp.zeros((T, LANE), F32)))
        st = block(qi, st[0], st[1], vis)
        dq_ref[...] = _bf(st[1])

    whole = lambda cb: pl.BlockSpec((S, LANE), lambda h, i: (0, cb + h))
    tok = pl.BlockSpec((T, LANE), lambda h, i: (i, h))
    acc = pl.BlockSpec((S, LANE), lambda h, i: (0, h))
    return pl.pallas_call(
        body, grid=(RET_HEADS, nq),
        in_specs=[pl.BlockSpec((T, LANE), lambda h, i: (i, CB_SQ + h)), whole(CB_SK), whole(CB_SV), tok,
                  pl.BlockSpec((T, LANE), lambda h, i: (i, 4 + h))],
        out_specs=[tok, acc, acc], out_shape=[_sds((S, 512), BF16), _sds((S, 512), F32), _sds((S, 512), F32)],
        scratch_shapes=[pltpu.VMEM((S, LANE), BF16), pltpu.VMEM((S, LANE), BF16)], name=name,
        compiler_params=_cparams("parallel", "arbitrary"))(proj, proj, proj, cin, dycat)


def _shift_down(x, d, row):
    return jnp.where(row >= d, pltpu.roll(x, d, 0), 0.0)


def _shift_up(x, d, row, S):
    return jnp.where(row < S - d, pltpu.roll(x, S - d, 0), 0.0)


def conv_fwd(proj, conv_w, conv_b, *, name):
    S = proj.shape[0]

    def body(x_ref, w_ref, b_ref, pre_ref, act_ref):
        x = x_ref[...]
        row = lax.broadcasted_iota(jnp.int32, x.shape, 0)
        pre = b_ref[...] + w_ref[3:4, :] * x
        for d in range(1, SSM_CONV):
            pre = pre + w_ref[3 - d:4 - d, :] * _shift_down(x, d, row)
        pre_ref[...] = pre
        act_ref[...] = pre * _sigmoid(pre)

    blk = pl.BlockSpec((S, LANE), lambda c: (0, c))
    return pl.pallas_call(
        body, grid=(8,),
        in_specs=[pl.BlockSpec((S, LANE), lambda c: (0, CB_XS + c)), pl.BlockSpec((SSM_CONV, LANE), lambda c: (0, c)),
                  pl.BlockSpec((1, LANE), lambda c: (0, c))],
        out_specs=[blk, blk], out_shape=[_sds((S, 1024), F32), _sds((S, 1024), F32)], name=name,
        compiler_params=_cparams("parallel"))(proj, conv_w, conv_b)


def conv_bwd(proj, pre, dact, conv_w, *, name):
    S = proj.shape[0]

    def body(x_ref, pre_ref, da_ref, w_ref, dx_ref, st_ref):
        x = x_ref[...]
        p = pre_ref[...]
        row = lax.broadcasted_iota(jnp.int32, x.shape, 0)
        sg = _sigmoid(p)
        dpre = da_ref[...] * (sg * (1.0 + p * (1.0 - sg)))
        dx = w_ref[3:4, :] * dpre
        st_ref[3:4, :] = jnp.sum(dpre * x, axis=0, keepdims=True)
        for d in range(1, SSM_CONV):
            dx = dx + w_ref[3 - d:4 - d, :] * _shift_up(dpre, d, row, S)
            st_ref[3 - d:4 - d, :] = jnp.sum(dpre * _shift_down(x, d, row), axis=0, keepdims=True)
        st_ref[4:5, :] = jnp.sum(dpre, axis=0, keepdims=True)
        st_ref[5:8, :] = jnp.zeros((3, LANE), F32)
        dx_ref[...] = _bf(dx)

    blk = pl.BlockSpec((S, LANE), lambda c: (0, c))
    return pl.pallas_call(
        body, grid=(8,),
        in_specs=[pl.BlockSpec((S, LANE), lambda c: (0, CB_XS + c)), blk, blk, pl.BlockSpec((SSM_CONV, LANE), lambda c: (0, c))],
        out_specs=[blk, pl.BlockSpec((8, LANE), lambda c: (0, c))],
        out_shape=[_sds((S, 1024), BF16), _sds((8, 1024), F32)], name=name,
        compiler_params=_cparams("parallel"))(proj, pre, dact, conv_w)


def _softplus(x):
    return jnp.maximum(x, 0.0) + jnp.log1p(jnp.exp(-jnp.abs(x)))


def _pair(lane, v0, v1):
    return jnp.where(lane < SSM_HEAD_DIM, v0, v1)


def _ssd_pair_common(raw, dtb, alog, xs, cm, hprev, T):
    lane = lax.broadcasted_iota(jnp.int32, (T, LANE), 1)
    dt = _softplus(raw + dtb)
    a = -jnp.exp(alog)
    incl = _tri(T, lambda l, s: s <= l).astype(BF16)
    h1, h2, h3 = _split3(dt * a)
    acum = _dot(incl, h1, NN) + _dot(incl, h2, NN) + _dot(incl, h3, NN)
    acum_t = acum.T
    causal = _tri(T, lambda l, s: s <= l)
    decay = [jnp.where(causal, jnp.exp(jnp.minimum(acum[:, j:j + 1] - acum_t[j:j + 1, :], 0.0)), 0.0) for j in (0, 1)]
    dtc = _pair(lane, dt[:, 0:1], dt[:, 1:2])
    ac = _pair(lane, acum[:, 0:1], acum[:, 1:2])
    xdt = xs * dtc
    ea = jnp.exp(ac)
    e_end = jnp.exp(ac[T - 1:T, :] - ac)
    sub = lax.broadcasted_iota(jnp.int32, (LANE, LANE), 0)
    cd = jnp.where(sub < SSM_HEAD_DIM, jnp.exp(acum[T - 1:T, 0:1]), jnp.exp(acum[T - 1:T, 1:2]))
    r = _dot(cm, _bf(hprev), NT)
    return lane, dt, a, acum, decay, dtc, xdt, ea, e_end, cd, r


def ssd_fwd(xact, proj, dtb, alog, dskip, *, name):
    S = xact.shape[0]
    T = min(SEQ_BLOCK, S)
    nb = S // T

    def body(xs_ref, bm_ref, cm_ref, dt_ref, dtb_ref, al_ref, ds_ref, y_ref, st_ref, state):
        @pl.when(pl.program_id(0) == 0)
        def _():
            state[...] = jnp.zeros_like(state)

        for g in range(2):
            bm, cm = _bf(bm_ref[:, LANE * g:LANE * (g + 1)]), _bf(cm_ref[:, LANE * g:LANE * (g + 1)])
            gm = _dot(cm, bm, NT)
            for i in range(2):
                p = 2 * g + i
                cols = slice(LANE * p, LANE * (p + 1))
                xs = xs_ref[:, cols]
                hprev = state[p]
                st_ref[g, i] = hprev
                lane, dt, a, acum, decay, dtc, xdt, ea, e_end, cd, r = _ssd_pair_common(
                    dt_ref[:, cols], dtb_ref[p], al_ref[p], xs, cm, hprev, T)
                xdtb = _bf(xdt)
                y_intra = _pair(lane, _dot(_bf(gm * decay[0]), xdtb, NN), _dot(_bf(gm * decay[1]), xdtb, NN))
                state[p] = cd * hprev + _dot(_bf(xdt * e_end), bm, TN)
                dsk = ds_ref[p]
                lane1 = lane[0:1, :]
                y_ref[:, cols] = y_intra + ea * r + _pair(lane1, dsk[:, 0:1], dsk[:, 1:2]) * xs

    rows = pl.BlockSpec((4, 1, LANE), lambda b: (0, 0, 0))
    return pl.pallas_call(
        body, grid=(nb,),
        in_specs=[pl.BlockSpec((T, 512), lambda b: (b, 0)), pl.BlockSpec((T, 256), lambda b: (b, 2)), pl.BlockSpec((T, 256), lambda b: (b, 3)),
                  pl.BlockSpec((T, 512), lambda b: (b, CB_DT // 4)), rows, rows, rows],
        out_specs=[pl.BlockSpec((T, 512), lambda b: (b, 0)),
                   pl.BlockSpec((2, None, 2, LANE, LANE), lambda b: (0, b, 0, 0, 0))],
        out_shape=[_sds((S, 512), F32), _sds((2, nb, 2, LANE, LANE), F32)],
        scratch_shapes=[pltpu.VMEM((4, LANE, LANE), F32)], name=name,
        compiler_params=_cparams("arbitrary"))(xact, xact, xact, proj, dtb, alog, dskip)


def ssd_bwd(xact, proj, dtb, alog, dskip, states, dy, *, name):
    S = xact.shape[0]
    T = min(SEQ_BLOCK, S)
    nb = S // T

    def body(xs_ref, bm_ref, cm_ref, dt0_ref, dt1_ref, dtb_ref, al_ref, ds_ref, st_ref, dy_ref,
             dxs_ref, dbm_ref, dcm_ref, ddt_ref, stat_ref, dstate):
        @pl.when(pl.program_id(1) == 0)
        def _():
            dstate[...] = jnp.zeros_like(dstate)
            stat_ref[...] = jnp.zeros_like(stat_ref)

        bm, cm = _bf(bm_ref[...]), _bf(cm_ref[...])
        gm = _dot(cm, bm, NT)
        dbm = jnp.zeros((T, LANE), F32)
        dcm = jnp.zeros((T, LANE), F32)
        after_eq = _tri(T, lambda i, l: l >= i).astype(BF16)
        rowi = lax.broadcasted_iota(jnp.int32, (T, 1), 0)
        for i, dt_ref in enumerate((dt0_ref, dt1_ref)):
            xs = xs_ref[:, LANE * i:LANE * (i + 1)]
            dyp = dy_ref[:, LANE * i:LANE * (i + 1)]
            hprev = st_ref[i]
            dh = dstate[i]
            raw = dt_ref[...]
            lane, dt, a, acum, decay, dtc, xdt, ea, e_end, cd, r = _ssd_pair_common(
                raw, dtb_ref[i], al_ref[i], xs, cm, hprev, T)
            lane1 = lane[0:1, :]
            dsk = ds_ref[i]
            dskp = _pair(lane1, dsk[:, 0:1], dsk[:, 1:2])
            head = [lane < SSM_HEAD_DIM, lane >= SSM_HEAD_DIM]
            hsum = lambda v, j: jnp.sum(jnp.where(head[j], v, 0.0), axis=1, keepdims=True)
            dhb = _bf(dh)
            xdtb = _bf(xdt)
            dyb = _bf(dyp)
            z = xdt * e_end
            dz = _dot(bm, dhb, NT)
            dbm = dbm + _dot(_bf(z), dhb, NN)
            dxdt = dz * e_end
            de_e = dz * z
            drr = dyp * ea
            dea_ea = drr * r
            dcm = dcm + _dot(_bf(drr), _bf(hprev), NN)
            dstate[i] = cd * dh + _dot(_bf(drr), cm, TN)
            dcd_cd = cd * dh * hprev
            dgs = jnp.zeros((T, T), F32)
            da_cols = []
            for j in (0, 1):
                w = gm * decay[j]
                dw = _dot(_bf(jnp.where(head[j], dyp, 0.0)), xdtb, NT)
                dxdt = dxdt + jnp.where(head[j], _dot(_bf(w), dyb, TN), 0.0)
                dgs = dgs + dw * decay[j]
                dseg = dw * w
                col = jnp.sum(dseg, axis=1, keepdims=True) - jnp.sum(dseg.T, axis=1, keepdims=True)
                col = col + hsum(dea_ea, j) - hsum(de_e, j)
                sub = lax.broadcasted_iota(jnp.int32, (LANE, LANE), 0)
                in_head = (sub < SSM_HEAD_DIM) if j == 0 else (sub >= SSM_HEAD_DIM)
                end = jnp.sum(hsum(de_e, j), axis=0, keepdims=True) + jnp.sum(
                    jnp.sum(jnp.where(in_head, dcd_cd, 0.0), axis=1, keepdims=True), axis=0, keepdims=True)
                da_cols.append(col + jnp.where(rowi == T - 1, end, 0.0))
            dgb = _bf(dgs)
            dcm = dcm + _dot(dgb, bm, NN)
            dbm = dbm + _dot(dgb, cm, TN)
            dacum = jnp.where(lane == 0, da_cols[0], jnp.where(lane == 1, da_cols[1], 0.0))
            h1, h2, h3 = _split3(dacum)
            ddta = _dot(after_eq, h1, NN) + _dot(after_eq, h2, NN) + _dot(after_eq, h3, NN)
            dxs_ref[:, LANE * i:LANE * (i + 1)] = dskp * dyp + dxdt * dtc
            dx_x = dxdt * xs
            ddt = ddta * a + jnp.where(lane == 0, hsum(dx_x, 0), jnp.where(lane == 1, hsum(dx_x, 1), 0.0))
            ddraw = jnp.where(lane < 2, ddt * _sigmoid(raw + dtb_ref[i]), 0.0)
            ddt_ref[:, LANE * i:LANE * (i + 1)] = _bf(ddraw)
            dsum = jnp.sum(dyp * xs, axis=0, keepdims=True)
            d0 = jnp.sum(jnp.where(lane1 < SSM_HEAD_DIM, dsum, 0.0), axis=1, keepdims=True)
            d1 = jnp.sum(jnp.where(lane1 >= SSM_HEAD_DIM, dsum, 0.0), axis=1, keepdims=True)
            dd = jnp.where(lane1 == 0, d0, jnp.where(lane1 == 1, d1, 0.0))
            stat_ref[i, 0:1, :] += jnp.sum(ddraw, axis=0, keepdims=True)
            stat_ref[i, 1:2, :] += jnp.where(lane1 < 2, jnp.sum(ddta * dt, axis=0, keepdims=True) * a, 0.0)
            stat_ref[i, 2:3, :] += dd
        dbm_ref[...] = dbm
        dcm_ref[...] = dcm

    rb = lambda b: nb - 1 - b
    rows3 = pl.BlockSpec((2, 1, LANE), lambda g, b: (g, 0, 0))
    tok256 = pl.BlockSpec((T, 256), lambda g, b: (rb(b), g))
    tok128 = pl.BlockSpec((T, LANE), lambda g, b: (rb(b), g))
    return pl.pallas_call(
        body, grid=(2, nb),
        in_specs=[tok256, pl.BlockSpec((T, LANE), lambda g, b: (rb(b), 4 + g)), pl.BlockSpec((T, LANE), lambda g, b: (rb(b), 6 + g)),
                  pl.BlockSpec((T, LANE), lambda g, b: (rb(b), CB_DT + 2 * g)),
                  pl.BlockSpec((T, LANE), lambda g, b: (rb(b), CB_DT + 2 * g + 1)),
                  rows3, rows3, rows3,
                  pl.BlockSpec((None, None, 2, LANE, LANE), lambda g, b: (g, rb(b), 0, 0, 0)), tok256],
        out_specs=[tok256, tok128, tok128, tok256, pl.BlockSpec((2, 8, LANE), lambda g, b: (g, 0, 0))],
        out_shape=[_sds((S, 512), F32), _sds((S, 256), F32), _sds((S, 256), F32), _sds((S, 512), BF16), _sds((4, 8, LANE), F32)],
        scratch_shapes=[pltpu.VMEM((2, LANE, LANE), F32)], name=name,
        compiler_params=_cparams("parallel", "arbitrary"))(xact, xact, xact, proj, proj, dtb, alog, dskip, states, dy)


def gated_norm(ypre, proj, gain, *, name):
    S, W = ypre.shape
    tm = min(TOKEN_TILE, S)

    def body(y_ref, z_ref, g_ref, o_ref):
        z = z_ref[...]
        yg = y_ref[...] * (z * _sigmoid(z))
        o_ref[...] = _bf(yg * lax.rsqrt(jnp.mean(yg * yg, axis=-1, keepdims=True) + NORM_EPS) * g_ref[...])

    tile = pl.BlockSpec((tm, W), lambda i: (i, 0))
    return pl.pallas_call(
        body, grid=(S // tm,), in_specs=[tile, pl.BlockSpec((tm, W), lambda i: (i, CB_MZ // 4)), pl.BlockSpec((1, W), lambda i: (0, 0))],
        out_specs=tile, out_shape=_sds((S, W), BF16), name=name, compiler_params=_cparams("parallel"))(ypre, proj, gain)


def gated_norm_bwd(ypre, proj, gain, dycat, *, name):
    S, W = ypre.shape
    tm = min(TOKEN_TILE, S)

    def body(y_ref, z_ref, g_ref, dy_ref, dyp_ref, dz_ref, st_ref):
        z = z_ref[...]
        y = y_ref[...]
        sg = _sigmoid(z)
        sz = z * sg
        yg = y * sz
        r = lax.rsqrt(jnp.mean(yg * yg, axis=-1, keepdims=True) + NORM_EPS)
        yhat = yg * r
        dyo = dy_ref[...]

        @pl.when(pl.program_id(0) == 0)
        def _():
            st_ref[...] = jnp.zeros_like(st_ref)

        st_ref[0:1, :] += jnp.sum(dyo * yhat, axis=0, keepdims=True)
        dyhat = dyo * g_ref[...]
        dyg = r * (dyhat - yhat * jnp.mean(dyhat * yhat, axis=-1, keepdims=True))
        dyp_ref[...] = dyg * sz
        dz_ref[...] = _bf(dyg * y * (sg * (1.0 + z * (1.0 - sg))))

    tile = pl.BlockSpec((tm, W), lambda i: (i, 0))
    return pl.pallas_call(
        body, grid=(S // tm,),
        in_specs=[tile, pl.BlockSpec((tm, W), lambda i: (i, CB_MZ // 4)), pl.BlockSpec((1, W), lambda i: (0, 0)),
                  pl.BlockSpec((tm, W), lambda i: (i, 2))],
        out_specs=[tile, tile, pl.BlockSpec((8, W), lambda i: (0, 0))],
        out_shape=[_sds((S, W), F32), _sds((S, W), BF16), _sds((8, W), F32)], name=name,
        compiler_params=_cparams("arbitrary"))(ypre, proj, gain, dycat)


def ada_mod(c_all, w, bias, *, name):
    M, K = c_all.shape
    N = w.shape[1]
    tn = _tile(N, 512)

    def body(c_ref, w_ref, b_ref, o_ref, cond_ref):
        cv = c_ref[...]
        cond = cv * _sigmoid(cv)
        cond_ref[...] = cond
        o_ref[...] = _dot(_bf(cond), _bf(w_ref[...]), NN) + b_ref[...]

    return pl.pallas_call(
        body, grid=(N // tn,),
        in_specs=[pl.BlockSpec((M, K), lambda j: (0, 0)), pl.BlockSpec((K, tn), lambda j: (0, j)), pl.BlockSpec((1, tn), lambda j: (0, j))],
        out_specs=[pl.BlockSpec((M, tn), lambda j: (0, j)), pl.BlockSpec((M, K), lambda j: (0, 0))],
        out_shape=[_sds((M, N), F32), _sds((M, K), F32)], name=name, compiler_params=_cparams("arbitrary"))(c_all, w, bias)


def _adamw(g, w, m, v):
    m = ADAM_B1 * m + (1.0 - ADAM_B1) * g
    v = ADAM_B2 * v + (1.0 - ADAM_B2) * (g * g)
    m_hat = m / (1.0 - ADAM_B1 ** ADAM_STEP)
    v_hat = v / (1.0 - ADAM_B2 ** ADAM_STEP)
    return -ADAM_LR * (m_hat / (jnp.sqrt(v_hat) + ADAM_EPS) + ADAM_WD * w), m, v


def adamw_parts(parts, w, m, v, *, name):
    P, R, C = parts.shape
    tr = _tile(R, 512, 256, 160, 128, 64, 32, 16)

    def body(p_ref, w_ref, m_ref, v_ref, g_ref, d_ref, mo_ref, vo_ref):
        g = p_ref[0].astype(F32)
        for j in range(1, P):
            g = g + p_ref[j].astype(F32)
        g_ref[...] = g
        d_ref[...], mo_ref[...], vo_ref[...] = _adamw(g, w_ref[...], m_ref[...], v_ref[...])

    tile = pl.BlockSpec((tr, C), lambda i: (i, 0))
    return pl.pallas_call(
        body, grid=(R // tr,), in_specs=[pl.BlockSpec((P, tr, C), lambda i: (0, i, 0)), tile, tile, tile],
        out_specs=[tile] * 4, out_shape=[_sds((R, C), F32)] * 4, name=name, compiler_params=_cparams("parallel"))(parts, w, m, v)


def ada_adamw(cond_t, dmod, w, m, v, *, name):
    D, N = w.shape
    tr = _tile(D, 256)

    def body(c_ref, d_ref, w_ref, m_ref, v_ref, g_ref, dl_ref, mo_ref, vo_ref):
        cc = c_ref[...]
        dd = d_ref[...]
        g = cc[:, 0:1] * dd[0:1, :]
        for b in range(1, N_DEV):
            g = g + cc[:, b:b + 1] * dd[b:b + 1, :]
        g_ref[...] = g
        dl_ref[...], mo_ref[...], vo_ref[...] = _adamw(g, w_ref[...], m_ref[...], v_ref[...])

    tile = pl.BlockSpec((tr, N), lambda i: (i, 0))
    return pl.pallas_call(
        body, grid=(D // tr,), in_specs=[pl.BlockSpec((tr, N_DEV), lambda i: (i, 0)), pl.BlockSpec((N_DEV, N), lambda i: (0, 0)), tile, tile, tile],
        out_specs=[tile] * 4, out_shape=[_sds((D, N), F32)] * 4, name=name, compiler_params=_cparams("parallel"))(cond_t, dmod, w, m, v)


def _my_place():
    mx, my, mc = lax.axis_index("x"), lax.axis_index("y"), lax.axis_index("c")
    return mx, my, mc, 4 * mx + 2 * my + mc


def _peer(mx, my, mc, k):
    px = 1 - mx if (k >> 2) & 1 else mx
    py = 1 - my if (k >> 1) & 1 else my
    pc = 1 - mc if k & 1 else mc
    return (px, py, pc), 4 * px + 2 * py + pc


def _comm_call(body, x, out_shape, space, name):
    spec = pl.BlockSpec(memory_space=space)
    return pl.pallas_call(
        body, in_specs=[spec], out_specs=spec, out_shape=out_shape,
        scratch_shapes=[pltpu.SemaphoreType.DMA((N_DEV - 1,)), pltpu.SemaphoreType.DMA((N_DEV - 1,)), pltpu.SemaphoreType.DMA(())],
        name=name, compiler_params=pltpu.CompilerParams(has_side_effects=True, vmem_limit_bytes=VMEM_LIMIT))(x)


def allgather(x, *, in_vmem, name):
    def body(x_ref, out_ref, send_sems, recv_sems, local_sem):
        mx, my, mc, me = _my_place()
        mine = pltpu.make_async_copy(x_ref, out_ref.at[me], local_sem)
        mine.start()
        copies = []
        for k in range(1, N_DEV):
            peer, _ = _peer(mx, my, mc, k)
            cp = pltpu.make_async_remote_copy(src_ref=x_ref, dst_ref=out_ref.at[me], send_sem=send_sems.at[k - 1],
                                              recv_sem=recv_sems.at[k - 1], device_id=peer, device_id_type=pl.DeviceIdType.MESH)
            cp.start()
            copies.append(cp)
        for cp in copies:
            cp.wait()
        mine.wait()

    return _comm_call(body, x, _sds((N_DEV,) + x.shape, x.dtype), pltpu.VMEM if in_vmem else pltpu.HBM, name)


def allgather_two_level(x, *, name):
    def body(x_ref, out_ref, send_sems, recv_sems, local_sem):
        mx, my, mc, _ = _my_place()
        me, sibling = (mx, my, mc), (mx, my, 1 - mc)
        chips = [(1 - mx, my), (mx, 1 - my), (1 - mx, 1 - my)]

        def copy(k, block, to, src=None):
            slot = out_ref.at[4 * block[0] + 2 * block[1] + block[2]]
            return pltpu.make_async_remote_copy(src_ref=slot if src is None else src, dst_ref=slot, send_sem=send_sems.at[k],
                                                recv_sem=recv_sems.at[k], device_id=to, device_id_type=pl.DeviceIdType.MESH)

        mine = pltpu.make_async_copy(x_ref, out_ref.at[4 * mx + 2 * my + mc], local_sem)
        mine.start()
        first = [copy(0, me, sibling, src=x_ref)] + [copy(1 + j, me, (*chip, mc), src=x_ref) for j, chip in enumerate(chips)]
        for cp in first:
            cp.start()
        passed = [copy(4 + j, (*chip, mc), sibling) for j, chip in enumerate(chips)]
        for j, chip in enumerate(chips):
            copy(1 + j, (*chip, mc), me).wait_recv()
            passed[j].start()
        copy(0, sibling, me).wait_recv()
        for j, chip in enumerate(chips):
            copy(4 + j, (*chip, 1 - mc), me).wait_recv()
        for cp in first + passed:
            cp.wait_send()
        mine.wait()

    return _comm_call(body, x, _sds((N_DEV,) + x.shape, x.dtype), pltpu.HBM, name)


def sibling_exchange(send, *, name):
    n_chip, _, R, C = send.shape

    def body(s_ref, r_ref, send_sems, recv_sems, local_sem):
        mx, my, mc, _ = _my_place()
        copies = []
        for q in range(n_chip):
            cp = pltpu.make_async_remote_copy(src_ref=s_ref.at[q, 1 - mc], dst_ref=r_ref.at[q], send_sem=send_sems.at[q],
                                              recv_sem=recv_sems.at[q], device_id=(mx, my, 1 - mc), device_id_type=pl.DeviceIdType.MESH)
            cp.start()
            copies.append(cp)
        for cp in copies:
            cp.wait()

    return _comm_call(body, send, _sds((n_chip, R, C), send.dtype), pltpu.HBM, name)


def chip_exchange(send, *, name):
    def body(s_ref, r_ref, send_sems, recv_sems, local_sem):
        mx, my, mc, _ = _my_place()
        mine = pltpu.make_async_copy(s_ref.at[2 * mx + my], r_ref.at[2 * mx + my], local_sem)
        mine.start()
        copies = []
        for k in range(1, 4):
            px = 1 - mx if (k >> 1) & 1 else mx
            py = 1 - my if k & 1 else my
            cp = pltpu.make_async_remote_copy(src_ref=s_ref.at[2 * px + py], dst_ref=r_ref.at[2 * mx + my], send_sem=send_sems.at[k - 1],
                                              recv_sem=recv_sems.at[k - 1], device_id=(px, py, mc), device_id_type=pl.DeviceIdType.MESH)
            cp.start()
            copies.append(cp)
        for cp in copies:
            cp.wait()
        mine.wait()

    return _comm_call(body, send, _sds(send.shape, send.dtype), pltpu.HBM, name)


def add_partials(a, b, *, name):
    P, R, C = a.shape
    tr = _tile(R, 512, 256, 160, 128, 64, 32, 16)

    def body(a_ref, b_ref, o_ref):
        o_ref[...] = _bf(a_ref[...].astype(F32) + b_ref[...].astype(F32))

    tile = pl.BlockSpec((P, tr, C), lambda i: (0, i, 0))
    return pl.pallas_call(body, grid=(R // tr,), in_specs=[tile, tile], out_specs=tile, out_shape=_sds((P, R, C), BF16), name=name,
                          compiler_params=_cparams("parallel"))(a, b)


def _rows128(a):
    f = a.reshape(-1)
    n = -(-f.shape[0] // (16 * LANE)) * (16 * LANE)
    return jnp.pad(f, (0, n - f.shape[0])).reshape(-1, LANE)


PACK_ROWS = 512


def _pad_rows(buf):
    r = buf.shape[-2]
    pad = -r % PACK_ROWS
    return jnp.pad(buf, [(0, 0)] * (buf.ndim - 2) + [(0, pad), (0, 0)])


def _pack(arrays):
    parts = [_rows128(a) for a in arrays]
    offs = np.cumsum([0] + [p.shape[0] for p in parts])
    return _pad_rows(jnp.concatenate(parts, axis=0)), [int(o) for o in offs]


def _unpack(buf, offs, shapes):
    lead = buf.shape[:-2]
    out = []
    for o, shp in zip(offs, shapes):
        n = int(np.prod(shp))
        rows = -(-n // LANE)
        seg = buf[..., o:o + rows, :].reshape(lead + (rows * LANE,))[..., :n]
        out.append(seg.reshape(lead + tuple(shp)))
    return out


def _pad_w_in_t(w_t):
    D = w_t.shape[1]
    dt = jnp.pad(w_t[IN_MAIN:].reshape(4, 2, D), ((0, 0), (0, LANE - 2), (0, 0)))
    return jnp.concatenate([w_t[:IN_MAIN], dt.reshape(4 * LANE, D)], axis=0)


def _unpad_w_in_t(g_t):
    D = g_t.shape[1]
    dt = g_t[IN_MAIN:].reshape(4, LANE, D)[:, :2].reshape(SSM_HEADS, D)
    return jnp.concatenate([g_t[:IN_MAIN], dt], axis=0)


def _piece_rows(n, shard_shape):
    r = shard_shape[0] if n in ROW_SHARDED else shard_shape[1]
    return r, -(-r // 16) * 16


def _to_piece(n, shard):
    t = shard if n in ROW_SHARDED else shard.T
    return jnp.pad(t, ((0, -t.shape[0] % 16), (0, 0)))


def _from_piece(n, piece, shard_shape):
    r, _ = _piece_rows(n, shard_shape)
    return piece[:r] if n in ROW_SHARDED else piece[:r].T


def _pair_rows(p):
    return jnp.pad(p.reshape(4, 1, 2), ((0, 0), (0, 0), (0, LANE - 2)))


def _row(v):
    return v.reshape(1, -1)


def _ffn_fwd(h, gain, mod3, wg, wu, wd, tag):
    shift, scale, gate = mod3
    u = norm_mod(h, gain, shift, scale, name=tag + "_norm")
    a, b, act = ffn_up(u, wg, wu, name=tag + "_up")
    hn, out = matmul_resid(act, wd, h, gate, 0.5, name=tag + "_down")
    return hn, (h, u, a, b, act, out)


def _wgrad(a, b, name):
    return matmul(a, b, ta=True, tm=_tile(a.shape[1], 1408, 1536, 1024, 512), tn=b.shape[1], tk=WGRAD_TOKENS, out_dtype=BF16, name=name)


def _ffn_bwd(dh, saved, gain, mod3, wg, wu, wd, tag):
    h, u, a, b, act, out = saved
    _, scale, gate = mod3
    dout, gst = gate_bwd(dh, out, gate, 0.5, name=tag + "_gate_bwd")
    da, db = ffn_dact(dout, wd, a, b, name=tag + "_dact")
    dh_prev, nst = dgrad_norm_bwd([da, db], [wg, wu], [(0, 0), (1, 0)], h, gain, scale, dh, name=tag + "_dgrad")
    grads = (_wgrad(da, u, tag + "_dwg"), _wgrad(db, u, tag + "_dwu"), _wgrad(act, dout, tag + "_dwd"))
    return dh_prev, grads, nst[0], [nst[1], nst[2], gst[0]]


def _mix_fwd(h, p, mod3, w_in, w_out, cos, sin, tag):
    shift, scale, gate = mod3
    u = norm_mod(h, p["norm_mix"], shift, scale, name=tag + "_norm")
    proj = matmul(u, w_in, tb=True, tm=BIG_TOKEN_TILE, tn=512, tk=D_MODEL, name=tag + "_proj")
    y_ret, ypre_ret, st_ret = ret_fwd(proj, p["ret_gn"], cos, sin, name=tag + "_ret")
    y_sb, sb_cin = sb_fwd(proj, name=tag + "_sb")
    pre, xact = conv_fwd(proj, p["conv_w"], p["conv_b"], name=tag + "_conv")
    ypre_ssm, st_ssm = ssd_fwd(xact, proj, p["dt_bias"], p["a_log"], p["d_skip"], name=tag + "_ssd")
    y_ssm = gated_norm(ypre_ssm, proj, p["ssm_norm"], name=tag + "_gnorm")
    ycat = jnp.concatenate([y_ret, y_sb.astype(BF16), y_ssm], axis=1)
    hn, mixed = matmul_resid(ycat, w_out, h, gate, 1.0, name=tag + "_out")
    return hn, (h, u, proj, ypre_ret, st_ret, sb_cin, pre, xact, ypre_ssm, st_ssm, ycat, mixed)


def _mix_bwd(dh, saved, p, mod3, w_in, w_out, cos, sin, tag):
    h, u, proj, ypre_ret, st_ret, sb_cin, pre, xact, ypre_ssm, st_ssm, ycat, mixed = saved
    _, scale, gate = mod3
    dmixed, gst = gate_bwd(dh, mixed, gate, 1.0, name=tag + "_gate_bwd")
    dycat = matmul(dmixed, w_out, tb=True, tm=BIG_TOKEN_TILE, tn=512, tk=D_MODEL, name=tag + "_dycat")
    dw_out = _wgrad(ycat, dmixed, tag + "_dw_out")
    dret, rst = ret_bwd(proj, p["ret_gn"], cos, sin, ypre_ret, st_ret, dycat, name=tag + "_ret_bwd")
    dsq, dsk, dsv = sb_bwd(proj, sb_cin, dycat, name=tag + "_sb_bwd")
    dypre, dz, nst2 = gated_norm_bwd(ypre_ssm, proj, p["ssm_norm"], dycat, name=tag + "_gnorm_bwd")
    dxs, dbm, dcm, ddt, sst = ssd_bwd(xact, proj, p["dt_bias"], p["a_log"], p["d_skip"], st_ssm, dypre, name=tag + "_ssd_bwd")
    dact = jnp.concatenate([dxs, dbm, dcm], axis=1)
    dxbc, cst = conv_bwd(proj, pre, dact, p["conv_w"], name=tag + "_conv_bwd")
    pieces = [dret, dsq, dsk, dsv, dz, dxbc, ddt]
    starts = np.cumsum([0] + [pc.shape[1] for pc in pieces])
    assert starts[-1] == IN_PAD
    dh_prev, nst = dgrad_norm_bwd(pieces, [w_in], [(0, int(r0)) for r0 in starts[:-1]], h, p["norm_mix"], scale, dh, name=tag + "_dgrad")
    dw_in = jnp.concatenate([_wgrad(pc, u, f"{tag}_dw_in{i}") for i, pc in enumerate(pieces)], axis=0)
    small = dict(norm_mix=nst[0], ret_gn=rst[0], ssm_norm=nst2[0], conv_w=cst[0:4], conv_b=cst[4],
                 dt_bias=sst[:, 0, :2].reshape(SSM_HEADS), a_log=sst[:, 1, :2].reshape(SSM_HEADS), d_skip=sst[:, 2, :2].reshape(SSM_HEADS))
    return dh_prev, dw_in, dw_out, small, [nst[1], nst[2], gst[0]]


BIG = ("ffn1_wg", "ffn1_wu", "ffn1_wd", "w_in", "w_out", "ffn2_wg", "ffn2_wu", "ffn2_wd")
ROW_SHARDED = ("ffn1_wd", "w_out", "ffn2_wd")
SMALL = ("ada_b", "norm_ffn1", "norm_mix", "conv_b", "dt_bias", "a_log", "d_skip", "ret_gn", "ssm_norm", "norm_ffn2",
         "final_ada_b", "final_norm")
NAMES = ("ada_w", "ada_b", "norm_ffn1", "ffn1_wg", "ffn1_wu", "ffn1_wd", "norm_mix", "w_in", "conv_w", "conv_b", "dt_bias", "a_log",
         "d_skip", "ret_gn", "ssm_norm", "w_out", "norm_ffn2", "ffn2_wg", "ffn2_wu", "ffn2_wd", "final_ada_w", "final_ada_b", "final_norm")


def kernel(x, c, ada_w, ada_b, norm_ffn1, ffn1_wg, ffn1_wu, ffn1_wd, norm_mix, w_in, conv_w, conv_b, dt_bias, a_log, d_skip, ret_gn, ssm_norm, w_out, norm_ffn2, ffn2_wg, ffn2_wu, ffn2_wd, final_ada_w, final_ada_b, final_norm, loss_target, m_ada_w, m_ada_b, m_norm_ffn1, m_ffn1_wg, m_ffn1_wu, m_ffn1_wd, m_norm_mix, m_w_in, m_conv_w, m_conv_b, m_dt_bias, m_a_log, m_d_skip, m_ret_gn, m_ssm_norm, m_w_out, m_norm_ffn2, m_ffn2_wg, m_ffn2_wu, m_ffn2_wd, m_final_ada_w, m_final_ada_b, m_final_norm, v_ada_w, v_ada_b, v_norm_ffn1, v_ffn1_wg, v_ffn1_wu, v_ffn1_wd, v_norm_mix, v_w_in, v_conv_w, v_conv_b, v_dt_bias, v_a_log, v_d_skip, v_ret_gn, v_ssm_norm, v_w_out, v_norm_ffn2, v_ffn2_wg, v_ffn2_wu, v_ffn2_wd, v_final_ada_w, v_final_ada_b, v_final_norm):
    W = dict(ada_w=ada_w, ada_b=ada_b, norm_ffn1=norm_ffn1, ffn1_wg=ffn1_wg, ffn1_wu=ffn1_wu, ffn1_wd=ffn1_wd, norm_mix=norm_mix,
             w_in=w_in, conv_w=conv_w, conv_b=conv_b, dt_bias=dt_bias, a_log=a_log, d_skip=d_skip, ret_gn=ret_gn, ssm_norm=ssm_norm,
             w_out=w_out, norm_ffn2=norm_ffn2, ffn2_wg=ffn2_wg, ffn2_wu=ffn2_wu, ffn2_wd=ffn2_wd, final_ada_w=final_ada_w,
             final_ada_b=final_ada_b, final_norm=final_norm)
    M1 = dict(ada_w=m_ada_w, ada_b=m_ada_b, norm_ffn1=m_norm_ffn1, ffn1_wg=m_ffn1_wg, ffn1_wu=m_ffn1_wu, ffn1_wd=m_ffn1_wd,
              norm_mix=m_norm_mix, w_in=m_w_in, conv_w=m_conv_w, conv_b=m_conv_b, dt_bias=m_dt_bias, a_log=m_a_log, d_skip=m_d_skip,
              ret_gn=m_ret_gn, ssm_norm=m_ssm_norm, w_out=m_w_out, norm_ffn2=m_norm_ffn2, ffn2_wg=m_ffn2_wg, ffn2_wu=m_ffn2_wu,
              ffn2_wd=m_ffn2_wd, final_ada_w=m_final_ada_w, final_ada_b=m_final_ada_b, final_norm=m_final_norm)
    V2 = dict(ada_w=v_ada_w, ada_b=v_ada_b, norm_ffn1=v_norm_ffn1, ffn1_wg=v_ffn1_wg, ffn1_wu=v_ffn1_wu, ffn1_wd=v_ffn1_wd,
              norm_mix=v_norm_mix, w_in=v_w_in, conv_w=v_conv_w, conv_b=v_conv_b, dt_bias=v_dt_bias, a_log=v_a_log, d_skip=v_d_skip,
              ret_gn=v_ret_gn, ssm_norm=v_ssm_norm, w_out=v_w_out, norm_ffn2=v_norm_ffn2, ffn2_wg=v_ffn2_wg, ffn2_wu=v_ffn2_wu,
              ffn2_wd=v_ffn2_wd, final_ada_w=v_final_ada_w, final_ada_b=v_final_ada_b, final_norm=v_final_norm)
    D = D_MODEL
    S = x.shape[1]
    me = 4 * lax.axis_index("x") + 2 * lax.axis_index("y") + lax.axis_index("c")
    n_mod = ada_w.shape[2]
    n_fmod = final_ada_w.shape[1]

    c_all = allgather(jnp.broadcast_to(c, (8, D)), in_vmem=True, name="gather_c")[:, 0, :]
    ada_cols = jnp.concatenate([ada_w[0], ada_w[1], final_ada_w], axis=1)
    ada_bias = jnp.concatenate([lax.dynamic_slice(ada_b, (0, me * n_mod), (DEPTH, n_mod)).reshape(1, -1),
                                lax.dynamic_slice(final_ada_b, (me * n_fmod,), (n_fmod,)).reshape(1, -1)], axis=1)
    mod_sh, cond = ada_mod(jnp.pad(c_all, ((0, 8), (0, 0))), ada_cols, ada_bias, name="ada_mod")
    n_cols = mod_sh.shape[1]
    small_in = jnp.concatenate([mod_sh[:8], jnp.pad(conv_w.reshape(8, LANE), ((0, 0), (0, n_cols - LANE)))], axis=0)
    small_g = allgather(small_in, in_vmem=True, name="gather_mod")
    mod_rows = lax.dynamic_index_in_dim(small_g[:, :8, :], me, axis=1, keepdims=False)
    mod = [mod_rows[:, l * n_mod:(l + 1) * n_mod].reshape(9, D) for l in range(DEPTH)]
    fmod = mod_rows[:, DEPTH * n_mod:].reshape(2, D)
    conv_w_full = small_g[:, 8:, :LANE].reshape(N_DEV, DEPTH, SSM_CONV, LANE).transpose(1, 2, 0, 3).reshape(DEPTH, SSM_CONV, 8 * LANE)

    order = [(l, n) for l in range(DEPTH) for n in BIG]
    rows = {n: _piece_rows(n, W[n].shape[1:]) for n in BIG}
    offs, o = {}, 0
    for l, n in order:
        offs[l, n] = o
        o += rows[n][1]
    pack_of = lambda src, dtype: jnp.concatenate([_to_piece(n, src[n][l]).astype(dtype) for l, n in order], axis=0)
    wgath = allgather_two_level(pack_of(W, BF16), name="gather_weights")
    full = {n: [None] * DEPTH for n in BIG}
    for l, n in order:
        r = rows[n][0]
        full[n][l] = wgath[:, offs[l, n]:offs[l, n] + r, :].reshape(N_DEV * r, D)
    full["w_in"] = [_pad_w_in_t(w) for w in full["w_in"]]

    cos, sin = _rope_tables(S)
    h = x[0]
    target = loss_target[0]
    layer_p = []
    for l in range(DEPTH):
        layer_p.append(dict(norm_ffn1=_row(norm_ffn1[l]), norm_mix=_row(norm_mix[l]), norm_ffn2=_row(norm_ffn2[l]),
                            ret_gn=_row(ret_gn[l]), ssm_norm=_row(ssm_norm[l]), conv_w=conv_w_full[l], conv_b=_row(conv_b[l]),
                            dt_bias=_pair_rows(dt_bias[l]), a_log=_pair_rows(a_log[l]), d_skip=_pair_rows(d_skip[l])))
    mods = [[[_row(mod[l][3 * s + k]) for k in range(3)] for s in range(3)] for l in range(DEPTH)]

    saved = []
    for l in range(DEPTH):
        p = layer_p[l]
        h, s1 = _ffn_fwd(h, p["norm_ffn1"], mods[l][0], full["ffn1_wg"][l], full["ffn1_wu"][l], full["ffn1_wd"][l], f"l{l}_ffn1")
        h, s2 = _mix_fwd(h, p, mods[l][1], full["w_in"][l], full["w_out"][l], cos, sin, f"l{l}_mix")
        h, s3 = _ffn_fwd(h, p["norm_ffn2"], mods[l][2], full["ffn2_wg"][l], full["ffn2_wu"][l], full["ffn2_wd"][l], f"l{l}_ffn2")
        saved.append((s1, s2, s3))

    dh, fst = final_loss_bwd(h, _row(final_norm), _row(fmod[0]), _row(fmod[1]), target, name="final")
    big_g = {n: [None] * DEPTH for n in BIG}
    small_g_l = [None] * DEPTH
    dmod = [None] * DEPTH
    for l in reversed(range(DEPTH)):
        p = layer_p[l]
        s1, s2, s3 = saved[l]
        dh, (g2g, g2u, g2d), gn2, dm2 = _ffn_bwd(dh, s3, p["norm_ffn2"], mods[l][2], full["ffn2_wg"][l], full["ffn2_wu"][l],
                                                 full["ffn2_wd"][l], f"l{l}_ffn2")
        dh, gw_in, gw_out, sm, dm1 = _mix_bwd(dh, s2, p, mods[l][1], full["w_in"][l], full["w_out"][l], cos, sin, f"l{l}_mix")
        dh, (g1g, g1u, g1d), gn1, dm0 = _ffn_bwd(dh, s1, p["norm_ffn1"], mods[l][0], full["ffn1_wg"][l], full["ffn1_wu"][l],
                                                 full["ffn1_wd"][l], f"l{l}_ffn1")
        for n, g in zip(BIG, (g1g, g1u, g1d, gw_in, gw_out, g2g, g2u, g2d)):
            big_g[n][l] = g
        sm["norm_ffn1"], sm["norm_ffn2"] = gn1, gn2
        small_g_l[l] = sm
        dmod[l] = jnp.concatenate(dm0 + dm1 + dm2, axis=0)
    grad_x = dh[None]

    def send_piece(l, n):
        g = _unpad_w_in_t(big_g[n][l]) if n == "w_in" else big_g[n][l]
        r, rp = rows[n]
        return jnp.pad(g.reshape(N_DEV, r, D), ((0, 0), (0, rp - r), (0, 0)))

    spack = jnp.concatenate([send_piece(l, n) for l, n in order], axis=1)
    by_core = spack.reshape((N_DEV // 2, 2) + spack.shape[1:])
    from_sibling = sibling_exchange(by_core, name="exchange_sibling")
    own = lax.dynamic_index_in_dim(by_core, lax.axis_index("c"), axis=1, keepdims=False)
    rpack = chip_exchange(add_partials(own, from_sibling, name="add_sibling"), name="exchange_chips")
    outs = adamw_parts(rpack, *[pack_of(src, F32) for src in (W, M1, V2)], name="adamw_big")
    big_out = [{n: jnp.stack([_from_piece(n, o[offs[l, n]:offs[l, n] + rows[n][1]], W[n].shape[1:]) for l in range(DEPTH)])
                for n in BIG} for o in outs]

    stack2 = lambda key: jnp.stack([small_g_l[l][key] for l in range(DEPTH)])
    pieces = [("loss", fst[3, 0:1]), ("ada_b", jnp.stack(dmod)), ("final_ada_b", jnp.concatenate([fst[1], fst[2]])),
              ("norm_ffn1", stack2("norm_ffn1")), ("norm_mix", stack2("norm_mix")), ("norm_ffn2", stack2("norm_ffn2")),
              ("conv_w", stack2("conv_w")), ("conv_b", stack2("conv_b")), ("dt_bias", stack2("dt_bias")), ("a_log", stack2("a_log")),
              ("d_skip", stack2("d_skip")), ("ret_gn", stack2("ret_gn")), ("ssm_norm", stack2("ssm_norm")), ("final_norm", fst[0])]
    names = [n for n, _ in pieces]
    shapes = [a.shape for _, a in pieces]
    ppack, poffs = _pack([a for _, a in pieces])
    pg = allgather(ppack, in_vmem=True, name="gather_small")
    zero_like = lambda n, a: jnp.zeros(a.shape, F32)
    spacks = [_pack([(src[n] if n in SMALL else zero_like(n, a)) for n, a in pieces])[0] for src in (W, M1, V2)]
    souts = adamw_parts(pg, *spacks, name="adamw_small")
    small_out = [dict(zip(names, _unpack(o, poffs, shapes))) for o in souts]
    loss = small_out[0]["loss"][0]

    gathered = dict(zip(names, _unpack(pg, poffs, shapes)))
    conv_parts = lax.dynamic_slice_in_dim(gathered["conv_w"], me * LANE, LANE, axis=3).reshape(N_DEV, DEPTH * SSM_CONV, LANE)
    conv_out = [o.reshape(conv_w.shape) for o in adamw_parts(conv_parts, conv_w.reshape(-1, LANE), m_conv_w.reshape(-1, LANE),
                                                              v_conv_w.reshape(-1, LANE), name="adamw_conv_w")]
    cond_t = cond[:8].T
    ada_out = []
    for l in range(DEPTH):
        dsel = lax.dynamic_slice_in_dim(gathered["ada_b"][:, l, :], me * n_mod, n_mod, axis=1)
        ada_out.append(ada_adamw(cond_t, dsel, ada_w[l], m_ada_w[l], v_ada_w[l], name=f"adamw_ada_w{l}"))
    ada_out = [jnp.stack([ada_out[l][k] for l in range(DEPTH)]) for k in range(4)]
    fsel = lax.dynamic_slice_in_dim(gathered["final_ada_b"].reshape(N_DEV, 2 * D), me * n_fmod, n_fmod, axis=1)
    fada_out = ada_adamw(cond_t, fsel, final_ada_w, m_final_ada_w, v_final_ada_w, name="adamw_final_ada_w")

    def pick(k, n):
        if n in BIG:
            return big_out[k][n]
        if n == "ada_w":
            return ada_out[k]
        if n == "final_ada_w":
            return fada_out[k]
        if n == "conv_w":
            return conv_out[k]
        return small_out[k][n]

    return (loss, grad_x) + tuple(pick(k, n) for k in range(4) for n in NAMES)
```

```python
import functools
import math

import numpy as np
import jax
import jax.numpy as jnp
from jax import lax
from jax.experimental import pallas as pl
from jax.experimental.pallas import tpu as pltpu

F32 = jnp.float32
BF16 = jnp.bfloat16

D_MODEL = 1024
DEPTH = 2
RET_HEADS = 4
HEAD_DIM = 128
SSM_HEADS = 8
SSM_HEAD_DIM = 64
SSM_STATE = 128
SSM_CONV = 4
D_FF = 2816
ROPE_BASE = 10000.0
NORM_EPS = 1e-6
MIX_W = 1536
IN_W = 5128
IN_MAIN = 5120
IN_PAD = 5632
N_DEV = 8
LANE = 128

ADAM_LR = 0.001
ADAM_B1 = 0.9
ADAM_B2 = 0.999
ADAM_EPS = 1e-08
ADAM_WD = 0.01
ADAM_STEP = 10

TOKEN_TILE = 512
WGRAD_TOKENS = 2048
BIG_TOKEN_TILE = 2048
SEQ_BLOCK = 256
VMEM_LIMIT = 56 << 20
SB_SKIP = 120.0
SB_UNVISITED = -1e30

CB_RQ, CB_RK, CB_RV, CB_RG = 0, 4, 8, 12
CB_SQ, CB_SK, CB_SV = 16, 20, 24
CB_MZ, CB_XS, CB_BM, CB_CM, CB_DT = 28, 32, 36, 38, 40


def _cparams(*sem):
    return pltpu.CompilerParams(dimension_semantics=sem, vmem_limit_bytes=VMEM_LIMIT)


def _sds(shape, dtype):
    return jax.ShapeDtypeStruct(tuple(shape), dtype)


def _tile(n, *prefs):
    for p in prefs:
        if n % p == 0:
            return p
    return n


def _dot(a, b, dims):
    return lax.dot_general(a, b, (dims, ((), ())), preferred_element_type=F32)


NN = ((1,), (0,))
NT = ((1,), (1,))
TN = ((0,), (0,))


def _bf(x):
    return x.astype(BF16)


def _sigmoid(x):
    return jax.nn.sigmoid(x)


def _split2(x):
    hi = x.astype(BF16)
    lo = (x - hi.astype(F32)).astype(BF16)
    return hi, lo


def _split3(x):
    hi = x.astype(BF16)
    r = x - hi.astype(F32)
    mid = r.astype(BF16)
    lo = (r - mid.astype(F32)).astype(BF16)
    return hi, mid, lo


def matmul(a, b, *, ta=False, tb=False, tm=512, tn=512, tk=512, out_dtype=F32, name):
    M, K = (a.shape[1], a.shape[0]) if ta else a.shape
    N = b.shape[0] if tb else b.shape[1]
    tm, tn, tk = min(tm, M), min(tn, N), min(tk, K)
    assert M % tm == 0 and N % tn == 0 and K % tk == 0, (name, M, N, K, tm, tn, tk)
    nk = K // tk
    a_spec = pl.BlockSpec((tk, tm), lambda i, j, k: (k, i)) if ta else pl.BlockSpec((tm, tk), lambda i, j, k: (i, k))
    b_spec = pl.BlockSpec((tn, tk), lambda i, j, k: (j, k)) if tb else pl.BlockSpec((tk, tn), lambda i, j, k: (k, j))
    dims = ((0 if ta else 1,), (1 if tb else 0,))

    def body(a_ref, b_ref, o_ref, acc_ref):
        k = pl.program_id(2)
        p = _dot(_bf(a_ref[...]), _bf(b_ref[...]), dims)

        @pl.when(k == 0)
        def _():
            acc_ref[...] = p

        @pl.when(k > 0)
        def _():
            acc_ref[...] += p

        @pl.when(k == nk - 1)
        def _():
            o_ref[...] = acc_ref[...].astype(out_dtype)

    return pl.pallas_call(
        body, grid=(M // tm, N // tn, nk), in_specs=[a_spec, b_spec],
        out_specs=pl.BlockSpec((tm, tn), lambda i, j, k: (i, j)), out_shape=_sds((M, N), out_dtype),
        scratch_shapes=[pltpu.VMEM((tm, tn), F32)], name=name,
        compiler_params=_cparams("parallel", "parallel", "arbitrary"))(a, b)


def matmul_resid(a, w, h, gate, factor, *, name):
    M, K = a.shape
    N = w.shape[1]
    tm = min(TOKEN_TILE, M)

    def body(a_ref, w_ref, h_ref, g_ref, hn_ref, o_ref):
        out = _dot(a_ref[...], w_ref[...], NN)
        o_ref[...] = out
        hn_ref[...] = h_ref[...] + (factor * (1.0 + g_ref[...])) * out

    mn = pl.BlockSpec((tm, N), lambda i: (i, 0))
    return pl.pallas_call(
        body, grid=(M // tm,),
        in_specs=[pl.BlockSpec((tm, K), lambda i: (i, 0)), pl.BlockSpec((K, N), lambda i: (0, 0)), mn,
                  pl.BlockSpec((1, N), lambda i: (0, 0))],
        out_specs=[mn, mn], out_shape=[_sds((M, N), F32), _sds((M, N), F32)], name=name,
        compiler_params=_cparams("parallel"))(a, w, h, gate)


def ffn_up(u, wg_t, wu_t, *, name):
    M, K = u.shape
    N = wg_t.shape[0]
    tm, tn = min(BIG_TOKEN_TILE, M), _tile(N, 256)

    def body(u_ref, wg_ref, wu_ref, a_ref, b_ref, act_ref):
        uu = u_ref[...]
        a = _dot(uu, wg_ref[...], NT)
        b = _dot(uu, wu_ref[...], NT)
        a_ref[...] = a
        b_ref[...] = b
        act_ref[...] = _bf(a * _sigmoid(a) * b)

    mn = pl.BlockSpec((tm, tn), lambda i, j: (i, j))
    wspec = pl.BlockSpec((tn, K), lambda i, j: (j, 0))
    return pl.pallas_call(
        body, grid=(M // tm, N // tn), in_specs=[pl.BlockSpec((tm, K), lambda i, j: (i, 0)), wspec, wspec],
        out_specs=[mn, mn, mn], out_shape=[_sds((M, N), F32), _sds((M, N), F32), _sds((M, N), BF16)], name=name,
        compiler_params=_cparams("parallel", "parallel"))(u, wg_t, wu_t)


def ffn_dact(dout, wd, a, b, *, name):
    M, K = dout.shape
    N = wd.shape[0]
    tm, tn = min(BIG_TOKEN_TILE, M), _tile(N, 256)

    def body(d_ref, w_ref, a_ref, b_ref, da_ref, db_ref):
        dact = _dot(d_ref[...], w_ref[...], NT)
        av = a_ref[...]
        sg = _sigmoid(av)
        db_ref[...] = _bf(dact * av * sg)
        da_ref[...] = _bf(dact * b_ref[...] * (sg * (1.0 + av * (1.0 - sg))))

    mn = pl.BlockSpec((tm, tn), lambda i, j: (i, j))
    return pl.pallas_call(
        body, grid=(M // tm, N // tn),
        in_specs=[pl.BlockSpec((tm, K), lambda i, j: (i, 0)), pl.BlockSpec((tn, K), lambda i, j: (j, 0)), mn, mn],
        out_specs=[mn, mn], out_shape=[_sds((M, N), BF16), _sds((M, N), BF16)], name=name,
        compiler_params=_cparams("parallel", "parallel"))(dout, wd, a, b)


def norm_mod(h, gain, shift, scale, *, name):
    S, D = h.shape
    tm = min(TOKEN_TILE, S)

    def body(h_ref, g_ref, sh_ref, sc_ref, u_ref):
        x = h_ref[...]
        r = lax.rsqrt(jnp.mean(x * x, axis=-1, keepdims=True) + NORM_EPS)
        n = x * r * g_ref[...]
        u_ref[...] = _bf(n * (1.0 + sc_ref[...]) + sh_ref[...])

    row = pl.BlockSpec((1, D), lambda i: (0, 0))
    tile = pl.BlockSpec((tm, D), lambda i: (i, 0))
    return pl.pallas_call(body, grid=(S // tm,), in_specs=[tile, row, row, row], out_specs=tile,
                          out_shape=_sds((S, D), BF16), name=name, compiler_params=_cparams("parallel"))(h, gain, shift, scale)


def dgrad_norm_bwd(lhs, ws, spans, h, gain, scale, dres, *, name):
    S, D = h.shape
    tm = min(TOKEN_TILE, S)
    n, nw = len(lhs), len(ws)

    def body(*refs):
        l_refs, w_refs = refs[:n], refs[n:n + nw]
        h_ref, g_ref, sc_ref, dres_ref, dh_ref, st_ref = refs[n + nw:]
        du = None
        for lr, (k, r0) in zip(l_refs, spans):
            part = _dot(_bf(lr[...]), w_refs[k][r0:r0 + lr.shape[1], :], NN)
            du = part if du is None else du + part
        x = h_ref[...]
        g = g_ref[...]
        r = lax.rsqrt(jnp.mean(x * x, axis=-1, keepdims=True) + NORM_EPS)
        xhat = x * r
        dn = du * (1.0 + sc_ref[...])
        dxhat = dn * g
        dh_ref[...] = dres_ref[...] + r * (dxhat - xhat * jnp.mean(dxhat * xhat, axis=-1, keepdims=True))

        @pl.when(pl.program_id(0) == 0)
        def _():
            st_ref[...] = jnp.zeros_like(st_ref)

        st_ref[0:1, :] += jnp.sum(dn * xhat, axis=0, keepdims=True)
        st_ref[1:2, :] += jnp.sum(du, axis=0, keepdims=True)
        st_ref[2:3, :] += jnp.sum(du * (xhat * g), axis=0, keepdims=True)

    row = pl.BlockSpec((1, D), lambda i: (0, 0))
    tile = pl.BlockSpec((tm, D), lambda i: (i, 0))
    in_specs = [pl.BlockSpec((tm, l.shape[1]), lambda i: (i, 0)) for l in lhs]
    in_specs += [pl.BlockSpec(w.shape, lambda i: (0, 0)) for w in ws]
    in_specs += [tile, row, row, tile]
    return pl.pallas_call(
        body, grid=(S // tm,), in_specs=in_specs, out_specs=[tile, pl.BlockSpec((8, D), lambda i: (0, 0))],
        out_shape=[_sds((S, D), F32), _sds((8, D), F32)], name=name,
        compiler_params=_cparams("arbitrary"))(*lhs, *ws, h, gain, scale, dres)


def gate_bwd(dh, out, gate, factor, *, name):
    S, D = dh.shape
    tm = min(TOKEN_TILE, S)

    def body(dh_ref, o_ref, g_ref, do_ref, st_ref):
        d = dh_ref[...]
        do_ref[...] = _bf(d * (factor * (1.0 + g_ref[...])))

        @pl.when(pl.program_id(0) == 0)
        def _():
            st_ref[...] = jnp.zeros_like(st_ref)

        st_ref[0:1, :] += factor * jnp.sum(d * o_ref[...], axis=0, keepdims=True)

    tile = pl.BlockSpec((tm, D), lambda i: (i, 0))
    return pl.pallas_call(
        body, grid=(S // tm,), in_specs=[tile, tile, pl.BlockSpec((1, D), lambda i: (0, 0))],
        out_specs=[tile, pl.BlockSpec((8, D), lambda i: (0, 0))], out_shape=[_sds((S, D), BF16), _sds((8, D), F32)],
        name=name, compiler_params=_cparams("arbitrary"))(dh, out, gate)


def final_loss_bwd(h, gain, shift, scale, target, *, name):
    S, D = h.shape
    tm = min(TOKEN_TILE, S)

    def body(h_ref, g_ref, sh_ref, sc_ref, t_ref, dh_ref, st_ref):
        x = h_ref[...]
        g = g_ref[...]
        r = lax.rsqrt(jnp.mean(x * x, axis=-1, keepdims=True) + NORM_EPS)
        xhat = x * r
        n = xhat * g
        err = n * (1.0 + sc_ref[...]) + sh_ref[...] - t_ref[...]
        dy = err * (1.0 / D)
        dn = dy * (1.0 + sc_ref[...])
        dxhat = dn * g
        dh_ref[...] = r * (dxhat - xhat * jnp.mean(dxhat * xhat, axis=-1, keepdims=True))

        @pl.when(pl.program_id(0) == 0)
        def _():
            st_ref[...] = jnp.zeros_like(st_ref)

        st_ref[0:1, :] += jnp.sum(dn * xhat, axis=0, keepdims=True)
        st_ref[1:2, :] += jnp.sum(dy, axis=0, keepdims=True)
        st_ref[2:3, :] += jnp.sum(dy * n, axis=0, keepdims=True)
        tok = jnp.mean(err * err, axis=-1, keepdims=True)
        st_ref[3:4, :] += 0.5 * jnp.sum(tok, axis=0, keepdims=True)

    row = pl.BlockSpec((1, D), lambda i: (0, 0))
    tile = pl.BlockSpec((tm, D), lambda i: (i, 0))
    return pl.pallas_call(
        body, grid=(S // tm,), in_specs=[tile, row, row, row, tile], out_specs=[tile, pl.BlockSpec((8, D), lambda i: (0, 0))],
        out_shape=[_sds((S, D), F32), _sds((8, D), F32)], name=name,
        compiler_params=_cparams("arbitrary"))(h, gain, shift, scale, target)


def _ret_tables(T):
    heads = np.arange(RET_HEADS, dtype=np.float64)
    lg = np.log1p(-(2.0 ** (-5.0 - heads)))
    t = np.arange(T)
    same = (t[:, None] // 64) == (t[None, :] // 64)
    earlier = (t[None, :] // 64) < (t[:, None] // 64)
    dist = np.abs(t[:, None] - t[None, :]).astype(np.float64)
    dmat = np.where(same | earlier, np.exp(lg[:, None, None] * dist[None]), 0.0)
    qdec = np.exp(lg[:, None] * (t + 1.0)[None, :])
    kdec = np.exp(lg[:, None] * (T - 1.0 - t)[None, :])
    cdec = np.exp(lg * T)
    bc = lambda v: jnp.asarray(np.broadcast_to(v[:, :, None], (RET_HEADS, T, LANE)), F32)
    cd = jnp.asarray(np.broadcast_to(cdec[:, None, None], (RET_HEADS, LANE, LANE)), F32)
    return jnp.asarray(dmat, F32), bc(qdec), bc(kdec), cd


def _rope_tables(S):
    half = HEAD_DIM // 2
    inv_freq = ROPE_BASE ** (-jnp.arange(half, dtype=F32) / half)
    ang = jnp.arange(S, dtype=F32)[:, None] * inv_freq[None, :]
    cos, sin = jnp.cos(ang), jnp.sin(ang)
    return jnp.concatenate([cos, cos], axis=-1), jnp.concatenate([-sin, sin], axis=-1)


def _rope(x, c, s):
    return x * c + pltpu.roll(x, HEAD_DIM // 2, 1) * s


def _rope_t(dx, c, s):
    return dx * c + pltpu.roll(dx * s, HEAD_DIM // 2, 1)


def ret_fwd(proj, gn, cos, sin, *, name):
    S = proj.shape[0]
    T = min(SEQ_BLOCK, S)
    nb = S // T
    dmat, qdec, kdec, cdec = _ret_tables(T)

    def body(q_ref, k_ref, v_ref, g_ref, c_ref, s_ref, dm_ref, qd_ref, kd_ref, cd_ref, gn_ref, yo_ref, yp_ref, st_ref, state):
        @pl.when(pl.program_id(0) == 0)
        def _():
            state[...] = jnp.zeros_like(state)

        c, s = c_ref[...], s_ref[...]
        for h in range(RET_HEADS):
            cols = slice(LANE * h, LANE * (h + 1))
            qr = _rope(q_ref[:, cols], c, s)
            kr = _rope(k_ref[:, cols], c, s) * (HEAD_DIM ** -0.5)
            v = _bf(v_ref[:, cols])
            sp = state[h]
            st_ref[h] = sp
            a = _dot(_bf(qr), _bf(kr), NT) * dm_ref[h]
            y = _dot(_bf(a), v, NN) + _dot(_bf(qr * qd_ref[h]), _bf(sp), NN)
            state[h] = cd_ref[h] * sp + _dot(_bf(kr * kd_ref[h]), v, TN)
            yp_ref[:, cols] = y
            yn = y * lax.rsqrt(jnp.mean(y * y, axis=-1, keepdims=True) + NORM_EPS) * gn_ref[:, cols]
            g = g_ref[:, cols]
            yo_ref[:, cols] = _bf(yn * (g * _sigmoid(g)))

    col = lambda cb: pl.BlockSpec((T, 512), lambda b: (b, cb // 4))
    tok = pl.BlockSpec((T, LANE), lambda b: (b, 0))
    const = lambda shape: pl.BlockSpec(shape, lambda b: (0,) * len(shape))
    out_tok = pl.BlockSpec((T, 512), lambda b: (b, 0))
    return pl.pallas_call(
        body, grid=(nb,),
        in_specs=[col(CB_RQ), col(CB_RK), col(CB_RV), col(CB_RG), tok, tok, const(dmat.shape), const(qdec.shape), const(kdec.shape),
                  const(cdec.shape), const((1, 512))],
        out_specs=[out_tok, out_tok, pl.BlockSpec((None, RET_HEADS, LANE, LANE), lambda b: (b, 0, 0, 0))],
        out_shape=[_sds((S, 512), BF16), _sds((S, 512), F32), _sds((nb, RET_HEADS, LANE, LANE), F32)],
        scratch_shapes=[pltpu.VMEM((RET_HEADS, LANE, LANE), F32)], name=name,
        compiler_params=_cparams("arbitrary"))(proj, proj, proj, proj, cos, sin, dmat, qdec, kdec, cdec, gn)


def ret_bwd(proj, gn, cos, sin, ypre, states, dycat, *, name):
    S = proj.shape[0]
    T = min(SEQ_BLOCK, S)
    nb = S // T
    dmat, qdec, kdec, cdec = _ret_tables(T)

    def body(q_ref, k_ref, v_ref, g_ref, c_ref, s_ref, dm_ref, qd_ref, kd_ref, cd_ref, gn_ref, yp_ref, st_ref, dy_ref,
             d_ref, stat_ref, gstate):
        @pl.when(pl.program_id(0) == 0)
        def _():
            gstate[...] = jnp.zeros_like(gstate)
            stat_ref[...] = jnp.zeros_like(stat_ref)

        c, s = c_ref[...], s_ref[...]
        scale = HEAD_DIM ** -0.5
        for h in range(RET_HEADS):
            cols = slice(LANE * h, LANE * (h + 1))
            qr = _rope(q_ref[:, cols], c, s)
            kr = _rope(k_ref[:, cols], c, s) * scale
            v = _bf(v_ref[:, cols])
            qd, kd, dm = qd_ref[h], kd_ref[h], dm_ref[h]
            sp = _bf(st_ref[h])
            gs = gstate[h]
            gsb = _bf(gs)
            g = g_ref[:, cols]
            sg = _sigmoid(g)
            y = yp_ref[:, cols]
            gn_row = gn_ref[:, cols]
            r = lax.rsqrt(jnp.mean(y * y, axis=-1, keepdims=True) + NORM_EPS)
            yhat = y * r
            dyo = dy_ref[:, cols]
            d_ref[:, 1536 + LANE * h:1536 + LANE * (h + 1)] = _bf(dyo * (yhat * gn_row) * (sg * (1.0 + g * (1.0 - sg))))
            dyn = dyo * (g * sg)
            stat_ref[0:1, cols] += jnp.sum(dyn * yhat, axis=0, keepdims=True)
            dyhat = dyn * gn_row
            dy = _bf(r * (dyhat - yhat * jnp.mean(dyhat * yhat, axis=-1, keepdims=True)))
            qrb, krb = _bf(qr), _bf(kr)
            qdb = _bf(qr * qd)
            kdb = _bf(kr * kd)
            a = _bf(_dot(qrb, krb, NT) * dm)
            da = _bf(_dot(dy, v, NT) * dm)
            d_ref[:, 1024 + LANE * h:1024 + LANE * (h + 1)] = _bf(_dot(a, dy, TN) + _dot(kdb, gsb, NN))
            dqr = _dot(da, krb, NN) + qd * _dot(dy, sp, NT)
            dkr = _dot(da, qrb, TN) + kd * _dot(v, gsb, NT)
            gstate[h] = cd_ref[h] * gs + _dot(qdb, dy, TN)
            d_ref[:, cols] = _bf(_rope_t(dqr, c, s))
            d_ref[:, 512 + LANE * h:512 + LANE * (h + 1)] = _bf(_rope_t(dkr * scale, c, s))

    rb = lambda b: nb - 1 - b
    col = lambda cb: pl.BlockSpec((T, 512), lambda b: (rb(b), cb // 4))
    tok = pl.BlockSpec((T, LANE), lambda b: (rb(b), 0))
    const = lambda shape: pl.BlockSpec(shape, lambda b: (0,) * len(shape))
    tok512 = pl.BlockSpec((T, 512), lambda b: (rb(b), 0))
    return pl.pallas_call(
        body, grid=(nb,),
        in_specs=[col(CB_RQ), col(CB_RK), col(CB_RV), col(CB_RG), tok, tok, const(dmat.shape), const(qdec.shape), const(kdec.shape),
                  const(cdec.shape), const((1, 512)), tok512,
                  pl.BlockSpec((None, RET_HEADS, LANE, LANE), lambda b: (rb(b), 0, 0, 0)), tok512],
        out_specs=[pl.BlockSpec((T, 2048), lambda b: (rb(b), 0)), const((8, 512))],
        out_shape=[_sds((S, 2048), BF16), _sds((8, 512), F32)],
        scratch_shapes=[pltpu.VMEM((RET_HEADS, LANE, LANE), F32)], name=name,
        compiler_params=_cparams("arbitrary"))(proj, proj, proj, proj, cos, sin, dmat, qdec, kdec, cdec, gn, ypre, states, dycat)


def _sb_cast_kv(k_ref, v_ref, kb, vb, S):
    step = min(TOKEN_TILE, S)
    for r in range(0, S, step):
        kb[r:r + step, :] = _bf(k_ref[r:r + step, :])
        vb[r:r + step, :] = _bf(v_ref[r:r + step, :])


def _sb_logits(q, kblk, vis):
    z = _dot(q, kblk, NT) * (HEAD_DIM ** -0.5)
    l = jnp.log1p(jnp.exp(-jnp.abs(z)))
    lb = jnp.minimum(z, 0.0) - l
    lk = jnp.minimum(-z, 0.0) - l
    if vis is not None:
        lk = jnp.where(vis, lk, 0.0)
    return lb, lk


def _tri(T, cmp):
    r = lax.broadcasted_iota(jnp.int32, (T, T), 0)
    c = lax.broadcasted_iota(jnp.int32, (T, T), 1)
    return cmp(r, c)


def _dot_split2(x, m):
    hi, lo = _split2(x)
    return _dot(hi, m, NN) + _dot(lo, m, NN)


def _carried(carry_fn, refs, n_heads, n_steps):
    h, i = pl.program_id(0), pl.program_id(1)

    @pl.when((h == 0) & (i == 0))
    def _():
        carry_fn(*refs, start=True, finish=False)

    def finish():
        @pl.when((h == n_heads - 1) & (i == n_steps - 1))
        def _():
            carry_fn(*refs, start=False, finish=True)

    return finish


def sb_fwd(proj, *, name, gather=None):
    S = proj.shape[0]
    T = min(SEQ_BLOCK, S)
    nq = S // T

    assert nq <= LANE

    def body(*refs):
        if gather is None:
            q_ref, k_ref, v_ref, o_ref, cin_ref, kb, vb = refs
            finish_comm = lambda: None
        else:
            q_ref, k_ref, v_ref, src_ref, o_ref, cin_ref, dst_ref, kb, vb, send_sems, recv_sems, local_sem = refs
            finish_comm = _carried(_gather_two_level, (src_ref, dst_ref, send_sems, recv_sems, local_sem), RET_HEADS, nq)
        qi = pl.program_id(1)

        @pl.when(qi == 0)
        def _():
            _sb_cast_kv(k_ref, v_ref, kb, vb, S)

        q = _bf(q_ref[...])
        vis = _tri(T, lambda t, s: s < t)
        after = _tri(T, lambda j, s: j > s).astype(BF16)
        lane = lax.broadcasted_iota(jnp.int32, (T, LANE), 1)

        def block(jb, carry, acc, cin, mask):
            rows = pl.ds(pl.multiple_of(jb * T, T), T)
            lb, lk = _sb_logits(q, kb[rows, :], mask)
            tail = _dot_split2(lk, after) + carry
            w = jnp.exp(lb + tail)
            if mask is not None:
                w = jnp.where(mask, w, 0.0)
            return (carry + jnp.sum(lk, axis=1, keepdims=True), acc + _dot(_bf(w), vb[rows, :], NN),
                    jnp.where(lane == jb, carry, cin))

        st = block(qi, jnp.zeros((T, 1), F32), jnp.zeros((T, LANE), F32), jnp.full((T, LANE), SB_UNVISITED, F32), vis)

        def more(c):
            return (c[0] < qi) & (jnp.max(c[1]) > -SB_SKIP)

        def step(c):
            return (c[0] + 1,) + block(qi - 1 - c[0], c[1], c[2], c[3], None)

        st = lax.while_loop(more, step, (jnp.int32(0),) + st)
        o_ref[...] = st[2]
        cin_ref[...] = st[3]
        finish_comm()

    whole = lambda cb: pl.BlockSpec((S, LANE), lambda h, i: (0, cb + h))
    tok = pl.BlockSpec((T, LANE), lambda h, i: (i, h))
    hbm = pl.BlockSpec(memory_space=pltpu.HBM)
    extra = gather is not None
    return pl.pallas_call(
        body, grid=(RET_HEADS, nq),
        in_specs=[pl.BlockSpec((T, LANE), lambda h, i: (i, CB_SQ + h)), whole(CB_SK), whole(CB_SV)] + [hbm] * extra,
        out_specs=[tok, tok] + [hbm] * extra,
        out_shape=[_sds((S, 512), F32), _sds((S, 512), F32)] + ([_sds((N_DEV,) + gather.shape, gather.dtype)] if extra else []),
        scratch_shapes=[pltpu.VMEM((S, LANE), BF16), pltpu.VMEM((S, LANE), BF16)] + COMM_SCRATCH * extra, name=name,
        compiler_params=_cparams("arbitrary", "arbitrary"))(*((proj, proj, proj) + ((gather,) if extra else ())))


def sb_bwd(proj, cin, dycat, *, name, exchange=None):
    S = proj.shape[0]
    T = min(SEQ_BLOCK, S)
    nq = S // T
    scale = HEAD_DIM ** -0.5

    def body(*refs):
        if exchange is None:
            q_ref, k_ref, v_ref, cin_ref, do_ref, dq_ref, dk_ref, dv_ref, kb, vb = refs
            finish_comm = lambda: None
        else:
            (q_ref, k_ref, v_ref, cin_ref, do_ref, src_ref, dq_ref, dk_ref, dv_ref, dst_ref, kb, vb,
             send_sems, recv_sems, local_sem) = refs
            finish_comm = _carried(_chip_exchange, (src_ref, dst_ref, send_sems, recv_sems, local_sem), RET_HEADS, nq)
        qi = pl.program_id(1)

        @pl.when(qi == 0)
        def _():
            _sb_cast_kv(k_ref, v_ref, kb, vb, S)
            dk_ref[...] = jnp.zeros_like(dk_ref)
            dv_ref[...] = jnp.zeros_like(dv_ref)

        q = _bf(q_ref[...])
        dob = _bf(do_ref[...])
        cin = cin_ref[...]
        vis = _tri(T, lambda t, s: s < t)
        after = _tri(T, lambda j, s: j > s).astype(BF16)
        before = _tri(T, lambda s, j: s < j).astype(BF16)
        lane = lax.broadcasted_iota(jnp.int32, (T, LANE), 1)

        def block(jb, ecarry, dq, mask):
            rows = pl.ds(pl.multiple_of(jb * T, T), T)
            kblk, vblk = kb[rows, :], vb[rows, :]
            lb, lk = _sb_logits(q, kblk, mask)
            carry = jnp.sum(jnp.where(lane == jb, cin, 0.0), axis=1, keepdims=True)
            w = jnp.exp(lb + _dot_split2(lk, after) + carry)
            if mask is not None:
                w = jnp.where(mask, w, 0.0)
            e = w * _dot(dob, vblk, NT)
            dv_ref[rows, :] += _dot(_bf(w), dob, TN)
            dlk = _dot_split2(e, before) + ecarry
            beta = jnp.exp(lb)
            dz = e * (1.0 - beta) - beta * dlk
            if mask is not None:
                dz = jnp.where(mask, dz, 0.0)
            dzb = _bf(dz * scale)
            dk_ref[rows, :] += _dot(dzb, q, TN)
            return ecarry + jnp.sum(e, axis=1, keepdims=True), dq + _dot(dzb, kblk, NN)

        lane1 = lane[0:1, :]
        skipped = (jnp.max(cin, axis=0, keepdims=True) <= -SB_SKIP) & (lane1 < qi)
        first = jnp.sum(jnp.where(skipped, 1, 0))
        st = lax.fori_loop(first, qi, lambda jb, c: block(jb, c[0], c[1], None), (jnp.zeros((T, 1), F32), jnp.zeros((T, LANE), F32)))
        st = block(qi, st[0], st[1], vis)
        dq_ref[...] = _bf(st[1])
        finish_comm()

    whole = lambda cb: pl.BlockSpec((S, LANE), lambda h, i: (0, cb + h))
    tok = pl.BlockSpec((T, LANE), lambda h, i: (i, h))
    acc = pl.BlockSpec((S, LANE), lambda h, i: (0, h))
    hbm = pl.BlockSpec(memory_space=pltpu.HBM)
    extra = exchange is not None
    return pl.pallas_call(
        body, grid=(RET_HEADS, nq),
        in_specs=[pl.BlockSpec((T, LANE), lambda h, i: (i, CB_SQ + h)), whole(CB_SK), whole(CB_SV), tok,
                  pl.BlockSpec((T, LANE), lambda h, i: (i, 4 + h))] + [hbm] * extra,
        out_specs=[tok, acc, acc] + [hbm] * extra,
        out_shape=[_sds((S, 512), BF16), _sds((S, 512), F32), _sds((S, 512), F32)] + ([_sds(exchange.shape, exchange.dtype)] if extra else []),
        scratch_shapes=[pltpu.VMEM((S, LANE), BF16), pltpu.VMEM((S, LANE), BF16)] + COMM_SCRATCH * extra, name=name,
        compiler_params=_cparams("arbitrary", "arbitrary"))(*((proj, proj, proj, cin, dycat) + ((exchange,) if extra else ())))


def _shift_down(x, d, row):
    return jnp.where(row >= d, pltpu.roll(x, d, 0), 0.0)


def _shift_up(x, d, row, S):
    return jnp.where(row < S - d, pltpu.roll(x, S - d, 0), 0.0)


def conv_fwd(proj, conv_w, conv_b, *, name):
    S = proj.shape[0]

    def body(x_ref, w_ref, b_ref, pre_ref, act_ref):
        x = x_ref[...]
        row = lax.broadcasted_iota(jnp.int32, x.shape, 0)
        pre = b_ref[...] + w_ref[3:4, :] * x
        for d in range(1, SSM_CONV):
            pre = pre + w_ref[3 - d:4 - d, :] * _shift_down(x, d, row)
        pre_ref[...] = pre
        act_ref[...] = pre * _sigmoid(pre)

    blk = pl.BlockSpec((S, LANE), lambda c: (0, c))
    return pl.pallas_call(
        body, grid=(8,),
        in_specs=[pl.BlockSpec((S, LANE), lambda c: (0, CB_XS + c)), pl.BlockSpec((SSM_CONV, LANE), lambda c: (0, c)),
                  pl.BlockSpec((1, LANE), lambda c: (0, c))],
        out_specs=[blk, blk], out_shape=[_sds((S, 1024), F32), _sds((S, 1024), F32)], name=name,
        compiler_params=_cparams("parallel"))(proj, conv_w, conv_b)


def conv_bwd(proj, pre, dact, conv_w, *, name):
    S = proj.shape[0]

    def body(x_ref, pre_ref, da_ref, w_ref, dx_ref, st_ref):
        x = x_ref[...]
        p = pre_ref[...]
        row = lax.broadcasted_iota(jnp.int32, x.shape, 0)
        sg = _sigmoid(p)
        dpre = da_ref[...] * (sg * (1.0 + p * (1.0 - sg)))
        dx = w_ref[3:4, :] * dpre
        st_ref[3:4, :] = jnp.sum(dpre * x, axis=0, keepdims=True)
        for d in range(1, SSM_CONV):
            dx = dx + w_ref[3 - d:4 - d, :] * _shift_up(dpre, d, row, S)
            st_ref[3 - d:4 - d, :] = jnp.sum(dpre * _shift_down(x, d, row), axis=0, keepdims=True)
        st_ref[4:5, :] = jnp.sum(dpre, axis=0, keepdims=True)
        st_ref[5:8, :] = jnp.zeros((3, LANE), F32)
        dx_ref[...] = _bf(dx)

    blk = pl.BlockSpec((S, LANE), lambda c: (0, c))
    return pl.pallas_call(
        body, grid=(8,),
        in_specs=[pl.BlockSpec((S, LANE), lambda c: (0, CB_XS + c)), blk, blk, pl.BlockSpec((SSM_CONV, LANE), lambda c: (0, c))],
        out_specs=[blk, pl.BlockSpec((8, LANE), lambda c: (0, c))],
        out_shape=[_sds((S, 1024), BF16), _sds((8, 1024), F32)], name=name,
        compiler_params=_cparams("parallel"))(proj, pre, dact, conv_w)


def _softplus(x):
    return jnp.maximum(x, 0.0) + jnp.log1p(jnp.exp(-jnp.abs(x)))


def _pair(lane, v0, v1):
    return jnp.where(lane < SSM_HEAD_DIM, v0, v1)


def _ssd_pair_common(raw, dtb, alog, xs, cm, hprev, T):
    lane = lax.broadcasted_iota(jnp.int32, (T, LANE), 1)
    dt = _softplus(raw + dtb)
    a = -jnp.exp(alog)
    incl = _tri(T, lambda l, s: s <= l).astype(BF16)
    h1, h2, h3 = _split3(dt * a)
    acum = _dot(incl, h1, NN) + _dot(incl, h2, NN) + _dot(incl, h3, NN)
    acum_t = acum.T
    causal = _tri(T, lambda l, s: s <= l)
    decay = [jnp.where(causal, jnp.exp(jnp.minimum(acum[:, j:j + 1] - acum_t[j:j + 1, :], 0.0)), 0.0) for j in (0, 1)]
    dtc = _pair(lane, dt[:, 0:1], dt[:, 1:2])
    ac = _pair(lane, acum[:, 0:1], acum[:, 1:2])
    xdt = xs * dtc
    ea = jnp.exp(ac)
    e_end = jnp.exp(ac[T - 1:T, :] - ac)
    sub = lax.broadcasted_iota(jnp.int32, (LANE, LANE), 0)
    cd = jnp.where(sub < SSM_HEAD_DIM, jnp.exp(acum[T - 1:T, 0:1]), jnp.exp(acum[T - 1:T, 1:2]))
    r = _dot(cm, _bf(hprev), NT)
    return lane, dt, a, acum, decay, dtc, xdt, ea, e_end, cd, r


def ssd_fwd(xact, proj, dtb, alog, dskip, *, name):
    S = xact.shape[0]
    T = min(SEQ_BLOCK, S)
    nb = S // T

    def body(xs_ref, bm_ref, cm_ref, dt_ref, dtb_ref, al_ref, ds_ref, y_ref, st_ref, state):
        @pl.when(pl.program_id(0) == 0)
        def _():
            state[...] = jnp.zeros_like(state)

        for g in range(2):
            bm, cm = _bf(bm_ref[:, LANE * g:LANE * (g + 1)]), _bf(cm_ref[:, LANE * g:LANE * (g + 1)])
            gm = _dot(cm, bm, NT)
            for i in range(2):
                p = 2 * g + i
                cols = slice(LANE * p, LANE * (p + 1))
                xs = xs_ref[:, cols]
                hprev = state[p]
                st_ref[g, i] = hprev
                lane, dt, a, acum, decay, dtc, xdt, ea, e_end, cd, r = _ssd_pair_common(
                    dt_ref[:, cols], dtb_ref[p], al_ref[p], xs, cm, hprev, T)
                xdtb = _bf(xdt)
                y_intra = _pair(lane, _dot(_bf(gm * decay[0]), xdtb, NN), _dot(_bf(gm * decay[1]), xdtb, NN))
                state[p] = cd * hprev + _dot(_bf(xdt * e_end), bm, TN)
                dsk = ds_ref[p]
                lane1 = lane[0:1, :]
                y_ref[:, cols] = y_intra + ea * r + _pair(lane1, dsk[:, 0:1], dsk[:, 1:2]) * xs

    rows = pl.BlockSpec((4, 1, LANE), lambda b: (0, 0, 0))
    return pl.pallas_call(
        body, grid=(nb,),
        in_specs=[pl.BlockSpec((T, 512), lambda b: (b, 0)), pl.BlockSpec((T, 256), lambda b: (b, 2)), pl.BlockSpec((T, 256), lambda b: (b, 3)),
                  pl.BlockSpec((T, 512), lambda b: (b, CB_DT // 4)), rows, rows, rows],
        out_specs=[pl.BlockSpec((T, 512), lambda b: (b, 0)),
                   pl.BlockSpec((2, None, 2, LANE, LANE), lambda b: (0, b, 0, 0, 0))],
        out_shape=[_sds((S, 512), F32), _sds((2, nb, 2, LANE, LANE), F32)],
        scratch_shapes=[pltpu.VMEM((4, LANE, LANE), F32)], name=name,
        compiler_params=_cparams("arbitrary"))(xact, xact, xact, proj, dtb, alog, dskip)


def ssd_bwd(xact, proj, dtb, alog, dskip, states, dy, *, name):
    S = xact.shape[0]
    T = min(SEQ_BLOCK, S)
    nb = S // T

    def body(xs_ref, bm_ref, cm_ref, dt0_ref, dt1_ref, dtb_ref, al_ref, ds_ref, st_ref, dy_ref,
             dxs_ref, dbm_ref, dcm_ref, ddt_ref, stat_ref, dstate):
        @pl.when(pl.program_id(1) == 0)
        def _():
            dstate[...] = jnp.zeros_like(dstate)
            stat_ref[...] = jnp.zeros_like(stat_ref)

        bm, cm = _bf(bm_ref[...]), _bf(cm_ref[...])
        gm = _dot(cm, bm, NT)
        dbm = jnp.zeros((T, LANE), F32)
        dcm = jnp.zeros((T, LANE), F32)
        after_eq = _tri(T, lambda i, l: l >= i).astype(BF16)
        rowi = lax.broadcasted_iota(jnp.int32, (T, 1), 0)
        for i, dt_ref in enumerate((dt0_ref, dt1_ref)):
            xs = xs_ref[:, LANE * i:LANE * (i + 1)]
            dyp = dy_ref[:, LANE * i:LANE * (i + 1)]
            hprev = st_ref[i]
            dh = dstate[i]
            raw = dt_ref[...]
            lane, dt, a, acum, decay, dtc, xdt, ea, e_end, cd, r = _ssd_pair_common(
                raw, dtb_ref[i], al_ref[i], xs, cm, hprev, T)
            lane1 = lane[0:1, :]
            dsk = ds_ref[i]
            dskp = _pair(lane1, dsk[:, 0:1], dsk[:, 1:2])
            head = [lane < SSM_HEAD_DIM, lane >= SSM_HEAD_DIM]
            hsum = lambda v, j: jnp.sum(jnp.where(head[j], v, 0.0), axis=1, keepdims=True)
            dhb = _bf(dh)
            xdtb = _bf(xdt)
            dyb = _bf(dyp)
            z = xdt * e_end
            dz = _dot(bm, dhb, NT)
            dbm = dbm + _dot(_bf(z), dhb, NN)
            dxdt = dz * e_end
            de_e = dz * z
            drr = dyp * ea
            dea_ea = drr * r
            dcm = dcm + _dot(_bf(drr), _bf(hprev), NN)
            dstate[i] = cd * dh + _dot(_bf(drr), cm, TN)
            dcd_cd = cd * dh * hprev
            dgs = jnp.zeros((T, T), F32)
            da_cols = []
            for j in (0, 1):
                w = gm * decay[j]
                dw = _dot(_bf(jnp.where(head[j], dyp, 0.0)), xdtb, NT)
                dxdt = dxdt + jnp.where(head[j], _dot(_bf(w), dyb, TN), 0.0)
                dgs = dgs + dw * decay[j]
                dseg = dw * w
                col = jnp.sum(dseg, axis=1, keepdims=True) - jnp.sum(dseg.T, axis=1, keepdims=True)
                col = col + hsum(dea_ea, j) - hsum(de_e, j)
                sub = lax.broadcasted_iota(jnp.int32, (LANE, LANE), 0)
                in_head = (sub < SSM_HEAD_DIM) if j == 0 else (sub >= SSM_HEAD_DIM)
                end = jnp.sum(hsum(de_e, j), axis=0, keepdims=True) + jnp.sum(
                    jnp.sum(jnp.where(in_head, dcd_cd, 0.0), axis=1, keepdims=True), axis=0, keepdims=True)
                da_cols.append(col + jnp.where(rowi == T - 1, end, 0.0))
            dgb = _bf(dgs)
            dcm = dcm + _dot(dgb, bm, NN)
            dbm = dbm + _dot(dgb, cm, TN)
            dacum = jnp.where(lane == 0, da_cols[0], jnp.where(lane == 1, da_cols[1], 0.0))
            h1, h2, h3 = _split3(dacum)
            ddta = _dot(after_eq, h1, NN) + _dot(after_eq, h2, NN) + _dot(after_eq, h3, NN)
            dxs_ref[:, LANE * i:LANE * (i + 1)] = dskp * dyp + dxdt * dtc
            dx_x = dxdt * xs
            ddt = ddta * a + jnp.where(lane == 0, hsum(dx_x, 0), jnp.where(lane == 1, hsum(dx_x, 1), 0.0))
            ddraw = jnp.where(lane < 2, ddt * _sigmoid(raw + dtb_ref[i]), 0.0)
            ddt_ref[:, LANE * i:LANE * (i + 1)] = _bf(ddraw)
            dsum = jnp.sum(dyp * xs, axis=0, keepdims=True)
            d0 = jnp.sum(jnp.where(lane1 < SSM_HEAD_DIM, dsum, 0.0), axis=1, keepdims=True)
            d1 = jnp.sum(jnp.where(lane1 >= SSM_HEAD_DIM, dsum, 0.0), axis=1, keepdims=True)
            dd = jnp.where(lane1 == 0, d0, jnp.where(lane1 == 1, d1, 0.0))
            stat_ref[i, 0:1, :] += jnp.sum(ddraw, axis=0, keepdims=True)
            stat_ref[i, 1:2, :] += jnp.where(lane1 < 2, jnp.sum(ddta * dt, axis=0, keepdims=True) * a, 0.0)
            stat_ref[i, 2:3, :] += dd
        dbm_ref[...] = dbm
        dcm_ref[...] = dcm

    rb = lambda b: nb - 1 - b
    rows3 = pl.BlockSpec((2, 1, LANE), lambda g, b: (g, 0, 0))
    tok256 = pl.BlockSpec((T, 256), lambda g, b: (rb(b), g))
    tok128 = pl.BlockSpec((T, LANE), lambda g, b: (rb(b), g))
    return pl.pallas_call(
        body, grid=(2, nb),
        in_specs=[tok256, pl.BlockSpec((T, LANE), lambda g, b: (rb(b), 4 + g)), pl.BlockSpec((T, LANE), lambda g, b: (rb(b), 6 + g)),
                  pl.BlockSpec((T, LANE), lambda g, b: (rb(b), CB_DT + 2 * g)),
                  pl.BlockSpec((T, LANE), lambda g, b: (rb(b), CB_DT + 2 * g + 1)),
                  rows3, rows3, rows3,
                  pl.BlockSpec((None, None, 2, LANE, LANE), lambda g, b: (g, rb(b), 0, 0, 0)), tok256],
        out_specs=[tok256, tok128, tok128, tok256, pl.BlockSpec((2, 8, LANE), lambda g, b: (g, 0, 0))],
        out_shape=[_sds((S, 512), F32), _sds((S, 256), F32), _sds((S, 256), F32), _sds((S, 512), BF16), _sds((4, 8, LANE), F32)],
        scratch_shapes=[pltpu.VMEM((2, LANE, LANE), F32)], name=name,
        compiler_params=_cparams("parallel", "arbitrary"))(xact, xact, xact, proj, proj, dtb, alog, dskip, states, dy)


def gated_norm(ypre, proj, gain, *, name):
    S, W = ypre.shape
    tm = min(TOKEN_TILE, S)

    def body(y_ref, z_ref, g_ref, o_ref):
        z = z_ref[...]
        yg = y_ref[...] * (z * _sigmoid(z))
        o_ref[...] = _bf(yg * lax.rsqrt(jnp.mean(yg * yg, axis=-1, keepdims=True) + NORM_EPS) * g_ref[...])

    tile = pl.BlockSpec((tm, W), lambda i: (i, 0))
    return pl.pallas_call(
        body, grid=(S // tm,), in_specs=[tile, pl.BlockSpec((tm, W), lambda i: (i, CB_MZ // 4)), pl.BlockSpec((1, W), lambda i: (0, 0))],
        out_specs=tile, out_shape=_sds((S, W), BF16), name=name, compiler_params=_cparams("parallel"))(ypre, proj, gain)


def gated_norm_bwd(ypre, proj, gain, dycat, *, name):
    S, W = ypre.shape
    tm = min(TOKEN_TILE, S)

    def body(y_ref, z_ref, g_ref, dy_ref, dyp_ref, dz_ref, st_ref):
        z = z_ref[...]
        y = y_ref[...]
        sg = _sigmoid(z)
        sz = z * sg
        yg = y * sz
        r = lax.rsqrt(jnp.mean(yg * yg, axis=-1, keepdims=True) + NORM_EPS)
        yhat = yg * r
        dyo = dy_ref[...]

        @pl.when(pl.program_id(0) == 0)
        def _():
            st_ref[...] = jnp.zeros_like(st_ref)

        st_ref[0:1, :] += jnp.sum(dyo * yhat, axis=0, keepdims=True)
        dyhat = dyo * g_ref[...]
        dyg = r * (dyhat - yhat * jnp.mean(dyhat * yhat, axis=-1, keepdims=True))
        dyp_ref[...] = dyg * sz
        dz_ref[...] = _bf(dyg * y * (sg * (1.0 + z * (1.0 - sg))))

    tile = pl.BlockSpec((tm, W), lambda i: (i, 0))
    return pl.pallas_call(
        body, grid=(S // tm,),
        in_specs=[tile, pl.BlockSpec((tm, W), lambda i: (i, CB_MZ // 4)), pl.BlockSpec((1, W), lambda i: (0, 0)),
                  pl.BlockSpec((tm, W), lambda i: (i, 2))],
        out_specs=[tile, tile, pl.BlockSpec((8, W), lambda i: (0, 0))],
        out_shape=[_sds((S, W), F32), _sds((S, W), BF16), _sds((8, W), F32)], name=name,
        compiler_params=_cparams("arbitrary"))(ypre, proj, gain, dycat)


def ada_mod(c_all, w, bias, *, name):
    M, K = c_all.shape
    N = w.shape[1]
    tn = _tile(N, 512)

    def body(c_ref, w_ref, b_ref, o_ref, cond_ref):
        cv = c_ref[...]
        cond = cv * _sigmoid(cv)
        cond_ref[...] = cond
        o_ref[...] = _dot(_bf(cond), _bf(w_ref[...]), NN) + b_ref[...]

    return pl.pallas_call(
        body, grid=(N // tn,),
        in_specs=[pl.BlockSpec((M, K), lambda j: (0, 0)), pl.BlockSpec((K, tn), lambda j: (0, j)), pl.BlockSpec((1, tn), lambda j: (0, j))],
        out_specs=[pl.BlockSpec((M, tn), lambda j: (0, j)), pl.BlockSpec((M, K), lambda j: (0, 0))],
        out_shape=[_sds((M, N), F32), _sds((M, K), F32)], name=name, compiler_params=_cparams("arbitrary"))(c_all, w, bias)


def _adamw(g, w, m, v):
    m = ADAM_B1 * m + (1.0 - ADAM_B1) * g
    v = ADAM_B2 * v + (1.0 - ADAM_B2) * (g * g)
    m_hat = m / (1.0 - ADAM_B1 ** ADAM_STEP)
    v_hat = v / (1.0 - ADAM_B2 ** ADAM_STEP)
    return -ADAM_LR * (m_hat / (jnp.sqrt(v_hat) + ADAM_EPS) + ADAM_WD * w), m, v


def adamw_parts(parts, w, m, v, *, name):
    P, R, C = parts.shape
    tr = _tile(R, 592, 512, 256, 160, 128, 80, 64, 32, 16)

    def body(p_ref, w_ref, m_ref, v_ref, g_ref, d_ref, mo_ref, vo_ref):
        g = p_ref[0].astype(F32)
        for j in range(1, P):
            g = g + p_ref[j].astype(F32)
        g_ref[...] = g
        d_ref[...], mo_ref[...], vo_ref[...] = _adamw(g, w_ref[...], m_ref[...], v_ref[...])

    tile = pl.BlockSpec((tr, C), lambda i: (i, 0))
    return pl.pallas_call(
        body, grid=(R // tr,), in_specs=[pl.BlockSpec((P, tr, C), lambda i: (0, i, 0)), tile, tile, tile],
        out_specs=[tile] * 4, out_shape=[_sds((R, C), F32)] * 4, name=name, compiler_params=_cparams("parallel"))(parts, w, m, v)


def ada_adamw(cond_t, dmod, w, m, v, *, name):
    D, N = w.shape
    tr = _tile(D, 256)

    def body(c_ref, d_ref, w_ref, m_ref, v_ref, g_ref, dl_ref, mo_ref, vo_ref):
        cc = c_ref[...]
        dd = d_ref[...]
        g = cc[:, 0:1] * dd[0:1, :]
        for b in range(1, N_DEV):
            g = g + cc[:, b:b + 1] * dd[b:b + 1, :]
        g_ref[...] = g
        dl_ref[...], mo_ref[...], vo_ref[...] = _adamw(g, w_ref[...], m_ref[...], v_ref[...])

    tile = pl.BlockSpec((tr, N), lambda i: (i, 0))
    return pl.pallas_call(
        body, grid=(D // tr,), in_specs=[pl.BlockSpec((tr, N_DEV), lambda i: (i, 0)), pl.BlockSpec((N_DEV, N), lambda i: (0, 0)), tile, tile, tile],
        out_specs=[tile] * 4, out_shape=[_sds((D, N), F32)] * 4, name=name, compiler_params=_cparams("parallel"))(cond_t, dmod, w, m, v)


def _my_place():
    mx, my, mc = lax.axis_index("x"), lax.axis_index("y"), lax.axis_index("c")
    return mx, my, mc, 4 * mx + 2 * my + mc


def _peer(mx, my, mc, k):
    px = 1 - mx if (k >> 2) & 1 else mx
    py = 1 - my if (k >> 1) & 1 else my
    pc = 1 - mc if k & 1 else mc
    return (px, py, pc), 4 * px + 2 * py + pc


def _comm_call(body, x, out_shape, space, name):
    spec = pl.BlockSpec(memory_space=space)
    return pl.pallas_call(
        body, in_specs=[spec], out_specs=spec, out_shape=out_shape,
        scratch_shapes=[pltpu.SemaphoreType.DMA((N_DEV - 1,)), pltpu.SemaphoreType.DMA((N_DEV - 1,)), pltpu.SemaphoreType.DMA(())],
        name=name, compiler_params=pltpu.CompilerParams(has_side_effects=True, vmem_limit_bytes=VMEM_LIMIT))(x)


def allgather(x, *, in_vmem, name):
    def body(x_ref, out_ref, send_sems, recv_sems, local_sem):
        mx, my, mc, me = _my_place()
        mine = pltpu.make_async_copy(x_ref, out_ref.at[me], local_sem)
        mine.start()
        copies = []
        for k in range(1, N_DEV):
            peer, _ = _peer(mx, my, mc, k)
            cp = pltpu.make_async_remote_copy(src_ref=x_ref, dst_ref=out_ref.at[me], send_sem=send_sems.at[k - 1],
                                              recv_sem=recv_sems.at[k - 1], device_id=peer, device_id_type=pl.DeviceIdType.MESH)
            cp.start()
            copies.append(cp)
        for cp in copies:
            cp.wait()
        mine.wait()

    return _comm_call(body, x, _sds((N_DEV,) + x.shape, x.dtype), pltpu.VMEM if in_vmem else pltpu.HBM, name)


def allgather_two_level(x, *, name):
    def body(x_ref, out_ref, send_sems, recv_sems, local_sem):
        _gather_two_level(x_ref, out_ref, send_sems, recv_sems, local_sem, start=True, finish=True)

    return _comm_call(body, x, _sds((N_DEV,) + x.shape, x.dtype), pltpu.HBM, name)


def _gather_two_level(x_ref, out_ref, send_sems, recv_sems, local_sem, *, start, finish):
    mx, my, mc, _ = _my_place()
    me, sibling = (mx, my, mc), (mx, my, 1 - mc)
    chips = [(1 - mx, my), (mx, 1 - my), (1 - mx, 1 - my)]

    def copy(k, block, to, src=None):
        slot = out_ref.at[4 * block[0] + 2 * block[1] + block[2]]
        return pltpu.make_async_remote_copy(src_ref=slot if src is None else src, dst_ref=slot, send_sem=send_sems.at[k],
                                            recv_sem=recv_sems.at[k], device_id=to, device_id_type=pl.DeviceIdType.MESH)

    mine = pltpu.make_async_copy(x_ref, out_ref.at[4 * mx + 2 * my + mc], local_sem)
    first = [copy(0, me, sibling, src=x_ref)] + [copy(1 + j, me, (*chip, mc), src=x_ref) for j, chip in enumerate(chips)]
    if start:
        mine.start()
        for cp in first:
            cp.start()
    if finish:
        passed = [copy(4 + j, (*chip, mc), sibling) for j, chip in enumerate(chips)]
        for j, chip in enumerate(chips):
            copy(1 + j, (*chip, mc), me).wait_recv()
            passed[j].start()
        copy(0, sibling, me).wait_recv()
        for j, chip in enumerate(chips):
            copy(4 + j, (*chip, 1 - mc), me).wait_recv()
        for cp in first + passed:
            cp.wait_send()
        mine.wait()


def _chip_exchange(s_ref, r_ref, send_sems, recv_sems, local_sem, *, start, finish):
    mx, my, mc, _ = _my_place()
    mine = pltpu.make_async_copy(s_ref.at[2 * mx + my], r_ref.at[2 * mx + my], local_sem)
    copies = []
    for k in range(1, 4):
        px = 1 - mx if (k >> 1) & 1 else mx
        py = 1 - my if k & 1 else my
        copies.append(pltpu.make_async_remote_copy(
            src_ref=s_ref.at[2 * px + py], dst_ref=r_ref.at[2 * mx + my], send_sem=send_sems.at[k - 1], recv_sem=recv_sems.at[k - 1],
            device_id=(px, py, mc), device_id_type=pl.DeviceIdType.MESH))
    if start:
        mine.start()
        for cp in copies:
            cp.start()
    if finish:
        for cp in copies:
            cp.wait()
        mine.wait()


COMM_SCRATCH = [pltpu.SemaphoreType.DMA((N_DEV - 1,)), pltpu.SemaphoreType.DMA((N_DEV - 1,)), pltpu.SemaphoreType.DMA(())]


def sibling_exchange(send, *, name):
    n_chip, _, R, C = send.shape

    def body(s_ref, r_ref, send_sems, recv_sems, local_sem):
        mx, my, mc, _ = _my_place()
        copies = []
        for q in range(n_chip):
            cp = pltpu.make_async_remote_copy(src_ref=s_ref.at[q, 1 - mc], dst_ref=r_ref.at[q], send_sem=send_sems.at[q],
                                              recv_sem=recv_sems.at[q], device_id=(mx, my, 1 - mc), device_id_type=pl.DeviceIdType.MESH)
            cp.start()
            copies.append(cp)
        for cp in copies:
            cp.wait()

    return _comm_call(body, send, _sds((n_chip, R, C), send.dtype), pltpu.HBM, name)


def chip_exchange(send, *, name):
    def body(s_ref, r_ref, send_sems, recv_sems, local_sem):
        _chip_exchange(s_ref, r_ref, send_sems, recv_sems, local_sem, start=True, finish=True)

    return _comm_call(body, send, _sds(send.shape, send.dtype), pltpu.HBM, name)


def add_partials(a, b, *, name):
    P, R, C = a.shape
    tr = _tile(R, 592, 512, 256, 160, 128, 80, 64, 32, 16)

    def body(a_ref, b_ref, o_ref):
        o_ref[...] = _bf(a_ref[...].astype(F32) + b_ref[...].astype(F32))

    tile = pl.BlockSpec((P, tr, C), lambda i: (0, i, 0))
    return pl.pallas_call(body, grid=(R // tr,), in_specs=[tile, tile], out_specs=tile, out_shape=_sds((P, R, C), BF16), name=name,
                          compiler_params=_cparams("parallel"))(a, b)


def _rows128(a):
    f = a.reshape(-1)
    n = -(-f.shape[0] // (16 * LANE)) * (16 * LANE)
    return jnp.pad(f, (0, n - f.shape[0])).reshape(-1, LANE)


PACK_ROWS = 512


def _pad_rows(buf):
    r = buf.shape[-2]
    pad = -r % PACK_ROWS
    return jnp.pad(buf, [(0, 0)] * (buf.ndim - 2) + [(0, pad), (0, 0)])


def _pack(arrays):
    parts = [_rows128(a) for a in arrays]
    offs = np.cumsum([0] + [p.shape[0] for p in parts])
    return _pad_rows(jnp.concatenate(parts, axis=0)), [int(o) for o in offs]


def _unpack(buf, offs, shapes):
    lead = buf.shape[:-2]
    out = []
    for o, shp in zip(offs, shapes):
        n = int(np.prod(shp))
        rows = -(-n // LANE)
        seg = buf[..., o:o + rows, :].reshape(lead + (rows * LANE,))[..., :n]
        out.append(seg.reshape(lead + tuple(shp)))
    return out


def _pad_w_in_t(w_t):
    D = w_t.shape[1]
    dt = jnp.pad(w_t[IN_MAIN:].reshape(4, 2, D), ((0, 0), (0, LANE - 2), (0, 0)))
    return jnp.concatenate([w_t[:IN_MAIN], dt.reshape(4 * LANE, D)], axis=0)


def _unpad_w_in_t(g_t):
    D = g_t.shape[1]
    dt = g_t[IN_MAIN:].reshape(4, LANE, D)[:, :2].reshape(SSM_HEADS, D)
    return jnp.concatenate([g_t[:IN_MAIN], dt], axis=0)


def _piece_rows(n, shard_shape):
    r = shard_shape[0] if n in ROW_SHARDED else shard_shape[1]
    return r, -(-r // 16) * 16


def _to_piece(n, shard):
    t = shard if n in ROW_SHARDED else shard.T
    return jnp.pad(t, ((0, -t.shape[0] % 16), (0, 0)))


def _from_piece(n, piece, shard_shape):
    r, _ = _piece_rows(n, shard_shape)
    return piece[:r] if n in ROW_SHARDED else piece[:r].T


def _pair_rows(p):
    return jnp.pad(p.reshape(4, 1, 2), ((0, 0), (0, 0), (0, LANE - 2)))


def _row(v):
    return v.reshape(1, -1)


def _ffn_fwd(h, gain, mod3, wg, wu, wd, tag):
    shift, scale, gate = mod3
    u = norm_mod(h, gain, shift, scale, name=tag + "_norm")
    a, b, act = ffn_up(u, wg, wu, name=tag + "_up")
    hn, out = matmul_resid(act, wd, h, gate, 0.5, name=tag + "_down")
    return hn, (h, u, a, b, act, out)


def _wgrad(a, b, name):
    return matmul(a, b, ta=True, tm=_tile(a.shape[1], 1408, 1536, 1024, 512), tn=b.shape[1], tk=WGRAD_TOKENS, out_dtype=BF16, name=name)


def _ffn_bwd(dh, saved, gain, mod3, wg, wu, wd, tag):
    h, u, a, b, act, out = saved
    _, scale, gate = mod3
    dout, gst = gate_bwd(dh, out, gate, 0.5, name=tag + "_gate_bwd")
    da, db = ffn_dact(dout, wd, a, b, name=tag + "_dact")
    dh_prev, nst = dgrad_norm_bwd([da, db], [wg, wu], [(0, 0), (1, 0)], h, gain, scale, dh, name=tag + "_dgrad")
    grads = (_wgrad(da, u, tag + "_dwg"), _wgrad(db, u, tag + "_dwu"), _wgrad(act, dout, tag + "_dwd"))
    return dh_prev, grads, nst[0], [nst[1], nst[2], gst[0]]


def _mix_fwd(h, p, mod3, w_in, w_out, cos, sin, tag, gather=None):
    shift, scale, gate = mod3
    u = norm_mod(h, p["norm_mix"], shift, scale, name=tag + "_norm")
    proj = matmul(u, w_in, tb=True, tm=BIG_TOKEN_TILE, tn=512, tk=D_MODEL, name=tag + "_proj")
    y_ret, ypre_ret, st_ret = ret_fwd(proj, p["ret_gn"], cos, sin, name=tag + "_ret")
    y_sb, sb_cin, *gathered = sb_fwd(proj, name=tag + "_sb", gather=gather)
    pre, xact = conv_fwd(proj, p["conv_w"], p["conv_b"], name=tag + "_conv")
    ypre_ssm, st_ssm = ssd_fwd(xact, proj, p["dt_bias"], p["a_log"], p["d_skip"], name=tag + "_ssd")
    y_ssm = gated_norm(ypre_ssm, proj, p["ssm_norm"], name=tag + "_gnorm")
    ycat = jnp.concatenate([y_ret, y_sb.astype(BF16), y_ssm], axis=1)
    hn, mixed = matmul_resid(ycat, w_out, h, gate, 1.0, name=tag + "_out")
    return hn, (h, u, proj, ypre_ret, st_ret, sb_cin, pre, xact, ypre_ssm, st_ssm, ycat, mixed), (gathered[0] if gathered else None)


def _mix_bwd(dh, saved, p, mod3, w_in, w_out, cos, sin, tag, exchange=None):
    h, u, proj, ypre_ret, st_ret, sb_cin, pre, xact, ypre_ssm, st_ssm, ycat, mixed = saved
    _, scale, gate = mod3
    dmixed, gst = gate_bwd(dh, mixed, gate, 1.0, name=tag + "_gate_bwd")
    dycat = matmul(dmixed, w_out, tb=True, tm=BIG_TOKEN_TILE, tn=512, tk=D_MODEL, name=tag + "_dycat")
    dw_out = _wgrad(ycat, dmixed, tag + "_dw_out")
    dret, rst = ret_bwd(proj, p["ret_gn"], cos, sin, ypre_ret, st_ret, dycat, name=tag + "_ret_bwd")
    dsq, dsk, dsv, *exchanged = sb_bwd(proj, sb_cin, dycat, name=tag + "_sb_bwd", exchange=exchange)
    dypre, dz, nst2 = gated_norm_bwd(ypre_ssm, proj, p["ssm_norm"], dycat, name=tag + "_gnorm_bwd")
    dxs, dbm, dcm, ddt, sst = ssd_bwd(xact, proj, p["dt_bias"], p["a_log"], p["d_skip"], st_ssm, dypre, name=tag + "_ssd_bwd")
    dact = jnp.concatenate([dxs, dbm, dcm], axis=1)
    dxbc, cst = conv_bwd(proj, pre, dact, p["conv_w"], name=tag + "_conv_bwd")
    pieces = [dret, dsq, dsk, dsv, dz, dxbc, ddt]
    starts = np.cumsum([0] + [pc.shape[1] for pc in pieces])
    assert starts[-1] == IN_PAD
    dh_prev, nst = dgrad_norm_bwd(pieces, [w_in], [(0, int(r0)) for r0 in starts[:-1]], h, p["norm_mix"], scale, dh, name=tag + "_dgrad")
    dw_in = jnp.concatenate([_wgrad(pc, u, f"{tag}_dw_in{i}") for i, pc in enumerate(pieces)], axis=0)
    small = dict(norm_mix=nst[0], ret_gn=rst[0], ssm_norm=nst2[0], conv_w=cst[0:4], conv_b=cst[4],
                 dt_bias=sst[:, 0, :2].reshape(SSM_HEADS), a_log=sst[:, 1, :2].reshape(SSM_HEADS), d_skip=sst[:, 2, :2].reshape(SSM_HEADS))
    return dh_prev, dw_in, dw_out, small, [nst[1], nst[2], gst[0]], (exchanged[0] if exchanged else None)


BIG = ("ffn1_wg", "ffn1_wu", "ffn1_wd", "w_in", "w_out", "ffn2_wg", "ffn2_wu", "ffn2_wd")
ROW_SHARDED = ("ffn1_wd", "w_out", "ffn2_wd")
SMALL = ("ada_b", "norm_ffn1", "norm_mix", "conv_b", "dt_bias", "a_log", "d_skip", "ret_gn", "ssm_norm", "norm_ffn2",
         "final_ada_b", "final_norm")
NAMES = ("ada_w", "ada_b", "norm_ffn1", "ffn1_wg", "ffn1_wu", "ffn1_wd", "norm_mix", "w_in", "conv_w", "conv_b", "dt_bias", "a_log",
         "d_skip", "ret_gn", "ssm_norm", "w_out", "norm_ffn2", "ffn2_wg", "ffn2_wu", "ffn2_wd", "final_ada_w", "final_ada_b", "final_norm")


def kernel(x, c, ada_w, ada_b, norm_ffn1, ffn1_wg, ffn1_wu, ffn1_wd, norm_mix, w_in, conv_w, conv_b, dt_bias, a_log, d_skip, ret_gn, ssm_norm, w_out, norm_ffn2, ffn2_wg, ffn2_wu, ffn2_wd, final_ada_w, final_ada_b, final_norm, loss_target, m_ada_w, m_ada_b, m_norm_ffn1, m_ffn1_wg, m_ffn1_wu, m_ffn1_wd, m_norm_mix, m_w_in, m_conv_w, m_conv_b, m_dt_bias, m_a_log, m_d_skip, m_ret_gn, m_ssm_norm, m_w_out, m_norm_ffn2, m_ffn2_wg, m_ffn2_wu, m_ffn2_wd, m_final_ada_w, m_final_ada_b, m_final_norm, v_ada_w, v_ada_b, v_norm_ffn1, v_ffn1_wg, v_ffn1_wu, v_ffn1_wd, v_norm_mix, v_w_in, v_conv_w, v_conv_b, v_dt_bias, v_a_log, v_d_skip, v_ret_gn, v_ssm_norm, v_w_out, v_norm_ffn2, v_ffn2_wg, v_ffn2_wu, v_ffn2_wd, v_final_ada_w, v_final_ada_b, v_final_norm):
    W = dict(ada_w=ada_w, ada_b=ada_b, norm_ffn1=norm_ffn1, ffn1_wg=ffn1_wg, ffn1_wu=ffn1_wu, ffn1_wd=ffn1_wd, norm_mix=norm_mix,
             w_in=w_in, conv_w=conv_w, conv_b=conv_b, dt_bias=dt_bias, a_log=a_log, d_skip=d_skip, ret_gn=ret_gn, ssm_norm=ssm_norm,
             w_out=w_out, norm_ffn2=norm_ffn2, ffn2_wg=ffn2_wg, ffn2_wu=ffn2_wu, ffn2_wd=ffn2_wd, final_ada_w=final_ada_w,
             final_ada_b=final_ada_b, final_norm=final_norm)
    M1 = dict(ada_w=m_ada_w, ada_b=m_ada_b, norm_ffn1=m_norm_ffn1, ffn1_wg=m_ffn1_wg, ffn1_wu=m_ffn1_wu, ffn1_wd=m_ffn1_wd,
              norm_mix=m_norm_mix, w_in=m_w_in, conv_w=m_conv_w, conv_b=m_conv_b, dt_bias=m_dt_bias, a_log=m_a_log, d_skip=m_d_skip,
              ret_gn=m_ret_gn, ssm_norm=m_ssm_norm, w_out=m_w_out, norm_ffn2=m_norm_ffn2, ffn2_wg=m_ffn2_wg, ffn2_wu=m_ffn2_wu,
              ffn2_wd=m_ffn2_wd, final_ada_w=m_final_ada_w, final_ada_b=m_final_ada_b, final_norm=m_final_norm)
    V2 = dict(ada_w=v_ada_w, ada_b=v_ada_b, norm_ffn1=v_norm_ffn1, ffn1_wg=v_ffn1_wg, ffn1_wu=v_ffn1_wu, ffn1_wd=v_ffn1_wd,
              norm_mix=v_norm_mix, w_in=v_w_in, conv_w=v_conv_w, conv_b=v_conv_b, dt_bias=v_dt_bias, a_log=v_a_log, d_skip=v_d_skip,
              ret_gn=v_ret_gn, ssm_norm=v_ssm_norm, w_out=v_w_out, norm_ffn2=v_norm_ffn2, ffn2_wg=v_ffn2_wg, ffn2_wu=v_ffn2_wu,
              ffn2_wd=v_ffn2_wd, final_ada_w=v_final_ada_w, final_ada_b=v_final_ada_b, final_norm=v_final_norm)
    D = D_MODEL
    S = x.shape[1]
    me = 4 * lax.axis_index("x") + 2 * lax.axis_index("y") + lax.axis_index("c")
    n_mod = ada_w.shape[2]
    n_fmod = final_ada_w.shape[1]

    c_all = allgather(jnp.broadcast_to(c, (8, D)), in_vmem=True, name="gather_c")[:, 0, :]
    ada_cols = jnp.concatenate([ada_w[0], ada_w[1], final_ada_w], axis=1)
    ada_bias = jnp.concatenate([lax.dynamic_slice(ada_b, (0, me * n_mod), (DEPTH, n_mod)).reshape(1, -1),
                                lax.dynamic_slice(final_ada_b, (me * n_fmod,), (n_fmod,)).reshape(1, -1)], axis=1)
    mod_sh, cond = ada_mod(jnp.pad(c_all, ((0, 8), (0, 0))), ada_cols, ada_bias, name="ada_mod")
    n_cols = mod_sh.shape[1]
    small_in = jnp.concatenate([mod_sh[:8], jnp.pad(conv_w.reshape(8, LANE), ((0, 0), (0, n_cols - LANE)))], axis=0)
    small_g = allgather(small_in, in_vmem=True, name="gather_mod")
    mod_rows = lax.dynamic_index_in_dim(small_g[:, :8, :], me, axis=1, keepdims=False)
    mod = [mod_rows[:, l * n_mod:(l + 1) * n_mod].reshape(9, D) for l in range(DEPTH)]
    fmod = mod_rows[:, DEPTH * n_mod:].reshape(2, D)
    conv_w_full = small_g[:, 8:, :LANE].reshape(N_DEV, DEPTH, SSM_CONV, LANE).transpose(1, 2, 0, 3).reshape(DEPTH, SSM_CONV, 8 * LANE)

    rows = {n: _piece_rows(n, W[n].shape[1:]) for n in BIG}
    offs, o = {}, 0
    for n in BIG:
        offs[n] = o
        o += rows[n][1]
    pack_of = lambda src, dtype, l: jnp.concatenate([_to_piece(n, src[n][l]).astype(dtype) for n in BIG], axis=0)

    def full_weights(wgath):
        f = {n: wgath[:, offs[n]:offs[n] + rows[n][0], :].reshape(N_DEV * rows[n][0], D) for n in BIG}
        f["w_in"] = _pad_w_in_t(f["w_in"])
        return f

    full = [full_weights(allgather_two_level(pack_of(W, BF16, 0), name="gather_weights0"))]

    cos, sin = _rope_tables(S)
    h = x[0]
    target = loss_target[0]
    layer_p = []
    for l in range(DEPTH):
        layer_p.append(dict(norm_ffn1=_row(norm_ffn1[l]), norm_mix=_row(norm_mix[l]), norm_ffn2=_row(norm_ffn2[l]),
                            ret_gn=_row(ret_gn[l]), ssm_norm=_row(ssm_norm[l]), conv_w=conv_w_full[l], conv_b=_row(conv_b[l]),
                            dt_bias=_pair_rows(dt_bias[l]), a_log=_pair_rows(a_log[l]), d_skip=_pair_rows(d_skip[l])))
    mods = [[[_row(mod[l][3 * s + k]) for k in range(3)] for s in range(3)] for l in range(DEPTH)]

    saved = []
    for l in range(DEPTH):
        p, f = layer_p[l], full[l]
        h, s1 = _ffn_fwd(h, p["norm_ffn1"], mods[l][0], f["ffn1_wg"], f["ffn1_wu"], f["ffn1_wd"], f"l{l}_ffn1")
        nxt = pack_of(W, BF16, l + 1) if l + 1 < DEPTH else None
        h, s2, gathered = _mix_fwd(h, p, mods[l][1], f["w_in"], f["w_out"], cos, sin, f"l{l}_mix", gather=nxt)
        if nxt is not None:
            full.append(full_weights(gathered))
        h, s3 = _ffn_fwd(h, p["norm_ffn2"], mods[l][2], f["ffn2_wg"], f["ffn2_wu"], f["ffn2_wd"], f"l{l}_ffn2")
        saved.append((s1, s2, s3))

    def chip_sums(grads):
        def send_piece(n):
            g = _unpad_w_in_t(grads[n]) if n == "w_in" else grads[n]
            r, rp = rows[n]
            return jnp.pad(g.reshape(N_DEV, r, D), ((0, 0), (0, rp - r), (0, 0)))

        spack = jnp.concatenate([send_piece(n) for n in BIG], axis=1)
        by_core = spack.reshape((N_DEV // 2, 2) + spack.shape[1:])
        from_sibling = sibling_exchange(by_core, name=f"exchange_sibling{l}")
        own = lax.dynamic_index_in_dim(by_core, lax.axis_index("c"), axis=1, keepdims=False)
        return add_partials(own, from_sibling, name=f"add_sibling{l}")

    dh, fst = final_loss_bwd(h, _row(final_norm), _row(fmod[0]), _row(fmod[1]), target, name="final")
    rpack = [None] * DEPTH
    small_g_l = [None] * DEPTH
    dmod = [None] * DEPTH
    pending = None
    for l in reversed(range(DEPTH)):
        p, f = layer_p[l], full[l]
        s1, s2, s3 = saved[l]
        dh, (g2g, g2u, g2d), gn2, dm2 = _ffn_bwd(dh, s3, p["norm_ffn2"], mods[l][2], f["ffn2_wg"], f["ffn2_wu"], f["ffn2_wd"], f"l{l}_ffn2")
        dh, gw_in, gw_out, sm, dm1, exchanged = _mix_bwd(dh, s2, p, mods[l][1], f["w_in"], f["w_out"], cos, sin, f"l{l}_mix",
                                                         exchange=pending)
        if pending is not None:
            rpack[l + 1] = exchanged
        dh, (g1g, g1u, g1d), gn1, dm0 = _ffn_bwd(dh, s1, p["norm_ffn1"], mods[l][0], f["ffn1_wg"], f["ffn1_wu"], f["ffn1_wd"], f"l{l}_ffn1")
        pending = chip_sums(dict(zip(BIG, (g1g, g1u, g1d, gw_in, gw_out, g2g, g2u, g2d))))
        sm["norm_ffn1"], sm["norm_ffn2"] = gn1, gn2
        small_g_l[l] = sm
        dmod[l] = jnp.concatenate(dm0 + dm1 + dm2, axis=0)
    rpack[0] = chip_exchange(pending, name="exchange_chips0")
    grad_x = dh[None]

    outs = [adamw_parts(rpack[l], *[pack_of(src, F32, l) for src in (W, M1, V2)], name=f"adamw_big{l}") for l in range(DEPTH)]
    big_out = [{n: jnp.stack([_from_piece(n, outs[l][k][offs[n]:offs[n] + rows[n][1]], W[n].shape[1:]) for l in range(DEPTH)])
                for n in BIG} for k in range(4)]

    stack2 = lambda key: jnp.stack([small_g_l[l][key] for l in range(DEPTH)])
    pieces = [("loss", fst[3, 0:1]), ("ada_b", jnp.stack(dmod)), ("final_ada_b", jnp.concatenate([fst[1], fst[2]])),
              ("norm_ffn1", stack2("norm_ffn1")), ("norm_mix", stack2("norm_mix")), ("norm_ffn2", stack2("norm_ffn2")),
              ("conv_w", stack2("conv_w")), ("conv_b", stack2("conv_b")), ("dt_bias", stack2("dt_bias")), ("a_log", stack2("a_log")),
              ("d_skip", stack2("d_skip")), ("ret_gn", stack2("ret_gn")), ("ssm_norm", stack2("ssm_norm")), ("final_norm", fst[0])]
    names = [n for n, _ in pieces]
    shapes = [a.shape for _, a in pieces]
    ppack, poffs = _pack([a for _, a in pieces])
    pg = allgather(ppack, in_vmem=True, name="gather_small")
    zero_like = lambda n, a: jnp.zeros(a.shape, F32)
    spacks = [_pack([(src[n] if n in SMALL else zero_like(n, a)) for n, a in pieces])[0] for src in (W, M1, V2)]
    souts = adamw_parts(pg, *spacks, name="adamw_small")
    small_out = [dict(zip(names, _unpack(o, poffs, shapes))) for o in souts]
    loss = small_out[0]["loss"][0]

    gathered = dict(zip(names, _unpack(pg, poffs, shapes)))
    conv_parts = lax.dynamic_slice_in_dim(gathered["conv_w"], me * LANE, LANE, axis=3).reshape(N_DEV, DEPTH * SSM_CONV, LANE)
    conv_out = [o.reshape(conv_w.shape) for o in adamw_parts(conv_parts, conv_w.reshape(-1, LANE), m_conv_w.reshape(-1, LANE),
                                                              v_conv_w.reshape(-1, LANE), name="adamw_conv_w")]
    cond_t = cond[:8].T
    ada_out = []
    for l in range(DEPTH):
        dsel = lax.dynamic_slice_in_dim(gathered["ada_b"][:, l, :], me * n_mod, n_mod, axis=1)
        ada_out.append(ada_adamw(cond_t, dsel, ada_w[l], m_ada_w[l], v_ada_w[l], name=f"adamw_ada_w{l}"))
    ada_out = [jnp.stack([ada_out[l][k] for l in range(DEPTH)]) for k in range(4)]
    fsel = lax.dynamic_slice_in_dim(gathered["final_ada_b"].reshape(N_DEV, 2 * D), me * n_fmod, n_fmod, axis=1)
    fada_out = ada_adamw(cond_t, fsel, final_ada_w, m_final_ada_w, v_final_ada_w, name="adamw_final_ada_w")

    def pick(k, n):
        if n in BIG:
            return big_out[k][n]
        if n == "ada_w":
            return ada_out[k]
        if n == "final_ada_w":
            return fada_out[k]
        if n == "conv_w":
            return conv_out[k]
        return small_out[k][n]

    return (loss, grad_x) + tuple(pick(k, n) for k in range(4) for n in NAMES)
```

```python
import functools
import math

import numpy as np
import jax
import jax.numpy as jnp
from jax import lax
from jax.experimental import pallas as pl
from jax.experimental.pallas import tpu as pltpu

F32 = jnp.float32
BF16 = jnp.bfloat16

D_MODEL = 1024
DEPTH = 2
RET_HEADS = 4
HEAD_DIM = 128
SSM_HEADS = 8
SSM_HEAD_DIM = 64
SSM_STATE = 128
SSM_CONV = 4
D_FF = 2816
ROPE_BASE = 10000.0
NORM_EPS = 1e-6
MIX_W = 1536
IN_W = 5128
IN_MAIN = 5120
IN_PAD = 5632
N_DEV = 8
LANE = 128

ADAM_LR = 0.001
ADAM_B1 = 0.9
ADAM_B2 = 0.999
ADAM_EPS = 1e-08
ADAM_WD = 0.01
ADAM_STEP = 10

TOKEN_TILE = 512
WGRAD_TOKENS = 2048
BIG_TOKEN_TILE = 2048
SEQ_BLOCK = 256
VMEM_LIMIT = 56 << 20
SB_SKIP = 120.0
SB_UNVISITED = -1e30

CB_RQ, CB_RK, CB_RV, CB_RG = 0, 4, 8, 12
CB_SQ, CB_SK, CB_SV = 16, 20, 24
CB_MZ, CB_XS, CB_BM, CB_CM, CB_DT = 28, 32, 36, 38, 40


def _cparams(*sem):
    return pltpu.CompilerParams(dimension_semantics=sem, vmem_limit_bytes=VMEM_LIMIT)


def _sds(shape, dtype):
    return jax.ShapeDtypeStruct(tuple(shape), dtype)


def _tile(n, *prefs):
    for p in prefs:
        if n % p == 0:
            return p
    return n


def _dot(a, b, dims):
    return lax.dot_general(a, b, (dims, ((), ())), preferred_element_type=F32)


NN = ((1,), (0,))
NT = ((1,), (1,))
TN = ((0,), (0,))


def _bf(x):
    return x.astype(BF16)


def _sigmoid(x):
    return jax.nn.sigmoid(x)


def _split2(x):
    hi = x.astype(BF16)
    lo = (x - hi.astype(F32)).astype(BF16)
    return hi, lo


def _split3(x):
    hi = x.astype(BF16)
    r = x - hi.astype(F32)
    mid = r.astype(BF16)
    lo = (r - mid.astype(F32)).astype(BF16)
    return hi, mid, lo


def matmul(a, b, *, ta=False, tb=False, tm=512, tn=512, tk=512, out_dtype=F32, name):
    M, K = (a.shape[1], a.shape[0]) if ta else a.shape
    N = b.shape[0] if tb else b.shape[1]
    tm, tn, tk = min(tm, M), min(tn, N), min(tk, K)
    assert M % tm == 0 and N % tn == 0 and K % tk == 0, (name, M, N, K, tm, tn, tk)
    nk = K // tk
    a_spec = pl.BlockSpec((tk, tm), lambda i, j, k: (k, i)) if ta else pl.BlockSpec((tm, tk), lambda i, j, k: (i, k))
    b_spec = pl.BlockSpec((tn, tk), lambda i, j, k: (j, k)) if tb else pl.BlockSpec((tk, tn), lambda i, j, k: (k, j))
    dims = ((0 if ta else 1,), (1 if tb else 0,))

    def body(a_ref, b_ref, o_ref, acc_ref):
        k = pl.program_id(2)
        p = _dot(_bf(a_ref[...]), _bf(b_ref[...]), dims)

        @pl.when(k == 0)
        def _():
            acc_ref[...] = p

        @pl.when(k > 0)
        def _():
            acc_ref[...] += p

        @pl.when(k == nk - 1)
        def _():
            o_ref[...] = acc_ref[...].astype(out_dtype)

    return pl.pallas_call(
        body, grid=(M // tm, N // tn, nk), in_specs=[a_spec, b_spec],
        out_specs=pl.BlockSpec((tm, tn), lambda i, j, k: (i, j)), out_shape=_sds((M, N), out_dtype),
        scratch_shapes=[pltpu.VMEM((tm, tn), F32)], name=name,
        compiler_params=_cparams("parallel", "parallel", "arbitrary"))(a, b)


def matmul_resid(a, w, h, gate, factor, *, name):
    M, K = a.shape
    N = w.shape[1]
    tm = min(TOKEN_TILE, M)

    def body(a_ref, w_ref, h_ref, g_ref, hn_ref, o_ref):
        out = _dot(a_ref[...], w_ref[...], NN)
        o_ref[...] = out
        hn_ref[...] = h_ref[...] + (factor * (1.0 + g_ref[...])) * out

    mn = pl.BlockSpec((tm, N), lambda i: (i, 0))
    return pl.pallas_call(
        body, grid=(M // tm,),
        in_specs=[pl.BlockSpec((tm, K), lambda i: (i, 0)), pl.BlockSpec((K, N), lambda i: (0, 0)), mn,
                  pl.BlockSpec((1, N), lambda i: (0, 0))],
        out_specs=[mn, mn], out_shape=[_sds((M, N), F32), _sds((M, N), F32)], name=name,
        compiler_params=_cparams("parallel"))(a, w, h, gate)


def ffn_up(u, wg_t, wu_t, *, name):
    M, K = u.shape
    N = wg_t.shape[0]
    tm, tn = min(BIG_TOKEN_TILE, M), _tile(N, 256)

    def body(u_ref, wg_ref, wu_ref, a_ref, b_ref, act_ref):
        uu = u_ref[...]
        a = _dot(uu, wg_ref[...], NT)
        b = _dot(uu, wu_ref[...], NT)
        a_ref[...] = _bf(a)
        b_ref[...] = _bf(b)
        act_ref[...] = _bf(a * _sigmoid(a) * b)

    mn = pl.BlockSpec((tm, tn), lambda i, j: (i, j))
    wspec = pl.BlockSpec((tn, K), lambda i, j: (j, 0))
    return pl.pallas_call(
        body, grid=(M // tm, N // tn), in_specs=[pl.BlockSpec((tm, K), lambda i, j: (i, 0)), wspec, wspec],
        out_specs=[mn, mn, mn], out_shape=[_sds((M, N), BF16)] * 3, name=name,
        compiler_params=_cparams("parallel", "parallel"))(u, wg_t, wu_t)


def ffn_dact(dout, wd, a, b, *, name):
    M, K = dout.shape
    N = wd.shape[0]
    tm, tn = min(BIG_TOKEN_TILE, M), _tile(N, 256)

    def body(d_ref, w_ref, a_ref, b_ref, da_ref, db_ref):
        dact = _dot(d_ref[...], w_ref[...], NT)
        av = a_ref[...].astype(F32)
        sg = _sigmoid(av)
        db_ref[...] = _bf(dact * av * sg)
        da_ref[...] = _bf(dact * b_ref[...].astype(F32) * (sg * (1.0 + av * (1.0 - sg))))

    mn = pl.BlockSpec((tm, tn), lambda i, j: (i, j))
    return pl.pallas_call(
        body, grid=(M // tm, N // tn),
        in_specs=[pl.BlockSpec((tm, K), lambda i, j: (i, 0)), pl.BlockSpec((tn, K), lambda i, j: (j, 0)), mn, mn],
        out_specs=[mn, mn], out_shape=[_sds((M, N), BF16), _sds((M, N), BF16)], name=name,
        compiler_params=_cparams("parallel", "parallel"))(dout, wd, a, b)


def norm_mod(h, gain, shift, scale, *, name):
    S, D = h.shape
    tm = min(TOKEN_TILE, S)

    def body(h_ref, g_ref, sh_ref, sc_ref, u_ref):
        x = h_ref[...]
        r = lax.rsqrt(jnp.mean(x * x, axis=-1, keepdims=True) + NORM_EPS)
        n = x * r * g_ref[...]
        u_ref[...] = _bf(n * (1.0 + sc_ref[...]) + sh_ref[...])

    row = pl.BlockSpec((1, D), lambda i: (0, 0))
    tile = pl.BlockSpec((tm, D), lambda i: (i, 0))
    return pl.pallas_call(body, grid=(S // tm,), in_specs=[tile, row, row, row], out_specs=tile,
                          out_shape=_sds((S, D), BF16), name=name, compiler_params=_cparams("parallel"))(h, gain, shift, scale)


def dgrad_norm_bwd(lhs, ws, spans, h, gain, scale, dres, *, name):
    S, D = h.shape
    tm = min(TOKEN_TILE, S)
    n, nw = len(lhs), len(ws)

    def body(*refs):
        l_refs, w_refs = refs[:n], refs[n:n + nw]
        h_ref, g_ref, sc_ref, dres_ref, dh_ref, st_ref = refs[n + nw:]
        du = None
        for lr, (k, r0) in zip(l_refs, spans):
            part = _dot(_bf(lr[...]), w_refs[k][r0:r0 + lr.shape[1], :], NN)
            du = part if du is None else du + part
        x = h_ref[...]
        g = g_ref[...]
        r = lax.rsqrt(jnp.mean(x * x, axis=-1, keepdims=True) + NORM_EPS)
        xhat = x * r
        dn = du * (1.0 + sc_ref[...])
        dxhat = dn * g
        dh_ref[...] = dres_ref[...] + r * (dxhat - xhat * jnp.mean(dxhat * xhat, axis=-1, keepdims=True))

        @pl.when(pl.program_id(0) == 0)
        def _():
            st_ref[...] = jnp.zeros_like(st_ref)

        st_ref[0:1, :] += jnp.sum(dn * xhat, axis=0, keepdims=True)
        st_ref[1:2, :] += jnp.sum(du, axis=0, keepdims=True)
        st_ref[2:3, :] += jnp.sum(du * (xhat * g), axis=0, keepdims=True)

    row = pl.BlockSpec((1, D), lambda i: (0, 0))
    tile = pl.BlockSpec((tm, D), lambda i: (i, 0))
    in_specs = [pl.BlockSpec((tm, l.shape[1]), lambda i: (i, 0)) for l in lhs]
    in_specs += [pl.BlockSpec(w.shape, lambda i: (0, 0)) for w in ws]
    in_specs += [tile, row, row, tile]
    return pl.pallas_call(
        body, grid=(S // tm,), in_specs=in_specs, out_specs=[tile, pl.BlockSpec((8, D), lambda i: (0, 0))],
        out_shape=[_sds((S, D), F32), _sds((8, D), F32)], name=name,
        compiler_params=_cparams("arbitrary"))(*lhs, *ws, h, gain, scale, dres)


def gate_bwd(dh, out, gate, factor, *, name):
    S, D = dh.shape
    tm = min(TOKEN_TILE, S)

    def body(dh_ref, o_ref, g_ref, do_ref, st_ref):
        d = dh_ref[...]
        do_ref[...] = _bf(d * (factor * (1.0 + g_ref[...])))

        @pl.when(pl.program_id(0) == 0)
        def _():
            st_ref[...] = jnp.zeros_like(st_ref)

        st_ref[0:1, :] += factor * jnp.sum(d * o_ref[...], axis=0, keepdims=True)

    tile = pl.BlockSpec((tm, D), lambda i: (i, 0))
    return pl.pallas_call(
        body, grid=(S // tm,), in_specs=[tile, tile, pl.BlockSpec((1, D), lambda i: (0, 0))],
        out_specs=[tile, pl.BlockSpec((8, D), lambda i: (0, 0))], out_shape=[_sds((S, D), BF16), _sds((8, D), F32)],
        name=name, compiler_params=_cparams("arbitrary"))(dh, out, gate)


def final_loss_bwd(h, gain, shift, scale, target, *, name):
    S, D = h.shape
    tm = min(TOKEN_TILE, S)

    def body(h_ref, g_ref, sh_ref, sc_ref, t_ref, dh_ref, st_ref):
        x = h_ref[...]
        g = g_ref[...]
        r = lax.rsqrt(jnp.mean(x * x, axis=-1, keepdims=True) + NORM_EPS)
        xhat = x * r
        n = xhat * g
        err = n * (1.0 + sc_ref[...]) + sh_ref[...] - t_ref[...]
        dy = err * (1.0 / D)
        dn = dy * (1.0 + sc_ref[...])
        dxhat = dn * g
        dh_ref[...] = r * (dxhat - xhat * jnp.mean(dxhat * xhat, axis=-1, keepdims=True))

        @pl.when(pl.program_id(0) == 0)
        def _():
            st_ref[...] = jnp.zeros_like(st_ref)

        st_ref[0:1, :] += jnp.sum(dn * xhat, axis=0, keepdims=True)
        st_ref[1:2, :] += jnp.sum(dy, axis=0, keepdims=True)
        st_ref[2:3, :] += jnp.sum(dy * n, axis=0, keepdims=True)
        tok = jnp.mean(err * err, axis=-1, keepdims=True)
        st_ref[3:4, :] += 0.5 * jnp.sum(tok, axis=0, keepdims=True)

    row = pl.BlockSpec((1, D), lambda i: (0, 0))
    tile = pl.BlockSpec((tm, D), lambda i: (i, 0))
    return pl.pallas_call(
        body, grid=(S // tm,), in_specs=[tile, row, row, row, tile], out_specs=[tile, pl.BlockSpec((8, D), lambda i: (0, 0))],
        out_shape=[_sds((S, D), F32), _sds((8, D), F32)], name=name,
        compiler_params=_cparams("arbitrary"))(h, gain, shift, scale, target)


def _ret_tables(T):
    heads = np.arange(RET_HEADS, dtype=np.float64)
    lg = np.log1p(-(2.0 ** (-5.0 - heads)))
    t = np.arange(T)
    same = (t[:, None] // 64) == (t[None, :] // 64)
    earlier = (t[None, :] // 64) < (t[:, None] // 64)
    dist = np.abs(t[:, None] - t[None, :]).astype(np.float64)
    dmat = np.where(same | earlier, np.exp(lg[:, None, None] * dist[None]), 0.0)
    qdec = np.exp(lg[:, None] * (t + 1.0)[None, :])
    kdec = np.exp(lg[:, None] * (T - 1.0 - t)[None, :])
    cdec = np.exp(lg * T)
    bc = lambda v: jnp.asarray(np.broadcast_to(v[:, :, None], (RET_HEADS, T, LANE)), F32)
    cd = jnp.asarray(np.broadcast_to(cdec[:, None, None], (RET_HEADS, LANE, LANE)), F32)
    return jnp.asarray(dmat, F32), bc(qdec), bc(kdec), cd


def _rope_tables(S):
    half = HEAD_DIM // 2
    inv_freq = ROPE_BASE ** (-jnp.arange(half, dtype=F32) / half)
    ang = jnp.arange(S, dtype=F32)[:, None] * inv_freq[None, :]
    cos, sin = jnp.cos(ang), jnp.sin(ang)
    return jnp.concatenate([cos, cos], axis=-1), jnp.concatenate([-sin, sin], axis=-1)


def _rope(x, c, s):
    return x * c + pltpu.roll(x, HEAD_DIM // 2, 1) * s


def _rope_t(dx, c, s):
    return dx * c + pltpu.roll(dx * s, HEAD_DIM // 2, 1)


def ret_fwd(proj, gn, cos, sin, *, name):
    S = proj.shape[0]
    T = min(SEQ_BLOCK, S)
    nb = S // T
    dmat, qdec, kdec, cdec = _ret_tables(T)

    def body(q_ref, k_ref, v_ref, g_ref, c_ref, s_ref, dm_ref, qd_ref, kd_ref, cd_ref, gn_ref, yo_ref, yp_ref, st_ref, state):
        @pl.when(pl.program_id(0) == 0)
        def _():
            state[...] = jnp.zeros_like(state)

        c, s = c_ref[...], s_ref[...]
        for h in range(RET_HEADS):
            cols = slice(LANE * h, LANE * (h + 1))
            qr = _rope(q_ref[:, cols], c, s)
            kr = _rope(k_ref[:, cols], c, s) * (HEAD_DIM ** -0.5)
            v = _bf(v_ref[:, cols])
            sp = state[h]
            st_ref[h] = sp
            a = _dot(_bf(qr), _bf(kr), NT) * dm_ref[h]
            y = _dot(_bf(a), v, NN) + _dot(_bf(qr * qd_ref[h]), _bf(sp), NN)
            state[h] = cd_ref[h] * sp + _dot(_bf(kr * kd_ref[h]), v, TN)
            yp_ref[:, cols] = y
            yn = y * lax.rsqrt(jnp.mean(y * y, axis=-1, keepdims=True) + NORM_EPS) * gn_ref[:, cols]
            g = g_ref[:, cols]
            yo_ref[:, cols] = _bf(yn * (g * _sigmoid(g)))

    col = lambda cb: pl.BlockSpec((T, 512), lambda b: (b, cb // 4))
    tok = pl.BlockSpec((T, LANE), lambda b: (b, 0))
    const = lambda shape: pl.BlockSpec(shape, lambda b: (0,) * len(shape))
    out_tok = pl.BlockSpec((T, 512), lambda b: (b, 0))
    return pl.pallas_call(
        body, grid=(nb,),
        in_specs=[col(CB_RQ), col(CB_RK), col(CB_RV), col(CB_RG), tok, tok, const(dmat.shape), const(qdec.shape), const(kdec.shape),
                  const(cdec.shape), const((1, 512))],
        out_specs=[out_tok, out_tok, pl.BlockSpec((None, RET_HEADS, LANE, LANE), lambda b: (b, 0, 0, 0))],
        out_shape=[_sds((S, 512), BF16), _sds((S, 512), F32), _sds((nb, RET_HEADS, LANE, LANE), F32)],
        scratch_shapes=[pltpu.VMEM((RET_HEADS, LANE, LANE), F32)], name=name,
        compiler_params=_cparams("arbitrary"))(proj, proj, proj, proj, cos, sin, dmat, qdec, kdec, cdec, gn)


def ret_bwd(proj, gn, cos, sin, ypre, states, dycat, *, name):
    S = proj.shape[0]
    T = min(SEQ_BLOCK, S)
    nb = S // T
    dmat, qdec, kdec, cdec = _ret_tables(T)

    def body(q_ref, k_ref, v_ref, g_ref, c_ref, s_ref, dm_ref, qd_ref, kd_ref, cd_ref, gn_ref, yp_ref, st_ref, dy_ref,
             d_ref, stat_ref, gstate):
        @pl.when(pl.program_id(0) == 0)
        def _():
            gstate[...] = jnp.zeros_like(gstate)
            stat_ref[...] = jnp.zeros_like(stat_ref)

        c, s = c_ref[...], s_ref[...]
        scale = HEAD_DIM ** -0.5
        for h in range(RET_HEADS):
            cols = slice(LANE * h, LANE * (h + 1))
            qr = _rope(q_ref[:, cols], c, s)
            kr = _rope(k_ref[:, cols], c, s) * scale
            v = _bf(v_ref[:, cols])
            qd, kd, dm = qd_ref[h], kd_ref[h], dm_ref[h]
            sp = _bf(st_ref[h])
            gs = gstate[h]
            gsb = _bf(gs)
            g = g_ref[:, cols]
            sg = _sigmoid(g)
            y = yp_ref[:, cols]
            gn_row = gn_ref[:, cols]
            r = lax.rsqrt(jnp.mean(y * y, axis=-1, keepdims=True) + NORM_EPS)
            yhat = y * r
            dyo = dy_ref[:, cols]
            d_ref[:, 1536 + LANE * h:1536 + LANE * (h + 1)] = _bf(dyo * (yhat * gn_row) * (sg * (1.0 + g * (1.0 - sg))))
            dyn = dyo * (g * sg)
            stat_ref[0:1, cols] += jnp.sum(dyn * yhat, axis=0, keepdims=True)
            dyhat = dyn * gn_row
            dy = _bf(r * (dyhat - yhat * jnp.mean(dyhat * yhat, axis=-1, keepdims=True)))
            qrb, krb = _bf(qr), _bf(kr)
            qdb = _bf(qr * qd)
            kdb = _bf(kr * kd)
            a = _bf(_dot(qrb, krb, NT) * dm)
            da = _bf(_dot(dy, v, NT) * dm)
            d_ref[:, 1024 + LANE * h:1024 + LANE * (h + 1)] = _bf(_dot(a, dy, TN) + _dot(kdb, gsb, NN))
            dqr = _dot(da, krb, NN) + qd * _dot(dy, sp, NT)
            dkr = _dot(da, qrb, TN) + kd * _dot(v, gsb, NT)
            gstate[h] = cd_ref[h] * gs + _dot(qdb, dy, TN)
            d_ref[:, cols] = _bf(_rope_t(dqr, c, s))
            d_ref[:, 512 + LANE * h:512 + LANE * (h + 1)] = _bf(_rope_t(dkr * scale, c, s))

    rb = lambda b: nb - 1 - b
    col = lambda cb: pl.BlockSpec((T, 512), lambda b: (rb(b), cb // 4))
    tok = pl.BlockSpec((T, LANE), lambda b: (rb(b), 0))
    const = lambda shape: pl.BlockSpec(shape, lambda b: (0,) * len(shape))
    tok512 = pl.BlockSpec((T, 512), lambda b: (rb(b), 0))
    return pl.pallas_call(
        body, grid=(nb,),
        in_specs=[col(CB_RQ), col(CB_RK), col(CB_RV), col(CB_RG), tok, tok, const(dmat.shape), const(qdec.shape), const(kdec.shape),
                  const(cdec.shape), const((1, 512)), tok512,
                  pl.BlockSpec((None, RET_HEADS, LANE, LANE), lambda b: (rb(b), 0, 0, 0)), tok512],
        out_specs=[pl.BlockSpec((T, 2048), lambda b: (rb(b), 0)), const((8, 512))],
        out_shape=[_sds((S, 2048), BF16), _sds((8, 512), F32)],
        scratch_shapes=[pltpu.VMEM((RET_HEADS, LANE, LANE), F32)], name=name,
        compiler_params=_cparams("arbitrary"))(proj, proj, proj, proj, cos, sin, dmat, qdec, kdec, cdec, gn, ypre, states, dycat)


def _sb_cast_kv(k_ref, v_ref, kb, vb, S):
    step = min(TOKEN_TILE, S)
    for r in range(0, S, step):
        kb[r:r + step, :] = _bf(k_ref[r:r + step, :])
        vb[r:r + step, :] = _bf(v_ref[r:r + step, :])


def _sb_logits(q, kblk, vis):
    z = _dot(q, kblk, NT) * (HEAD_DIM ** -0.5)
    l = jnp.log1p(jnp.exp(-jnp.abs(z)))
    lb = jnp.minimum(z, 0.0) - l
    lk = jnp.minimum(-z, 0.0) - l
    if vis is not None:
        lk = jnp.where(vis, lk, 0.0)
    return lb, lk


def _tri(T, cmp):
    r = lax.broadcasted_iota(jnp.int32, (T, T), 0)
    c = lax.broadcasted_iota(jnp.int32, (T, T), 1)
    return cmp(r, c)


def _dot_split2(x, m):
    hi, lo = _split2(x)
    return _dot(hi, m, NN) + _dot(lo, m, NN)


def _carried(carry_fn, refs, n_heads, n_steps):
    h, i = pl.program_id(0), pl.program_id(1)

    @pl.when((h == 0) & (i == 0))
    def _():
        carry_fn(*refs, start=True, finish=False)

    def finish():
        @pl.when((h == n_heads - 1) & (i == n_steps - 1))
        def _():
            carry_fn(*refs, start=False, finish=True)

    return finish


def sb_fwd(proj, *, name, gather=None):
    S = proj.shape[0]
    T = min(SEQ_BLOCK, S)
    nq = S // T

    assert nq <= LANE

    def body(*refs):
        if gather is None:
            q_ref, k_ref, v_ref, o_ref, cin_ref, kb, vb = refs
            finish_comm = lambda: None
        else:
            q_ref, k_ref, v_ref, src_ref, o_ref, cin_ref, dst_ref, kb, vb, send_sems, recv_sems, local_sem = refs
            finish_comm = _carried(_gather_two_level, (src_ref, dst_ref, send_sems, recv_sems, local_sem), RET_HEADS, nq)
        qi = pl.program_id(1)

        @pl.when(qi == 0)
        def _():
            _sb_cast_kv(k_ref, v_ref, kb, vb, S)

        q = _bf(q_ref[...])
        vis = _tri(T, lambda t, s: s < t)
        after = _tri(T, lambda j, s: j > s).astype(BF16)
        lane = lax.broadcasted_iota(jnp.int32, (T, LANE), 1)

        def block(jb, carry, acc, cin, mask):
            rows = pl.ds(pl.multiple_of(jb * T, T), T)
            lb, lk = _sb_logits(q, kb[rows, :], mask)
            tail = _dot_split2(lk, after) + carry
            w = jnp.exp(lb + tail)
            if mask is not None:
                w = jnp.where(mask, w, 0.0)
            return (carry + jnp.sum(lk, axis=1, keepdims=True), acc + _dot(_bf(w), vb[rows, :], NN),
                    jnp.where(lane == jb, carry, cin))

        st = block(qi, jnp.zeros((T, 1), F32), jnp.zeros((T, LANE), F32), jnp.full((T, LANE), SB_UNVISITED, F32), vis)

        def more(c):
            return (c[0] < qi) & (jnp.max(c[1]) > -SB_SKIP)

        def step(c):
            return (c[0] + 1,) + block(qi - 1 - c[0], c[1], c[2], c[3], None)

        st = lax.while_loop(more, step, (jnp.int32(0),) + st)
        o_ref[...] = st[2]
        cin_ref[...] = st[3]
        finish_comm()

    whole = lambda cb: pl.BlockSpec((S, LANE), lambda h, i: (0, cb + h))
    tok = pl.BlockSpec((T, LANE), lambda h, i: (i, h))
    hbm = pl.BlockSpec(memory_space=pltpu.HBM)
    extra = gather is not None
    return pl.pallas_call(
        body, grid=(RET_HEADS, nq),
        in_specs=[pl.BlockSpec((T, LANE), lambda h, i: (i, CB_SQ + h)), whole(CB_SK), whole(CB_SV)] + [hbm] * extra,
        out_specs=[tok, tok] + [hbm] * extra,
        out_shape=[_sds((S, 512), F32), _sds((S, 512), F32)] + ([_sds((N_DEV,) + gather.shape, gather.dtype)] if extra else []),
        scratch_shapes=[pltpu.VMEM((S, LANE), BF16), pltpu.VMEM((S, LANE), BF16)] + COMM_SCRATCH * extra, name=name,
        compiler_params=_cparams("arbitrary", "arbitrary"))(*((proj, proj, proj) + ((gather,) if extra else ())))


def sb_bwd(proj, cin, dycat, *, name, exchange=None):
    S = proj.shape[0]
    T = min(SEQ_BLOCK, S)
    nq = S // T
    scale = HEAD_DIM ** -0.5

    def body(*refs):
        if exchange is None:
            q_ref, k_ref, v_ref, cin_ref, do_ref, dq_ref, dk_ref, dv_ref, kb, vb = refs
            finish_comm = lambda: None
        else:
            (q_ref, k_ref, v_ref, cin_ref, do_ref, src_ref, dq_ref, dk_ref, dv_ref, dst_ref, kb, vb,
             send_sems, recv_sems, local_sem) = refs
            finish_comm = _carried(_chip_exchange, (src_ref, dst_ref, send_sems, recv_sems, local_sem), RET_HEADS, nq)
        qi = pl.program_id(1)

        @pl.when(qi == 0)
        def _():
            _sb_cast_kv(k_ref, v_ref, kb, vb, S)
            dk_ref[...] = jnp.zeros_like(dk_ref)
            dv_ref[...] = jnp.zeros_like(dv_ref)

        q = _bf(q_ref[...])
        dob = _bf(do_ref[...])
        cin = cin_ref[...]
        vis = _tri(T, lambda t, s: s < t)
        after = _tri(T, lambda j, s: j > s).astype(BF16)
        before = _tri(T, lambda s, j: s < j).astype(BF16)
        lane = lax.broadcasted_iota(jnp.int32, (T, LANE), 1)

        def block(jb, ecarry, dq, mask):
            rows = pl.ds(pl.multiple_of(jb * T, T), T)
            kblk, vblk = kb[rows, :], vb[rows, :]
            lb, lk = _sb_logits(q, kblk, mask)
            carry = jnp.sum(jnp.where(lane == jb, cin, 0.0), axis=1, keepdims=True)
            w = jnp.exp(lb + _dot_split2(lk, after) + carry)
            if mask is not None:
                w = jnp.where(mask, w, 0.0)
            e = w * _dot(dob, vblk, NT)
            dv_ref[rows, :] += _dot(_bf(w), dob, TN)
            dlk = _dot_split2(e, before) + ecarry
            beta = jnp.exp(lb)
            dz = e * (1.0 - beta) - beta * dlk
            if mask is not None:
                dz = jnp.where(mask, dz, 0.0)
            dzb = _bf(dz * scale)
            dk_ref[rows, :] += _dot(dzb, q, TN)
            return ecarry + jnp.sum(e, axis=1, keepdims=True), dq + _dot(dzb, kblk, NN)

        lane1 = lane[0:1, :]
        skipped = (jnp.max(cin, axis=0, keepdims=True) <= -SB_SKIP) & (lane1 < qi)
        first = jnp.sum(jnp.where(skipped, 1, 0))
        st = lax.fori_loop(first, qi, lambda jb, c: block(jb, c[0], c[1], None), (jnp.zeros((T, 1), F32), jnp.zeros((T, LANE), F32)))
        st = block(qi, st[0], st[1], vis)
        dq_ref[...] = _bf(st[1])
        finish_comm()

    whole = lambda cb: pl.BlockSpec((S, LANE), lambda h, i: (0, cb + h))
    tok = pl.BlockSpec((T, LANE), lambda h, i: (i, h))
    acc = pl.BlockSpec((S, LANE), lambda h, i: (0, h))
    hbm = pl.BlockSpec(memory_space=pltpu.HBM)
    extra = exchange is not None
    return pl.pallas_call(
        body, grid=(RET_HEADS, nq),
        in_specs=[pl.BlockSpec((T, LANE), lambda h, i: (i, CB_SQ + h)), whole(CB_SK), whole(CB_SV), tok,
                  pl.BlockSpec((T, LANE), lambda h, i: (i, 4 + h))] + [hbm] * extra,
        out_specs=[tok, acc, acc] + [hbm] * extra,
        out_shape=[_sds((S, 512), BF16), _sds((S, 512), F32), _sds((S, 512), F32)] + ([_sds(exchange.shape, exchange.dtype)] if extra else []),
        scratch_shapes=[pltpu.VMEM((S, LANE), BF16), pltpu.VMEM((S, LANE), BF16)] + COMM_SCRATCH * extra, name=name,
        compiler_params=_cparams("arbitrary", "arbitrary"))(*((proj, proj, proj, cin, dycat) + ((exchange,) if extra else ())))


def _shift_down(x, d, row):
    return jnp.where(row >= d, pltpu.roll(x, d, 0), 0.0)


def _shift_up(x, d, row, S):
    return jnp.where(row < S - d, pltpu.roll(x, S - d, 0), 0.0)


def conv_fwd(proj, conv_w, conv_b, *, name):
    S = proj.shape[0]

    def body(x_ref, w_ref, b_ref, pre_ref, act_ref):
        x = x_ref[...]
        row = lax.broadcasted_iota(jnp.int32, x.shape, 0)
        pre = b_ref[...] + w_ref[3:4, :] * x
        for d in range(1, SSM_CONV):
            pre = pre + w_ref[3 - d:4 - d, :] * _shift_down(x, d, row)
        pre_ref[...] = pre
        act_ref[...] = pre * _sigmoid(pre)

    blk = pl.BlockSpec((S, LANE), lambda c: (0, c))
    return pl.pallas_call(
        body, grid=(8,),
        in_specs=[pl.BlockSpec((S, LANE), lambda c: (0, CB_XS + c)), pl.BlockSpec((SSM_CONV, LANE), lambda c: (0, c)),
                  pl.BlockSpec((1, LANE), lambda c: (0, c))],
        out_specs=[blk, blk], out_shape=[_sds((S, 1024), F32), _sds((S, 1024), F32)], name=name,
        compiler_params=_cparams("parallel"))(proj, conv_w, conv_b)


def conv_bwd(proj, pre, dact, conv_w, *, name):
    S = proj.shape[0]

    def body(x_ref, pre_ref, da_ref, w_ref, dx_ref, st_ref):
        x = x_ref[...]
        p = pre_ref[...]
        row = lax.broadcasted_iota(jnp.int32, x.shape, 0)
        sg = _sigmoid(p)
        dpre = da_ref[...] * (sg * (1.0 + p * (1.0 - sg)))
        dx = w_ref[3:4, :] * dpre
        st_ref[3:4, :] = jnp.sum(dpre * x, axis=0, keepdims=True)
        for d in range(1, SSM_CONV):
            dx = dx + w_ref[3 - d:4 - d, :] * _shift_up(dpre, d, row, S)
            st_ref[3 - d:4 - d, :] = jnp.sum(dpre * _shift_down(x, d, row), axis=0, keepdims=True)
        st_ref[4:5, :] = jnp.sum(dpre, axis=0, keepdims=True)
        st_ref[5:8, :] = jnp.zeros((3, LANE), F32)
        dx_ref[...] = _bf(dx)

    blk = pl.BlockSpec((S, LANE), lambda c: (0, c))
    return pl.pallas_call(
        body, grid=(8,),
        in_specs=[pl.BlockSpec((S, LANE), lambda c: (0, CB_XS + c)), blk, blk, pl.BlockSpec((SSM_CONV, LANE), lambda c: (0, c))],
        out_specs=[blk, pl.BlockSpec((8, LANE), lambda c: (0, c))],
        out_shape=[_sds((S, 1024), BF16), _sds((8, 1024), F32)], name=name,
        compiler_params=_cparams("parallel"))(proj, pre, dact, conv_w)


def _softplus(x):
    return jnp.maximum(x, 0.0) + jnp.log1p(jnp.exp(-jnp.abs(x)))


def _pair(lane, v0, v1):
    return jnp.where(lane < SSM_HEAD_DIM, v0, v1)


def _ssd_pair_common(raw, dtb, alog, xs, cm, hprev, T):
    lane = lax.broadcasted_iota(jnp.int32, (T, LANE), 1)
    dt = _softplus(raw + dtb)
    a = -jnp.exp(alog)
    incl = _tri(T, lambda l, s: s <= l).astype(BF16)
    h1, h2, h3 = _split3(dt * a)
    acum = _dot(incl, h1, NN) + _dot(incl, h2, NN) + _dot(incl, h3, NN)
    acum_t = acum.T
    causal = _tri(T, lambda l, s: s <= l)
    decay = [jnp.where(causal, jnp.exp(jnp.minimum(acum[:, j:j + 1] - acum_t[j:j + 1, :], 0.0)), 0.0) for j in (0, 1)]
    dtc = _pair(lane, dt[:, 0:1], dt[:, 1:2])
    ac = _pair(lane, acum[:, 0:1], acum[:, 1:2])
    xdt = xs * dtc
    ea = jnp.exp(ac)
    e_end = jnp.exp(ac[T - 1:T, :] - ac)
    sub = lax.broadcasted_iota(jnp.int32, (LANE, LANE), 0)
    cd = jnp.where(sub < SSM_HEAD_DIM, jnp.exp(acum[T - 1:T, 0:1]), jnp.exp(acum[T - 1:T, 1:2]))
    r = _dot(cm, _bf(hprev), NT)
    return lane, dt, a, acum, decay, dtc, xdt, ea, e_end, cd, r


def ssd_fwd(xact, proj, dtb, alog, dskip, *, name):
    S = xact.shape[0]
    T = min(SEQ_BLOCK, S)
    nb = S // T

    def body(xs_ref, bm_ref, cm_ref, dt_ref, dtb_ref, al_ref, ds_ref, y_ref, st_ref, state):
        @pl.when(pl.program_id(0) == 0)
        def _():
            state[...] = jnp.zeros_like(state)

        for g in range(2):
            bm, cm = _bf(bm_ref[:, LANE * g:LANE * (g + 1)]), _bf(cm_ref[:, LANE * g:LANE * (g + 1)])
            gm = _dot(cm, bm, NT)
            for i in range(2):
                p = 2 * g + i
                cols = slice(LANE * p, LANE * (p + 1))
                xs = xs_ref[:, cols]
                hprev = state[p]
                st_ref[g, i] = hprev
                lane, dt, a, acum, decay, dtc, xdt, ea, e_end, cd, r = _ssd_pair_common(
                    dt_ref[:, cols], dtb_ref[p], al_ref[p], xs, cm, hprev, T)
                xdtb = _bf(xdt)
                y_intra = _pair(lane, _dot(_bf(gm * decay[0]), xdtb, NN), _dot(_bf(gm * decay[1]), xdtb, NN))
                state[p] = cd * hprev + _dot(_bf(xdt * e_end), bm, TN)
                dsk = ds_ref[p]
                lane1 = lane[0:1, :]
                y_ref[:, cols] = y_intra + ea * r + _pair(lane1, dsk[:, 0:1], dsk[:, 1:2]) * xs

    rows = pl.BlockSpec((4, 1, LANE), lambda b: (0, 0, 0))
    return pl.pallas_call(
        body, grid=(nb,),
        in_specs=[pl.BlockSpec((T, 512), lambda b: (b, 0)), pl.BlockSpec((T, 256), lambda b: (b, 2)), pl.BlockSpec((T, 256), lambda b: (b, 3)),
                  pl.BlockSpec((T, 512), lambda b: (b, CB_DT // 4)), rows, rows, rows],
        out_specs=[pl.BlockSpec((T, 512), lambda b: (b, 0)),
                   pl.BlockSpec((2, None, 2, LANE, LANE), lambda b: (0, b, 0, 0, 0))],
        out_shape=[_sds((S, 512), F32), _sds((2, nb, 2, LANE, LANE), F32)],
        scratch_shapes=[pltpu.VMEM((4, LANE, LANE), F32)], name=name,
        compiler_params=_cparams("arbitrary"))(xact, xact, xact, proj, dtb, alog, dskip)


def ssd_bwd(xact, proj, dtb, alog, dskip, states, dy, *, name):
    S = xact.shape[0]
    T = min(SEQ_BLOCK, S)
    nb = S // T

    def body(xs_ref, bm_ref, cm_ref, dt_ref, dtb_ref, al_ref, ds_ref, st_ref, dy_ref,
             dxs_ref, dbm_ref, dcm_ref, ddt_ref, stat_ref, dstate):
        @pl.when(pl.program_id(0) == 0)
        def _():
            dstate[...] = jnp.zeros_like(dstate)
            stat_ref[...] = jnp.zeros_like(stat_ref)

        for g in range(2):
            wide = (slice(None), slice(256 * g, 256 * (g + 1)))
            narrow = (slice(None), slice(LANE * g, LANE * (g + 1)))
            pair = slice(2 * g, 2 * g + 2)
            group(xs_ref.at[wide], bm_ref.at[narrow], cm_ref.at[narrow], dt_ref.at[:, LANE * 2 * g:LANE * (2 * g + 1)],
                  dt_ref.at[:, LANE * (2 * g + 1):LANE * (2 * g + 2)], dtb_ref.at[pair], al_ref.at[pair], ds_ref.at[pair], st_ref.at[g],
                  dy_ref.at[wide], dxs_ref.at[wide], dbm_ref.at[narrow], dcm_ref.at[narrow], ddt_ref.at[wide], stat_ref.at[pair],
                  dstate.at[pair])

    def group(xs_ref, bm_ref, cm_ref, dt0_ref, dt1_ref, dtb_ref, al_ref, ds_ref, st_ref, dy_ref,
              dxs_ref, dbm_ref, dcm_ref, ddt_ref, stat_ref, dstate):
        bm, cm = _bf(bm_ref[...]), _bf(cm_ref[...])
        gm = _dot(cm, bm, NT)
        dbm = jnp.zeros((T, LANE), F32)
        dcm = jnp.zeros((T, LANE), F32)
        after_eq = _tri(T, lambda i, l: l >= i).astype(BF16)
        rowi = lax.broadcasted_iota(jnp.int32, (T, 1), 0)
        for i, dt_ref in enumerate((dt0_ref, dt1_ref)):
            xs = xs_ref[:, LANE * i:LANE * (i + 1)]
            dyp = dy_ref[:, LANE * i:LANE * (i + 1)]
            hprev = st_ref[i]
            dh = dstate[i]
            raw = dt_ref[...]
            lane, dt, a, acum, decay, dtc, xdt, ea, e_end, cd, r = _ssd_pair_common(
                raw, dtb_ref[i], al_ref[i], xs, cm, hprev, T)
            lane1 = lane[0:1, :]
            dsk = ds_ref[i]
            dskp = _pair(lane1, dsk[:, 0:1], dsk[:, 1:2])
            head = [lane < SSM_HEAD_DIM, lane >= SSM_HEAD_DIM]
            hsum = lambda v, j: jnp.sum(jnp.where(head[j], v, 0.0), axis=1, keepdims=True)
            dhb = _bf(dh)
            xdtb = _bf(xdt)
            dyb = _bf(dyp)
            z = xdt * e_end
            dz = _dot(bm, dhb, NT)
            dbm = dbm + _dot(_bf(z), dhb, NN)
            dxdt = dz * e_end
            de_e = dz * z
            drr = dyp * ea
            dea_ea = drr * r
            dcm = dcm + _dot(_bf(drr), _bf(hprev), NN)
            dstate[i] = cd * dh + _dot(_bf(drr), cm, TN)
            dcd_cd = cd * dh * hprev
            dgs = jnp.zeros((T, T), F32)
            da_cols = []
            for j in (0, 1):
                w = gm * decay[j]
                dw = _dot(_bf(jnp.where(head[j], dyp, 0.0)), xdtb, NT)
                dxdt = dxdt + jnp.where(head[j], _dot(_bf(w), dyb, TN), 0.0)
                dgs = dgs + dw * decay[j]
                dseg = dw * w
                col = jnp.sum(dseg, axis=1, keepdims=True) - jnp.sum(dseg.T, axis=1, keepdims=True)
                col = col + hsum(dea_ea, j) - hsum(de_e, j)
                sub = lax.broadcasted_iota(jnp.int32, (LANE, LANE), 0)
                in_head = (sub < SSM_HEAD_DIM) if j == 0 else (sub >= SSM_HEAD_DIM)
                end = jnp.sum(hsum(de_e, j), axis=0, keepdims=True) + jnp.sum(
                    jnp.sum(jnp.where(in_head, dcd_cd, 0.0), axis=1, keepdims=True), axis=0, keepdims=True)
                da_cols.append(col + jnp.where(rowi == T - 1, end, 0.0))
            dgb = _bf(dgs)
            dcm = dcm + _dot(dgb, bm, NN)
            dbm = dbm + _dot(dgb, cm, TN)
            dacum = jnp.where(lane == 0, da_cols[0], jnp.where(lane == 1, da_cols[1], 0.0))
            h1, h2, h3 = _split3(dacum)
            ddta = _dot(after_eq, h1, NN) + _dot(after_eq, h2, NN) + _dot(after_eq, h3, NN)
            dxs_ref[:, LANE * i:LANE * (i + 1)] = dskp * dyp + dxdt * dtc
            dx_x = dxdt * xs
            ddt = ddta * a + jnp.where(lane == 0, hsum(dx_x, 0), jnp.where(lane == 1, hsum(dx_x, 1), 0.0))
            ddraw = jnp.where(lane < 2, ddt * _sigmoid(raw + dtb_ref[i]), 0.0)
            ddt_ref[:, LANE * i:LANE * (i + 1)] = _bf(ddraw)
            dsum = jnp.sum(dyp * xs, axis=0, keepdims=True)
            d0 = jnp.sum(jnp.where(lane1 < SSM_HEAD_DIM, dsum, 0.0), axis=1, keepdims=True)
            d1 = jnp.sum(jnp.where(lane1 >= SSM_HEAD_DIM, dsum, 0.0), axis=1, keepdims=True)
            dd = jnp.where(lane1 == 0, d0, jnp.where(lane1 == 1, d1, 0.0))
            stat_ref[i, 0:1, :] += jnp.sum(ddraw, axis=0, keepdims=True)
            stat_ref[i, 1:2, :] += jnp.where(lane1 < 2, jnp.sum(ddta * dt, axis=0, keepdims=True) * a, 0.0)
            stat_ref[i, 2:3, :] += dd
        dbm_ref[...] = dbm
        dcm_ref[...] = dcm

    rb = lambda b: nb - 1 - b
    rows = pl.BlockSpec((4, 1, LANE), lambda b: (0, 0, 0))
    tok512 = pl.BlockSpec((T, 512), lambda b: (rb(b), 0))
    tok256 = pl.BlockSpec((T, 256), lambda b: (rb(b), 0))
    return pl.pallas_call(
        body, grid=(nb,),
        in_specs=[tok512, pl.BlockSpec((T, 256), lambda b: (rb(b), 2)), pl.BlockSpec((T, 256), lambda b: (rb(b), 3)),
                  pl.BlockSpec((T, 512), lambda b: (rb(b), CB_DT // 4)), rows, rows, rows,
                  pl.BlockSpec((2, None, 2, LANE, LANE), lambda b: (0, rb(b), 0, 0, 0)), tok512],
        out_specs=[tok512, tok256, tok256, tok512, pl.BlockSpec((4, 8, LANE), lambda b: (0, 0, 0))],
        out_shape=[_sds((S, 512), F32), _sds((S, 256), F32), _sds((S, 256), F32), _sds((S, 512), BF16), _sds((4, 8, LANE), F32)],
        scratch_shapes=[pltpu.VMEM((4, LANE, LANE), F32)], name=name,
        compiler_params=_cparams("arbitrary"))(xact, xact, xact, proj, dtb, alog, dskip, states, dy)


def gated_norm(ypre, proj, gain, *, name):
    S, W = ypre.shape
    tm = min(TOKEN_TILE, S)

    def body(y_ref, z_ref, g_ref, o_ref):
        z = z_ref[...]
        yg = y_ref[...] * (z * _sigmoid(z))
        o_ref[...] = _bf(yg * lax.rsqrt(jnp.mean(yg * yg, axis=-1, keepdims=True) + NORM_EPS) * g_ref[...])

    tile = pl.BlockSpec((tm, W), lambda i: (i, 0))
    return pl.pallas_call(
        body, grid=(S // tm,), in_specs=[tile, pl.BlockSpec((tm, W), lambda i: (i, CB_MZ // 4)), pl.BlockSpec((1, W), lambda i: (0, 0))],
        out_specs=tile, out_shape=_sds((S, W), BF16), name=name, compiler_params=_cparams("parallel"))(ypre, proj, gain)


def gated_norm_bwd(ypre, proj, gain, dycat, *, name):
    S, W = ypre.shape
    tm = min(TOKEN_TILE, S)

    def body(y_ref, z_ref, g_ref, dy_ref, dyp_ref, dz_ref, st_ref):
        z = z_ref[...]
        y = y_ref[...]
        sg = _sigmoid(z)
        sz = z * sg
        yg = y * sz
        r = lax.rsqrt(jnp.mean(yg * yg, axis=-1, keepdims=True) + NORM_EPS)
        yhat = yg * r
        dyo = dy_ref[...]

        @pl.when(pl.program_id(0) == 0)
        def _():
            st_ref[...] = jnp.zeros_like(st_ref)

        st_ref[0:1, :] += jnp.sum(dyo * yhat, axis=0, keepdims=True)
        dyhat = dyo * g_ref[...]
        dyg = r * (dyhat - yhat * jnp.mean(dyhat * yhat, axis=-1, keepdims=True))
        dyp_ref[...] = dyg * sz
        dz_ref[...] = _bf(dyg * y * (sg * (1.0 + z * (1.0 - sg))))

    tile = pl.BlockSpec((tm, W), lambda i: (i, 0))
    return pl.pallas_call(
        body, grid=(S // tm,),
        in_specs=[tile, pl.BlockSpec((tm, W), lambda i: (i, CB_MZ // 4)), pl.BlockSpec((1, W), lambda i: (0, 0)),
                  pl.BlockSpec((tm, W), lambda i: (i, 2))],
        out_specs=[tile, tile, pl.BlockSpec((8, W), lambda i: (0, 0))],
        out_shape=[_sds((S, W), F32), _sds((S, W), BF16), _sds((8, W), F32)], name=name,
        compiler_params=_cparams("arbitrary"))(ypre, proj, gain, dycat)


def ada_mod(c_all, w, bias, *, name):
    M, K = c_all.shape
    N = w.shape[1]
    tn = _tile(N, 512)

    def body(c_ref, w_ref, b_ref, o_ref, cond_ref):
        cv = c_ref[...]
        cond = cv * _sigmoid(cv)
        cond_ref[...] = cond
        o_ref[...] = _dot(_bf(cond), _bf(w_ref[...]), NN) + b_ref[...]

    return pl.pallas_call(
        body, grid=(N // tn,),
        in_specs=[pl.BlockSpec((M, K), lambda j: (0, 0)), pl.BlockSpec((K, tn), lambda j: (0, j)), pl.BlockSpec((1, tn), lambda j: (0, j))],
        out_specs=[pl.BlockSpec((M, tn), lambda j: (0, j)), pl.BlockSpec((M, K), lambda j: (0, 0))],
        out_shape=[_sds((M, N), F32), _sds((M, K), F32)], name=name, compiler_params=_cparams("arbitrary"))(c_all, w, bias)


def _adamw(g, w, m, v):
    m = ADAM_B1 * m + (1.0 - ADAM_B1) * g
    v = ADAM_B2 * v + (1.0 - ADAM_B2) * (g * g)
    m_hat = m / (1.0 - ADAM_B1 ** ADAM_STEP)
    v_hat = v / (1.0 - ADAM_B2 ** ADAM_STEP)
    return -ADAM_LR * (m_hat / (jnp.sqrt(v_hat) + ADAM_EPS) + ADAM_WD * w), m, v


def adamw_parts(parts, w, m, v, *, name):
    P, R, C = parts.shape
    tr = _tile(R, 592, 512, 256, 160, 128, 80, 64, 32, 16)

    def body(p_ref, w_ref, m_ref, v_ref, g_ref, d_ref, mo_ref, vo_ref):
        g = p_ref[0].astype(F32)
        for j in range(1, P):
            g = g + p_ref[j].astype(F32)
        g_ref[...] = g
        d_ref[...], mo_ref[...], vo_ref[...] = _adamw(g, w_ref[...], m_ref[...], v_ref[...])

    tile = pl.BlockSpec((tr, C), lambda i: (i, 0))
    return pl.pallas_call(
        body, grid=(R // tr,), in_specs=[pl.BlockSpec((P, tr, C), lambda i: (0, i, 0)), tile, tile, tile],
        out_specs=[tile] * 4, out_shape=[_sds((R, C), F32)] * 4, name=name, compiler_params=_cparams("parallel"))(parts, w, m, v)


def ada_adamw(cond_t, dmod, w, m, v, *, name):
    D, N = w.shape
    tr = _tile(D, 256)

    def body(c_ref, d_ref, w_ref, m_ref, v_ref, g_ref, dl_ref, mo_ref, vo_ref):
        cc = c_ref[...]
        dd = d_ref[...]
        g = cc[:, 0:1] * dd[0:1, :]
        for b in range(1, N_DEV):
            g = g + cc[:, b:b + 1] * dd[b:b + 1, :]
        g_ref[...] = g
        dl_ref[...], mo_ref[...], vo_ref[...] = _adamw(g, w_ref[...], m_ref[...], v_ref[...])

    tile = pl.BlockSpec((tr, N), lambda i: (i, 0))
    return pl.pallas_call(
        body, grid=(D // tr,), in_specs=[pl.BlockSpec((tr, N_DEV), lambda i: (i, 0)), pl.BlockSpec((N_DEV, N), lambda i: (0, 0)), tile, tile, tile],
        out_specs=[tile] * 4, out_shape=[_sds((D, N), F32)] * 4, name=name, compiler_params=_cparams("parallel"))(cond_t, dmod, w, m, v)


def _my_place():
    mx, my, mc = lax.axis_index("x"), lax.axis_index("y"), lax.axis_index("c")
    return mx, my, mc, 4 * mx + 2 * my + mc


def _peer(mx, my, mc, k):
    px = 1 - mx if (k >> 2) & 1 else mx
    py = 1 - my if (k >> 1) & 1 else my
    pc = 1 - mc if k & 1 else mc
    return (px, py, pc), 4 * px + 2 * py + pc


def _comm_call(body, x, out_shape, space, name):
    spec = pl.BlockSpec(memory_space=space)
    return pl.pallas_call(
        body, in_specs=[spec], out_specs=spec, out_shape=out_shape,
        scratch_shapes=[pltpu.SemaphoreType.DMA((N_DEV - 1,)), pltpu.SemaphoreType.DMA((N_DEV - 1,)), pltpu.SemaphoreType.DMA(())],
        name=name, compiler_params=pltpu.CompilerParams(has_side_effects=True, vmem_limit_bytes=VMEM_LIMIT))(x)


def allgather(x, *, in_vmem, name):
    def body(x_ref, out_ref, send_sems, recv_sems, local_sem):
        mx, my, mc, me = _my_place()
        mine = pltpu.make_async_copy(x_ref, out_ref.at[me], local_sem)
        mine.start()
        copies = []
        for k in range(1, N_DEV):
            peer, _ = _peer(mx, my, mc, k)
            cp = pltpu.make_async_remote_copy(src_ref=x_ref, dst_ref=out_ref.at[me], send_sem=send_sems.at[k - 1],
                                              recv_sem=recv_sems.at[k - 1], device_id=peer, device_id_type=pl.DeviceIdType.MESH)
            cp.start()
            copies.append(cp)
        for cp in copies:
            cp.wait()
        mine.wait()

    return _comm_call(body, x, _sds((N_DEV,) + x.shape, x.dtype), pltpu.VMEM if in_vmem else pltpu.HBM, name)


def allgather_two_level(x, *, name):
    def body(x_ref, out_ref, send_sems, recv_sems, local_sem):
        _gather_two_level(x_ref, out_ref, send_sems, recv_sems, local_sem, start=True, finish=True)

    return _comm_call(body, x, _sds((N_DEV,) + x.shape, x.dtype), pltpu.HBM, name)


def _gather_two_level(x_ref, out_ref, send_sems, recv_sems, local_sem, *, start, finish):
    mx, my, mc, _ = _my_place()
    me, sibling = (mx, my, mc), (mx, my, 1 - mc)
    chips = [(1 - mx, my), (mx, 1 - my), (1 - mx, 1 - my)]

    def copy(k, block, to, src=None):
        slot = out_ref.at[4 * block[0] + 2 * block[1] + block[2]]
        return pltpu.make_async_remote_copy(src_ref=slot if src is None else src, dst_ref=slot, send_sem=send_sems.at[k],
                                            recv_sem=recv_sems.at[k], device_id=to, device_id_type=pl.DeviceIdType.MESH)

    mine = pltpu.make_async_copy(x_ref, out_ref.at[4 * mx + 2 * my + mc], local_sem)
    first = [copy(0, me, sibling, src=x_ref)] + [copy(1 + j, me, (*chip, mc), src=x_ref) for j, chip in enumerate(chips)]
    if start:
        mine.start()
        for cp in first:
            cp.start()
    if finish:
        passed = [copy(4 + j, (*chip, mc), sibling) for j, chip in enumerate(chips)]
        for j, chip in enumerate(chips):
            copy(1 + j, (*chip, mc), me).wait_recv()
            passed[j].start()
        copy(0, sibling, me).wait_recv()
        for j, chip in enumerate(chips):
            copy(4 + j, (*chip, 1 - mc), me).wait_recv()
        for cp in first + passed:
            cp.wait_send()
        mine.wait()


def _chip_exchange(s_ref, r_ref, send_sems, recv_sems, local_sem, *, start, finish):
    mx, my, mc, _ = _my_place()
    mine = pltpu.make_async_copy(s_ref.at[2 * mx + my], r_ref.at[2 * mx + my], local_sem)
    copies = []
    for k in range(1, 4):
        px = 1 - mx if (k >> 1) & 1 else mx
        py = 1 - my if k & 1 else my
        copies.append(pltpu.make_async_remote_copy(
            src_ref=s_ref.at[2 * px + py], dst_ref=r_ref.at[2 * mx + my], send_sem=send_sems.at[k - 1], recv_sem=recv_sems.at[k - 1],
            device_id=(px, py, mc), device_id_type=pl.DeviceIdType.MESH))
    if start:
        mine.start()
        for cp in copies:
            cp.start()
    if finish:
        for cp in copies:
            cp.wait()
        mine.wait()


COMM_SCRATCH = [pltpu.SemaphoreType.DMA((N_DEV - 1,)), pltpu.SemaphoreType.DMA((N_DEV - 1,)), pltpu.SemaphoreType.DMA(())]


def sibling_exchange(send, *, name):
    n_chip, _, R, C = send.shape

    def body(s_ref, r_ref, send_sems, recv_sems, local_sem):
        mx, my, mc, _ = _my_place()
        copies = []
        for q in range(n_chip):
            cp = pltpu.make_async_remote_copy(src_ref=s_ref.at[q, 1 - mc], dst_ref=r_ref.at[q], send_sem=send_sems.at[q],
                                              recv_sem=recv_sems.at[q], device_id=(mx, my, 1 - mc), device_id_type=pl.DeviceIdType.MESH)
            cp.start()
            copies.append(cp)
        for cp in copies:
            cp.wait()

    return _comm_call(body, send, _sds((n_chip, R, C), send.dtype), pltpu.HBM, name)


def chip_exchange(send, *, name):
    def body(s_ref, r_ref, send_sems, recv_sems, local_sem):
        _chip_exchange(s_ref, r_ref, send_sems, recv_sems, local_sem, start=True, finish=True)

    return _comm_call(body, send, _sds(send.shape, send.dtype), pltpu.HBM, name)


def add_partials(a, b, *, name):
    P, R, C = a.shape
    tr = _tile(R, 592, 512, 256, 160, 128, 80, 64, 32, 16)

    def body(a_ref, b_ref, o_ref):
        o_ref[...] = _bf(a_ref[...].astype(F32) + b_ref[...].astype(F32))

    tile = pl.BlockSpec((P, tr, C), lambda i: (0, i, 0))
    return pl.pallas_call(body, grid=(R // tr,), in_specs=[tile, tile], out_specs=tile, out_shape=_sds((P, R, C), BF16), name=name,
                          compiler_params=_cparams("parallel"))(a, b)


def _rows128(a):
    f = a.reshape(-1)
    n = -(-f.shape[0] // (16 * LANE)) * (16 * LANE)
    return jnp.pad(f, (0, n - f.shape[0])).reshape(-1, LANE)


PACK_ROWS = 512


def _pad_rows(buf):
    r = buf.shape[-2]
    pad = -r % PACK_ROWS
    return jnp.pad(buf, [(0, 0)] * (buf.ndim - 2) + [(0, pad), (0, 0)])


def _pack(arrays):
    parts = [_rows128(a) for a in arrays]
    offs = np.cumsum([0] + [p.shape[0] for p in parts])
    return _pad_rows(jnp.concatenate(parts, axis=0)), [int(o) for o in offs]


def _unpack(buf, offs, shapes):
    lead = buf.shape[:-2]
    out = []
    for o, shp in zip(offs, shapes):
        n = int(np.prod(shp))
        rows = -(-n // LANE)
        seg = buf[..., o:o + rows, :].reshape(lead + (rows * LANE,))[..., :n]
        out.append(seg.reshape(lead + tuple(shp)))
    return out


def _pad_w_in_t(w_t):
    D = w_t.shape[1]
    dt = jnp.pad(w_t[IN_MAIN:].reshape(4, 2, D), ((0, 0), (0, LANE - 2), (0, 0)))
    return jnp.concatenate([w_t[:IN_MAIN], dt.reshape(4 * LANE, D)], axis=0)


def _unpad_w_in_t(g_t):
    D = g_t.shape[1]
    dt = g_t[IN_MAIN:].reshape(4, LANE, D)[:, :2].reshape(SSM_HEADS, D)
    return jnp.concatenate([g_t[:IN_MAIN], dt], axis=0)


def _piece_rows(n, shard_shape):
    r = shard_shape[0] if n in ROW_SHARDED else shard_shape[1]
    return r, -(-r // 16) * 16


def _to_piece(n, shard):
    t = shard if n in ROW_SHARDED else shard.T
    return jnp.pad(t, ((0, -t.shape[0] % 16), (0, 0)))


def _from_piece(n, piece, shard_shape):
    r, _ = _piece_rows(n, shard_shape)
    return piece[:r] if n in ROW_SHARDED else piece[:r].T


def _pair_rows(p):
    return jnp.pad(p.reshape(4, 1, 2), ((0, 0), (0, 0), (0, LANE - 2)))


def _row(v):
    return v.reshape(1, -1)


def _ffn_fwd(h, gain, mod3, wg, wu, wd, tag):
    shift, scale, gate = mod3
    u = norm_mod(h, gain, shift, scale, name=tag + "_norm")
    a, b, act = ffn_up(u, wg, wu, name=tag + "_up")
    hn, out = matmul_resid(act, wd, h, gate, 0.5, name=tag + "_down")
    return hn, (h, u, a, b, act, out)


def _wgrad(a, b, name):
    return matmul(a, b, ta=True, tm=_tile(a.shape[1], 1408, 1536, 1024, 512), tn=b.shape[1], tk=WGRAD_TOKENS, out_dtype=BF16, name=name)


def _ffn_bwd(dh, saved, gain, mod3, wg, wu, wd, tag):
    h, u, a, b, act, out = saved
    _, scale, gate = mod3
    dout, gst = gate_bwd(dh, out, gate, 0.5, name=tag + "_gate_bwd")
    da, db = ffn_dact(dout, wd, a, b, name=tag + "_dact")
    dh_prev, nst = dgrad_norm_bwd([da, db], [wg, wu], [(0, 0), (1, 0)], h, gain, scale, dh, name=tag + "_dgrad")
    grads = (_wgrad(da, u, tag + "_dwg"), _wgrad(db, u, tag + "_dwu"), _wgrad(act, dout, tag + "_dwd"))
    return dh_prev, grads, nst[0], [nst[1], nst[2], gst[0]]


def _mix_fwd(h, p, mod3, w_in, w_out, cos, sin, tag, gather=None):
    shift, scale, gate = mod3
    u = norm_mod(h, p["norm_mix"], shift, scale, name=tag + "_norm")
    proj = matmul(u, w_in, tb=True, tm=BIG_TOKEN_TILE, tn=512, tk=D_MODEL, name=tag + "_proj")
    y_ret, ypre_ret, st_ret = ret_fwd(proj, p["ret_gn"], cos, sin, name=tag + "_ret")
    y_sb, sb_cin, *gathered = sb_fwd(proj, name=tag + "_sb", gather=gather)
    pre, xact = conv_fwd(proj, p["conv_w"], p["conv_b"], name=tag + "_conv")
    ypre_ssm, st_ssm = ssd_fwd(xact, proj, p["dt_bias"], p["a_log"], p["d_skip"], name=tag + "_ssd")
    y_ssm = gated_norm(ypre_ssm, proj, p["ssm_norm"], name=tag + "_gnorm")
    ycat = jnp.concatenate([y_ret, y_sb.astype(BF16), y_ssm], axis=1)
    hn, mixed = matmul_resid(ycat, w_out, h, gate, 1.0, name=tag + "_out")
    return hn, (h, u, proj, ypre_ret, st_ret, sb_cin, pre, xact, ypre_ssm, st_ssm, ycat, mixed), (gathered[0] if gathered else None)


def _mix_bwd(dh, saved, p, mod3, w_in, w_out, cos, sin, tag, exchange=None):
    h, u, proj, ypre_ret, st_ret, sb_cin, pre, xact, ypre_ssm, st_ssm, ycat, mixed = saved
    _, scale, gate = mod3
    dmixed, gst = gate_bwd(dh, mixed, gate, 1.0, name=tag + "_gate_bwd")
    dycat = matmul(dmixed, w_out, tb=True, tm=BIG_TOKEN_TILE, tn=512, tk=D_MODEL, name=tag + "_dycat")
    dw_out = _wgrad(ycat, dmixed, tag + "_dw_out")
    dret, rst = ret_bwd(proj, p["ret_gn"], cos, sin, ypre_ret, st_ret, dycat, name=tag + "_ret_bwd")
    dsq, dsk, dsv, *exchanged = sb_bwd(proj, sb_cin, dycat, name=tag + "_sb_bwd", exchange=exchange)
    dypre, dz, nst2 = gated_norm_bwd(ypre_ssm, proj, p["ssm_norm"], dycat, name=tag + "_gnorm_bwd")
    dxs, dbm, dcm, ddt, sst = ssd_bwd(xact, proj, p["dt_bias"], p["a_log"], p["d_skip"], st_ssm, dypre, name=tag + "_ssd_bwd")
    dact = jnp.concatenate([dxs, dbm, dcm], axis=1)
    dxbc, cst = conv_bwd(proj, pre, dact, p["conv_w"], name=tag + "_conv_bwd")
    pieces = [dret, dsq, dsk, dsv, dz, dxbc, ddt]
    starts = np.cumsum([0] + [pc.shape[1] for pc in pieces])
    assert starts[-1] == IN_PAD
    dh_prev, nst = dgrad_norm_bwd(pieces, [w_in], [(0, int(r0)) for r0 in starts[:-1]], h, p["norm_mix"], scale, dh, name=tag + "_dgrad")
    dw_in = jnp.concatenate([_wgrad(pc, u, f"{tag}_dw_in{i}") for i, pc in enumerate(pieces)], axis=0)
    small = dict(norm_mix=nst[0], ret_gn=rst[0], ssm_norm=nst2[0], conv_w=cst[0:4], conv_b=cst[4],
                 dt_bias=sst[:, 0, :2].reshape(SSM_HEADS), a_log=sst[:, 1, :2].reshape(SSM_HEADS), d_skip=sst[:, 2, :2].reshape(SSM_HEADS))
    return dh_prev, dw_in, dw_out, small, [nst[1], nst[2], gst[0]], (exchanged[0] if exchanged else None)


BIG = ("ffn1_wg", "ffn1_wu", "ffn1_wd", "w_in", "w_out", "ffn2_wg", "ffn2_wu", "ffn2_wd")
ROW_SHARDED = ("ffn1_wd", "w_out", "ffn2_wd")
SMALL = ("ada_b", "norm_ffn1", "norm_mix", "conv_b", "dt_bias", "a_log", "d_skip", "ret_gn", "ssm_norm", "norm_ffn2",
         "final_ada_b", "final_norm")
NAMES = ("ada_w", "ada_b", "norm_ffn1", "ffn1_wg", "ffn1_wu", "ffn1_wd", "norm_mix", "w_in", "conv_w", "conv_b", "dt_bias", "a_log",
         "d_skip", "ret_gn", "ssm_norm", "w_out", "norm_ffn2", "ffn2_wg", "ffn2_wu", "ffn2_wd", "final_ada_w", "final_ada_b", "final_norm")


def kernel(x, c, ada_w, ada_b, norm_ffn1, ffn1_wg, ffn1_wu, ffn1_wd, norm_mix, w_in, conv_w, conv_b, dt_bias, a_log, d_skip, ret_gn, ssm_norm, w_out, norm_ffn2, ffn2_wg, ffn2_wu, ffn2_wd, final_ada_w, final_ada_b, final_norm, loss_target, m_ada_w, m_ada_b, m_norm_ffn1, m_ffn1_wg, m_ffn1_wu, m_ffn1_wd, m_norm_mix, m_w_in, m_conv_w, m_conv_b, m_dt_bias, m_a_log, m_d_skip, m_ret_gn, m_ssm_norm, m_w_out, m_norm_ffn2, m_ffn2_wg, m_ffn2_wu, m_ffn2_wd, m_final_ada_w, m_final_ada_b, m_final_norm, v_ada_w, v_ada_b, v_norm_ffn1, v_ffn1_wg, v_ffn1_wu, v_ffn1_wd, v_norm_mix, v_w_in, v_conv_w, v_conv_b, v_dt_bias, v_a_log, v_d_skip, v_ret_gn, v_ssm_norm, v_w_out, v_norm_ffn2, v_ffn2_wg, v_ffn2_wu, v_ffn2_wd, v_final_ada_w, v_final_ada_b, v_final_norm):
    W = dict(ada_w=ada_w, ada_b=ada_b, norm_ffn1=norm_ffn1, ffn1_wg=ffn1_wg, ffn1_wu=ffn1_wu, ffn1_wd=ffn1_wd, norm_mix=norm_mix,
             w_in=w_in, conv_w=conv_w, conv_b=conv_b, dt_bias=dt_bias, a_log=a_log, d_skip=d_skip, ret_gn=ret_gn, ssm_norm=ssm_norm,
             w_out=w_out, norm_ffn2=norm_ffn2, ffn2_wg=ffn2_wg, ffn2_wu=ffn2_wu, ffn2_wd=ffn2_wd, final_ada_w=final_ada_w,
             final_ada_b=final_ada_b, final_norm=final_norm)
    M1 = dict(ada_w=m_ada_w, ada_b=m_ada_b, norm_ffn1=m_norm_ffn1, ffn1_wg=m_ffn1_wg, ffn1_wu=m_ffn1_wu, ffn1_wd=m_ffn1_wd,
              norm_mix=m_norm_mix, w_in=m_w_in, conv_w=m_conv_w, conv_b=m_conv_b, dt_bias=m_dt_bias, a_log=m_a_log, d_skip=m_d_skip,
              ret_gn=m_ret_gn, ssm_norm=m_ssm_norm, w_out=m_w_out, norm_ffn2=m_norm_ffn2, ffn2_wg=m_ffn2_wg, ffn2_wu=m_ffn2_wu,
              ffn2_wd=m_ffn2_wd, final_ada_w=m_final_ada_w, final_ada_b=m_final_ada_b, final_norm=m_final_norm)
    V2 = dict(ada_w=v_ada_w, ada_b=v_ada_b, norm_ffn1=v_norm_ffn1, ffn1_wg=v_ffn1_wg, ffn1_wu=v_ffn1_wu, ffn1_wd=v_ffn1_wd,
              norm_mix=v_norm_mix, w_in=v_w_in, conv_w=v_conv_w, conv_b=v_conv_b, dt_bias=v_dt_bias, a_log=v_a_log, d_skip=v_d_skip,
              ret_gn=v_ret_gn, ssm_norm=v_ssm_norm, w_out=v_w_out, norm_ffn2=v_norm_ffn2, ffn2_wg=v_ffn2_wg, ffn2_wu=v_ffn2_wu,
              ffn2_wd=v_ffn2_wd, final_ada_w=v_final_ada_w, final_ada_b=v_final_ada_b, final_norm=v_final_norm)
    D = D_MODEL
    S = x.shape[1]
    me = 4 * lax.axis_index("x") + 2 * lax.axis_index("y") + lax.axis_index("c")
    n_mod = ada_w.shape[2]
    n_fmod = final_ada_w.shape[1]

    c_all = allgather(jnp.broadcast_to(c, (8, D)), in_vmem=True, name="gather_c")[:, 0, :]
    ada_cols = jnp.concatenate([ada_w[0], ada_w[1], final_ada_w], axis=1)
    ada_bias = jnp.concatenate([lax.dynamic_slice(ada_b, (0, me * n_mod), (DEPTH, n_mod)).reshape(1, -1),
                                lax.dynamic_slice(final_ada_b, (me * n_fmod,), (n_fmod,)).reshape(1, -1)], axis=1)
    mod_sh, cond = ada_mod(jnp.pad(c_all, ((0, 8), (0, 0))), ada_cols, ada_bias, name="ada_mod")
    n_cols = mod_sh.shape[1]
    small_in = jnp.concatenate([mod_sh[:8], jnp.pad(conv_w.reshape(8, LANE), ((0, 0), (0, n_cols - LANE)))], axis=0)
    small_g = allgather(small_in, in_vmem=True, name="gather_mod")
    mod_rows = lax.dynamic_index_in_dim(small_g[:, :8, :], me, axis=1, keepdims=False)
    mod = [mod_rows[:, l * n_mod:(l + 1) * n_mod].reshape(9, D) for l in range(DEPTH)]
    fmod = mod_rows[:, DEPTH * n_mod:].reshape(2, D)
    conv_w_full = small_g[:, 8:, :LANE].reshape(N_DEV, DEPTH, SSM_CONV, LANE).transpose(1, 2, 0, 3).reshape(DEPTH, SSM_CONV, 8 * LANE)

    rows = {n: _piece_rows(n, W[n].shape[1:]) for n in BIG}
    offs, o = {}, 0
    for n in BIG:
        offs[n] = o
        o += rows[n][1]
    pack_of = lambda src, dtype, l: jnp.concatenate([_to_piece(n, src[n][l]).astype(dtype) for n in BIG], axis=0)

    def full_weights(wgath):
        f = {n: wgath[:, offs[n]:offs[n] + rows[n][0], :].reshape(N_DEV * rows[n][0], D) for n in BIG}
        f["w_in"] = _pad_w_in_t(f["w_in"])
        return f

    full = [full_weights(allgather_two_level(pack_of(W, BF16, 0), name="gather_weights0"))]

    cos, sin = _rope_tables(S)
    h = x[0]
    target = loss_target[0]
    layer_p = []
    for l in range(DEPTH):
        layer_p.append(dict(norm_ffn1=_row(norm_ffn1[l]), norm_mix=_row(norm_mix[l]), norm_ffn2=_row(norm_ffn2[l]),
                            ret_gn=_row(ret_gn[l]), ssm_norm=_row(ssm_norm[l]), conv_w=conv_w_full[l], conv_b=_row(conv_b[l]),
                            dt_bias=_pair_rows(dt_bias[l]), a_log=_pair_rows(a_log[l]), d_skip=_pair_rows(d_skip[l])))
    mods = [[[_row(mod[l][3 * s + k]) for k in range(3)] for s in range(3)] for l in range(DEPTH)]

    saved = []
    for l in range(DEPTH):
        p, f = layer_p[l], full[l]
        h, s1 = _ffn_fwd(h, p["norm_ffn1"], mods[l][0], f["ffn1_wg"], f["ffn1_wu"], f["ffn1_wd"], f"l{l}_ffn1")
        nxt = pack_of(W, BF16, l + 1) if l + 1 < DEPTH else None
        h, s2, gathered = _mix_fwd(h, p, mods[l][1], f["w_in"], f["w_out"], cos, sin, f"l{l}_mix", gather=nxt)
        if nxt is not None:
            full.append(full_weights(gathered))
        h, s3 = _ffn_fwd(h, p["norm_ffn2"], mods[l][2], f["ffn2_wg"], f["ffn2_wu"], f["ffn2_wd"], f"l{l}_ffn2")
        saved.append((s1, s2, s3))

    def chip_sums(grads):
        def send_piece(n):
            g = _unpad_w_in_t(grads[n]) if n == "w_in" else grads[n]
            r, rp = rows[n]
            return jnp.pad(g.reshape(N_DEV, r, D), ((0, 0), (0, rp - r), (0, 0)))

        spack = jnp.concatenate([send_piece(n) for n in BIG], axis=1)
        by_core = spack.reshape((N_DEV // 2, 2) + spack.shape[1:])
        from_sibling = sibling_exchange(by_core, name=f"exchange_sibling{l}")
        own = lax.dynamic_index_in_dim(by_core, lax.axis_index("c"), axis=1, keepdims=False)
        return add_partials(own, from_sibling, name=f"add_sibling{l}")

    dh, fst = final_loss_bwd(h, _row(final_norm), _row(fmod[0]), _row(fmod[1]), target, name="final")
    rpack = [None] * DEPTH
    small_g_l = [None] * DEPTH
    dmod = [None] * DEPTH
    pending = None
    for l in reversed(range(DEPTH)):
        p, f = layer_p[l], full[l]
        s1, s2, s3 = saved[l]
        dh, (g2g, g2u, g2d), gn2, dm2 = _ffn_bwd(dh, s3, p["norm_ffn2"], mods[l][2], f["ffn2_wg"], f["ffn2_wu"], f["ffn2_wd"], f"l{l}_ffn2")
        dh, gw_in, gw_out, sm, dm1, exchanged = _mix_bwd(dh, s2, p, mods[l][1], f["w_in"], f["w_out"], cos, sin, f"l{l}_mix",
                                                         exchange=pending)
        if pending is not None:
            rpack[l + 1] = exchanged
        dh, (g1g, g1u, g1d), gn1, dm0 = _ffn_bwd(dh, s1, p["norm_ffn1"], mods[l][0], f["ffn1_wg"], f["ffn1_wu"], f["ffn1_wd"], f"l{l}_ffn1")
        pending = chip_sums(dict(zip(BIG, (g1g, g1u, g1d, gw_in, gw_out, g2g, g2u, g2d))))
        sm["norm_ffn1"], sm["norm_ffn2"] = gn1, gn2
        small_g_l[l] = sm
        dmod[l] = jnp.concatenate(dm0 + dm1 + dm2, axis=0)
    rpack[0] = chip_exchange(pending, name="exchange_chips0")
    grad_x = dh[None]

    outs = [adamw_parts(rpack[l], *[pack_of(src, F32, l) for src in (W, M1, V2)], name=f"adamw_big{l}") for l in range(DEPTH)]
    big_out = [{n: jnp.stack([_from_piece(n, outs[l][k][offs[n]:offs[n] + rows[n][1]], W[n].shape[1:]) for l in range(DEPTH)])
                for n in BIG} for k in range(4)]

    stack2 = lambda key: jnp.stack([small_g_l[l][key] for l in range(DEPTH)])
    pieces = [("loss", fst[3, 0:1]), ("ada_b", jnp.stack(dmod)), ("final_ada_b", jnp.concatenate([fst[1], fst[2]])),
              ("norm_ffn1", stack2("norm_ffn1")), ("norm_mix", stack2("norm_mix")), ("norm_ffn2", stack2("norm_ffn2")),
              ("conv_w", stack2("conv_w")), ("conv_b", stack2("conv_b")), ("dt_bias", stack2("dt_bias")), ("a_log", stack2("a_log")),
              ("d_skip", stack2("d_skip")), ("ret_gn", stack2("ret_gn")), ("ssm_norm", stack2("ssm_norm")), ("final_norm", fst[0])]
    names = [n for n, _ in pieces]
    shapes = [a.shape for _, a in pieces]
    ppack, poffs = _pack([a for _, a in pieces])
    pg = allgather(ppack, in_vmem=True, name="gather_small")
    zero_like = lambda n, a: jnp.zeros(a.shape, F32)
    spacks = [_pack([(src[n] if n in SMALL else zero_like(n, a)) for n, a in pieces])[0] for src in (W, M1, V2)]
    souts = adamw_parts(pg, *spacks, name="adamw_small")
    small_out = [dict(zip(names, _unpack(o, poffs, shapes))) for o in souts]
    loss = small_out[0]["loss"][0]

    gathered = dict(zip(names, _unpack(pg, poffs, shapes)))
    conv_parts = lax.dynamic_slice_in_dim(gathered["conv_w"], me * LANE, LANE, axis=3).reshape(N_DEV, DEPTH * SSM_CONV, LANE)
    conv_out = [o.reshape(conv_w.shape) for o in adamw_parts(conv_parts, conv_w.reshape(-1, LANE), m_conv_w.reshape(-1, LANE),
                                                              v_conv_w.reshape(-1, LANE), name="adamw_conv_w")]
    cond_t = cond[:8].T
    ada_out = []
    for l in range(DEPTH):
        dsel = lax.dynamic_slice_in_dim(gathered["ada_b"][:, l, :], me * n_mod, n_mod, axis=1)
        ada_out.append(ada_adamw(cond_t, dsel, ada_w[l], m_ada_w[l], v_ada_w[l], name=f"adamw_ada_w{l}"))
    ada_out = [jnp.stack([ada_out[l][k] for l in range(DEPTH)]) for k in range(4)]
    fsel = lax.dynamic_slice_in_dim(gathered["final_ada_b"].reshape(N_DEV, 2 * D), me * n_fmod, n_fmod, axis=1)
    fada_out = ada_adamw(cond_t, fsel, final_ada_w, m_final_ada_w, v_final_ada_w, name="adamw_final_ada_w")

    def pick(k, n):
        if n in BIG:
            return big_out[k][n]
        if n == "ada_w":
            return ada_out[k]
        if n == "final_ada_w":
            return fada_out[k]
        if n == "conv_w":
            return conv_out[k]
        return small_out[k][n]

    return (loss, grad_x) + tuple(pick(k, n) for k in range(4) for n in NAMES)
```

```python
import functools
import math

import numpy as np
import jax
import jax.numpy as jnp
from jax import lax
from jax.experimental import pallas as pl
from jax.experimental.pallas import tpu as pltpu

F32 = jnp.float32
BF16 = jnp.bfloat16

D_MODEL = 1024
DEPTH = 2
RET_HEADS = 4
HEAD_DIM = 128
SSM_HEADS = 8
SSM_HEAD_DIM = 64
SSM_STATE = 128
SSM_CONV = 4
D_FF = 2816
ROPE_BASE = 10000.0
NORM_EPS = 1e-6
MIX_W = 1536
IN_W = 5128
IN_MAIN = 5120
IN_PAD = 5632
N_DEV = 8
LANE = 128

ADAM_LR = 0.001
ADAM_B1 = 0.9
ADAM_B2 = 0.999
ADAM_EPS = 1e-08
ADAM_WD = 0.01
ADAM_STEP = 10

TOKEN_TILE = 512
WGRAD_TOKENS = 2048
BIG_TOKEN_TILE = 2048
SEQ_BLOCK = 256
VMEM_LIMIT = 56 << 20
SB_SKIP = 120.0
SB_UNVISITED = -1e30

CB_RQ, CB_RK, CB_RV, CB_RG = 0, 4, 8, 12
CB_SQ, CB_SK, CB_SV = 16, 20, 24
CB_MZ, CB_XS, CB_BM, CB_CM, CB_DT = 28, 32, 36, 38, 40


def _cparams(*sem):
    return pltpu.CompilerParams(dimension_semantics=sem, vmem_limit_bytes=VMEM_LIMIT)


def _sds(shape, dtype):
    return jax.ShapeDtypeStruct(tuple(shape), dtype)


def _tile(n, *prefs):
    for p in prefs:
        if n % p == 0:
            return p
    return n


def _dot(a, b, dims):
    return lax.dot_general(a, b, (dims, ((), ())), preferred_element_type=F32)


NN = ((1,), (0,))
NT = ((1,), (1,))
TN = ((0,), (0,))


def _bf(x):
    return x.astype(BF16)


def _sigmoid(x):
    return jax.nn.sigmoid(x)


def _split2(x):
    hi = x.astype(BF16)
    lo = (x - hi.astype(F32)).astype(BF16)
    return hi, lo


def _split3(x):
    hi = x.astype(BF16)
    r = x - hi.astype(F32)
    mid = r.astype(BF16)
    lo = (r - mid.astype(F32)).astype(BF16)
    return hi, mid, lo


def matmul(a, b, *, ta=False, tb=False, tm=512, tn=512, tk=512, out_dtype=F32, name):
    M, K = (a.shape[1], a.shape[0]) if ta else a.shape
    N = b.shape[0] if tb else b.shape[1]
    tm, tn, tk = min(tm, M), min(tn, N), min(tk, K)
    assert M % tm == 0 and N % tn == 0 and K % tk == 0, (name, M, N, K, tm, tn, tk)
    nk = K // tk
    a_spec = pl.BlockSpec((tk, tm), lambda i, j, k: (k, i)) if ta else pl.BlockSpec((tm, tk), lambda i, j, k: (i, k))
    b_spec = pl.BlockSpec((tn, tk), lambda i, j, k: (j, k)) if tb else pl.BlockSpec((tk, tn), lambda i, j, k: (k, j))
    dims = ((0 if ta else 1,), (1 if tb else 0,))

    def body(a_ref, b_ref, o_ref, acc_ref):
        k = pl.program_id(2)
        p = _dot(_bf(a_ref[...]), _bf(b_ref[...]), dims)

        @pl.when(k == 0)
        def _():
            acc_ref[...] = p

        @pl.when(k > 0)
        def _():
            acc_ref[...] += p

        @pl.when(k == nk - 1)
        def _():
            o_ref[...] = acc_ref[...].astype(out_dtype)

    return pl.pallas_call(
        body, grid=(M // tm, N // tn, nk), in_specs=[a_spec, b_spec],
        out_specs=pl.BlockSpec((tm, tn), lambda i, j, k: (i, j)), out_shape=_sds((M, N), out_dtype),
        scratch_shapes=[pltpu.VMEM((tm, tn), F32)], name=name,
        compiler_params=_cparams("parallel", "parallel", "arbitrary"))(a, b)


def matmul_resid(a, w, h, gate, factor, *, name):
    M, K = a.shape
    N = w.shape[1]
    tm = min(TOKEN_TILE, M)

    def body(a_ref, w_ref, h_ref, g_ref, hn_ref, o_ref):
        out = _dot(a_ref[...], w_ref[...], NN)
        o_ref[...] = out
        hn_ref[...] = h_ref[...] + (factor * (1.0 + g_ref[...])) * out

    mn = pl.BlockSpec((tm, N), lambda i: (i, 0))
    return pl.pallas_call(
        body, grid=(M // tm,),
        in_specs=[pl.BlockSpec((tm, K), lambda i: (i, 0)), pl.BlockSpec((K, N), lambda i: (0, 0)), mn,
                  pl.BlockSpec((1, N), lambda i: (0, 0))],
        out_specs=[mn, mn], out_shape=[_sds((M, N), F32), _sds((M, N), F32)], name=name,
        compiler_params=_cparams("parallel"))(a, w, h, gate)


def ffn_up(u, wg_t, wu_t, *, name, gather=None):
    M, K = u.shape
    N = wg_t.shape[0]
    tm, tn = min(BIG_TOKEN_TILE, M), _tile(N, 256)
    grid = (M // tm, N // tn)
    extra = gather is not None

    def body(*refs):
        if extra:
            u_ref, wg_ref, wu_ref, src_ref, a_ref, b_ref, act_ref, dst_ref, send_sems, recv_sems, local_sem = refs
            finish_comm = _carried(_gather_two_level, (src_ref, dst_ref, send_sems, recv_sems, local_sem), *grid)
        else:
            u_ref, wg_ref, wu_ref, a_ref, b_ref, act_ref = refs
            finish_comm = lambda: None
        uu = u_ref[...]
        a = _dot(uu, wg_ref[...], NT)
        b = _dot(uu, wu_ref[...], NT)
        a_ref[...] = _bf(a)
        b_ref[...] = _bf(b)
        act_ref[...] = _bf(a * _sigmoid(a) * b)
        finish_comm()

    mn = pl.BlockSpec((tm, tn), lambda i, j: (i, j))
    wspec = pl.BlockSpec((tn, K), lambda i, j: (j, 0))
    hbm = pl.BlockSpec(memory_space=pltpu.HBM)
    return pl.pallas_call(
        body, grid=grid, in_specs=[pl.BlockSpec((tm, K), lambda i, j: (i, 0)), wspec, wspec] + [hbm] * extra,
        out_specs=[mn, mn, mn] + [hbm] * extra,
        out_shape=[_sds((M, N), BF16)] * 3 + ([_sds((N_DEV,) + gather.shape, gather.dtype)] if extra else []),
        scratch_shapes=COMM_SCRATCH * extra, name=name,
        compiler_params=_cparams("arbitrary", "arbitrary") if extra else _cparams("parallel", "parallel"))(
            *((u, wg_t, wu_t) + ((gather,) if extra else ())))


def ffn_dact(dout, wd, a, b, *, name):
    M, K = dout.shape
    N = wd.shape[0]
    tm, tn = min(BIG_TOKEN_TILE, M), _tile(N, 256)

    def body(d_ref, w_ref, a_ref, b_ref, da_ref, db_ref):
        dact = _dot(d_ref[...], w_ref[...], NT)
        av = a_ref[...].astype(F32)
        sg = _sigmoid(av)
        db_ref[...] = _bf(dact * av * sg)
        da_ref[...] = _bf(dact * b_ref[...].astype(F32) * (sg * (1.0 + av * (1.0 - sg))))

    mn = pl.BlockSpec((tm, tn), lambda i, j: (i, j))
    return pl.pallas_call(
        body, grid=(M // tm, N // tn),
        in_specs=[pl.BlockSpec((tm, K), lambda i, j: (i, 0)), pl.BlockSpec((tn, K), lambda i, j: (j, 0)), mn, mn],
        out_specs=[mn, mn], out_shape=[_sds((M, N), BF16), _sds((M, N), BF16)], name=name,
        compiler_params=_cparams("parallel", "parallel"))(dout, wd, a, b)


def norm_mod(h, gain, shift, scale, *, name):
    S, D = h.shape
    tm = min(TOKEN_TILE, S)

    def body(h_ref, g_ref, sh_ref, sc_ref, u_ref):
        x = h_ref[...]
        r = lax.rsqrt(jnp.mean(x * x, axis=-1, keepdims=True) + NORM_EPS)
        n = x * r * g_ref[...]
        u_ref[...] = _bf(n * (1.0 + sc_ref[...]) + sh_ref[...])

    row = pl.BlockSpec((1, D), lambda i: (0, 0))
    tile = pl.BlockSpec((tm, D), lambda i: (i, 0))
    return pl.pallas_call(body, grid=(S // tm,), in_specs=[tile, row, row, row], out_specs=tile,
                          out_shape=_sds((S, D), BF16), name=name, compiler_params=_cparams("parallel"))(h, gain, shift, scale)


def dgrad_norm_bwd(lhs, ws, spans, h, gain, scale, dres, *, name, exchange=None):
    S, D = h.shape
    tm = min(TOKEN_TILE, S)
    n, nw = len(lhs), len(ws)
    extra = exchange is not None

    def body(*refs):
        l_refs, w_refs = refs[:n], refs[n:n + nw]
        if extra:
            h_ref, g_ref, sc_ref, dres_ref, src_ref, dh_ref, st_ref, dst_ref, send_sems, recv_sems, local_sem = refs[n + nw:]
            finish_comm = _carried(_chip_exchange, (src_ref, dst_ref, send_sems, recv_sems, local_sem), S // tm)
        else:
            h_ref, g_ref, sc_ref, dres_ref, dh_ref, st_ref = refs[n + nw:]
            finish_comm = lambda: None
        du = None
        for lr, (k, r0) in zip(l_refs, spans):
            part = _dot(_bf(lr[...]), w_refs[k][r0:r0 + lr.shape[1], :], NN)
            du = part if du is None else du + part
        x = h_ref[...]
        g = g_ref[...]
        r = lax.rsqrt(jnp.mean(x * x, axis=-1, keepdims=True) + NORM_EPS)
        xhat = x * r
        dn = du * (1.0 + sc_ref[...])
        dxhat = dn * g
        dh_ref[...] = dres_ref[...] + r * (dxhat - xhat * jnp.mean(dxhat * xhat, axis=-1, keepdims=True))

        @pl.when(pl.program_id(0) == 0)
        def _():
            st_ref[...] = jnp.zeros_like(st_ref)

        st_ref[0:1, :] += jnp.sum(dn * xhat, axis=0, keepdims=True)
        st_ref[1:2, :] += jnp.sum(du, axis=0, keepdims=True)
        st_ref[2:3, :] += jnp.sum(du * (xhat * g), axis=0, keepdims=True)
        finish_comm()

    row = pl.BlockSpec((1, D), lambda i: (0, 0))
    tile = pl.BlockSpec((tm, D), lambda i: (i, 0))
    hbm = pl.BlockSpec(memory_space=pltpu.HBM)
    in_specs = [pl.BlockSpec((tm, l.shape[1]), lambda i: (i, 0)) for l in lhs]
    in_specs += [pl.BlockSpec(w.shape, lambda i: (0, 0)) for w in ws]
    in_specs += [tile, row, row, tile] + [hbm] * extra
    return pl.pallas_call(
        body, grid=(S // tm,), in_specs=in_specs, out_specs=[tile, pl.BlockSpec((8, D), lambda i: (0, 0))] + [hbm] * extra,
        out_shape=[_sds((S, D), F32), _sds((8, D), F32)] + ([_sds(exchange.shape, exchange.dtype)] if extra else []),
        scratch_shapes=COMM_SCRATCH * extra, name=name,
        compiler_params=_cparams("arbitrary"))(*lhs, *ws, h, gain, scale, dres, *((exchange,) if extra else ()))


def gate_bwd(dh, out, gate, factor, *, name):
    S, D = dh.shape
    tm = min(TOKEN_TILE, S)

    def body(dh_ref, o_ref, g_ref, do_ref, st_ref):
        d = dh_ref[...]
        do_ref[...] = _bf(d * (factor * (1.0 + g_ref[...])))

        @pl.when(pl.program_id(0) == 0)
        def _():
            st_ref[...] = jnp.zeros_like(st_ref)

        st_ref[0:1, :] += factor * jnp.sum(d * o_ref[...], axis=0, keepdims=True)

    tile = pl.BlockSpec((tm, D), lambda i: (i, 0))
    return pl.pallas_call(
        body, grid=(S // tm,), in_specs=[tile, tile, pl.BlockSpec((1, D), lambda i: (0, 0))],
        out_specs=[tile, pl.BlockSpec((8, D), lambda i: (0, 0))], out_shape=[_sds((S, D), BF16), _sds((8, D), F32)],
        name=name, compiler_params=_cparams("arbitrary"))(dh, out, gate)


def final_loss_bwd(h, gain, shift, scale, target, *, name):
    S, D = h.shape
    tm = min(TOKEN_TILE, S)

    def body(h_ref, g_ref, sh_ref, sc_ref, t_ref, dh_ref, st_ref):
        x = h_ref[...]
        g = g_ref[...]
        r = lax.rsqrt(jnp.mean(x * x, axis=-1, keepdims=True) + NORM_EPS)
        xhat = x * r
        n = xhat * g
        err = n * (1.0 + sc_ref[...]) + sh_ref[...] - t_ref[...]
        dy = err * (1.0 / D)
        dn = dy * (1.0 + sc_ref[...])
        dxhat = dn * g
        dh_ref[...] = r * (dxhat - xhat * jnp.mean(dxhat * xhat, axis=-1, keepdims=True))

        @pl.when(pl.program_id(0) == 0)
        def _():
            st_ref[...] = jnp.zeros_like(st_ref)

        st_ref[0:1, :] += jnp.sum(dn * xhat, axis=0, keepdims=True)
        st_ref[1:2, :] += jnp.sum(dy, axis=0, keepdims=True)
        st_ref[2:3, :] += jnp.sum(dy * n, axis=0, keepdims=True)
        tok = jnp.mean(err * err, axis=-1, keepdims=True)
        st_ref[3:4, :] += 0.5 * jnp.sum(tok, axis=0, keepdims=True)

    row = pl.BlockSpec((1, D), lambda i: (0, 0))
    tile = pl.BlockSpec((tm, D), lambda i: (i, 0))
    return pl.pallas_call(
        body, grid=(S // tm,), in_specs=[tile, row, row, row, tile], out_specs=[tile, pl.BlockSpec((8, D), lambda i: (0, 0))],
        out_shape=[_sds((S, D), F32), _sds((8, D), F32)], name=name,
        compiler_params=_cparams("arbitrary"))(h, gain, shift, scale, target)


def _ret_tables(T):
    heads = np.arange(RET_HEADS, dtype=np.float64)
    lg = np.log1p(-(2.0 ** (-5.0 - heads)))
    t = np.arange(T)
    same = (t[:, None] // 64) == (t[None, :] // 64)
    earlier = (t[None, :] // 64) < (t[:, None] // 64)
    dist = np.abs(t[:, None] - t[None, :]).astype(np.float64)
    dmat = np.where(same | earlier, np.exp(lg[:, None, None] * dist[None]), 0.0)
    qdec = np.exp(lg[:, None] * (t + 1.0)[None, :])
    kdec = np.exp(lg[:, None] * (T - 1.0 - t)[None, :])
    cdec = np.exp(lg * T)
    bc = lambda v: jnp.asarray(np.broadcast_to(v[:, :, None], (RET_HEADS, T, LANE)), F32)
    cd = jnp.asarray(np.broadcast_to(cdec[:, None, None], (RET_HEADS, LANE, LANE)), F32)
    return jnp.asarray(dmat, F32), bc(qdec), bc(kdec), cd


def _rope_tables(S):
    half = HEAD_DIM // 2
    inv_freq = ROPE_BASE ** (-jnp.arange(half, dtype=F32) / half)
    ang = jnp.arange(S, dtype=F32)[:, None] * inv_freq[None, :]
    cos, sin = jnp.cos(ang), jnp.sin(ang)
    return jnp.concatenate([cos, cos], axis=-1), jnp.concatenate([-sin, sin], axis=-1)


def _rope(x, c, s):
    return x * c + pltpu.roll(x, HEAD_DIM // 2, 1) * s


def _rope_t(dx, c, s):
    return dx * c + pltpu.roll(dx * s, HEAD_DIM // 2, 1)


def ret_fwd(proj, gn, cos, sin, *, name):
    S = proj.shape[0]
    T = min(SEQ_BLOCK, S)
    nb = S // T
    dmat, qdec, kdec, cdec = _ret_tables(T)

    def body(q_ref, k_ref, v_ref, g_ref, c_ref, s_ref, dm_ref, qd_ref, kd_ref, cd_ref, gn_ref, yo_ref, yp_ref, st_ref, state):
        @pl.when(pl.program_id(0) == 0)
        def _():
            state[...] = jnp.zeros_like(state)

        c, s = c_ref[...], s_ref[...]
        for h in range(RET_HEADS):
            cols = slice(LANE * h, LANE * (h + 1))
            qr = _rope(q_ref[:, cols], c, s)
            kr = _rope(k_ref[:, cols], c, s) * (HEAD_DIM ** -0.5)
            v = _bf(v_ref[:, cols])
            sp = state[h]
            st_ref[h] = sp
            a = _dot(_bf(qr), _bf(kr), NT) * dm_ref[h]
            y = _dot(_bf(a), v, NN) + _dot(_bf(qr * qd_ref[h]), _bf(sp), NN)
            state[h] = cd_ref[h] * sp + _dot(_bf(kr * kd_ref[h]), v, TN)
            yp_ref[:, cols] = y
            yn = y * lax.rsqrt(jnp.mean(y * y, axis=-1, keepdims=True) + NORM_EPS) * gn_ref[:, cols]
            g = g_ref[:, cols]
            yo_ref[:, cols] = _bf(yn * (g * _sigmoid(g)))

    col = lambda cb: pl.BlockSpec((T, 512), lambda b: (b, cb // 4))
    tok = pl.BlockSpec((T, LANE), lambda b: (b, 0))
    const = lambda shape: pl.BlockSpec(shape, lambda b: (0,) * len(shape))
    out_tok = pl.BlockSpec((T, 512), lambda b: (b, 0))
    return pl.pallas_call(
        body, grid=(nb,),
        in_specs=[col(CB_RQ), col(CB_RK), col(CB_RV), col(CB_RG), tok, tok, const(dmat.shape), const(qdec.shape), const(kdec.shape),
                  const(cdec.shape), const((1, 512))],
        out_specs=[out_tok, out_tok, pl.BlockSpec((None, RET_HEADS, LANE, LANE), lambda b: (b, 0, 0, 0))],
        out_shape=[_sds((S, 512), BF16), _sds((S, 512), F32), _sds((nb, RET_HEADS, LANE, LANE), F32)],
        scratch_shapes=[pltpu.VMEM((RET_HEADS, LANE, LANE), F32)], name=name,
        compiler_params=_cparams("arbitrary"))(proj, proj, proj, proj, cos, sin, dmat, qdec, kdec, cdec, gn)


def ret_bwd(proj, gn, cos, sin, ypre, states, dycat, *, name):
    S = proj.shape[0]
    T = min(SEQ_BLOCK, S)
    nb = S // T
    dmat, qdec, kdec, cdec = _ret_tables(T)

    def body(q_ref, k_ref, v_ref, g_ref, c_ref, s_ref, dm_ref, qd_ref, kd_ref, cd_ref, gn_ref, yp_ref, st_ref, dy_ref,
             d_ref, stat_ref, gstate):
        @pl.when(pl.program_id(0) == 0)
        def _():
            gstate[...] = jnp.zeros_like(gstate)
            stat_ref[...] = jnp.zeros_like(stat_ref)

        c, s = c_ref[...], s_ref[...]
        scale = HEAD_DIM ** -0.5
        for h in range(RET_HEADS):
            cols = slice(LANE * h, LANE * (h + 1))
            qr = _rope(q_ref[:, cols], c, s)
            kr = _rope(k_ref[:, cols], c, s) * scale
            v = _bf(v_ref[:, cols])
            qd, kd, dm = qd_ref[h], kd_ref[h], dm_ref[h]
            sp = _bf(st_ref[h])
            gs = gstate[h]
            gsb = _bf(gs)
            g = g_ref[:, cols]
            sg = _sigmoid(g)
            y = yp_ref[:, cols]
            gn_row = gn_ref[:, cols]
            r = lax.rsqrt(jnp.mean(y * y, axis=-1, keepdims=True) + NORM_EPS)
            yhat = y * r
            dyo = dy_ref[:, cols]
            d_ref[:, 1536 + LANE * h:1536 + LANE * (h + 1)] = _bf(dyo * (yhat * gn_row) * (sg * (1.0 + g * (1.0 - sg))))
            dyn = dyo * (g * sg)
            stat_ref[0:1, cols] += jnp.sum(dyn * yhat, axis=0, keepdims=True)
            dyhat = dyn * gn_row
            dy = _bf(r * (dyhat - yhat * jnp.mean(dyhat * yhat, axis=-1, keepdims=True)))
            qrb, krb = _bf(qr), _bf(kr)
            qdb = _bf(qr * qd)
            kdb = _bf(kr * kd)
            a = _bf(_dot(qrb, krb, NT) * dm)
            da = _bf(_dot(dy, v, NT) * dm)
            d_ref[:, 1024 + LANE * h:1024 + LANE * (h + 1)] = _bf(_dot(a, dy, TN) + _dot(kdb, gsb, NN))
            dqr = _dot(da, krb, NN) + qd * _dot(dy, sp, NT)
            dkr = _dot(da, qrb, TN) + kd * _dot(v, gsb, NT)
            gstate[h] = cd_ref[h] * gs + _dot(qdb, dy, TN)
            d_ref[:, cols] = _bf(_rope_t(dqr, c, s))
            d_ref[:, 512 + LANE * h:512 + LANE * (h + 1)] = _bf(_rope_t(dkr * scale, c, s))

    rb = lambda b: nb - 1 - b
    col = lambda cb: pl.BlockSpec((T, 512), lambda b: (rb(b), cb // 4))
    tok = pl.BlockSpec((T, LANE), lambda b: (rb(b), 0))
    const = lambda shape: pl.BlockSpec(shape, lambda b: (0,) * len(shape))
    tok512 = pl.BlockSpec((T, 512), lambda b: (rb(b), 0))
    return pl.pallas_call(
        body, grid=(nb,),
        in_specs=[col(CB_RQ), col(CB_RK), col(CB_RV), col(CB_RG), tok, tok, const(dmat.shape), const(qdec.shape), const(kdec.shape),
                  const(cdec.shape), const((1, 512)), tok512,
                  pl.BlockSpec((None, RET_HEADS, LANE, LANE), lambda b: (rb(b), 0, 0, 0)), tok512],
        out_specs=[pl.BlockSpec((T, 2048), lambda b: (rb(b), 0)), const((8, 512))],
        out_shape=[_sds((S, 2048), BF16), _sds((8, 512), F32)],
        scratch_shapes=[pltpu.VMEM((RET_HEADS, LANE, LANE), F32)], name=name,
        compiler_params=_cparams("arbitrary"))(proj, proj, proj, proj, cos, sin, dmat, qdec, kdec, cdec, gn, ypre, states, dycat)


def _sb_cast_kv(k_ref, v_ref, kb, vb, S):
    step = min(TOKEN_TILE, S)
    for r in range(0, S, step):
        kb[r:r + step, :] = _bf(k_ref[r:r + step, :])
        vb[r:r + step, :] = _bf(v_ref[r:r + step, :])


def _sb_logits(q, kblk, vis):
    z = _dot(q, kblk, NT) * (HEAD_DIM ** -0.5)
    l = jnp.log1p(jnp.exp(-jnp.abs(z)))
    lb = jnp.minimum(z, 0.0) - l
    lk = jnp.minimum(-z, 0.0) - l
    if vis is not None:
        lk = jnp.where(vis, lk, 0.0)
    return lb, lk


def _tri(T, cmp):
    r = lax.broadcasted_iota(jnp.int32, (T, T), 0)
    c = lax.broadcasted_iota(jnp.int32, (T, T), 1)
    return cmp(r, c)


def _dot_split2(x, m):
    hi, lo = _split2(x)
    return _dot(hi, m, NN) + _dot(lo, m, NN)


def _carried(carry_fn, refs, *grid):
    ids = [pl.program_id(d) for d in range(len(grid))]
    first, last = ids[0] == 0, ids[0] == grid[0] - 1
    for d in range(1, len(grid)):
        first, last = first & (ids[d] == 0), last & (ids[d] == grid[d] - 1)

    @pl.when(first)
    def _():
        carry_fn(*refs, start=True, finish=False)

    def finish():
        @pl.when(last)
        def _():
            carry_fn(*refs, start=False, finish=True)

    return finish


def sb_fwd(proj, *, name, gather=None):
    S = proj.shape[0]
    T = min(SEQ_BLOCK, S)
    nq = S // T

    assert nq <= LANE

    def body(*refs):
        if gather is None:
            q_ref, k_ref, v_ref, o_ref, cin_ref, kb, vb = refs
            finish_comm = lambda: None
        else:
            q_ref, k_ref, v_ref, src_ref, o_ref, cin_ref, dst_ref, kb, vb, send_sems, recv_sems, local_sem = refs
            finish_comm = _carried(_gather_two_level, (src_ref, dst_ref, send_sems, recv_sems, local_sem), RET_HEADS, nq)
        qi = pl.program_id(1)

        @pl.when(qi == 0)
        def _():
            _sb_cast_kv(k_ref, v_ref, kb, vb, S)

        q = _bf(q_ref[...])
        vis = _tri(T, lambda t, s: s < t)
        after = _tri(T, lambda j, s: j > s).astype(BF16)
        lane = lax.broadcasted_iota(jnp.int32, (T, LANE), 1)

        def block(jb, carry, acc, cin, mask):
            rows = pl.ds(pl.multiple_of(jb * T, T), T)
            lb, lk = _sb_logits(q, kb[rows, :], mask)
            tail = _dot_split2(lk, after) + carry
            w = jnp.exp(lb + tail)
            if mask is not None:
                w = jnp.where(mask, w, 0.0)
            return (carry + jnp.sum(lk, axis=1, keepdims=True), acc + _dot(_bf(w), vb[rows, :], NN),
                    jnp.where(lane == jb, carry, cin))

        st = block(qi, jnp.zeros((T, 1), F32), jnp.zeros((T, LANE), F32), jnp.full((T, LANE), SB_UNVISITED, F32), vis)

        def more(c):
            return (c[0] < qi) & (jnp.max(c[1]) > -SB_SKIP)

        def step(c):
            return (c[0] + 1,) + block(qi - 1 - c[0], c[1], c[2], c[3], None)

        st = lax.while_loop(more, step, (jnp.int32(0),) + st)
        o_ref[...] = st[2]
        cin_ref[...] = st[3]
        finish_comm()

    whole = lambda cb: pl.BlockSpec((S, LANE), lambda h, i: (0, cb + h))
    tok = pl.BlockSpec((T, LANE), lambda h, i: (i, h))
    hbm = pl.BlockSpec(memory_space=pltpu.HBM)
    extra = gather is not None
    return pl.pallas_call(
        body, grid=(RET_HEADS, nq),
        in_specs=[pl.BlockSpec((T, LANE), lambda h, i: (i, CB_SQ + h)), whole(CB_SK), whole(CB_SV)] + [hbm] * extra,
        out_specs=[tok, tok] + [hbm] * extra,
        out_shape=[_sds((S, 512), F32), _sds((S, 512), F32)] + ([_sds((N_DEV,) + gather.shape, gather.dtype)] if extra else []),
        scratch_shapes=[pltpu.VMEM((S, LANE), BF16), pltpu.VMEM((S, LANE), BF16)] + COMM_SCRATCH * extra, name=name,
        compiler_params=_cparams("arbitrary", "arbitrary"))(*((proj, proj, proj) + ((gather,) if extra else ())))


def sb_bwd(proj, cin, dycat, *, name, exchange=None):
    S = proj.shape[0]
    T = min(SEQ_BLOCK, S)
    nq = S // T
    scale = HEAD_DIM ** -0.5

    def body(*refs):
        if exchange is None:
            q_ref, k_ref, v_ref, cin_ref, do_ref, dq_ref, dk_ref, dv_ref, kb, vb = refs
            finish_comm = lambda: None
        else:
            (q_ref, k_ref, v_ref, cin_ref, do_ref, src_ref, dq_ref, dk_ref, dv_ref, dst_ref, kb, vb,
             send_sems, recv_sems, local_sem) = refs
            finish_comm = _carried(_chip_exchange, (src_ref, dst_ref, send_sems, recv_sems, local_sem), RET_HEADS, nq)
        qi = pl.program_id(1)

        @pl.when(qi == 0)
        def _():
            _sb_cast_kv(k_ref, v_ref, kb, vb, S)
            dk_ref[...] = jnp.zeros_like(dk_ref)
            dv_ref[...] = jnp.zeros_like(dv_ref)

        q = _bf(q_ref[...])
        dob = _bf(do_ref[...])
        cin = cin_ref[...]
        vis = _tri(T, lambda t, s: s < t)
        after = _tri(T, lambda j, s: j > s).astype(BF16)
        before = _tri(T, lambda s, j: s < j).astype(BF16)
        lane = lax.broadcasted_iota(jnp.int32, (T, LANE), 1)

        def block(jb, ecarry, dq, mask):
            rows = pl.ds(pl.multiple_of(jb * T, T), T)
            kblk, vblk = kb[rows, :], vb[rows, :]
            lb, lk = _sb_logits(q, kblk, mask)
            carry = jnp.sum(jnp.where(lane == jb, cin, 0.0), axis=1, keepdims=True)
            w = jnp.exp(lb + _dot_split2(lk, after) + carry)
            if mask is not None:
                w = jnp.where(mask, w, 0.0)
            e = w * _dot(dob, vblk, NT)
            dv_ref[rows, :] += _dot(_bf(w), dob, TN)
            dlk = _dot_split2(e, before) + ecarry
            beta = jnp.exp(lb)
            dz = e * (1.0 - beta) - beta * dlk
            if mask is not None:
                dz = jnp.where(mask, dz, 0.0)
            dzb = _bf(dz * scale)
            dk_ref[rows, :] += _dot(dzb, q, TN)
            return ecarry + jnp.sum(e, axis=1, keepdims=True), dq + _dot(dzb, kblk, NN)

        lane1 = lane[0:1, :]
        skipped = (jnp.max(cin, axis=0, keepdims=True) <= -SB_SKIP) & (lane1 < qi)
        first = jnp.sum(jnp.where(skipped, 1, 0))
        st = lax.fori_loop(first, qi, lambda jb, c: block(jb, c[0], c[1], None), (jnp.zeros((T, 1), F32), jnp.zeros((T, LANE), F32)))
        st = block(qi, st[0], st[1], vis)
        dq_ref[...] = _bf(st[1])
        finish_comm()

    whole = lambda cb: pl.BlockSpec((S, LANE), lambda h, i: (0, cb + h))
    tok = pl.BlockSpec((T, LANE), lambda h, i: (i, h))
    acc = pl.BlockSpec((S, LANE), lambda h, i: (0, h))
    hbm = pl.BlockSpec(memory_space=pltpu.HBM)
    extra = exchange is not None
    return pl.pallas_call(
        body, grid=(RET_HEADS, nq),
        in_specs=[pl.BlockSpec((T, LANE), lambda h, i: (i, CB_SQ + h)), whole(CB_SK), whole(CB_SV), tok,
                  pl.BlockSpec((T, LANE), lambda h, i: (i, 4 + h))] + [hbm] * extra,
        out_specs=[tok, acc, acc] + [hbm] * extra,
        out_shape=[_sds((S, 512), BF16), _sds((S, 512), F32), _sds((S, 512), F32)] + ([_sds(exchange.shape, exchange.dtype)] if extra else []),
        scratch_shapes=[pltpu.VMEM((S, LANE), BF16), pltpu.VMEM((S, LANE), BF16)] + COMM_SCRATCH * extra, name=name,
        compiler_params=_cparams("arbitrary", "arbitrary"))(*((proj, proj, proj, cin, dycat) + ((exchange,) if extra else ())))


def _shift_down(x, d, row):
    return jnp.where(row >= d, pltpu.roll(x, d, 0), 0.0)


def _shift_up(x, d, row, S):
    return jnp.where(row < S - d, pltpu.roll(x, S - d, 0), 0.0)


def conv_fwd(proj, conv_w, conv_b, *, name):
    S = proj.shape[0]

    def body(x_ref, w_ref, b_ref, pre_ref, act_ref):
        x = x_ref[...]
        row = lax.broadcasted_iota(jnp.int32, x.shape, 0)
        pre = b_ref[...] + w_ref[3:4, :] * x
        for d in range(1, SSM_CONV):
            pre = pre + w_ref[3 - d:4 - d, :] * _shift_down(x, d, row)
        pre_ref[...] = pre
        act_ref[...] = pre * _sigmoid(pre)

    blk = pl.BlockSpec((S, LANE), lambda c: (0, c))
    return pl.pallas_call(
        body, grid=(8,),
        in_specs=[pl.BlockSpec((S, LANE), lambda c: (0, CB_XS + c)), pl.BlockSpec((SSM_CONV, LANE), lambda c: (0, c)),
                  pl.BlockSpec((1, LANE), lambda c: (0, c))],
        out_specs=[blk, blk], out_shape=[_sds((S, 1024), F32), _sds((S, 1024), F32)], name=name,
        compiler_params=_cparams("parallel"))(proj, conv_w, conv_b)


def conv_bwd(proj, pre, dact, conv_w, *, name):
    S = proj.shape[0]

    def body(x_ref, pre_ref, da_ref, w_ref, dx_ref, st_ref):
        x = x_ref[...]
        p = pre_ref[...]
        row = lax.broadcasted_iota(jnp.int32, x.shape, 0)
        sg = _sigmoid(p)
        dpre = da_ref[...] * (sg * (1.0 + p * (1.0 - sg)))
        dx = w_ref[3:4, :] * dpre
        st_ref[3:4, :] = jnp.sum(dpre * x, axis=0, keepdims=True)
        for d in range(1, SSM_CONV):
            dx = dx + w_ref[3 - d:4 - d, :] * _shift_up(dpre, d, row, S)
            st_ref[3 - d:4 - d, :] = jnp.sum(dpre * _shift_down(x, d, row), axis=0, keepdims=True)
        st_ref[4:5, :] = jnp.sum(dpre, axis=0, keepdims=True)
        st_ref[5:8, :] = jnp.zeros((3, LANE), F32)
        dx_ref[...] = _bf(dx)

    blk = pl.BlockSpec((S, LANE), lambda c: (0, c))
    return pl.pallas_call(
        body, grid=(8,),
        in_specs=[pl.BlockSpec((S, LANE), lambda c: (0, CB_XS + c)), blk, blk, pl.BlockSpec((SSM_CONV, LANE), lambda c: (0, c))],
        out_specs=[blk, pl.BlockSpec((8, LANE), lambda c: (0, c))],
        out_shape=[_sds((S, 1024), BF16), _sds((8, 1024), F32)], name=name,
        compiler_params=_cparams("parallel"))(proj, pre, dact, conv_w)


def _softplus(x):
    return jnp.maximum(x, 0.0) + jnp.log1p(jnp.exp(-jnp.abs(x)))


def _pair(lane, v0, v1):
    return jnp.where(lane < SSM_HEAD_DIM, v0, v1)


def _ssd_pair_common(raw, dtb, alog, xs, cm, hprev, T):
    lane = lax.broadcasted_iota(jnp.int32, (T, LANE), 1)
    dt = _softplus(raw + dtb)
    a = -jnp.exp(alog)
    incl = _tri(T, lambda l, s: s <= l).astype(BF16)
    h1, h2, h3 = _split3(dt * a)
    acum = _dot(incl, h1, NN) + _dot(incl, h2, NN) + _dot(incl, h3, NN)
    acum_t = acum.T
    causal = _tri(T, lambda l, s: s <= l)
    decay = [jnp.where(causal, jnp.exp(jnp.minimum(acum[:, j:j + 1] - acum_t[j:j + 1, :], 0.0)), 0.0) for j in (0, 1)]
    dtc = _pair(lane, dt[:, 0:1], dt[:, 1:2])
    ac = _pair(lane, acum[:, 0:1], acum[:, 1:2])
    xdt = xs * dtc
    ea = jnp.exp(ac)
    e_end = jnp.exp(ac[T - 1:T, :] - ac)
    sub = lax.broadcasted_iota(jnp.int32, (LANE, LANE), 0)
    cd = jnp.where(sub < SSM_HEAD_DIM, jnp.exp(acum[T - 1:T, 0:1]), jnp.exp(acum[T - 1:T, 1:2]))
    r = _dot(cm, _bf(hprev), NT)
    return lane, dt, a, acum, decay, dtc, xdt, ea, e_end, cd, r


def ssd_fwd(xact, proj, dtb, alog, dskip, *, name):
    S = xact.shape[0]
    T = min(SEQ_BLOCK, S)
    nb = S // T

    def body(xs_ref, bm_ref, cm_ref, dt_ref, dtb_ref, al_ref, ds_ref, y_ref, st_ref, state):
        @pl.when(pl.program_id(0) == 0)
        def _():
            state[...] = jnp.zeros_like(state)

        for g in range(2):
            bm, cm = _bf(bm_ref[:, LANE * g:LANE * (g + 1)]), _bf(cm_ref[:, LANE * g:LANE * (g + 1)])
            gm = _dot(cm, bm, NT)
            for i in range(2):
                p = 2 * g + i
                cols = slice(LANE * p, LANE * (p + 1))
                xs = xs_ref[:, cols]
                hprev = state[p]
                st_ref[g, i] = hprev
                lane, dt, a, acum, decay, dtc, xdt, ea, e_end, cd, r = _ssd_pair_common(
                    dt_ref[:, cols], dtb_ref[p], al_ref[p], xs, cm, hprev, T)
                xdtb = _bf(xdt)
                y_intra = _pair(lane, _dot(_bf(gm * decay[0]), xdtb, NN), _dot(_bf(gm * decay[1]), xdtb, NN))
                state[p] = cd * hprev + _dot(_bf(xdt * e_end), bm, TN)
                dsk = ds_ref[p]
                lane1 = lane[0:1, :]
                y_ref[:, cols] = y_intra + ea * r + _pair(lane1, dsk[:, 0:1], dsk[:, 1:2]) * xs

    rows = pl.BlockSpec((4, 1, LANE), lambda b: (0, 0, 0))
    return pl.pallas_call(
        body, grid=(nb,),
        in_specs=[pl.BlockSpec((T, 512), lambda b: (b, 0)), pl.BlockSpec((T, 256), lambda b: (b, 2)), pl.BlockSpec((T, 256), lambda b: (b, 3)),
                  pl.BlockSpec((T, 512), lambda b: (b, CB_DT // 4)), rows, rows, rows],
        out_specs=[pl.BlockSpec((T, 512), lambda b: (b, 0)),
                   pl.BlockSpec((2, None, 2, LANE, LANE), lambda b: (0, b, 0, 0, 0))],
        out_shape=[_sds((S, 512), F32), _sds((2, nb, 2, LANE, LANE), F32)],
        scratch_shapes=[pltpu.VMEM((4, LANE, LANE), F32)], name=name,
        compiler_params=_cparams("arbitrary"))(xact, xact, xact, proj, dtb, alog, dskip)


def ssd_bwd(xact, proj, dtb, alog, dskip, states, dy, *, name):
    S = xact.shape[0]
    T = min(SEQ_BLOCK, S)
    nb = S // T

    def body(xs_ref, bm_ref, cm_ref, dt_ref, dtb_ref, al_ref, ds_ref, st_ref, dy_ref,
             dxs_ref, dbm_ref, dcm_ref, ddt_ref, stat_ref, dstate):
        @pl.when(pl.program_id(0) == 0)
        def _():
            dstate[...] = jnp.zeros_like(dstate)
            stat_ref[...] = jnp.zeros_like(stat_ref)

        for g in range(2):
            wide = (slice(None), slice(256 * g, 256 * (g + 1)))
            narrow = (slice(None), slice(LANE * g, LANE * (g + 1)))
            pair = slice(2 * g, 2 * g + 2)
            group(xs_ref.at[wide], bm_ref.at[narrow], cm_ref.at[narrow], dt_ref.at[:, LANE * 2 * g:LANE * (2 * g + 1)],
                  dt_ref.at[:, LANE * (2 * g + 1):LANE * (2 * g + 2)], dtb_ref.at[pair], al_ref.at[pair], ds_ref.at[pair], st_ref.at[g],
                  dy_ref.at[wide], dxs_ref.at[wide], dbm_ref.at[narrow], dcm_ref.at[narrow], ddt_ref.at[wide], stat_ref.at[pair],
                  dstate.at[pair])

    def group(xs_ref, bm_ref, cm_ref, dt0_ref, dt1_ref, dtb_ref, al_ref, ds_ref, st_ref, dy_ref,
              dxs_ref, dbm_ref, dcm_ref, ddt_ref, stat_ref, dstate):
        bm, cm = _bf(bm_ref[...]), _bf(cm_ref[...])
        gm = _dot(cm, bm, NT)
        dbm = jnp.zeros((T, LANE), F32)
        dcm = jnp.zeros((T, LANE), F32)
        after_eq = _tri(T, lambda i, l: l >= i).astype(BF16)
        rowi = lax.broadcasted_iota(jnp.int32, (T, 1), 0)
        for i, dt_ref in enumerate((dt0_ref, dt1_ref)):
            xs = xs_ref[:, LANE * i:LANE * (i + 1)]
            dyp = dy_ref[:, LANE * i:LANE * (i + 1)]
            hprev = st_ref[i]
            dh = dstate[i]
            raw = dt_ref[...]
            lane, dt, a, acum, decay, dtc, xdt, ea, e_end, cd, r = _ssd_pair_common(
                raw, dtb_ref[i], al_ref[i], xs, cm, hprev, T)
            lane1 = lane[0:1, :]
            dsk = ds_ref[i]
            dskp = _pair(lane1, dsk[:, 0:1], dsk[:, 1:2])
            head = [lane < SSM_HEAD_DIM, lane >= SSM_HEAD_DIM]
            hsum = lambda v, j: jnp.sum(jnp.where(head[j], v, 0.0), axis=1, keepdims=True)
            dhb = _bf(dh)
            xdtb = _bf(xdt)
            dyb = _bf(dyp)
            z = xdt * e_end
            dz = _dot(bm, dhb, NT)
            dbm = dbm + _dot(_bf(z), dhb, NN)
            dxdt = dz * e_end
            de_e = dz * z
            drr = dyp * ea
            dea_ea = drr * r
            dcm = dcm + _dot(_bf(drr), _bf(hprev), NN)
            dstate[i] = cd * dh + _dot(_bf(drr), cm, TN)
            dcd_cd = cd * dh * hprev
            dgs = jnp.zeros((T, T), F32)
            da_cols = []
            for j in (0, 1):
                w = gm * decay[j]
                dw = _dot(_bf(jnp.where(head[j], dyp, 0.0)), xdtb, NT)
                dxdt = dxdt + jnp.where(head[j], _dot(_bf(w), dyb, TN), 0.0)
                dgs = dgs + dw * decay[j]
                dseg = dw * w
                col = jnp.sum(dseg, axis=1, keepdims=True) - jnp.sum(dseg.T, axis=1, keepdims=True)
                col = col + hsum(dea_ea, j) - hsum(de_e, j)
                sub = lax.broadcasted_iota(jnp.int32, (LANE, LANE), 0)
                in_head = (sub < SSM_HEAD_DIM) if j == 0 else (sub >= SSM_HEAD_DIM)
                end = jnp.sum(hsum(de_e, j), axis=0, keepdims=True) + jnp.sum(
                    jnp.sum(jnp.where(in_head, dcd_cd, 0.0), axis=1, keepdims=True), axis=0, keepdims=True)
                da_cols.append(col + jnp.where(rowi == T - 1, end, 0.0))
            dgb = _bf(dgs)
            dcm = dcm + _dot(dgb, bm, NN)
            dbm = dbm + _dot(dgb, cm, TN)
            dacum = jnp.where(lane == 0, da_cols[0], jnp.where(lane == 1, da_cols[1], 0.0))
            h1, h2, h3 = _split3(dacum)
            ddta = _dot(after_eq, h1, NN) + _dot(after_eq, h2, NN) + _dot(after_eq, h3, NN)
            dxs_ref[:, LANE * i:LANE * (i + 1)] = dskp * dyp + dxdt * dtc
            dx_x = dxdt * xs
            ddt = ddta * a + jnp.where(lane == 0, hsum(dx_x, 0), jnp.where(lane == 1, hsum(dx_x, 1), 0.0))
            ddraw = jnp.where(lane < 2, ddt * _sigmoid(raw + dtb_ref[i]), 0.0)
            ddt_ref[:, LANE * i:LANE * (i + 1)] = _bf(ddraw)
            dsum = jnp.sum(dyp * xs, axis=0, keepdims=True)
            d0 = jnp.sum(jnp.where(lane1 < SSM_HEAD_DIM, dsum, 0.0), axis=1, keepdims=True)
            d1 = jnp.sum(jnp.where(lane1 >= SSM_HEAD_DIM, dsum, 0.0), axis=1, keepdims=True)
            dd = jnp.where(lane1 == 0, d0, jnp.where(lane1 == 1, d1, 0.0))
            stat_ref[i, 0:1, :] += jnp.sum(ddraw, axis=0, keepdims=True)
            stat_ref[i, 1:2, :] += jnp.where(lane1 < 2, jnp.sum(ddta * dt, axis=0, keepdims=True) * a, 0.0)
            stat_ref[i, 2:3, :] += dd
        dbm_ref[...] = dbm
        dcm_ref[...] = dcm

    rb = lambda b: nb - 1 - b
    rows = pl.BlockSpec((4, 1, LANE), lambda b: (0, 0, 0))
    tok512 = pl.BlockSpec((T, 512), lambda b: (rb(b), 0))
    tok256 = pl.BlockSpec((T, 256), lambda b: (rb(b), 0))
    return pl.pallas_call(
        body, grid=(nb,),
        in_specs=[tok512, pl.BlockSpec((T, 256), lambda b: (rb(b), 2)), pl.BlockSpec((T, 256), lambda b: (rb(b), 3)),
                  pl.BlockSpec((T, 512), lambda b: (rb(b), CB_DT // 4)), rows, rows, rows,
                  pl.BlockSpec((2, None, 2, LANE, LANE), lambda b: (0, rb(b), 0, 0, 0)), tok512],
        out_specs=[tok512, tok256, tok256, tok512, pl.BlockSpec((4, 8, LANE), lambda b: (0, 0, 0))],
        out_shape=[_sds((S, 512), F32), _sds((S, 256), F32), _sds((S, 256), F32), _sds((S, 512), BF16), _sds((4, 8, LANE), F32)],
        scratch_shapes=[pltpu.VMEM((4, LANE, LANE), F32)], name=name,
        compiler_params=_cparams("arbitrary"))(xact, xact, xact, proj, dtb, alog, dskip, states, dy)


def gated_norm(ypre, proj, gain, *, name):
    S, W = ypre.shape
    tm = min(TOKEN_TILE, S)

    def body(y_ref, z_ref, g_ref, o_ref):
        z = z_ref[...]
        yg = y_ref[...] * (z * _sigmoid(z))
        o_ref[...] = _bf(yg * lax.rsqrt(jnp.mean(yg * yg, axis=-1, keepdims=True) + NORM_EPS) * g_ref[...])

    tile = pl.BlockSpec((tm, W), lambda i: (i, 0))
    return pl.pallas_call(
        body, grid=(S // tm,), in_specs=[tile, pl.BlockSpec((tm, W), lambda i: (i, CB_MZ // 4)), pl.BlockSpec((1, W), lambda i: (0, 0))],
        out_specs=tile, out_shape=_sds((S, W), BF16), name=name, compiler_params=_cparams("parallel"))(ypre, proj, gain)


def gated_norm_bwd(ypre, proj, gain, dycat, *, name):
    S, W = ypre.shape
    tm = min(TOKEN_TILE, S)

    def body(y_ref, z_ref, g_ref, dy_ref, dyp_ref, dz_ref, st_ref):
        z = z_ref[...]
        y = y_ref[...]
        sg = _sigmoid(z)
        sz = z * sg
        yg = y * sz
        r = lax.rsqrt(jnp.mean(yg * yg, axis=-1, keepdims=True) + NORM_EPS)
        yhat = yg * r
        dyo = dy_ref[...]

        @pl.when(pl.program_id(0) == 0)
        def _():
            st_ref[...] = jnp.zeros_like(st_ref)

        st_ref[0:1, :] += jnp.sum(dyo * yhat, axis=0, keepdims=True)
        dyhat = dyo * g_ref[...]
        dyg = r * (dyhat - yhat * jnp.mean(dyhat * yhat, axis=-1, keepdims=True))
        dyp_ref[...] = dyg * sz
        dz_ref[...] = _bf(dyg * y * (sg * (1.0 + z * (1.0 - sg))))

    tile = pl.BlockSpec((tm, W), lambda i: (i, 0))
    return pl.pallas_call(
        body, grid=(S // tm,),
        in_specs=[tile, pl.BlockSpec((tm, W), lambda i: (i, CB_MZ // 4)), pl.BlockSpec((1, W), lambda i: (0, 0)),
                  pl.BlockSpec((tm, W), lambda i: (i, 2))],
        out_specs=[tile, tile, pl.BlockSpec((8, W), lambda i: (0, 0))],
        out_shape=[_sds((S, W), F32), _sds((S, W), BF16), _sds((8, W), F32)], name=name,
        compiler_params=_cparams("arbitrary"))(ypre, proj, gain, dycat)


def ada_mod(c_all, w, bias, *, name):
    M, K = c_all.shape
    N = w.shape[1]
    tn = _tile(N, 512)

    def body(c_ref, w_ref, b_ref, o_ref, cond_ref):
        cv = c_ref[...]
        cond = cv * _sigmoid(cv)
        cond_ref[...] = cond
        o_ref[...] = _dot(_bf(cond), _bf(w_ref[...]), NN) + b_ref[...]

    return pl.pallas_call(
        body, grid=(N // tn,),
        in_specs=[pl.BlockSpec((M, K), lambda j: (0, 0)), pl.BlockSpec((K, tn), lambda j: (0, j)), pl.BlockSpec((1, tn), lambda j: (0, j))],
        out_specs=[pl.BlockSpec((M, tn), lambda j: (0, j)), pl.BlockSpec((M, K), lambda j: (0, 0))],
        out_shape=[_sds((M, N), F32), _sds((M, K), F32)], name=name, compiler_params=_cparams("arbitrary"))(c_all, w, bias)


def _adamw(g, w, m, v):
    m = ADAM_B1 * m + (1.0 - ADAM_B1) * g
    v = ADAM_B2 * v + (1.0 - ADAM_B2) * (g * g)
    m_hat = m / (1.0 - ADAM_B1 ** ADAM_STEP)
    v_hat = v / (1.0 - ADAM_B2 ** ADAM_STEP)
    return -ADAM_LR * (m_hat / (jnp.sqrt(v_hat) + ADAM_EPS) + ADAM_WD * w), m, v


def adamw_parts(parts, w, m, v, *, name):
    P, R, C = parts.shape
    tr = _tile(R, 592, 512, 352, 272, 256, 160, 128, 80, 64, 32, 16)

    def body(p_ref, w_ref, m_ref, v_ref, g_ref, d_ref, mo_ref, vo_ref):
        g = p_ref[0].astype(F32)
        for j in range(1, P):
            g = g + p_ref[j].astype(F32)
        g_ref[...] = g
        d_ref[...], mo_ref[...], vo_ref[...] = _adamw(g, w_ref[...], m_ref[...], v_ref[...])

    tile = pl.BlockSpec((tr, C), lambda i: (i, 0))
    return pl.pallas_call(
        body, grid=(R // tr,), in_specs=[pl.BlockSpec((P, tr, C), lambda i: (0, i, 0)), tile, tile, tile],
        out_specs=[tile] * 4, out_shape=[_sds((R, C), F32)] * 4, name=name, compiler_params=_cparams("parallel"))(parts, w, m, v)


def ada_adamw(cond_t, dmod, w, m, v, *, name):
    D, N = w.shape
    tr = _tile(D, 256)

    def body(c_ref, d_ref, w_ref, m_ref, v_ref, g_ref, dl_ref, mo_ref, vo_ref):
        cc = c_ref[...]
        dd = d_ref[...]
        g = cc[:, 0:1] * dd[0:1, :]
        for b in range(1, N_DEV):
            g = g + cc[:, b:b + 1] * dd[b:b + 1, :]
        g_ref[...] = g
        dl_ref[...], mo_ref[...], vo_ref[...] = _adamw(g, w_ref[...], m_ref[...], v_ref[...])

    tile = pl.BlockSpec((tr, N), lambda i: (i, 0))
    return pl.pallas_call(
        body, grid=(D // tr,), in_specs=[pl.BlockSpec((tr, N_DEV), lambda i: (i, 0)), pl.BlockSpec((N_DEV, N), lambda i: (0, 0)), tile, tile, tile],
        out_specs=[tile] * 4, out_shape=[_sds((D, N), F32)] * 4, name=name, compiler_params=_cparams("parallel"))(cond_t, dmod, w, m, v)


def _my_place():
    mx, my, mc = lax.axis_index("x"), lax.axis_index("y"), lax.axis_index("c")
    return mx, my, mc, 4 * mx + 2 * my + mc


def _peer(mx, my, mc, k):
    px = 1 - mx if (k >> 2) & 1 else mx
    py = 1 - my if (k >> 1) & 1 else my
    pc = 1 - mc if k & 1 else mc
    return (px, py, pc), 4 * px + 2 * py + pc


def _comm_call(body, x, out_shape, space, name):
    spec = pl.BlockSpec(memory_space=space)
    return pl.pallas_call(
        body, in_specs=[spec], out_specs=spec, out_shape=out_shape,
        scratch_shapes=[pltpu.SemaphoreType.DMA((N_DEV - 1,)), pltpu.SemaphoreType.DMA((N_DEV - 1,)), pltpu.SemaphoreType.DMA(())],
        name=name, compiler_params=pltpu.CompilerParams(has_side_effects=True, vmem_limit_bytes=VMEM_LIMIT))(x)


def allgather(x, *, in_vmem, name):
    def body(x_ref, out_ref, send_sems, recv_sems, local_sem):
        mx, my, mc, me = _my_place()
        mine = pltpu.make_async_copy(x_ref, out_ref.at[me], local_sem)
        mine.start()
        copies = []
        for k in range(1, N_DEV):
            peer, _ = _peer(mx, my, mc, k)
            cp = pltpu.make_async_remote_copy(src_ref=x_ref, dst_ref=out_ref.at[me], send_sem=send_sems.at[k - 1],
                                              recv_sem=recv_sems.at[k - 1], device_id=peer, device_id_type=pl.DeviceIdType.MESH)
            cp.start()
            copies.append(cp)
        for cp in copies:
            cp.wait()
        mine.wait()

    return _comm_call(body, x, _sds((N_DEV,) + x.shape, x.dtype), pltpu.VMEM if in_vmem else pltpu.HBM, name)


def allgather_two_level(x, *, name):
    def body(x_ref, out_ref, send_sems, recv_sems, local_sem):
        _gather_two_level(x_ref, out_ref, send_sems, recv_sems, local_sem, start=True, finish=True)

    return _comm_call(body, x, _sds((N_DEV,) + x.shape, x.dtype), pltpu.HBM, name)


def _gather_two_level(x_ref, out_ref, send_sems, recv_sems, local_sem, *, start, finish):
    mx, my, mc, _ = _my_place()
    me, sibling = (mx, my, mc), (mx, my, 1 - mc)
    chips = [(1 - mx, my), (mx, 1 - my), (1 - mx, 1 - my)]

    def copy(k, block, to, src=None):
        slot = out_ref.at[4 * block[0] + 2 * block[1] + block[2]]
        return pltpu.make_async_remote_copy(src_ref=slot if src is None else src, dst_ref=slot, send_sem=send_sems.at[k],
                                            recv_sem=recv_sems.at[k], device_id=to, device_id_type=pl.DeviceIdType.MESH)

    mine = pltpu.make_async_copy(x_ref, out_ref.at[4 * mx + 2 * my + mc], local_sem)
    first = [copy(0, me, sibling, src=x_ref)] + [copy(1 + j, me, (*chip, mc), src=x_ref) for j, chip in enumerate(chips)]
    if start:
        mine.start()
        for cp in first:
            cp.start()
    if finish:
        passed = [copy(4 + j, (*chip, mc), sibling) for j, chip in enumerate(chips)]
        for j, chip in enumerate(chips):
            copy(1 + j, (*chip, mc), me).wait_recv()
            passed[j].start()
        copy(0, sibling, me).wait_recv()
        for j, chip in enumerate(chips):
            copy(4 + j, (*chip, 1 - mc), me).wait_recv()
        for cp in first + passed:
            cp.wait_send()
        mine.wait()


def _chip_exchange(s_ref, r_ref, send_sems, recv_sems, local_sem, *, start, finish):
    mx, my, mc, _ = _my_place()
    mine = pltpu.make_async_copy(s_ref.at[2 * mx + my], r_ref.at[2 * mx + my], local_sem)
    copies = []
    for k in range(1, 4):
        px = 1 - mx if (k >> 1) & 1 else mx
        py = 1 - my if k & 1 else my
        copies.append(pltpu.make_async_remote_copy(
            src_ref=s_ref.at[2 * px + py], dst_ref=r_ref.at[2 * mx + my], send_sem=send_sems.at[k - 1], recv_sem=recv_sems.at[k - 1],
            device_id=(px, py, mc), device_id_type=pl.DeviceIdType.MESH))
    if start:
        mine.start()
        for cp in copies:
            cp.start()
    if finish:
        for cp in copies:
            cp.wait()
        mine.wait()


COMM_SCRATCH = [pltpu.SemaphoreType.DMA((N_DEV - 1,)), pltpu.SemaphoreType.DMA((N_DEV - 1,)), pltpu.SemaphoreType.DMA(())]


def sibling_exchange(send, *, name):
    n_chip, _, R, C = send.shape

    def body(s_ref, r_ref, send_sems, recv_sems, local_sem):
        mx, my, mc, _ = _my_place()
        copies = []
        for q in range(n_chip):
            cp = pltpu.make_async_remote_copy(src_ref=s_ref.at[q, 1 - mc], dst_ref=r_ref.at[q], send_sem=send_sems.at[q],
                                              recv_sem=recv_sems.at[q], device_id=(mx, my, 1 - mc), device_id_type=pl.DeviceIdType.MESH)
            cp.start()
            copies.append(cp)
        for cp in copies:
            cp.wait()

    return _comm_call(body, send, _sds((n_chip, R, C), send.dtype), pltpu.HBM, name)


def chip_exchange(send, *, name):
    def body(s_ref, r_ref, send_sems, recv_sems, local_sem):
        _chip_exchange(s_ref, r_ref, send_sems, recv_sems, local_sem, start=True, finish=True)

    return _comm_call(body, send, _sds(send.shape, send.dtype), pltpu.HBM, name)


def add_partials(a, b, *, name):
    P, R, C = a.shape
    tr = _tile(R, 592, 512, 352, 272, 256, 160, 128, 80, 64, 32, 16)

    def body(a_ref, b_ref, o_ref):
        o_ref[...] = _bf(a_ref[...].astype(F32) + b_ref[...].astype(F32))

    tile = pl.BlockSpec((P, tr, C), lambda i: (0, i, 0))
    return pl.pallas_call(body, grid=(R // tr,), in_specs=[tile, tile], out_specs=tile, out_shape=_sds((P, R, C), BF16), name=name,
                          compiler_params=_cparams("parallel"))(a, b)


def _rows128(a):
    f = a.reshape(-1)
    n = -(-f.shape[0] // (16 * LANE)) * (16 * LANE)
    return jnp.pad(f, (0, n - f.shape[0])).reshape(-1, LANE)


PACK_ROWS = 512


def _pad_rows(buf):
    r = buf.shape[-2]
    pad = -r % PACK_ROWS
    return jnp.pad(buf, [(0, 0)] * (buf.ndim - 2) + [(0, pad), (0, 0)])


def _pack(arrays):
    parts = [_rows128(a) for a in arrays]
    offs = np.cumsum([0] + [p.shape[0] for p in parts])
    return _pad_rows(jnp.concatenate(parts, axis=0)), [int(o) for o in offs]


def _unpack(buf, offs, shapes):
    lead = buf.shape[:-2]
    out = []
    for o, shp in zip(offs, shapes):
        n = int(np.prod(shp))
        rows = -(-n // LANE)
        seg = buf[..., o:o + rows, :].reshape(lead + (rows * LANE,))[..., :n]
        out.append(seg.reshape(lead + tuple(shp)))
    return out


def _pad_w_in_t(w_t):
    D = w_t.shape[1]
    dt = jnp.pad(w_t[IN_MAIN:].reshape(4, 2, D), ((0, 0), (0, LANE - 2), (0, 0)))
    return jnp.concatenate([w_t[:IN_MAIN], dt.reshape(4 * LANE, D)], axis=0)


def _unpad_w_in_t(g_t):
    D = g_t.shape[1]
    dt = g_t[IN_MAIN:].reshape(4, LANE, D)[:, :2].reshape(SSM_HEADS, D)
    return jnp.concatenate([g_t[:IN_MAIN], dt], axis=0)


def _piece_rows(n, shard_shape):
    r = shard_shape[0] if n in ROW_SHARDED else shard_shape[1]
    return r, -(-r // 16) * 16


def _to_piece(n, shard):
    t = shard if n in ROW_SHARDED else shard.T
    return jnp.pad(t, ((0, -t.shape[0] % 16), (0, 0)))


def _from_piece(n, piece, shard_shape):
    r, _ = _piece_rows(n, shard_shape)
    return piece[:r] if n in ROW_SHARDED else piece[:r].T


def _pair_rows(p):
    return jnp.pad(p.reshape(4, 1, 2), ((0, 0), (0, 0), (0, LANE - 2)))


def _row(v):
    return v.reshape(1, -1)


def _ffn_fwd(h, gain, mod3, wg, wu, wd, tag, gather=None):
    shift, scale, gate = mod3
    u = norm_mod(h, gain, shift, scale, name=tag + "_norm")
    a, b, act, *gathered = ffn_up(u, wg, wu, name=tag + "_up", gather=gather)
    hn, out = matmul_resid(act, wd, h, gate, 0.5, name=tag + "_down")
    return hn, (h, u, a, b, act, out), (gathered[0] if gathered else None)


def _wgrad(a, b, name):
    return matmul(a, b, ta=True, tm=_tile(a.shape[1], 1408, 1536, 1024, 512), tn=b.shape[1], tk=WGRAD_TOKENS, out_dtype=BF16, name=name)


def _ffn_bwd(dh, saved, gain, mod3, wg, wu, wd, tag, exchange=None):
    h, u, a, b, act, out = saved
    _, scale, gate = mod3
    dout, gst = gate_bwd(dh, out, gate, 0.5, name=tag + "_gate_bwd")
    da, db = ffn_dact(dout, wd, a, b, name=tag + "_dact")
    dh_prev, nst, *exchanged = dgrad_norm_bwd([da, db], [wg, wu], [(0, 0), (1, 0)], h, gain, scale, dh, name=tag + "_dgrad",
                                              exchange=exchange)
    grads = (_wgrad(da, u, tag + "_dwg"), _wgrad(db, u, tag + "_dwu"), _wgrad(act, dout, tag + "_dwd"))
    return dh_prev, grads, nst[0], [nst[1], nst[2], gst[0]], (exchanged[0] if exchanged else None)


def _mix_fwd(h, p, mod3, w_in, w_out, cos, sin, tag, gather=None):
    shift, scale, gate = mod3
    u = norm_mod(h, p["norm_mix"], shift, scale, name=tag + "_norm")
    proj = matmul(u, w_in, tb=True, tm=BIG_TOKEN_TILE, tn=512, tk=D_MODEL, name=tag + "_proj")
    y_ret, ypre_ret, st_ret = ret_fwd(proj, p["ret_gn"], cos, sin, name=tag + "_ret")
    y_sb, sb_cin, *gathered = sb_fwd(proj, name=tag + "_sb", gather=gather)
    pre, xact = conv_fwd(proj, p["conv_w"], p["conv_b"], name=tag + "_conv")
    ypre_ssm, st_ssm = ssd_fwd(xact, proj, p["dt_bias"], p["a_log"], p["d_skip"], name=tag + "_ssd")
    y_ssm = gated_norm(ypre_ssm, proj, p["ssm_norm"], name=tag + "_gnorm")
    ycat = jnp.concatenate([y_ret, y_sb.astype(BF16), y_ssm], axis=1)
    hn, mixed = matmul_resid(ycat, w_out, h, gate, 1.0, name=tag + "_out")
    return hn, (h, u, proj, ypre_ret, st_ret, sb_cin, pre, xact, ypre_ssm, st_ssm, ycat, mixed), (gathered[0] if gathered else None)


def _mix_bwd(dh, saved, p, mod3, w_in, w_out, cos, sin, tag, exchange=None):
    h, u, proj, ypre_ret, st_ret, sb_cin, pre, xact, ypre_ssm, st_ssm, ycat, mixed = saved
    _, scale, gate = mod3
    dmixed, gst = gate_bwd(dh, mixed, gate, 1.0, name=tag + "_gate_bwd")
    dycat = matmul(dmixed, w_out, tb=True, tm=BIG_TOKEN_TILE, tn=512, tk=D_MODEL, name=tag + "_dycat")
    dw_out = _wgrad(ycat, dmixed, tag + "_dw_out")
    dret, rst = ret_bwd(proj, p["ret_gn"], cos, sin, ypre_ret, st_ret, dycat, name=tag + "_ret_bwd")
    dsq, dsk, dsv, *exchanged = sb_bwd(proj, sb_cin, dycat, name=tag + "_sb_bwd", exchange=exchange)
    dypre, dz, nst2 = gated_norm_bwd(ypre_ssm, proj, p["ssm_norm"], dycat, name=tag + "_gnorm_bwd")
    dxs, dbm, dcm, ddt, sst = ssd_bwd(xact, proj, p["dt_bias"], p["a_log"], p["d_skip"], st_ssm, dypre, name=tag + "_ssd_bwd")
    dact = jnp.concatenate([dxs, dbm, dcm], axis=1)
    dxbc, cst = conv_bwd(proj, pre, dact, p["conv_w"], name=tag + "_conv_bwd")
    pieces = [dret, dsq, dsk, dsv, dz, dxbc, ddt]
    starts = np.cumsum([0] + [pc.shape[1] for pc in pieces])
    assert starts[-1] == IN_PAD
    dh_prev, nst = dgrad_norm_bwd(pieces, [w_in], [(0, int(r0)) for r0 in starts[:-1]], h, p["norm_mix"], scale, dh, name=tag + "_dgrad")
    dw_in = jnp.concatenate([_wgrad(pc, u, f"{tag}_dw_in{i}") for i, pc in enumerate(pieces)], axis=0)
    small = dict(norm_mix=nst[0], ret_gn=rst[0], ssm_norm=nst2[0], conv_w=cst[0:4], conv_b=cst[4],
                 dt_bias=sst[:, 0, :2].reshape(SSM_HEADS), a_log=sst[:, 1, :2].reshape(SSM_HEADS), d_skip=sst[:, 2, :2].reshape(SSM_HEADS))
    return dh_prev, dw_in, dw_out, small, [nst[1], nst[2], gst[0]], (exchanged[0] if exchanged else None)


BIG = ("ffn1_wg", "ffn1_wu", "ffn1_wd", "w_in", "w_out", "ffn2_wg", "ffn2_wu", "ffn2_wd")
FIRST, REST = BIG[:3], BIG[3:]
ROW_SHARDED = ("ffn1_wd", "w_out", "ffn2_wd")
SMALL = ("ada_b", "norm_ffn1", "norm_mix", "conv_b", "dt_bias", "a_log", "d_skip", "ret_gn", "ssm_norm", "norm_ffn2",
         "final_ada_b", "final_norm")
NAMES = ("ada_w", "ada_b", "norm_ffn1", "ffn1_wg", "ffn1_wu", "ffn1_wd", "norm_mix", "w_in", "conv_w", "conv_b", "dt_bias", "a_log",
         "d_skip", "ret_gn", "ssm_norm", "w_out", "norm_ffn2", "ffn2_wg", "ffn2_wu", "ffn2_wd", "final_ada_w", "final_ada_b", "final_norm")


def kernel(x, c, ada_w, ada_b, norm_ffn1, ffn1_wg, ffn1_wu, ffn1_wd, norm_mix, w_in, conv_w, conv_b, dt_bias, a_log, d_skip, ret_gn, ssm_norm, w_out, norm_ffn2, ffn2_wg, ffn2_wu, ffn2_wd, final_ada_w, final_ada_b, final_norm, loss_target, m_ada_w, m_ada_b, m_norm_ffn1, m_ffn1_wg, m_ffn1_wu, m_ffn1_wd, m_norm_mix, m_w_in, m_conv_w, m_conv_b, m_dt_bias, m_a_log, m_d_skip, m_ret_gn, m_ssm_norm, m_w_out, m_norm_ffn2, m_ffn2_wg, m_ffn2_wu, m_ffn2_wd, m_final_ada_w, m_final_ada_b, m_final_norm, v_ada_w, v_ada_b, v_norm_ffn1, v_ffn1_wg, v_ffn1_wu, v_ffn1_wd, v_norm_mix, v_w_in, v_conv_w, v_conv_b, v_dt_bias, v_a_log, v_d_skip, v_ret_gn, v_ssm_norm, v_w_out, v_norm_ffn2, v_ffn2_wg, v_ffn2_wu, v_ffn2_wd, v_final_ada_w, v_final_ada_b, v_final_norm):
    W = dict(ada_w=ada_w, ada_b=ada_b, norm_ffn1=norm_ffn1, ffn1_wg=ffn1_wg, ffn1_wu=ffn1_wu, ffn1_wd=ffn1_wd, norm_mix=norm_mix,
             w_in=w_in, conv_w=conv_w, conv_b=conv_b, dt_bias=dt_bias, a_log=a_log, d_skip=d_skip, ret_gn=ret_gn, ssm_norm=ssm_norm,
             w_out=w_out, norm_ffn2=norm_ffn2, ffn2_wg=ffn2_wg, ffn2_wu=ffn2_wu, ffn2_wd=ffn2_wd, final_ada_w=final_ada_w,
             final_ada_b=final_ada_b, final_norm=final_norm)
    M1 = dict(ada_w=m_ada_w, ada_b=m_ada_b, norm_ffn1=m_norm_ffn1, ffn1_wg=m_ffn1_wg, ffn1_wu=m_ffn1_wu, ffn1_wd=m_ffn1_wd,
              norm_mix=m_norm_mix, w_in=m_w_in, conv_w=m_conv_w, conv_b=m_conv_b, dt_bias=m_dt_bias, a_log=m_a_log, d_skip=m_d_skip,
              ret_gn=m_ret_gn, ssm_norm=m_ssm_norm, w_out=m_w_out, norm_ffn2=m_norm_ffn2, ffn2_wg=m_ffn2_wg, ffn2_wu=m_ffn2_wu,
              ffn2_wd=m_ffn2_wd, final_ada_w=m_final_ada_w, final_ada_b=m_final_ada_b, final_norm=m_final_norm)
    V2 = dict(ada_w=v_ada_w, ada_b=v_ada_b, norm_ffn1=v_norm_ffn1, ffn1_wg=v_ffn1_wg, ffn1_wu=v_ffn1_wu, ffn1_wd=v_ffn1_wd,
              norm_mix=v_norm_mix, w_in=v_w_in, conv_w=v_conv_w, conv_b=v_conv_b, dt_bias=v_dt_bias, a_log=v_a_log, d_skip=v_d_skip,
              ret_gn=v_ret_gn, ssm_norm=v_ssm_norm, w_out=v_w_out, norm_ffn2=v_norm_ffn2, ffn2_wg=v_ffn2_wg, ffn2_wu=v_ffn2_wu,
              ffn2_wd=v_ffn2_wd, final_ada_w=v_final_ada_w, final_ada_b=v_final_ada_b, final_norm=v_final_norm)
    D = D_MODEL
    S = x.shape[1]
    me = 4 * lax.axis_index("x") + 2 * lax.axis_index("y") + lax.axis_index("c")
    n_mod = ada_w.shape[2]
    n_fmod = final_ada_w.shape[1]

    c_all = allgather(jnp.broadcast_to(c, (8, D)), in_vmem=True, name="gather_c")[:, 0, :]
    ada_cols = jnp.concatenate([ada_w[0], ada_w[1], final_ada_w], axis=1)
    ada_bias = jnp.concatenate([lax.dynamic_slice(ada_b, (0, me * n_mod), (DEPTH, n_mod)).reshape(1, -1),
                                lax.dynamic_slice(final_ada_b, (me * n_fmod,), (n_fmod,)).reshape(1, -1)], axis=1)
    mod_sh, cond = ada_mod(jnp.pad(c_all, ((0, 8), (0, 0))), ada_cols, ada_bias, name="ada_mod")
    n_cols = mod_sh.shape[1]
    small_in = jnp.concatenate([mod_sh[:8], jnp.pad(conv_w.reshape(8, LANE), ((0, 0), (0, n_cols - LANE)))], axis=0)
    small_g = allgather(small_in, in_vmem=True, name="gather_mod")
    mod_rows = lax.dynamic_index_in_dim(small_g[:, :8, :], me, axis=1, keepdims=False)
    mod = [mod_rows[:, l * n_mod:(l + 1) * n_mod].reshape(9, D) for l in range(DEPTH)]
    fmod = mod_rows[:, DEPTH * n_mod:].reshape(2, D)
    conv_w_full = small_g[:, 8:, :LANE].reshape(N_DEV, DEPTH, SSM_CONV, LANE).transpose(1, 2, 0, 3).reshape(DEPTH, SSM_CONV, 8 * LANE)

    rows = {n: _piece_rows(n, W[n].shape[1:]) for n in BIG}

    def offsets(names):
        offs, o = {}, 0
        for n in names:
            offs[n] = o
            o += rows[n][1]
        return offs

    pack_of = lambda src, dtype, l, names: jnp.concatenate([_to_piece(n, src[n][l]).astype(dtype) for n in names], axis=0)

    def full_weights(wgath, names):
        offs = offsets(names)
        f = {n: wgath[:, offs[n]:offs[n] + rows[n][0], :].reshape(N_DEV * rows[n][0], D) for n in names}
        if "w_in" in f:
            f["w_in"] = _pad_w_in_t(f["w_in"])
        return f

    full = [full_weights(allgather_two_level(pack_of(W, BF16, 0, FIRST), name="gather_weights0"), FIRST)]

    cos, sin = _rope_tables(S)
    h = x[0]
    target = loss_target[0]
    layer_p = []
    for l in range(DEPTH):
        layer_p.append(dict(norm_ffn1=_row(norm_ffn1[l]), norm_mix=_row(norm_mix[l]), norm_ffn2=_row(norm_ffn2[l]),
                            ret_gn=_row(ret_gn[l]), ssm_norm=_row(ssm_norm[l]), conv_w=conv_w_full[l], conv_b=_row(conv_b[l]),
                            dt_bias=_pair_rows(dt_bias[l]), a_log=_pair_rows(a_log[l]), d_skip=_pair_rows(d_skip[l])))
    mods = [[[_row(mod[l][3 * s + k]) for k in range(3)] for s in range(3)] for l in range(DEPTH)]

    saved = []
    for l in range(DEPTH):
        p, f = layer_p[l], full[l]
        rest = pack_of(W, BF16, 0, REST) if l == 0 else None
        h, s1, gathered = _ffn_fwd(h, p["norm_ffn1"], mods[l][0], f["ffn1_wg"], f["ffn1_wu"], f["ffn1_wd"], f"l{l}_ffn1", gather=rest)
        if rest is not None:
            f.update(full_weights(gathered, REST))
        nxt = pack_of(W, BF16, l + 1, BIG) if l + 1 < DEPTH else None
        h, s2, gathered = _mix_fwd(h, p, mods[l][1], f["w_in"], f["w_out"], cos, sin, f"l{l}_mix", gather=nxt)
        if nxt is not None:
            full.append(full_weights(gathered, BIG))
        h, s3, _ = _ffn_fwd(h, p["norm_ffn2"], mods[l][2], f["ffn2_wg"], f["ffn2_wu"], f["ffn2_wd"], f"l{l}_ffn2")
        saved.append((s1, s2, s3))

    def chip_sums(grads, names, tag):
        def send_piece(n):
            g = _unpad_w_in_t(grads[n]) if n == "w_in" else grads[n]
            r, rp = rows[n]
            return jnp.pad(g.reshape(N_DEV, r, D), ((0, 0), (0, rp - r), (0, 0)))

        spack = jnp.concatenate([send_piece(n) for n in names], axis=1)
        by_core = spack.reshape((N_DEV // 2, 2) + spack.shape[1:])
        from_sibling = sibling_exchange(by_core, name="exchange_sibling" + tag)
        own = lax.dynamic_index_in_dim(by_core, lax.axis_index("c"), axis=1, keepdims=False)
        return add_partials(own, from_sibling, name="add_sibling" + tag)

    dh, fst = final_loss_bwd(h, _row(final_norm), _row(fmod[0]), _row(fmod[1]), target, name="final")
    reduced = []
    small_g_l = [None] * DEPTH
    dmod = [None] * DEPTH
    pending = None
    for l in reversed(range(DEPTH)):
        p, f = layer_p[l], full[l]
        s1, s2, s3 = saved[l]
        dh, g2, gn2, dm2, _ = _ffn_bwd(dh, s3, p["norm_ffn2"], mods[l][2], f["ffn2_wg"], f["ffn2_wu"], f["ffn2_wd"], f"l{l}_ffn2")
        dh, gw_in, gw_out, sm, dm1, exchanged = _mix_bwd(dh, s2, p, mods[l][1], f["w_in"], f["w_out"], cos, sin, f"l{l}_mix",
                                                         exchange=pending)
        if pending is not None:
            reduced.append((l + 1, BIG, exchanged))
        grads = dict(zip(REST, (gw_in, gw_out) + g2))
        early = chip_sums(grads, REST, f"{l}r") if l == 0 else None
        dh, g1, gn1, dm0, exchanged = _ffn_bwd(dh, s1, p["norm_ffn1"], mods[l][0], f["ffn1_wg"], f["ffn1_wu"], f["ffn1_wd"], f"l{l}_ffn1",
                                               exchange=early)
        grads.update(zip(FIRST, g1))
        if l == 0:
            reduced.append((0, REST, exchanged))
            reduced.append((0, FIRST, chip_exchange(chip_sums(grads, FIRST, "0f"), name="exchange_chips0")))
        else:
            pending = chip_sums(grads, BIG, str(l))
        sm["norm_ffn1"], sm["norm_ffn2"] = gn1, gn2
        small_g_l[l] = sm
        dmod[l] = jnp.concatenate(dm0 + dm1 + dm2, axis=0)
    grad_x = dh[None]

    big_res = {}
    for l, names, rp in reduced:
        outs = adamw_parts(rp, *[pack_of(src, F32, l, names) for src in (W, M1, V2)], name=f"adamw_big{l}_{names[0]}")
        offs = offsets(names)
        for n in names:
            big_res[l, n] = [_from_piece(n, o[offs[n]:offs[n] + rows[n][1]], W[n].shape[1:]) for o in outs]
    big_out = [{n: jnp.stack([big_res[l, n][k] for l in range(DEPTH)]) for n in BIG} for k in range(4)]

    stack2 = lambda key: jnp.stack([small_g_l[l][key] for l in range(DEPTH)])
    pieces = [("loss", fst[3, 0:1]), ("ada_b", jnp.stack(dmod)), ("final_ada_b", jnp.concatenate([fst[1], fst[2]])),
              ("norm_ffn1", stack2("norm_ffn1")), ("norm_mix", stack2("norm_mix")), ("norm_ffn2", stack2("norm_ffn2")),
              ("conv_w", stack2("conv_w")), ("conv_b", stack2("conv_b")), ("dt_bias", stack2("dt_bias")), ("a_log", stack2("a_log")),
              ("d_skip", stack2("d_skip")), ("ret_gn", stack2("ret_gn")), ("ssm_norm", stack2("ssm_norm")), ("final_norm", fst[0])]
    names = [n for n, _ in pieces]
    shapes = [a.shape for _, a in pieces]
    ppack, poffs = _pack([a for _, a in pieces])
    pg = allgather(ppack, in_vmem=True, name="gather_small")
    zero_like = lambda n, a: jnp.zeros(a.shape, F32)
    spacks = [_pack([(src[n] if n in SMALL else zero_like(n, a)) for n, a in pieces])[0] for src in (W, M1, V2)]
    souts = adamw_parts(pg, *spacks, name="adamw_small")
    small_out = [dict(zip(names, _unpack(o, poffs, shapes))) for o in souts]
    loss = small_out[0]["loss"][0]

    gathered = dict(zip(names, _unpack(pg, poffs, shapes)))
    conv_parts = lax.dynamic_slice_in_dim(gathered["conv_w"], me * LANE, LANE, axis=3).reshape(N_DEV, DEPTH * SSM_CONV, LANE)
    conv_out = [o.reshape(conv_w.shape) for o in adamw_parts(conv_parts, conv_w.reshape(-1, LANE), m_conv_w.reshape(-1, LANE),
                                                              v_conv_w.reshape(-1, LANE), name="adamw_conv_w")]
    cond_t = cond[:8].T
    ada_out = []
    for l in range(DEPTH):
        dsel = lax.dynamic_slice_in_dim(gathered["ada_b"][:, l, :], me * n_mod, n_mod, axis=1)
        ada_out.append(ada_adamw(cond_t, dsel, ada_w[l], m_ada_w[l], v_ada_w[l], name=f"adamw_ada_w{l}"))
    ada_out = [jnp.stack([ada_out[l][k] for l in range(DEPTH)]) for k in range(4)]
    fsel = lax.dynamic_slice_in_dim(gathered["final_ada_b"].reshape(N_DEV, 2 * D), me * n_fmod, n_fmod, axis=1)
    fada_out = ada_adamw(cond_t, fsel, final_ada_w, m_final_ada_w, v_final_ada_w, name="adamw_final_ada_w")

    def pick(k, n):
        if n in BIG:
            return big_out[k][n]
        if n == "ada_w":
            return ada_out[k]
        if n == "final_ada_w":
            return fada_out[k]
        if n == "conv_w":
            return conv_out[k]
        return small_out[k][n]

    return (loss, grad_x) + tuple(pick(k, n) for k in range(4) for n in NAMES)
```

```python
import functools
import math

import numpy as np
import jax
import jax.numpy as jnp
from jax import lax
from jax.experimental import pallas as pl
from jax.experimental.pallas import tpu as pltpu

F32 = jnp.float32
BF16 = jnp.bfloat16

D_MODEL = 1024
DEPTH = 2
RET_HEADS = 4
HEAD_DIM = 128
SSM_HEADS = 8
SSM_HEAD_DIM = 64
SSM_STATE = 128
SSM_CONV = 4
D_FF = 2816
ROPE_BASE = 10000.0
NORM_EPS = 1e-6
MIX_W = 1536
IN_W = 5128
IN_MAIN = 5120
IN_PAD = 5632
N_DEV = 8
LANE = 128

ADAM_LR = 0.001
ADAM_B1 = 0.9
ADAM_B2 = 0.999
ADAM_EPS = 1e-08
ADAM_WD = 0.01
ADAM_STEP = 10

TOKEN_TILE = 512
WGRAD_TOKENS = 2048
BIG_TOKEN_TILE = 2048
SEQ_BLOCK = 256
VMEM_LIMIT = 56 << 20
SB_SKIP = 120.0
SB_UNVISITED = -1e30

CB_RQ, CB_RK, CB_RV, CB_RG = 0, 4, 8, 12
CB_SQ, CB_SK, CB_SV = 16, 20, 24
CB_MZ, CB_XS, CB_BM, CB_CM, CB_DT = 28, 32, 36, 38, 40


def _cparams(*sem):
    return pltpu.CompilerParams(dimension_semantics=sem, vmem_limit_bytes=VMEM_LIMIT)


def _sds(shape, dtype):
    return jax.ShapeDtypeStruct(tuple(shape), dtype)


def _tile(n, *prefs):
    for p in prefs:
        if n % p == 0:
            return p
    return n


def _dot(a, b, dims):
    return lax.dot_general(a, b, (dims, ((), ())), preferred_element_type=F32)


NN = ((1,), (0,))
NT = ((1,), (1,))
TN = ((0,), (0,))


def _bf(x):
    return x.astype(BF16)


def _sigmoid(x):
    return jax.nn.sigmoid(x)


def _split2(x):
    hi = x.astype(BF16)
    lo = (x - hi.astype(F32)).astype(BF16)
    return hi, lo


def _split3(x):
    hi = x.astype(BF16)
    r = x - hi.astype(F32)
    mid = r.astype(BF16)
    lo = (r - mid.astype(F32)).astype(BF16)
    return hi, mid, lo


def matmul(a, b, *, ta=False, tb=False, tm=512, tn=512, tk=512, out_dtype=F32, name):
    M, K = (a.shape[1], a.shape[0]) if ta else a.shape
    N = b.shape[0] if tb else b.shape[1]
    tm, tn, tk = min(tm, M), min(tn, N), min(tk, K)
    assert M % tm == 0 and N % tn == 0 and K % tk == 0, (name, M, N, K, tm, tn, tk)
    nk = K // tk
    a_spec = pl.BlockSpec((tk, tm), lambda i, j, k: (k, i)) if ta else pl.BlockSpec((tm, tk), lambda i, j, k: (i, k))
    b_spec = pl.BlockSpec((tn, tk), lambda i, j, k: (j, k)) if tb else pl.BlockSpec((tk, tn), lambda i, j, k: (k, j))
    dims = ((0 if ta else 1,), (1 if tb else 0,))

    def body(a_ref, b_ref, o_ref, acc_ref):
        k = pl.program_id(2)
        p = _dot(_bf(a_ref[...]), _bf(b_ref[...]), dims)

        @pl.when(k == 0)
        def _():
            acc_ref[...] = p

        @pl.when(k > 0)
        def _():
            acc_ref[...] += p

        @pl.when(k == nk - 1)
        def _():
            o_ref[...] = acc_ref[...].astype(out_dtype)

    return pl.pallas_call(
        body, grid=(M // tm, N // tn, nk), in_specs=[a_spec, b_spec],
        out_specs=pl.BlockSpec((tm, tn), lambda i, j, k: (i, j)), out_shape=_sds((M, N), out_dtype),
        scratch_shapes=[pltpu.VMEM((tm, tn), F32)], name=name,
        compiler_params=_cparams("parallel", "parallel", "arbitrary"))(a, b)


def matmul_resid(a, w, h, gate, factor, *, name):
    M, K = a.shape
    N = w.shape[1]
    tm = min(TOKEN_TILE, M)

    def body(a_ref, w_ref, h_ref, g_ref, hn_ref, o_ref):
        out = _dot(a_ref[...], w_ref[...], NN)
        o_ref[...] = out
        hn_ref[...] = h_ref[...] + (factor * (1.0 + g_ref[...])) * out

    mn = pl.BlockSpec((tm, N), lambda i: (i, 0))
    return pl.pallas_call(
        body, grid=(M // tm,),
        in_specs=[pl.BlockSpec((tm, K), lambda i: (i, 0)), pl.BlockSpec((K, N), lambda i: (0, 0)), mn,
                  pl.BlockSpec((1, N), lambda i: (0, 0))],
        out_specs=[mn, mn], out_shape=[_sds((M, N), F32), _sds((M, N), F32)], name=name,
        compiler_params=_cparams("parallel"))(a, w, h, gate)


def ffn_up(u, wg_t, wu_t, *, name, gather=None):
    M, K = u.shape
    N = wg_t.shape[0]
    tm, tn = min(BIG_TOKEN_TILE, M), _tile(N, 256)
    grid = (M // tm, N // tn)
    extra = gather is not None

    def body(*refs):
        if extra:
            u_ref, wg_ref, wu_ref, src_ref, a_ref, b_ref, act_ref, dst_ref, send_sems, recv_sems, local_sem = refs
            finish_comm = _carried(_gather_two_level, (src_ref, dst_ref, send_sems, recv_sems, local_sem), *grid)
        else:
            u_ref, wg_ref, wu_ref, a_ref, b_ref, act_ref = refs
            finish_comm = lambda: None
        uu = u_ref[...]
        a = _dot(uu, wg_ref[...], NT)
        b = _dot(uu, wu_ref[...], NT)
        a_ref[...] = _bf(a)
        b_ref[...] = _bf(b)
        act_ref[...] = _bf(a * _sigmoid(a) * b)
        finish_comm()

    mn = pl.BlockSpec((tm, tn), lambda i, j: (i, j))
    wspec = pl.BlockSpec((tn, K), lambda i, j: (j, 0))
    hbm = pl.BlockSpec(memory_space=pltpu.HBM)
    return pl.pallas_call(
        body, grid=grid, in_specs=[pl.BlockSpec((tm, K), lambda i, j: (i, 0)), wspec, wspec] + [hbm] * extra,
        out_specs=[mn, mn, mn] + [hbm] * extra,
        out_shape=[_sds((M, N), BF16)] * 3 + ([_sds((N_DEV,) + gather.shape, gather.dtype)] if extra else []),
        scratch_shapes=COMM_SCRATCH * extra, name=name,
        compiler_params=_cparams("arbitrary", "arbitrary") if extra else _cparams("parallel", "parallel"))(
            *((u, wg_t, wu_t) + ((gather,) if extra else ())))


def ffn_dact(dout, wd, a, b, *, name):
    M, K = dout.shape
    N = wd.shape[0]
    tm, tn = min(BIG_TOKEN_TILE, M), _tile(N, 256)

    def body(d_ref, w_ref, a_ref, b_ref, da_ref, db_ref):
        dact = _dot(d_ref[...], w_ref[...], NT)
        av = a_ref[...].astype(F32)
        sg = _sigmoid(av)
        db_ref[...] = _bf(dact * av * sg)
        da_ref[...] = _bf(dact * b_ref[...].astype(F32) * (sg * (1.0 + av * (1.0 - sg))))

    mn = pl.BlockSpec((tm, tn), lambda i, j: (i, j))
    return pl.pallas_call(
        body, grid=(M // tm, N // tn),
        in_specs=[pl.BlockSpec((tm, K), lambda i, j: (i, 0)), pl.BlockSpec((tn, K), lambda i, j: (j, 0)), mn, mn],
        out_specs=[mn, mn], out_shape=[_sds((M, N), BF16), _sds((M, N), BF16)], name=name,
        compiler_params=_cparams("parallel", "parallel"))(dout, wd, a, b)


def norm_mod(h, gain, shift, scale, *, name):
    S, D = h.shape
    tm = min(TOKEN_TILE, S)

    def body(h_ref, g_ref, sh_ref, sc_ref, u_ref):
        x = h_ref[...]
        r = lax.rsqrt(jnp.mean(x * x, axis=-1, keepdims=True) + NORM_EPS)
        n = x * r * g_ref[...]
        u_ref[...] = _bf(n * (1.0 + sc_ref[...]) + sh_ref[...])

    row = pl.BlockSpec((1, D), lambda i: (0, 0))
    tile = pl.BlockSpec((tm, D), lambda i: (i, 0))
    return pl.pallas_call(body, grid=(S // tm,), in_specs=[tile, row, row, row], out_specs=tile,
                          out_shape=_sds((S, D), BF16), name=name, compiler_params=_cparams("parallel"))(h, gain, shift, scale)


def dgrad_norm_bwd(lhs, ws, spans, h, gain, scale, dres, *, name, exchange=None):
    S, D = h.shape
    tm = min(TOKEN_TILE, S)
    n, nw = len(lhs), len(ws)
    extra = exchange is not None

    def body(*refs):
        l_refs, w_refs = refs[:n], refs[n:n + nw]
        if extra:
            h_ref, g_ref, sc_ref, dres_ref, src_ref, dh_ref, st_ref, dst_ref, send_sems, recv_sems, local_sem = refs[n + nw:]
            finish_comm = _carried(_chip_exchange, (src_ref, dst_ref, send_sems, recv_sems, local_sem), S // tm)
        else:
            h_ref, g_ref, sc_ref, dres_ref, dh_ref, st_ref = refs[n + nw:]
            finish_comm = lambda: None
        du = None
        for lr, (k, r0) in zip(l_refs, spans):
            part = _dot(_bf(lr[...]), w_refs[k][r0:r0 + lr.shape[1], :], NN)
            du = part if du is None else du + part
        x = h_ref[...]
        g = g_ref[...]
        r = lax.rsqrt(jnp.mean(x * x, axis=-1, keepdims=True) + NORM_EPS)
        xhat = x * r
        dn = du * (1.0 + sc_ref[...])
        dxhat = dn * g
        dh_ref[...] = dres_ref[...] + r * (dxhat - xhat * jnp.mean(dxhat * xhat, axis=-1, keepdims=True))

        @pl.when(pl.program_id(0) == 0)
        def _():
            st_ref[...] = jnp.zeros_like(st_ref)

        st_ref[0:1, :] += jnp.sum(dn * xhat, axis=0, keepdims=True)
        st_ref[1:2, :] += jnp.sum(du, axis=0, keepdims=True)
        st_ref[2:3, :] += jnp.sum(du * (xhat * g), axis=0, keepdims=True)
        finish_comm()

    row = pl.BlockSpec((1, D), lambda i: (0, 0))
    tile = pl.BlockSpec((tm, D), lambda i: (i, 0))
    hbm = pl.BlockSpec(memory_space=pltpu.HBM)
    in_specs = [pl.BlockSpec((tm, l.shape[1]), lambda i: (i, 0)) for l in lhs]
    in_specs += [pl.BlockSpec(w.shape, lambda i: (0, 0)) for w in ws]
    in_specs += [tile, row, row, tile] + [hbm] * extra
    return pl.pallas_call(
        body, grid=(S // tm,), in_specs=in_specs, out_specs=[tile, pl.BlockSpec((8, D), lambda i: (0, 0))] + [hbm] * extra,
        out_shape=[_sds((S, D), F32), _sds((8, D), F32)] + ([_sds(exchange.shape, exchange.dtype)] if extra else []),
        scratch_shapes=COMM_SCRATCH * extra, name=name,
        compiler_params=_cparams("arbitrary"))(*lhs, *ws, h, gain, scale, dres, *((exchange,) if extra else ()))


def gate_bwd(dh, out, gate, factor, *, name):
    S, D = dh.shape
    tm = min(TOKEN_TILE, S)

    def body(dh_ref, o_ref, g_ref, do_ref, st_ref):
        d = dh_ref[...]
        do_ref[...] = _bf(d * (factor * (1.0 + g_ref[...])))

        @pl.when(pl.program_id(0) == 0)
        def _():
            st_ref[...] = jnp.zeros_like(st_ref)

        st_ref[0:1, :] += factor * jnp.sum(d * o_ref[...], axis=0, keepdims=True)

    tile = pl.BlockSpec((tm, D), lambda i: (i, 0))
    return pl.pallas_call(
        body, grid=(S // tm,), in_specs=[tile, tile, pl.BlockSpec((1, D), lambda i: (0, 0))],
        out_specs=[tile, pl.BlockSpec((8, D), lambda i: (0, 0))], out_shape=[_sds((S, D), BF16), _sds((8, D), F32)],
        name=name, compiler_params=_cparams("arbitrary"))(dh, out, gate)


def final_loss_bwd(h, gain, shift, scale, target, *, name):
    S, D = h.shape
    tm = min(TOKEN_TILE, S)

    def body(h_ref, g_ref, sh_ref, sc_ref, t_ref, dh_ref, st_ref):
        x = h_ref[...]
        g = g_ref[...]
        r = lax.rsqrt(jnp.mean(x * x, axis=-1, keepdims=True) + NORM_EPS)
        xhat = x * r
        n = xhat * g
        err = n * (1.0 + sc_ref[...]) + sh_ref[...] - t_ref[...]
        dy = err * (1.0 / D)
        dn = dy * (1.0 + sc_ref[...])
        dxhat = dn * g
        dh_ref[...] = r * (dxhat - xhat * jnp.mean(dxhat * xhat, axis=-1, keepdims=True))

        @pl.when(pl.program_id(0) == 0)
        def _():
            st_ref[...] = jnp.zeros_like(st_ref)

        st_ref[0:1, :] += jnp.sum(dn * xhat, axis=0, keepdims=True)
        st_ref[1:2, :] += jnp.sum(dy, axis=0, keepdims=True)
        st_ref[2:3, :] += jnp.sum(dy * n, axis=0, keepdims=True)
        tok = jnp.mean(err * err, axis=-1, keepdims=True)
        st_ref[3:4, :] += 0.5 * jnp.sum(tok, axis=0, keepdims=True)

    row = pl.BlockSpec((1, D), lambda i: (0, 0))
    tile = pl.BlockSpec((tm, D), lambda i: (i, 0))
    return pl.pallas_call(
        body, grid=(S // tm,), in_specs=[tile, row, row, row, tile], out_specs=[tile, pl.BlockSpec((8, D), lambda i: (0, 0))],
        out_shape=[_sds((S, D), F32), _sds((8, D), F32)], name=name,
        compiler_params=_cparams("arbitrary"))(h, gain, shift, scale, target)


def _ret_tables(T):
    heads = np.arange(RET_HEADS, dtype=np.float64)
    lg = np.log1p(-(2.0 ** (-5.0 - heads)))
    t = np.arange(T)
    same = (t[:, None] // 64) == (t[None, :] // 64)
    earlier = (t[None, :] // 64) < (t[:, None] // 64)
    dist = np.abs(t[:, None] - t[None, :]).astype(np.float64)
    dmat = np.where(same | earlier, np.exp(lg[:, None, None] * dist[None]), 0.0)
    qdec = np.exp(lg[:, None] * (t + 1.0)[None, :])
    kdec = np.exp(lg[:, None] * (T - 1.0 - t)[None, :])
    cdec = np.exp(lg * T)
    bc = lambda v: jnp.asarray(np.broadcast_to(v[:, :, None], (RET_HEADS, T, LANE)), F32)
    cd = jnp.asarray(np.broadcast_to(cdec[:, None, None], (RET_HEADS, LANE, LANE)), F32)
    return jnp.asarray(dmat, F32), bc(qdec), bc(kdec), cd


def _rope_tables(S):
    half = HEAD_DIM // 2
    inv_freq = ROPE_BASE ** (-jnp.arange(half, dtype=F32) / half)
    ang = jnp.arange(S, dtype=F32)[:, None] * inv_freq[None, :]
    cos, sin = jnp.cos(ang), jnp.sin(ang)
    return jnp.concatenate([cos, cos], axis=-1), jnp.concatenate([-sin, sin], axis=-1)


def _rope(x, c, s):
    return x * c + pltpu.roll(x, HEAD_DIM // 2, 1) * s


def _rope_t(dx, c, s):
    return dx * c + pltpu.roll(dx * s, HEAD_DIM // 2, 1)


def ret_fwd(proj, gn, cos, sin, *, name):
    S = proj.shape[0]
    T = min(SEQ_BLOCK, S)
    nb = S // T
    dmat, qdec, kdec, cdec = _ret_tables(T)

    def body(q_ref, k_ref, v_ref, g_ref, c_ref, s_ref, dm_ref, qd_ref, kd_ref, cd_ref, gn_ref, yo_ref, yp_ref, st_ref, state):
        @pl.when(pl.program_id(0) == 0)
        def _():
            state[...] = jnp.zeros_like(state)

        c, s = c_ref[...], s_ref[...]
        for h in range(RET_HEADS):
            cols = slice(LANE * h, LANE * (h + 1))
            qr = _rope(q_ref[:, cols], c, s)
            kr = _rope(k_ref[:, cols], c, s) * (HEAD_DIM ** -0.5)
            v = _bf(v_ref[:, cols])
            sp = state[h]
            st_ref[h] = sp
            a = _dot(_bf(qr), _bf(kr), NT) * dm_ref[h]
            y = _dot(_bf(a), v, NN) + _dot(_bf(qr * qd_ref[h]), _bf(sp), NN)
            state[h] = cd_ref[h] * sp + _dot(_bf(kr * kd_ref[h]), v, TN)
            yp_ref[:, cols] = y
            yn = y * lax.rsqrt(jnp.mean(y * y, axis=-1, keepdims=True) + NORM_EPS) * gn_ref[:, cols]
            g = g_ref[:, cols]
            yo_ref[:, cols] = _bf(yn * (g * _sigmoid(g)))

    col = lambda cb: pl.BlockSpec((T, 512), lambda b: (b, cb // 4))
    tok = pl.BlockSpec((T, LANE), lambda b: (b, 0))
    const = lambda shape: pl.BlockSpec(shape, lambda b: (0,) * len(shape))
    out_tok = pl.BlockSpec((T, 512), lambda b: (b, 0))
    return pl.pallas_call(
        body, grid=(nb,),
        in_specs=[col(CB_RQ), col(CB_RK), col(CB_RV), col(CB_RG), tok, tok, const(dmat.shape), const(qdec.shape), const(kdec.shape),
                  const(cdec.shape), const((1, 512))],
        out_specs=[out_tok, out_tok, pl.BlockSpec((None, RET_HEADS, LANE, LANE), lambda b: (b, 0, 0, 0))],
        out_shape=[_sds((S, 512), BF16), _sds((S, 512), F32), _sds((nb, RET_HEADS, LANE, LANE), F32)],
        scratch_shapes=[pltpu.VMEM((RET_HEADS, LANE, LANE), F32)], name=name,
        compiler_params=_cparams("arbitrary"))(proj, proj, proj, proj, cos, sin, dmat, qdec, kdec, cdec, gn)


def ret_bwd(proj, gn, cos, sin, ypre, states, dycat, *, name):
    S = proj.shape[0]
    T = min(SEQ_BLOCK, S)
    nb = S // T
    dmat, qdec, kdec, cdec = _ret_tables(T)

    def body(q_ref, k_ref, v_ref, g_ref, c_ref, s_ref, dm_ref, qd_ref, kd_ref, cd_ref, gn_ref, yp_ref, st_ref, dy_ref,
             d_ref, stat_ref, gstate):
        @pl.when(pl.program_id(0) == 0)
        def _():
            gstate[...] = jnp.zeros_like(gstate)
            stat_ref[...] = jnp.zeros_like(stat_ref)

        c, s = c_ref[...], s_ref[...]
        scale = HEAD_DIM ** -0.5
        for h in range(RET_HEADS):
            cols = slice(LANE * h, LANE * (h + 1))
            qr = _rope(q_ref[:, cols], c, s)
            kr = _rope(k_ref[:, cols], c, s) * scale
            v = _bf(v_ref[:, cols])
            qd, kd, dm = qd_ref[h], kd_ref[h], dm_ref[h]
            sp = _bf(st_ref[h])
            gs = gstate[h]
            gsb = _bf(gs)
            g = g_ref[:, cols]
            sg = _sigmoid(g)
            y = yp_ref[:, cols]
            gn_row = gn_ref[:, cols]
            r = lax.rsqrt(jnp.mean(y * y, axis=-1, keepdims=True) + NORM_EPS)
            yhat = y * r
            dyo = dy_ref[:, cols]
            d_ref[:, 1536 + LANE * h:1536 + LANE * (h + 1)] = _bf(dyo * (yhat * gn_row) * (sg * (1.0 + g * (1.0 - sg))))
            dyn = dyo * (g * sg)
            stat_ref[0:1, cols] += jnp.sum(dyn * yhat, axis=0, keepdims=True)
            dyhat = dyn * gn_row
            dy = _bf(r * (dyhat - yhat * jnp.mean(dyhat * yhat, axis=-1, keepdims=True)))
            qrb, krb = _bf(qr), _bf(kr)
            qdb = _bf(qr * qd)
            kdb = _bf(kr * kd)
            a = _bf(_dot(qrb, krb, NT) * dm)
            da = _bf(_dot(dy, v, NT) * dm)
            d_ref[:, 1024 + LANE * h:1024 + LANE * (h + 1)] = _bf(_dot(a, dy, TN) + _dot(kdb, gsb, NN))
            dqr = _dot(da, krb, NN) + qd * _dot(dy, sp, NT)
            dkr = _dot(da, qrb, TN) + kd * _dot(v, gsb, NT)
            gstate[h] = cd_ref[h] * gs + _dot(qdb, dy, TN)
            d_ref[:, cols] = _bf(_rope_t(dqr, c, s))
            d_ref[:, 512 + LANE * h:512 + LANE * (h + 1)] = _bf(_rope_t(dkr * scale, c, s))

    rb = lambda b: nb - 1 - b
    col = lambda cb: pl.BlockSpec((T, 512), lambda b: (rb(b), cb // 4))
    tok = pl.BlockSpec((T, LANE), lambda b: (rb(b), 0))
    const = lambda shape: pl.BlockSpec(shape, lambda b: (0,) * len(shape))
    tok512 = pl.BlockSpec((T, 512), lambda b: (rb(b), 0))
    return pl.pallas_call(
        body, grid=(nb,),
        in_specs=[col(CB_RQ), col(CB_RK), col(CB_RV), col(CB_RG), tok, tok, const(dmat.shape), const(qdec.shape), const(kdec.shape),
                  const(cdec.shape), const((1, 512)), tok512,
                  pl.BlockSpec((None, RET_HEADS, LANE, LANE), lambda b: (rb(b), 0, 0, 0)), tok512],
        out_specs=[pl.BlockSpec((T, 2048), lambda b: (rb(b), 0)), const((8, 512))],
        out_shape=[_sds((S, 2048), BF16), _sds((8, 512), F32)],
        scratch_shapes=[pltpu.VMEM((RET_HEADS, LANE, LANE), F32)], name=name,
        compiler_params=_cparams("arbitrary"))(proj, proj, proj, proj, cos, sin, dmat, qdec, kdec, cdec, gn, ypre, states, dycat)


def _sb_cast_kv(k_ref, v_ref, kb, vb, S):
    step = min(TOKEN_TILE, S)
    for r in range(0, S, step):
        kb[r:r + step, :] = _bf(k_ref[r:r + step, :])
        vb[r:r + step, :] = _bf(v_ref[r:r + step, :])


def _sb_logits(q, kblk, vis):
    z = _dot(q, kblk, NT) * (HEAD_DIM ** -0.5)
    l = jnp.log1p(jnp.exp(-jnp.abs(z)))
    lb = jnp.minimum(z, 0.0) - l
    lk = jnp.minimum(-z, 0.0) - l
    if vis is not None:
        lk = jnp.where(vis, lk, 0.0)
    return lb, lk


def _tri(T, cmp):
    r = lax.broadcasted_iota(jnp.int32, (T, T), 0)
    c = lax.broadcasted_iota(jnp.int32, (T, T), 1)
    return cmp(r, c)


def _dot_split2(x, m):
    hi, lo = _split2(x)
    return _dot(hi, m, NN) + _dot(lo, m, NN)


def _carried(carry_fn, refs, *grid):
    ids = [pl.program_id(d) for d in range(len(grid))]
    first, last = ids[0] == 0, ids[0] == grid[0] - 1
    for d in range(1, len(grid)):
        first, last = first & (ids[d] == 0), last & (ids[d] == grid[d] - 1)

    @pl.when(first)
    def _():
        carry_fn(*refs, start=True, finish=False)

    def finish():
        @pl.when(last)
        def _():
            carry_fn(*refs, start=False, finish=True)

    return finish


def sb_fwd(proj, *, name, gather=None):
    S = proj.shape[0]
    T = min(SEQ_BLOCK, S)
    nq = S // T

    assert nq <= LANE

    def body(*refs):
        if gather is None:
            q_ref, k_ref, v_ref, o_ref, cin_ref, kb, vb = refs
            finish_comm = lambda: None
        else:
            q_ref, k_ref, v_ref, src_ref, o_ref, cin_ref, dst_ref, kb, vb, send_sems, recv_sems, local_sem = refs
            finish_comm = _carried(_gather_two_level, (src_ref, dst_ref, send_sems, recv_sems, local_sem), RET_HEADS, nq)
        qi = pl.program_id(1)

        @pl.when(qi == 0)
        def _():
            _sb_cast_kv(k_ref, v_ref, kb, vb, S)

        q = _bf(q_ref[...])
        vis = _tri(T, lambda t, s: s < t)
        after = _tri(T, lambda j, s: j > s).astype(BF16)
        lane = lax.broadcasted_iota(jnp.int32, (T, LANE), 1)

        def block(jb, carry, acc, cin, mask):
            rows = pl.ds(pl.multiple_of(jb * T, T), T)
            lb, lk = _sb_logits(q, kb[rows, :], mask)
            tail = _dot_split2(lk, after) + carry
            w = jnp.exp(lb + tail)
            if mask is not None:
                w = jnp.where(mask, w, 0.0)
            return (carry + jnp.sum(lk, axis=1, keepdims=True), acc + _dot(_bf(w), vb[rows, :], NN),
                    jnp.where(lane == jb, carry, cin))

        st = block(qi, jnp.zeros((T, 1), F32), jnp.zeros((T, LANE), F32), jnp.full((T, LANE), SB_UNVISITED, F32), vis)

        def more(c):
            return (c[0] < qi) & (jnp.max(c[1]) > -SB_SKIP)

        def step(c):
            return (c[0] + 1,) + block(qi - 1 - c[0], c[1], c[2], c[3], None)

        st = lax.while_loop(more, step, (jnp.int32(0),) + st)
        o_ref[...] = st[2]
        cin_ref[...] = st[3]
        finish_comm()

    whole = lambda cb: pl.BlockSpec((S, LANE), lambda h, i: (0, cb + h))
    tok = pl.BlockSpec((T, LANE), lambda h, i: (i, h))
    hbm = pl.BlockSpec(memory_space=pltpu.HBM)
    extra = gather is not None
    return pl.pallas_call(
        body, grid=(RET_HEADS, nq),
        in_specs=[pl.BlockSpec((T, LANE), lambda h, i: (i, CB_SQ + h)), whole(CB_SK), whole(CB_SV)] + [hbm] * extra,
        out_specs=[tok, tok] + [hbm] * extra,
        out_shape=[_sds((S, 512), F32), _sds((S, 512), F32)] + ([_sds((N_DEV,) + gather.shape, gather.dtype)] if extra else []),
        scratch_shapes=[pltpu.VMEM((S, LANE), BF16), pltpu.VMEM((S, LANE), BF16)] + COMM_SCRATCH * extra, name=name,
        compiler_params=_cparams("arbitrary", "arbitrary"))(*((proj, proj, proj) + ((gather,) if extra else ())))


def sb_bwd(proj, cin, dycat, *, name, exchange=None):
    S = proj.shape[0]
    T = min(SEQ_BLOCK, S)
    nq = S // T
    scale = HEAD_DIM ** -0.5

    def body(*refs):
        if exchange is None:
            q_ref, k_ref, v_ref, cin_ref, do_ref, dq_ref, dk_ref, dv_ref, kb, vb = refs
            finish_comm = lambda: None
        else:
            (q_ref, k_ref, v_ref, cin_ref, do_ref, src_ref, dq_ref, dk_ref, dv_ref, dst_ref, kb, vb,
             send_sems, recv_sems, local_sem) = refs
            finish_comm = _carried(_chip_exchange, (src_ref, dst_ref, send_sems, recv_sems, local_sem), RET_HEADS, nq)
        qi = pl.program_id(1)

        @pl.when(qi == 0)
        def _():
            _sb_cast_kv(k_ref, v_ref, kb, vb, S)
            dk_ref[...] = jnp.zeros_like(dk_ref)
            dv_ref[...] = jnp.zeros_like(dv_ref)

        q = _bf(q_ref[...])
        dob = _bf(do_ref[...])
        cin = cin_ref[...]
        vis = _tri(T, lambda t, s: s < t)
        after = _tri(T, lambda j, s: j > s).astype(BF16)
        before = _tri(T, lambda s, j: s < j).astype(BF16)
        lane = lax.broadcasted_iota(jnp.int32, (T, LANE), 1)

        def block(jb, ecarry, dq, mask):
            rows = pl.ds(pl.multiple_of(jb * T, T), T)
            kblk, vblk = kb[rows, :], vb[rows, :]
            lb, lk = _sb_logits(q, kblk, mask)
            carry = jnp.sum(jnp.where(lane == jb, cin, 0.0), axis=1, keepdims=True)
            w = jnp.exp(lb + _dot_split2(lk, after) + carry)
            if mask is not None:
                w = jnp.where(mask, w, 0.0)
            e = w * _dot(dob, vblk, NT)
            dv_ref[rows, :] += _dot(_bf(w), dob, TN)
            dlk = _dot_split2(e, before) + ecarry
            beta = jnp.exp(lb)
            dz = e * (1.0 - beta) - beta * dlk
            if mask is not None:
                dz = jnp.where(mask, dz, 0.0)
            dzb = _bf(dz * scale)
            dk_ref[rows, :] += _dot(dzb, q, TN)
            return ecarry + jnp.sum(e, axis=1, keepdims=True), dq + _dot(dzb, kblk, NN)

        lane1 = lane[0:1, :]
        skipped = (jnp.max(cin, axis=0, keepdims=True) <= -SB_SKIP) & (lane1 < qi)
        first = jnp.sum(jnp.where(skipped, 1, 0))
        st = lax.fori_loop(first, qi, lambda jb, c: block(jb, c[0], c[1], None), (jnp.zeros((T, 1), F32), jnp.zeros((T, LANE), F32)))
        st = block(qi, st[0], st[1], vis)
        dq_ref[...] = _bf(st[1])
        finish_comm()

    whole = lambda cb: pl.BlockSpec((S, LANE), lambda h, i: (0, cb + h))
    tok = pl.BlockSpec((T, LANE), lambda h, i: (i, h))
    acc = pl.BlockSpec((S, LANE), lambda h, i: (0, h))
    hbm = pl.BlockSpec(memory_space=pltpu.HBM)
    extra = exchange is not None
    return pl.pallas_call(
        body, grid=(RET_HEADS, nq),
        in_specs=[pl.BlockSpec((T, LANE), lambda h, i: (i, CB_SQ + h)), whole(CB_SK), whole(CB_SV), tok,
                  pl.BlockSpec((T, LANE), lambda h, i: (i, 4 + h))] + [hbm] * extra,
        out_specs=[tok, acc, acc] + [hbm] * extra,
        out_shape=[_sds((S, 512), BF16), _sds((S, 512), F32), _sds((S, 512), F32)] + ([_sds(exchange.shape, exchange.dtype)] if extra else []),
        scratch_shapes=[pltpu.VMEM((S, LANE), BF16), pltpu.VMEM((S, LANE), BF16)] + COMM_SCRATCH * extra, name=name,
        compiler_params=_cparams("arbitrary", "arbitrary"))(*((proj, proj, proj, cin, dycat) + ((exchange,) if extra else ())))


def _shift_down(x, d, row):
    return jnp.where(row >= d, pltpu.roll(x, d, 0), 0.0)


def _shift_up(x, d, row, S):
    return jnp.where(row < S - d, pltpu.roll(x, S - d, 0), 0.0)


def conv_fwd(proj, conv_w, conv_b, *, name):
    S = proj.shape[0]

    def body(x_ref, w_ref, b_ref, pre_ref, act_ref):
        x = x_ref[...]
        row = lax.broadcasted_iota(jnp.int32, x.shape, 0)
        pre = b_ref[...] + w_ref[3:4, :] * x
        for d in range(1, SSM_CONV):
            pre = pre + w_ref[3 - d:4 - d, :] * _shift_down(x, d, row)
        pre_ref[...] = pre
        act_ref[...] = pre * _sigmoid(pre)

    blk = pl.BlockSpec((S, LANE), lambda c: (0, c))
    return pl.pallas_call(
        body, grid=(8,),
        in_specs=[pl.BlockSpec((S, LANE), lambda c: (0, CB_XS + c)), pl.BlockSpec((SSM_CONV, LANE), lambda c: (0, c)),
                  pl.BlockSpec((1, LANE), lambda c: (0, c))],
        out_specs=[blk, blk], out_shape=[_sds((S, 1024), F32), _sds((S, 1024), F32)], name=name,
        compiler_params=_cparams("parallel"))(proj, conv_w, conv_b)


def conv_bwd(proj, pre, dact, conv_w, *, name):
    S = proj.shape[0]

    def body(x_ref, pre_ref, da_ref, w_ref, dx_ref, st_ref):
        x = x_ref[...]
        p = pre_ref[...]
        row = lax.broadcasted_iota(jnp.int32, x.shape, 0)
        sg = _sigmoid(p)
        dpre = da_ref[...] * (sg * (1.0 + p * (1.0 - sg)))
        dx = w_ref[3:4, :] * dpre
        st_ref[3:4, :] = jnp.sum(dpre * x, axis=0, keepdims=True)
        for d in range(1, SSM_CONV):
            dx = dx + w_ref[3 - d:4 - d, :] * _shift_up(dpre, d, row, S)
            st_ref[3 - d:4 - d, :] = jnp.sum(dpre * _shift_down(x, d, row), axis=0, keepdims=True)
        st_ref[4:5, :] = jnp.sum(dpre, axis=0, keepdims=True)
        st_ref[5:8, :] = jnp.zeros((3, LANE), F32)
        dx_ref[...] = _bf(dx)

    blk = pl.BlockSpec((S, LANE), lambda c: (0, c))
    return pl.pallas_call(
        body, grid=(8,),
        in_specs=[pl.BlockSpec((S, LANE), lambda c: (0, CB_XS + c)), blk, blk, pl.BlockSpec((SSM_CONV, LANE), lambda c: (0, c))],
        out_specs=[blk, pl.BlockSpec((8, LANE), lambda c: (0, c))],
        out_shape=[_sds((S, 1024), BF16), _sds((8, 1024), F32)], name=name,
        compiler_params=_cparams("parallel"))(proj, pre, dact, conv_w)


def _softplus(x):
    return jnp.maximum(x, 0.0) + jnp.log1p(jnp.exp(-jnp.abs(x)))


def _pair(lane, v0, v1):
    return jnp.where(lane < SSM_HEAD_DIM, v0, v1)


def _ssd_pair_common(raw, dtb, alog, xs, cm, hprev, T):
    lane = lax.broadcasted_iota(jnp.int32, (T, LANE), 1)
    dt = _softplus(raw + dtb)
    a = -jnp.exp(alog)
    incl = _tri(T, lambda l, s: s <= l).astype(BF16)
    h1, h2, h3 = _split3(dt * a)
    acum = _dot(incl, h1, NN) + _dot(incl, h2, NN) + _dot(incl, h3, NN)
    acum_t = acum.T
    causal = _tri(T, lambda l, s: s <= l)
    decay = [jnp.where(causal, jnp.exp(jnp.minimum(acum[:, j:j + 1] - acum_t[j:j + 1, :], 0.0)), 0.0) for j in (0, 1)]
    dtc = _pair(lane, dt[:, 0:1], dt[:, 1:2])
    ac = _pair(lane, acum[:, 0:1], acum[:, 1:2])
    xdt = xs * dtc
    ea = jnp.exp(ac)
    e_end = jnp.exp(ac[T - 1:T, :] - ac)
    sub = lax.broadcasted_iota(jnp.int32, (LANE, LANE), 0)
    cd = jnp.where(sub < SSM_HEAD_DIM, jnp.exp(acum[T - 1:T, 0:1]), jnp.exp(acum[T - 1:T, 1:2]))
    r = _dot(cm, _bf(hprev), NT)
    return lane, dt, a, acum, decay, dtc, xdt, ea, e_end, cd, r


def ssd_fwd(xact, proj, dtb, alog, dskip, *, name):
    S = xact.shape[0]
    T = min(SEQ_BLOCK, S)
    nb = S // T

    def body(xs_ref, bm_ref, cm_ref, dt_ref, dtb_ref, al_ref, ds_ref, y_ref, st_ref, state):
        @pl.when(pl.program_id(0) == 0)
        def _():
            state[...] = jnp.zeros_like(state)

        for g in range(2):
            bm, cm = _bf(bm_ref[:, LANE * g:LANE * (g + 1)]), _bf(cm_ref[:, LANE * g:LANE * (g + 1)])
            gm = _dot(cm, bm, NT)
            for i in range(2):
                p = 2 * g + i
                cols = slice(LANE * p, LANE * (p + 1))
                xs = xs_ref[:, cols]
                hprev = state[p]
                st_ref[g, i] = hprev
                lane, dt, a, acum, decay, dtc, xdt, ea, e_end, cd, r = _ssd_pair_common(
                    dt_ref[:, cols], dtb_ref[p], al_ref[p], xs, cm, hprev, T)
                xdtb = _bf(xdt)
                y_intra = _pair(lane, _dot(_bf(gm * decay[0]), xdtb, NN), _dot(_bf(gm * decay[1]), xdtb, NN))
                state[p] = cd * hprev + _dot(_bf(xdt * e_end), bm, TN)
                dsk = ds_ref[p]
                lane1 = lane[0:1, :]
                y_ref[:, cols] = y_intra + ea * r + _pair(lane1, dsk[:, 0:1], dsk[:, 1:2]) * xs

    rows = pl.BlockSpec((4, 1, LANE), lambda b: (0, 0, 0))
    return pl.pallas_call(
        body, grid=(nb,),
        in_specs=[pl.BlockSpec((T, 512), lambda b: (b, 0)), pl.BlockSpec((T, 256), lambda b: (b, 2)), pl.BlockSpec((T, 256), lambda b: (b, 3)),
                  pl.BlockSpec((T, 512), lambda b: (b, CB_DT // 4)), rows, rows, rows],
        out_specs=[pl.BlockSpec((T, 512), lambda b: (b, 0)),
                   pl.BlockSpec((2, None, 2, LANE, LANE), lambda b: (0, b, 0, 0, 0))],
        out_shape=[_sds((S, 512), F32), _sds((2, nb, 2, LANE, LANE), F32)],
        scratch_shapes=[pltpu.VMEM((4, LANE, LANE), F32)], name=name,
        compiler_params=_cparams("arbitrary"))(xact, xact, xact, proj, dtb, alog, dskip)


def ssd_bwd(xact, proj, dtb, alog, dskip, states, dy, *, name):
    S = xact.shape[0]
    T = min(SEQ_BLOCK, S)
    nb = S // T

    def body(xs_ref, bm_ref, cm_ref, dt_ref, dtb_ref, al_ref, ds_ref, st_ref, dy_ref,
             dxs_ref, dbm_ref, dcm_ref, ddt_ref, stat_ref, dstate):
        @pl.when(pl.program_id(0) == 0)
        def _():
            dstate[...] = jnp.zeros_like(dstate)
            stat_ref[...] = jnp.zeros_like(stat_ref)

        for g in range(2):
            wide = (slice(None), slice(256 * g, 256 * (g + 1)))
            narrow = (slice(None), slice(LANE * g, LANE * (g + 1)))
            pair = slice(2 * g, 2 * g + 2)
            group(xs_ref.at[wide], bm_ref.at[narrow], cm_ref.at[narrow], dt_ref.at[:, LANE * 2 * g:LANE * (2 * g + 1)],
                  dt_ref.at[:, LANE * (2 * g + 1):LANE * (2 * g + 2)], dtb_ref.at[pair], al_ref.at[pair], ds_ref.at[pair], st_ref.at[g],
                  dy_ref.at[wide], dxs_ref.at[wide], dbm_ref.at[narrow], dcm_ref.at[narrow], ddt_ref.at[wide], stat_ref.at[pair],
                  dstate.at[pair])

    def group(xs_ref, bm_ref, cm_ref, dt0_ref, dt1_ref, dtb_ref, al_ref, ds_ref, st_ref, dy_ref,
              dxs_ref, dbm_ref, dcm_ref, ddt_ref, stat_ref, dstate):
        bm, cm = _bf(bm_ref[...]), _bf(cm_ref[...])
        gm = _dot(cm, bm, NT)
        dbm = jnp.zeros((T, LANE), F32)
        dcm = jnp.zeros((T, LANE), F32)
        after_eq = _tri(T, lambda i, l: l >= i).astype(BF16)
        rowi = lax.broadcasted_iota(jnp.int32, (T, 1), 0)
        for i, dt_ref in enumerate((dt0_ref, dt1_ref)):
            xs = xs_ref[:, LANE * i:LANE * (i + 1)]
            dyp = dy_ref[:, LANE * i:LANE * (i + 1)]
            hprev = st_ref[i]
            dh = dstate[i]
            raw = dt_ref[...]
            lane, dt, a, acum, decay, dtc, xdt, ea, e_end, cd, r = _ssd_pair_common(
                raw, dtb_ref[i], al_ref[i], xs, cm, hprev, T)
            lane1 = lane[0:1, :]
            dsk = ds_ref[i]
            dskp = _pair(lane1, dsk[:, 0:1], dsk[:, 1:2])
            head = [lane < SSM_HEAD_DIM, lane >= SSM_HEAD_DIM]
            hsum = lambda v, j: jnp.sum(jnp.where(head[j], v, 0.0), axis=1, keepdims=True)
            dhb = _bf(dh)
            xdtb = _bf(xdt)
            dyb = _bf(dyp)
            z = xdt * e_end
            dz = _dot(bm, dhb, NT)
            dbm = dbm + _dot(_bf(z), dhb, NN)
            dxdt = dz * e_end
            de_e = dz * z
            drr = dyp * ea
            dea_ea = drr * r
            dcm = dcm + _dot(_bf(drr), _bf(hprev), NN)
            dstate[i] = cd * dh + _dot(_bf(drr), cm, TN)
            dcd_cd = cd * dh * hprev
            dgs = jnp.zeros((T, T), F32)
            da_cols = []
            for j in (0, 1):
                w = gm * decay[j]
                dw = _dot(_bf(jnp.where(head[j], dyp, 0.0)), xdtb, NT)
                dxdt = dxdt + jnp.where(head[j], _dot(_bf(w), dyb, TN), 0.0)
                dgs = dgs + dw * decay[j]
                dseg = dw * w
                col = jnp.sum(dseg, axis=1, keepdims=True) - jnp.sum(dseg.T, axis=1, keepdims=True)
                col = col + hsum(dea_ea, j) - hsum(de_e, j)
                sub = lax.broadcasted_iota(jnp.int32, (LANE, LANE), 0)
                in_head = (sub < SSM_HEAD_DIM) if j == 0 else (sub >= SSM_HEAD_DIM)
                end = jnp.sum(hsum(de_e, j), axis=0, keepdims=True) + jnp.sum(
                    jnp.sum(jnp.where(in_head, dcd_cd, 0.0), axis=1, keepdims=True), axis=0, keepdims=True)
                da_cols.append(col + jnp.where(rowi == T - 1, end, 0.0))
            dgb = _bf(dgs)
            dcm = dcm + _dot(dgb, bm, NN)
            dbm = dbm + _dot(dgb, cm, TN)
            dacum = jnp.where(lane == 0, da_cols[0], jnp.where(lane == 1, da_cols[1], 0.0))
            h1, h2, h3 = _split3(dacum)
            ddta = _dot(after_eq, h1, NN) + _dot(after_eq, h2, NN) + _dot(after_eq, h3, NN)
            dxs_ref[:, LANE * i:LANE * (i + 1)] = dskp * dyp + dxdt * dtc
            dx_x = dxdt * xs
            ddt = ddta * a + jnp.where(lane == 0, hsum(dx_x, 0), jnp.where(lane == 1, hsum(dx_x, 1), 0.0))
            ddraw = jnp.where(lane < 2, ddt * _sigmoid(raw + dtb_ref[i]), 0.0)
            ddt_ref[:, LANE * i:LANE * (i + 1)] = _bf(ddraw)
            dsum = jnp.sum(dyp * xs, axis=0, keepdims=True)
            d0 = jnp.sum(jnp.where(lane1 < SSM_HEAD_DIM, dsum, 0.0), axis=1, keepdims=True)
            d1 = jnp.sum(jnp.where(lane1 >= SSM_HEAD_DIM, dsum, 0.0), axis=1, keepdims=True)
            dd = jnp.where(lane1 == 0, d0, jnp.where(lane1 == 1, d1, 0.0))
            stat_ref[i, 0:1, :] += jnp.sum(ddraw, axis=0, keepdims=True)
            stat_ref[i, 1:2, :] += jnp.where(lane1 < 2, jnp.sum(ddta * dt, axis=0, keepdims=True) * a, 0.0)
            stat_ref[i, 2:3, :] += dd
        dbm_ref[...] = dbm
        dcm_ref[...] = dcm

    rb = lambda b: nb - 1 - b
    rows = pl.BlockSpec((4, 1, LANE), lambda b: (0, 0, 0))
    tok512 = pl.BlockSpec((T, 512), lambda b: (rb(b), 0))
    tok256 = pl.BlockSpec((T, 256), lambda b: (rb(b), 0))
    return pl.pallas_call(
        body, grid=(nb,),
        in_specs=[tok512, pl.BlockSpec((T, 256), lambda b: (rb(b), 2)), pl.BlockSpec((T, 256), lambda b: (rb(b), 3)),
                  pl.BlockSpec((T, 512), lambda b: (rb(b), CB_DT // 4)), rows, rows, rows,
                  pl.BlockSpec((2, None, 2, LANE, LANE), lambda b: (0, rb(b), 0, 0, 0)), tok512],
        out_specs=[tok512, tok256, tok256, tok512, pl.BlockSpec((4, 8, LANE), lambda b: (0, 0, 0))],
        out_shape=[_sds((S, 512), F32), _sds((S, 256), F32), _sds((S, 256), F32), _sds((S, 512), BF16), _sds((4, 8, LANE), F32)],
        scratch_shapes=[pltpu.VMEM((4, LANE, LANE), F32)], name=name,
        compiler_params=_cparams("arbitrary"))(xact, xact, xact, proj, dtb, alog, dskip, states, dy)


def gated_norm(ypre, proj, gain, *, name):
    S, W = ypre.shape
    tm = min(TOKEN_TILE, S)

    def body(y_ref, z_ref, g_ref, o_ref):
        z = z_ref[...]
        yg = y_ref[...] * (z * _sigmoid(z))
        o_ref[...] = _bf(yg * lax.rsqrt(jnp.mean(yg * yg, axis=-1, keepdims=True) + NORM_EPS) * g_ref[...])

    tile = pl.BlockSpec((tm, W), lambda i: (i, 0))
    return pl.pallas_call(
        body, grid=(S // tm,), in_specs=[tile, pl.BlockSpec((tm, W), lambda i: (i, CB_MZ // 4)), pl.BlockSpec((1, W), lambda i: (0, 0))],
        out_specs=tile, out_shape=_sds((S, W), BF16), name=name, compiler_params=_cparams("parallel"))(ypre, proj, gain)


def gated_norm_bwd(ypre, proj, gain, dycat, *, name):
    S, W = ypre.shape
    tm = min(TOKEN_TILE, S)

    def body(y_ref, z_ref, g_ref, dy_ref, dyp_ref, dz_ref, st_ref):
        z = z_ref[...]
        y = y_ref[...]
        sg = _sigmoid(z)
        sz = z * sg
        yg = y * sz
        r = lax.rsqrt(jnp.mean(yg * yg, axis=-1, keepdims=True) + NORM_EPS)
        yhat = yg * r
        dyo = dy_ref[...]

        @pl.when(pl.program_id(0) == 0)
        def _():
            st_ref[...] = jnp.zeros_like(st_ref)

        st_ref[0:1, :] += jnp.sum(dyo * yhat, axis=0, keepdims=True)
        dyhat = dyo * g_ref[...]
        dyg = r * (dyhat - yhat * jnp.mean(dyhat * yhat, axis=-1, keepdims=True))
        dyp_ref[...] = dyg * sz
        dz_ref[...] = _bf(dyg * y * (sg * (1.0 + z * (1.0 - sg))))

    tile = pl.BlockSpec((tm, W), lambda i: (i, 0))
    return pl.pallas_call(
        body, grid=(S // tm,),
        in_specs=[tile, pl.BlockSpec((tm, W), lambda i: (i, CB_MZ // 4)), pl.BlockSpec((1, W), lambda i: (0, 0)),
                  pl.BlockSpec((tm, W), lambda i: (i, 2))],
        out_specs=[tile, tile, pl.BlockSpec((8, W), lambda i: (0, 0))],
        out_shape=[_sds((S, W), F32), _sds((S, W), BF16), _sds((8, W), F32)], name=name,
        compiler_params=_cparams("arbitrary"))(ypre, proj, gain, dycat)


def ada_mod(c_all, w, bias, *, name):
    M, K = c_all.shape
    N = w.shape[1]
    tn = _tile(N, 512)

    def body(c_ref, w_ref, b_ref, o_ref, cond_ref):
        cv = c_ref[...]
        cond = cv * _sigmoid(cv)
        cond_ref[...] = cond
        o_ref[...] = _dot(_bf(cond), _bf(w_ref[...]), NN) + b_ref[...]

    return pl.pallas_call(
        body, grid=(N // tn,),
        in_specs=[pl.BlockSpec((M, K), lambda j: (0, 0)), pl.BlockSpec((K, tn), lambda j: (0, j)), pl.BlockSpec((1, tn), lambda j: (0, j))],
        out_specs=[pl.BlockSpec((M, tn), lambda j: (0, j)), pl.BlockSpec((M, K), lambda j: (0, 0))],
        out_shape=[_sds((M, N), F32), _sds((M, K), F32)], name=name, compiler_params=_cparams("arbitrary"))(c_all, w, bias)


def _adamw(g, w, m, v):
    m = ADAM_B1 * m + (1.0 - ADAM_B1) * g
    v = ADAM_B2 * v + (1.0 - ADAM_B2) * (g * g)
    m_hat = m / (1.0 - ADAM_B1 ** ADAM_STEP)
    v_hat = v / (1.0 - ADAM_B2 ** ADAM_STEP)
    return -ADAM_LR * (m_hat / (jnp.sqrt(v_hat) + ADAM_EPS) + ADAM_WD * w), m, v


def adamw_parts(parts, w, m, v, *, name):
    P, R, C = parts.shape
    tr = _tile(R, 592, 512, 352, 272, 256, 160, 128, 80, 64, 32, 16)

    def body(p_ref, w_ref, m_ref, v_ref, g_ref, d_ref, mo_ref, vo_ref):
        g = p_ref[0].astype(F32)
        for j in range(1, P):
            g = g + p_ref[j].astype(F32)
        g_ref[...] = g
        d_ref[...], mo_ref[...], vo_ref[...] = _adamw(g, w_ref[...], m_ref[...], v_ref[...])

    tile = pl.BlockSpec((tr, C), lambda i: (i, 0))
    return pl.pallas_call(
        body, grid=(R // tr,), in_specs=[pl.BlockSpec((P, tr, C), lambda i: (0, i, 0)), tile, tile, tile],
        out_specs=[tile] * 4, out_shape=[_sds((R, C), F32)] * 4, name=name, compiler_params=_cparams("parallel"))(parts, w, m, v)


def ada_adamw(cond_t, dmod, w, m, v, *, name):
    D, N = w.shape
    tr = _tile(D, 256)

    def body(c_ref, d_ref, w_ref, m_ref, v_ref, g_ref, dl_ref, mo_ref, vo_ref):
        cc = c_ref[...]
        dd = d_ref[...]
        g = cc[:, 0:1] * dd[0:1, :]
        for b in range(1, N_DEV):
            g = g + cc[:, b:b + 1] * dd[b:b + 1, :]
        g_ref[...] = g
        dl_ref[...], mo_ref[...], vo_ref[...] = _adamw(g, w_ref[...], m_ref[...], v_ref[...])

    tile = pl.BlockSpec((tr, N), lambda i: (i, 0))
    return pl.pallas_call(
        body, grid=(D // tr,), in_specs=[pl.BlockSpec((tr, N_DEV), lambda i: (i, 0)), pl.BlockSpec((N_DEV, N), lambda i: (0, 0)), tile, tile, tile],
        out_specs=[tile] * 4, out_shape=[_sds((D, N), F32)] * 4, name=name, compiler_params=_cparams("parallel"))(cond_t, dmod, w, m, v)


def _my_place():
    mx, my, mc = lax.axis_index("x"), lax.axis_index("y"), lax.axis_index("c")
    return mx, my, mc, 4 * mx + 2 * my + mc


def _peer(mx, my, mc, k):
    px = 1 - mx if (k >> 2) & 1 else mx
    py = 1 - my if (k >> 1) & 1 else my
    pc = 1 - mc if k & 1 else mc
    return (px, py, pc), 4 * px + 2 * py + pc


def _comm_call(body, x, out_shape, space, name):
    spec = pl.BlockSpec(memory_space=space)
    return pl.pallas_call(
        body, in_specs=[spec], out_specs=spec, out_shape=out_shape,
        scratch_shapes=[pltpu.SemaphoreType.DMA((N_DEV - 1,)), pltpu.SemaphoreType.DMA((N_DEV - 1,)), pltpu.SemaphoreType.DMA(())],
        name=name, compiler_params=pltpu.CompilerParams(has_side_effects=True, vmem_limit_bytes=VMEM_LIMIT))(x)


def allgather(x, *, in_vmem, name):
    def body(x_ref, out_ref, send_sems, recv_sems, local_sem):
        mx, my, mc, me = _my_place()
        mine = pltpu.make_async_copy(x_ref, out_ref.at[me], local_sem)
        mine.start()
        copies = []
        for k in range(1, N_DEV):
            peer, _ = _peer(mx, my, mc, k)
            cp = pltpu.make_async_remote_copy(src_ref=x_ref, dst_ref=out_ref.at[me], send_sem=send_sems.at[k - 1],
                                              recv_sem=recv_sems.at[k - 1], device_id=peer, device_id_type=pl.DeviceIdType.MESH)
            cp.start()
            copies.append(cp)
        for cp in copies:
            cp.wait()
        mine.wait()

    return _comm_call(body, x, _sds((N_DEV,) + x.shape, x.dtype), pltpu.VMEM if in_vmem else pltpu.HBM, name)


def allgather_two_level(x, *, name):
    def body(x_ref, out_ref, send_sems, recv_sems, local_sem):
        _gather_two_level(x_ref, out_ref, send_sems, recv_sems, local_sem, start=True, finish=True)

    return _comm_call(body, x, _sds((N_DEV,) + x.shape, x.dtype), pltpu.HBM, name)


def _gather_two_level(x_ref, out_ref, send_sems, recv_sems, local_sem, *, start, finish):
    mx, my, mc, _ = _my_place()
    me, sibling = (mx, my, mc), (mx, my, 1 - mc)
    chips = [(1 - mx, my), (mx, 1 - my), (1 - mx, 1 - my)]

    def copy(k, block, to, src=None):
        slot = out_ref.at[4 * block[0] + 2 * block[1] + block[2]]
        return pltpu.make_async_remote_copy(src_ref=slot if src is None else src, dst_ref=slot, send_sem=send_sems.at[k],
                                            recv_sem=recv_sems.at[k], device_id=to, device_id_type=pl.DeviceIdType.MESH)

    mine = pltpu.make_async_copy(x_ref, out_ref.at[4 * mx + 2 * my + mc], local_sem)
    first = [copy(0, me, sibling, src=x_ref)] + [copy(1 + j, me, (*chip, mc), src=x_ref) for j, chip in enumerate(chips)]
    if start:
        mine.start()
        for cp in first:
            cp.start()
    if finish:
        passed = [copy(4 + j, (*chip, mc), sibling) for j, chip in enumerate(chips)]
        for j, chip in enumerate(chips):
            copy(1 + j, (*chip, mc), me).wait_recv()
            passed[j].start()
        copy(0, sibling, me).wait_recv()
        for j, chip in enumerate(chips):
            copy(4 + j, (*chip, 1 - mc), me).wait_recv()
        for cp in first + passed:
            cp.wait_send()
        mine.wait()


def _chip_exchange(s_ref, r_ref, send_sems, recv_sems, local_sem, *, start, finish):
    mx, my, mc, _ = _my_place()
    mine = pltpu.make_async_copy(s_ref.at[2 * mx + my], r_ref.at[2 * mx + my], local_sem)
    copies = []
    for k in range(1, 4):
        px = 1 - mx if (k >> 1) & 1 else mx
        py = 1 - my if k & 1 else my
        copies.append(pltpu.make_async_remote_copy(
            src_ref=s_ref.at[2 * px + py], dst_ref=r_ref.at[2 * mx + my], send_sem=send_sems.at[k - 1], recv_sem=recv_sems.at[k - 1],
            device_id=(px, py, mc), device_id_type=pl.DeviceIdType.MESH))
    if start:
        mine.start()
        for cp in copies:
            cp.start()
    if finish:
        for cp in copies:
            cp.wait()
        mine.wait()


COMM_SCRATCH = [pltpu.SemaphoreType.DMA((N_DEV - 1,)), pltpu.SemaphoreType.DMA((N_DEV - 1,)), pltpu.SemaphoreType.DMA(())]


def sibling_exchange(send, *, name):
    n_chip, _, R, C = send.shape

    def body(s_ref, r_ref, send_sems, recv_sems, local_sem):
        mx, my, mc, _ = _my_place()
        copies = []
        for q in range(n_chip):
            cp = pltpu.make_async_remote_copy(src_ref=s_ref.at[q, 1 - mc], dst_ref=r_ref.at[q], send_sem=send_sems.at[q],
                                              recv_sem=recv_sems.at[q], device_id=(mx, my, 1 - mc), device_id_type=pl.DeviceIdType.MESH)
            cp.start()
            copies.append(cp)
        for cp in copies:
            cp.wait()

    return _comm_call(body, send, _sds((n_chip, R, C), send.dtype), pltpu.HBM, name)


def chip_exchange(send, *, name):
    def body(s_ref, r_ref, send_sems, recv_sems, local_sem):
        _chip_exchange(s_ref, r_ref, send_sems, recv_sems, local_sem, start=True, finish=True)

    return _comm_call(body, send, _sds(send.shape, send.dtype), pltpu.HBM, name)


def add_partials(a, b, *, name):
    P, R, C = a.shape
    tr = _tile(R, 592, 512, 352, 272, 256, 160, 128, 80, 64, 32, 16)

    def body(a_ref, b_ref, o_ref):
        o_ref[...] = _bf(a_ref[...].astype(F32) + b_ref[...].astype(F32))

    tile = pl.BlockSpec((P, tr, C), lambda i: (0, i, 0))
    return pl.pallas_call(body, grid=(R // tr,), in_specs=[tile, tile], out_specs=tile, out_shape=_sds((P, R, C), BF16), name=name,
                          compiler_params=_cparams("parallel"))(a, b)


def _rows128(a):
    f = a.reshape(-1)
    n = -(-f.shape[0] // (16 * LANE)) * (16 * LANE)
    return jnp.pad(f, (0, n - f.shape[0])).reshape(-1, LANE)


PACK_ROWS = 512


def _pad_rows(buf):
    r = buf.shape[-2]
    pad = -r % PACK_ROWS
    return jnp.pad(buf, [(0, 0)] * (buf.ndim - 2) + [(0, pad), (0, 0)])


def _pack(arrays):
    parts = [_rows128(a) for a in arrays]
    offs = np.cumsum([0] + [p.shape[0] for p in parts])
    return _pad_rows(jnp.concatenate(parts, axis=0)), [int(o) for o in offs]


def _unpack(buf, offs, shapes):
    lead = buf.shape[:-2]
    out = []
    for o, shp in zip(offs, shapes):
        n = int(np.prod(shp))
        rows = -(-n // LANE)
        seg = buf[..., o:o + rows, :].reshape(lead + (rows * LANE,))[..., :n]
        out.append(seg.reshape(lead + tuple(shp)))
    return out


def _pad_w_in_t(w_t):
    D = w_t.shape[1]
    dt = jnp.pad(w_t[IN_MAIN:].reshape(4, 2, D), ((0, 0), (0, LANE - 2), (0, 0)))
    return jnp.concatenate([w_t[:IN_MAIN], dt.reshape(4 * LANE, D)], axis=0)


def _unpad_w_in_t(g_t):
    D = g_t.shape[1]
    dt = g_t[IN_MAIN:].reshape(4, LANE, D)[:, :2].reshape(SSM_HEADS, D)
    return jnp.concatenate([g_t[:IN_MAIN], dt], axis=0)


def _piece_rows(n, shard_shape):
    r = shard_shape[0] if n in ROW_SHARDED else shard_shape[1]
    return r, -(-r // 16) * 16


def _to_piece(n, shard):
    t = shard if n in ROW_SHARDED else shard.T
    return jnp.pad(t, ((0, -t.shape[0] % 16), (0, 0)))


def _from_piece(n, piece, shard_shape):
    r, _ = _piece_rows(n, shard_shape)
    return piece[:r] if n in ROW_SHARDED else piece[:r].T


def _pair_rows(p):
    return jnp.pad(p.reshape(4, 1, 2), ((0, 0), (0, 0), (0, LANE - 2)))


def _row(v):
    return v.reshape(1, -1)


def _ffn_fwd(h, gain, mod3, wg, wu, get_wd, tag, gather=None, arrived=None):
    shift, scale, gate = mod3
    u = norm_mod(h, gain, shift, scale, name=tag + "_norm")
    a, b, act, *gathered = ffn_up(u, wg, wu, name=tag + "_up", gather=gather)
    if gathered:
        arrived(gathered[0])
    hn, out = matmul_resid(act, get_wd(), h, gate, 0.5, name=tag + "_down")
    return hn, (h, u, a, b, act, out)


def _wgrad(a, b, name):
    return matmul(a, b, ta=True, tm=_tile(a.shape[1], 1408, 1536, 1024, 512), tn=b.shape[1], tk=WGRAD_TOKENS, out_dtype=BF16, name=name)


def _ffn_bwd(dh, saved, gain, mod3, wg, wu, wd, tag, exchange=None):
    h, u, a, b, act, out = saved
    _, scale, gate = mod3
    dout, gst = gate_bwd(dh, out, gate, 0.5, name=tag + "_gate_bwd")
    da, db = ffn_dact(dout, wd, a, b, name=tag + "_dact")
    dh_prev, nst, *exchanged = dgrad_norm_bwd([da, db], [wg, wu], [(0, 0), (1, 0)], h, gain, scale, dh, name=tag + "_dgrad",
                                              exchange=exchange)
    grads = (_wgrad(da, u, tag + "_dwg"), _wgrad(db, u, tag + "_dwu"), _wgrad(act, dout, tag + "_dwd"))
    return dh_prev, grads, nst[0], [nst[1], nst[2], gst[0]], (exchanged[0] if exchanged else None)


def _mix_fwd(h, p, mod3, w_in, w_out, cos, sin, tag, gather=None):
    shift, scale, gate = mod3
    u = norm_mod(h, p["norm_mix"], shift, scale, name=tag + "_norm")
    proj = matmul(u, w_in, tb=True, tm=BIG_TOKEN_TILE, tn=512, tk=D_MODEL, name=tag + "_proj")
    y_ret, ypre_ret, st_ret = ret_fwd(proj, p["ret_gn"], cos, sin, name=tag + "_ret")
    y_sb, sb_cin, *gathered = sb_fwd(proj, name=tag + "_sb", gather=gather)
    pre, xact = conv_fwd(proj, p["conv_w"], p["conv_b"], name=tag + "_conv")
    ypre_ssm, st_ssm = ssd_fwd(xact, proj, p["dt_bias"], p["a_log"], p["d_skip"], name=tag + "_ssd")
    y_ssm = gated_norm(ypre_ssm, proj, p["ssm_norm"], name=tag + "_gnorm")
    ycat = jnp.concatenate([y_ret, y_sb.astype(BF16), y_ssm], axis=1)
    hn, mixed = matmul_resid(ycat, w_out, h, gate, 1.0, name=tag + "_out")
    return hn, (h, u, proj, ypre_ret, st_ret, sb_cin, pre, xact, ypre_ssm, st_ssm, ycat, mixed), (gathered[0] if gathered else None)


def _mix_bwd(dh, saved, p, mod3, w_in, w_out, cos, sin, tag, exchange=None):
    h, u, proj, ypre_ret, st_ret, sb_cin, pre, xact, ypre_ssm, st_ssm, ycat, mixed = saved
    _, scale, gate = mod3
    dmixed, gst = gate_bwd(dh, mixed, gate, 1.0, name=tag + "_gate_bwd")
    dycat = matmul(dmixed, w_out, tb=True, tm=BIG_TOKEN_TILE, tn=512, tk=D_MODEL, name=tag + "_dycat")
    dw_out = _wgrad(ycat, dmixed, tag + "_dw_out")
    dret, rst = ret_bwd(proj, p["ret_gn"], cos, sin, ypre_ret, st_ret, dycat, name=tag + "_ret_bwd")
    dsq, dsk, dsv, *exchanged = sb_bwd(proj, sb_cin, dycat, name=tag + "_sb_bwd", exchange=exchange)
    dypre, dz, nst2 = gated_norm_bwd(ypre_ssm, proj, p["ssm_norm"], dycat, name=tag + "_gnorm_bwd")
    dxs, dbm, dcm, ddt, sst = ssd_bwd(xact, proj, p["dt_bias"], p["a_log"], p["d_skip"], st_ssm, dypre, name=tag + "_ssd_bwd")
    dact = jnp.concatenate([dxs, dbm, dcm], axis=1)
    dxbc, cst = conv_bwd(proj, pre, dact, p["conv_w"], name=tag + "_conv_bwd")
    pieces = [dret, dsq, dsk, dsv, dz, dxbc, ddt]
    starts = np.cumsum([0] + [pc.shape[1] for pc in pieces])
    assert starts[-1] == IN_PAD
    dh_prev, nst = dgrad_norm_bwd(pieces, [w_in], [(0, int(r0)) for r0 in starts[:-1]], h, p["norm_mix"], scale, dh, name=tag + "_dgrad")
    dw_in = jnp.concatenate([_wgrad(pc, u, f"{tag}_dw_in{i}") for i, pc in enumerate(pieces)], axis=0)
    small = dict(norm_mix=nst[0], ret_gn=rst[0], ssm_norm=nst2[0], conv_w=cst[0:4], conv_b=cst[4],
                 dt_bias=sst[:, 0, :2].reshape(SSM_HEADS), a_log=sst[:, 1, :2].reshape(SSM_HEADS), d_skip=sst[:, 2, :2].reshape(SSM_HEADS))
    return dh_prev, dw_in, dw_out, small, [nst[1], nst[2], gst[0]], (exchanged[0] if exchanged else None)


BIG = ("ffn1_wg", "ffn1_wu", "ffn1_wd", "w_in", "w_out", "ffn2_wg", "ffn2_wu", "ffn2_wd")
FIRST, REST = BIG[:3], BIG[3:]
ROW_SHARDED = ("ffn1_wd", "w_out", "ffn2_wd")
SMALL = ("ada_b", "norm_ffn1", "norm_mix", "conv_b", "dt_bias", "a_log", "d_skip", "ret_gn", "ssm_norm", "norm_ffn2",
         "final_ada_b", "final_norm")
NAMES = ("ada_w", "ada_b", "norm_ffn1", "ffn1_wg", "ffn1_wu", "ffn1_wd", "norm_mix", "w_in", "conv_w", "conv_b", "dt_bias", "a_log",
         "d_skip", "ret_gn", "ssm_norm", "w_out", "norm_ffn2", "ffn2_wg", "ffn2_wu", "ffn2_wd", "final_ada_w", "final_ada_b", "final_norm")


def kernel(x, c, ada_w, ada_b, norm_ffn1, ffn1_wg, ffn1_wu, ffn1_wd, norm_mix, w_in, conv_w, conv_b, dt_bias, a_log, d_skip, ret_gn, ssm_norm, w_out, norm_ffn2, ffn2_wg, ffn2_wu, ffn2_wd, final_ada_w, final_ada_b, final_norm, loss_target, m_ada_w, m_ada_b, m_norm_ffn1, m_ffn1_wg, m_ffn1_wu, m_ffn1_wd, m_norm_mix, m_w_in, m_conv_w, m_conv_b, m_dt_bias, m_a_log, m_d_skip, m_ret_gn, m_ssm_norm, m_w_out, m_norm_ffn2, m_ffn2_wg, m_ffn2_wu, m_ffn2_wd, m_final_ada_w, m_final_ada_b, m_final_norm, v_ada_w, v_ada_b, v_norm_ffn1, v_ffn1_wg, v_ffn1_wu, v_ffn1_wd, v_norm_mix, v_w_in, v_conv_w, v_conv_b, v_dt_bias, v_a_log, v_d_skip, v_ret_gn, v_ssm_norm, v_w_out, v_norm_ffn2, v_ffn2_wg, v_ffn2_wu, v_ffn2_wd, v_final_ada_w, v_final_ada_b, v_final_norm):
    W = dict(ada_w=ada_w, ada_b=ada_b, norm_ffn1=norm_ffn1, ffn1_wg=ffn1_wg, ffn1_wu=ffn1_wu, ffn1_wd=ffn1_wd, norm_mix=norm_mix,
             w_in=w_in, conv_w=conv_w, conv_b=conv_b, dt_bias=dt_bias, a_log=a_log, d_skip=d_skip, ret_gn=ret_gn, ssm_norm=ssm_norm,
             w_out=w_out, norm_ffn2=norm_ffn2, ffn2_wg=ffn2_wg, ffn2_wu=ffn2_wu, ffn2_wd=ffn2_wd, final_ada_w=final_ada_w,
             final_ada_b=final_ada_b, final_norm=final_norm)
    M1 = dict(ada_w=m_ada_w, ada_b=m_ada_b, norm_ffn1=m_norm_ffn1, ffn1_wg=m_ffn1_wg, ffn1_wu=m_ffn1_wu, ffn1_wd=m_ffn1_wd,
              norm_mix=m_norm_mix, w_in=m_w_in, conv_w=m_conv_w, conv_b=m_conv_b, dt_bias=m_dt_bias, a_log=m_a_log, d_skip=m_d_skip,
              ret_gn=m_ret_gn, ssm_norm=m_ssm_norm, w_out=m_w_out, norm_ffn2=m_norm_ffn2, ffn2_wg=m_ffn2_wg, ffn2_wu=m_ffn2_wu,
              ffn2_wd=m_ffn2_wd, final_ada_w=m_final_ada_w, final_ada_b=m_final_ada_b, final_norm=m_final_norm)
    V2 = dict(ada_w=v_ada_w, ada_b=v_ada_b, norm_ffn1=v_norm_ffn1, ffn1_wg=v_ffn1_wg, ffn1_wu=v_ffn1_wu, ffn1_wd=v_ffn1_wd,
              norm_mix=v_norm_mix, w_in=v_w_in, conv_w=v_conv_w, conv_b=v_conv_b, dt_bias=v_dt_bias, a_log=v_a_log, d_skip=v_d_skip,
              ret_gn=v_ret_gn, ssm_norm=v_ssm_norm, w_out=v_w_out, norm_ffn2=v_norm_ffn2, ffn2_wg=v_ffn2_wg, ffn2_wu=v_ffn2_wu,
              ffn2_wd=v_ffn2_wd, final_ada_w=v_final_ada_w, final_ada_b=v_final_ada_b, final_norm=v_final_norm)
    D = D_MODEL
    S = x.shape[1]
    me = 4 * lax.axis_index("x") + 2 * lax.axis_index("y") + lax.axis_index("c")
    n_mod = ada_w.shape[2]
    n_fmod = final_ada_w.shape[1]

    c_all = allgather(jnp.broadcast_to(c, (8, D)), in_vmem=True, name="gather_c")[:, 0, :]
    ada_cols = jnp.concatenate([ada_w[0], ada_w[1], final_ada_w], axis=1)
    ada_bias = jnp.concatenate([lax.dynamic_slice(ada_b, (0, me * n_mod), (DEPTH, n_mod)).reshape(1, -1),
                                lax.dynamic_slice(final_ada_b, (me * n_fmod,), (n_fmod,)).reshape(1, -1)], axis=1)
    mod_sh, cond = ada_mod(jnp.pad(c_all, ((0, 8), (0, 0))), ada_cols, ada_bias, name="ada_mod")
    n_cols = mod_sh.shape[1]
    small_in = jnp.concatenate([mod_sh[:8], jnp.pad(conv_w.reshape(8, LANE), ((0, 0), (0, n_cols - LANE)))], axis=0)
    small_g = allgather(small_in, in_vmem=True, name="gather_mod")
    mod_rows = lax.dynamic_index_in_dim(small_g[:, :8, :], me, axis=1, keepdims=False)
    mod = [mod_rows[:, l * n_mod:(l + 1) * n_mod].reshape(9, D) for l in range(DEPTH)]
    fmod = mod_rows[:, DEPTH * n_mod:].reshape(2, D)
    conv_w_full = small_g[:, 8:, :LANE].reshape(N_DEV, DEPTH, SSM_CONV, LANE).transpose(1, 2, 0, 3).reshape(DEPTH, SSM_CONV, 8 * LANE)

    rows = {n: _piece_rows(n, W[n].shape[1:]) for n in BIG}

    def offsets(names):
        offs, o = {}, 0
        for n in names:
            offs[n] = o
            o += rows[n][1]
        return offs

    pack_of = lambda src, dtype, l, names: jnp.concatenate([_to_piece(n, src[n][l]).astype(dtype) for n in names], axis=0)
    full = {}

    def file_weights(wgath, items):
        o = 0
        for l, n in items:
            w = wgath[:, o:o + rows[n][0], :].reshape(N_DEV * rows[n][0], D)
            full[l, n] = _pad_w_in_t(w) if n == "w_in" else w
            o += rows[n][1]

    first_items = [(0, "ffn1_wg"), (0, "ffn1_wu")]
    riding = {(0, "ffn1_up"): [(0, "ffn1_wd"), (0, "w_in"), (0, "w_out")],
              (0, "sb"): [(0, n) for n in BIG[5:]] + ([(1, n) for n in FIRST] if DEPTH > 1 else []),
              (0, "ffn2_up"): [(1, "w_in"), (1, "w_out")] if DEPTH > 1 else [],
              (1, "ffn1_up"): [(1, n) for n in BIG[5:]]}
    pack_items = lambda items: jnp.concatenate([_to_piece(n, W[n][l]).astype(BF16) for l, n in items], axis=0)
    file_weights(allgather_two_level(pack_items(first_items), name="gather_weights0"), first_items)

    def ride(l, kernel):
        items = riding.get((l, kernel))
        if not items:
            return None, None
        return pack_items(items), lambda wgath: file_weights(wgath, items)

    cos, sin = _rope_tables(S)
    h = x[0]
    target = loss_target[0]
    layer_p = []
    for l in range(DEPTH):
        layer_p.append(dict(norm_ffn1=_row(norm_ffn1[l]), norm_mix=_row(norm_mix[l]), norm_ffn2=_row(norm_ffn2[l]),
                            ret_gn=_row(ret_gn[l]), ssm_norm=_row(ssm_norm[l]), conv_w=conv_w_full[l], conv_b=_row(conv_b[l]),
                            dt_bias=_pair_rows(dt_bias[l]), a_log=_pair_rows(a_log[l]), d_skip=_pair_rows(d_skip[l])))
    mods = [[[_row(mod[l][3 * s + k]) for k in range(3)] for s in range(3)] for l in range(DEPTH)]

    saved = []
    for l in range(DEPTH):
        p = layer_p[l]
        pack, arrived = ride(l, "ffn1_up")
        h, s1 = _ffn_fwd(h, p["norm_ffn1"], mods[l][0], full[l, "ffn1_wg"], full[l, "ffn1_wu"], lambda: full[l, "ffn1_wd"],
                         f"l{l}_ffn1", gather=pack, arrived=arrived)
        pack, arrived = ride(l, "sb")
        h, s2, gathered = _mix_fwd(h, p, mods[l][1], full[l, "w_in"], full[l, "w_out"], cos, sin, f"l{l}_mix", gather=pack)
        if pack is not None:
            arrived(gathered)
        pack, arrived = ride(l, "ffn2_up")
        h, s3 = _ffn_fwd(h, p["norm_ffn2"], mods[l][2], full[l, "ffn2_wg"], full[l, "ffn2_wu"], lambda: full[l, "ffn2_wd"],
                         f"l{l}_ffn2", gather=pack, arrived=arrived)
        saved.append((s1, s2, s3))

    def chip_sums(grads, names, tag):
        def send_piece(n):
            g = _unpad_w_in_t(grads[n]) if n == "w_in" else grads[n]
            r, rp = rows[n]
            return jnp.pad(g.reshape(N_DEV, r, D), ((0, 0), (0, rp - r), (0, 0)))

        spack = jnp.concatenate([send_piece(n) for n in names], axis=1)
        by_core = spack.reshape((N_DEV // 2, 2) + spack.shape[1:])
        from_sibling = sibling_exchange(by_core, name="exchange_sibling" + tag)
        own = lax.dynamic_index_in_dim(by_core, lax.axis_index("c"), axis=1, keepdims=False)
        return add_partials(own, from_sibling, name="add_sibling" + tag)

    dh, fst = final_loss_bwd(h, _row(final_norm), _row(fmod[0]), _row(fmod[1]), target, name="final")
    reduced = []
    small_g_l = [None] * DEPTH
    dmod = [None] * DEPTH
    pending = None
    for l in reversed(range(DEPTH)):
        p, f = layer_p[l], {n: full[l, n] for n in BIG}
        s1, s2, s3 = saved[l]
        dh, g2, gn2, dm2, _ = _ffn_bwd(dh, s3, p["norm_ffn2"], mods[l][2], f["ffn2_wg"], f["ffn2_wu"], f["ffn2_wd"], f"l{l}_ffn2")
        dh, gw_in, gw_out, sm, dm1, exchanged = _mix_bwd(dh, s2, p, mods[l][1], f["w_in"], f["w_out"], cos, sin, f"l{l}_mix",
                                                         exchange=pending)
        if pending is not None:
            reduced.append((l + 1, BIG, exchanged))
        grads = dict(zip(REST, (gw_in, gw_out) + g2))
        early = chip_sums(grads, REST, f"{l}r") if l == 0 else None
        dh, g1, gn1, dm0, exchanged = _ffn_bwd(dh, s1, p["norm_ffn1"], mods[l][0], f["ffn1_wg"], f["ffn1_wu"], f["ffn1_wd"], f"l{l}_ffn1",
                                               exchange=early)
        grads.update(zip(FIRST, g1))
        if l == 0:
            reduced.append((0, REST, exchanged))
            reduced.append((0, FIRST, chip_exchange(chip_sums(grads, FIRST, "0f"), name="exchange_chips0")))
        else:
            pending = chip_sums(grads, BIG, str(l))
        sm["norm_ffn1"], sm["norm_ffn2"] = gn1, gn2
        small_g_l[l] = sm
        dmod[l] = jnp.concatenate(dm0 + dm1 + dm2, axis=0)
    grad_x = dh[None]

    big_res = {}
    for l, names, rp in reduced:
        outs = adamw_parts(rp, *[pack_of(src, F32, l, names) for src in (W, M1, V2)], name=f"adamw_big{l}_{names[0]}")
        offs = offsets(names)
        for n in names:
            big_res[l, n] = [_from_piece(n, o[offs[n]:offs[n] + rows[n][1]], W[n].shape[1:]) for o in outs]
    big_out = [{n: jnp.stack([big_res[l, n][k] for l in range(DEPTH)]) for n in BIG} for k in range(4)]

    stack2 = lambda key: jnp.stack([small_g_l[l][key] for l in range(DEPTH)])
    pieces = [("loss", fst[3, 0:1]), ("ada_b", jnp.stack(dmod)), ("final_ada_b", jnp.concatenate([fst[1], fst[2]])),
              ("norm_ffn1", stack2("norm_ffn1")), ("norm_mix", stack2("norm_mix")), ("norm_ffn2", stack2("norm_ffn2")),
              ("conv_w", stack2("conv_w")), ("conv_b", stack2("conv_b")), ("dt_bias", stack2("dt_bias")), ("a_log", stack2("a_log")),
              ("d_skip", stack2("d_skip")), ("ret_gn", stack2("ret_gn")), ("ssm_norm", stack2("ssm_norm")), ("final_norm", fst[0])]
    names = [n for n, _ in pieces]
    shapes = [a.shape for _, a in pieces]
    ppack, poffs = _pack([a for _, a in pieces])
    pg = allgather(ppack, in_vmem=True, name="gather_small")
    zero_like = lambda n, a: jnp.zeros(a.shape, F32)
    spacks = [_pack([(src[n] if n in SMALL else zero_like(n, a)) for n, a in pieces])[0] for src in (W, M1, V2)]
    souts = adamw_parts(pg, *spacks, name="adamw_small")
    small_out = [dict(zip(names, _unpack(o, poffs, shapes))) for o in souts]
    loss = small_out[0]["loss"][0]

    gathered = dict(zip(names, _unpack(pg, poffs, shapes)))
    conv_parts = lax.dynamic_slice_in_dim(gathered["conv_w"], me * LANE, LANE, axis=3).reshape(N_DEV, DEPTH * SSM_CONV, LANE)
    conv_out = [o.reshape(conv_w.shape) for o in adamw_parts(conv_parts, conv_w.reshape(-1, LANE), m_conv_w.reshape(-1, LANE),
                                                              v_conv_w.reshape(-1, LANE), name="adamw_conv_w")]
    cond_t = cond[:8].T
    ada_out = []
    for l in range(DEPTH):
        dsel = lax.dynamic_slice_in_dim(gathered["ada_b"][:, l, :], me * n_mod, n_mod, axis=1)
        ada_out.append(ada_adamw(cond_t, dsel, ada_w[l], m_ada_w[l], v_ada_w[l], name=f"adamw_ada_w{l}"))
    ada_out = [jnp.stack([ada_out[l][k] for l in range(DEPTH)]) for k in range(4)]
    fsel = lax.dynamic_slice_in_dim(gathered["final_ada_b"].reshape(N_DEV, 2 * D), me * n_fmod, n_fmod, axis=1)
    fada_out = ada_adamw(cond_t, fsel, final_ada_w, m_final_ada_w, v_final_ada_w, name="adamw_final_ada_w")

    def pick(k, n):
        if n in BIG:
            return big_out[k][n]
        if n == "ada_w":
            return ada_out[k]
        if n == "final_ada_w":
            return fada_out[k]
        if n == "conv_w":
            return conv_out[k]
        return small_out[k][n]

    return (loss, grad_x) + tuple(pick(k, n) for k in range(4) for n in NAMES)
```

```python
import numpy as np
import jax
import jax.numpy as jnp
from jax import lax
from jax.experimental import pallas as pl
from jax.experimental.pallas import tpu as pltpu

F32 = jnp.float32
BF16 = jnp.bfloat16

D_MODEL = 1024
DEPTH = 2
RET_HEADS = 4
HEAD_DIM = 128
SSM_HEADS = 8
SSM_HEAD_DIM = 64
SSM_STATE = 128
SSM_CONV = 4
D_FF = 2816
ROPE_BASE = 10000.0
NORM_EPS = 1e-6
MIX_W = 1536
IN_W = 5128
IN_MAIN = 5120
IN_PAD = 5632
N_DEV = 8
LANE = 128

ADAM_LR = 0.001
ADAM_B1 = 0.9
ADAM_B2 = 0.999
ADAM_EPS = 1e-08
ADAM_WD = 0.01
ADAM_STEP = 10

TOKEN_TILE = 512
WGRAD_TOKENS = 2048
BIG_TOKEN_TILE = 4096
SEQ_BLOCK = 256
VMEM_LIMIT = 56 << 20
SB_SKIP = 120.0
SB_UNVISITED = -1e30

CB_RQ, CB_RK, CB_RV, CB_RG = 0, 4, 8, 12
CB_SQ, CB_SK, CB_SV = 16, 20, 24
CB_MZ, CB_XS, CB_BM, CB_CM, CB_DT = 28, 32, 36, 38, 40


def _cparams(*sem):
    return pltpu.CompilerParams(dimension_semantics=sem, vmem_limit_bytes=VMEM_LIMIT)


def _sds(shape, dtype):
    return jax.ShapeDtypeStruct(tuple(shape), dtype)


def _tile(n, *prefs):
    for p in prefs:
        if n % p == 0:
            return p
    return n


def _dot(a, b, dims):
    return lax.dot_general(a, b, (dims, ((), ())), preferred_element_type=F32)


NN = ((1,), (0,))
NT = ((1,), (1,))
TN = ((0,), (0,))


def _bf(x):
    return x.astype(BF16)


def _sigmoid(x):
    return jax.nn.sigmoid(x)


def _split2(x):
    hi = x.astype(BF16)
    lo = (x - hi.astype(F32)).astype(BF16)
    return hi, lo


def _split3(x):
    hi = x.astype(BF16)
    r = x - hi.astype(F32)
    mid = r.astype(BF16)
    lo = (r - mid.astype(F32)).astype(BF16)
    return hi, mid, lo


def matmul(a, b, *, ta=False, tb=False, tm=512, tn=512, tk=512, out_dtype=F32, name):
    M, K = (a.shape[1], a.shape[0]) if ta else a.shape
    N = b.shape[0] if tb else b.shape[1]
    tm, tn, tk = min(tm, M), min(tn, N), min(tk, K)
    assert M % tm == 0 and N % tn == 0 and K % tk == 0, (name, M, N, K, tm, tn, tk)
    nk = K // tk
    a_spec = pl.BlockSpec((tk, tm), lambda i, j, k: (k, i)) if ta else pl.BlockSpec((tm, tk), lambda i, j, k: (i, k))
    b_spec = pl.BlockSpec((tn, tk), lambda i, j, k: (j, k)) if tb else pl.BlockSpec((tk, tn), lambda i, j, k: (k, j))
    dims = ((0 if ta else 1,), (1 if tb else 0,))

    def body(a_ref, b_ref, o_ref, acc_ref):
        k = pl.program_id(2)
        p = _dot(_bf(a_ref[...]), _bf(b_ref[...]), dims)

        @pl.when(k == 0)
        def _():
            acc_ref[...] = p

        @pl.when(k > 0)
        def _():
            acc_ref[...] += p

        @pl.when(k == nk - 1)
        def _():
            o_ref[...] = acc_ref[...].astype(out_dtype)

    return pl.pallas_call(
        body, grid=(M // tm, N // tn, nk), in_specs=[a_spec, b_spec],
        out_specs=pl.BlockSpec((tm, tn), lambda i, j, k: (i, j)), out_shape=_sds((M, N), out_dtype),
        scratch_shapes=[pltpu.VMEM((tm, tn), F32)], name=name,
        compiler_params=_cparams("parallel", "parallel", "arbitrary"))(a, b)


def matmul_resid(a, w, h, gate, factor, *, name):
    M, K = a.shape
    N = w.shape[1]
    tm = min(TOKEN_TILE, M)

    def body(a_ref, w_ref, h_ref, g_ref, hn_ref, o_ref):
        out = _dot(a_ref[...], w_ref[...], NN)
        o_ref[...] = out
        hn_ref[...] = h_ref[...] + (factor * (1.0 + g_ref[...])) * out

    mn = pl.BlockSpec((tm, N), lambda i: (i, 0))
    return pl.pallas_call(
        body, grid=(M // tm,),
        in_specs=[pl.BlockSpec((tm, K), lambda i: (i, 0)), pl.BlockSpec((K, N), lambda i: (0, 0)), mn,
                  pl.BlockSpec((1, N), lambda i: (0, 0))],
        out_specs=[mn, mn], out_shape=[_sds((M, N), F32), _sds((M, N), F32)], name=name,
        compiler_params=_cparams("parallel"))(a, w, h, gate)


def ffn_up(u, wg_t, wu_t, *, name, gather=None):
    M, K = u.shape
    N = wg_t.shape[0]
    tm, tn = min(BIG_TOKEN_TILE, M), _tile(N, 256)
    grid = (M // tm, N // tn)
    extra = gather is not None

    def body(*refs):
        if extra:
            u_ref, wg_ref, wu_ref, src_ref, a_ref, b_ref, act_ref, dst_ref, send_sems, recv_sems, local_sem = refs
            finish_comm = _carried(_gather_two_level, (src_ref, dst_ref, send_sems, recv_sems, local_sem), *grid)
        else:
            u_ref, wg_ref, wu_ref, a_ref, b_ref, act_ref = refs
            finish_comm = lambda: None
        uu = u_ref[...]
        a = _dot(uu, wg_ref[...], NT)
        b = _dot(uu, wu_ref[...], NT)
        a_ref[...] = _bf(a)
        b_ref[...] = _bf(b)
        act_ref[...] = _bf(a * _sigmoid(a) * b)
        finish_comm()

    mn = pl.BlockSpec((tm, tn), lambda i, j: (i, j))
    wspec = pl.BlockSpec((tn, K), lambda i, j: (j, 0))
    hbm = pl.BlockSpec(memory_space=pltpu.HBM)
    return pl.pallas_call(
        body, grid=grid, in_specs=[pl.BlockSpec((tm, K), lambda i, j: (i, 0)), wspec, wspec] + [hbm] * extra,
        out_specs=[mn, mn, mn] + [hbm] * extra,
        out_shape=[_sds((M, N), BF16)] * 3 + ([_sds((N_DEV,) + gather.shape, gather.dtype)] if extra else []),
        scratch_shapes=COMM_SCRATCH * extra, name=name,
        compiler_params=_cparams("arbitrary", "arbitrary") if extra else _cparams("parallel", "parallel"))(
            *((u, wg_t, wu_t) + ((gather,) if extra else ())))


def ffn_dact(dout, wd, a, b, *, name):
    M, K = dout.shape
    N = wd.shape[0]
    tm, tn = min(BIG_TOKEN_TILE, M), _tile(N, 256)

    def body(d_ref, w_ref, a_ref, b_ref, da_ref, db_ref):
        dact = _dot(d_ref[...], w_ref[...], NT)
        av = a_ref[...].astype(F32)
        sg = _sigmoid(av)
        db_ref[...] = _bf(dact * av * sg)
        da_ref[...] = _bf(dact * b_ref[...].astype(F32) * (sg * (1.0 + av * (1.0 - sg))))

    mn = pl.BlockSpec((tm, tn), lambda i, j: (i, j))
    return pl.pallas_call(
        body, grid=(M // tm, N // tn),
        in_specs=[pl.BlockSpec((tm, K), lambda i, j: (i, 0)), pl.BlockSpec((tn, K), lambda i, j: (j, 0)), mn, mn],
        out_specs=[mn, mn], out_shape=[_sds((M, N), BF16), _sds((M, N), BF16)], name=name,
        compiler_params=_cparams("parallel", "parallel"))(dout, wd, a, b)


def norm_mod(h, gain, shift, scale, *, name):
    S, D = h.shape
    tm = min(TOKEN_TILE, S)

    def body(h_ref, g_ref, sh_ref, sc_ref, u_ref):
        x = h_ref[...]
        r = lax.rsqrt(jnp.mean(x * x, axis=-1, keepdims=True) + NORM_EPS)
        n = x * r * g_ref[...]
        u_ref[...] = _bf(n * (1.0 + sc_ref[...]) + sh_ref[...])

    row = pl.BlockSpec((1, D), lambda i: (0, 0))
    tile = pl.BlockSpec((tm, D), lambda i: (i, 0))
    return pl.pallas_call(body, grid=(S // tm,), in_specs=[tile, row, row, row], out_specs=tile,
                          out_shape=_sds((S, D), BF16), name=name, compiler_params=_cparams("parallel"))(h, gain, shift, scale)


def dgrad_norm_bwd(lhs, ws, spans, h, gain, scale, dres, *, name, exchange=None):
    S, D = h.shape
    tm = min(TOKEN_TILE, S)
    n, nw = len(lhs), len(ws)
    extra = exchange is not None

    def body(*refs):
        l_refs, w_refs = refs[:n], refs[n:n + nw]
        if extra:
            h_ref, g_ref, sc_ref, dres_ref, src_ref, dh_ref, st_ref, dst_ref, send_sems, recv_sems, local_sem = refs[n + nw:]
            finish_comm = _carried(_chip_exchange, (src_ref, dst_ref, send_sems, recv_sems, local_sem), S // tm)
        else:
            h_ref, g_ref, sc_ref, dres_ref, dh_ref, st_ref = refs[n + nw:]
            finish_comm = lambda: None
        du = None
        for lr, (k, r0) in zip(l_refs, spans):
            part = _dot(_bf(lr[...]), w_refs[k][r0:r0 + lr.shape[1], :], NN)
            du = part if du is None else du + part
        x = h_ref[...]
        g = g_ref[...]
        r = lax.rsqrt(jnp.mean(x * x, axis=-1, keepdims=True) + NORM_EPS)
        xhat = x * r
        dn = du * (1.0 + sc_ref[...])
        dxhat = dn * g
        dh_ref[...] = dres_ref[...] + r * (dxhat - xhat * jnp.mean(dxhat * xhat, axis=-1, keepdims=True))

        @pl.when(pl.program_id(0) == 0)
        def _():
            st_ref[...] = jnp.zeros_like(st_ref)

        st_ref[0:1, :] += jnp.sum(dn * xhat, axis=0, keepdims=True)
        st_ref[1:2, :] += jnp.sum(du, axis=0, keepdims=True)
        st_ref[2:3, :] += jnp.sum(du * (xhat * g), axis=0, keepdims=True)
        finish_comm()

    row = pl.BlockSpec((1, D), lambda i: (0, 0))
    tile = pl.BlockSpec((tm, D), lambda i: (i, 0))
    hbm = pl.BlockSpec(memory_space=pltpu.HBM)
    in_specs = [pl.BlockSpec((tm, l.shape[1]), lambda i: (i, 0)) for l in lhs]
    in_specs += [pl.BlockSpec(w.shape, lambda i: (0, 0)) for w in ws]
    in_specs += [tile, row, row, tile] + [hbm] * extra
    return pl.pallas_call(
        body, grid=(S // tm,), in_specs=in_specs, out_specs=[tile, pl.BlockSpec((8, D), lambda i: (0, 0))] + [hbm] * extra,
        out_shape=[_sds((S, D), F32), _sds((8, D), F32)] + ([_sds(exchange.shape, exchange.dtype)] if extra else []),
        scratch_shapes=COMM_SCRATCH * extra, name=name,
        compiler_params=_cparams("arbitrary"))(*lhs, *ws, h, gain, scale, dres, *((exchange,) if extra else ()))


def gate_bwd(dh, out, gate, factor, *, name):
    S, D = dh.shape
    tm = min(TOKEN_TILE, S)

    def body(dh_ref, o_ref, g_ref, do_ref, st_ref):
        d = dh_ref[...]
        do_ref[...] = _bf(d * (factor * (1.0 + g_ref[...])))

        @pl.when(pl.program_id(0) == 0)
        def _():
            st_ref[...] = jnp.zeros_like(st_ref)

        st_ref[0:1, :] += factor * jnp.sum(d * o_ref[...], axis=0, keepdims=True)

    tile = pl.BlockSpec((tm, D), lambda i: (i, 0))
    return pl.pallas_call(
        body, grid=(S // tm,), in_specs=[tile, tile, pl.BlockSpec((1, D), lambda i: (0, 0))],
        out_specs=[tile, pl.BlockSpec((8, D), lambda i: (0, 0))], out_shape=[_sds((S, D), BF16), _sds((8, D), F32)],
        name=name, compiler_params=_cparams("arbitrary"))(dh, out, gate)


def final_loss_bwd(h, gain, shift, scale, target, *, name):
    S, D = h.shape
    tm = min(TOKEN_TILE, S)

    def body(h_ref, g_ref, sh_ref, sc_ref, t_ref, dh_ref, st_ref):
        x = h_ref[...]
        g = g_ref[...]
        r = lax.rsqrt(jnp.mean(x * x, axis=-1, keepdims=True) + NORM_EPS)
        xhat = x * r
        n = xhat * g
        err = n * (1.0 + sc_ref[...]) + sh_ref[...] - t_ref[...]
        dy = err * (1.0 / D)
        dn = dy * (1.0 + sc_ref[...])
        dxhat = dn * g
        dh_ref[...] = r * (dxhat - xhat * jnp.mean(dxhat * xhat, axis=-1, keepdims=True))

        @pl.when(pl.program_id(0) == 0)
        def _():
            st_ref[...] = jnp.zeros_like(st_ref)

        st_ref[0:1, :] += jnp.sum(dn * xhat, axis=0, keepdims=True)
        st_ref[1:2, :] += jnp.sum(dy, axis=0, keepdims=True)
        st_ref[2:3, :] += jnp.sum(dy * n, axis=0, keepdims=True)
        tok = jnp.mean(err * err, axis=-1, keepdims=True)
        st_ref[3:4, :] += 0.5 * jnp.sum(tok, axis=0, keepdims=True)

    row = pl.BlockSpec((1, D), lambda i: (0, 0))
    tile = pl.BlockSpec((tm, D), lambda i: (i, 0))
    return pl.pallas_call(
        body, grid=(S // tm,), in_specs=[tile, row, row, row, tile], out_specs=[tile, pl.BlockSpec((8, D), lambda i: (0, 0))],
        out_shape=[_sds((S, D), F32), _sds((8, D), F32)], name=name,
        compiler_params=_cparams("arbitrary"))(h, gain, shift, scale, target)


def _ret_tables(T):
    heads = np.arange(RET_HEADS, dtype=np.float64)
    lg = np.log1p(-(2.0 ** (-5.0 - heads)))
    t = np.arange(T)
    same = (t[:, None] // 64) == (t[None, :] // 64)
    earlier = (t[None, :] // 64) < (t[:, None] // 64)
    dist = np.abs(t[:, None] - t[None, :]).astype(np.float64)
    dmat = np.where(same | earlier, np.exp(lg[:, None, None] * dist[None]), 0.0)
    qdec = np.exp(lg[:, None] * (t + 1.0)[None, :])
    kdec = np.exp(lg[:, None] * (T - 1.0 - t)[None, :])
    cdec = np.exp(lg * T)
    bc = lambda v: jnp.asarray(np.broadcast_to(v[:, :, None], (RET_HEADS, T, LANE)), F32)
    cd = jnp.asarray(np.broadcast_to(cdec[:, None, None], (RET_HEADS, LANE, LANE)), F32)
    return jnp.asarray(dmat, F32), bc(qdec), bc(kdec), cd


def _rope_tables(S):
    half = HEAD_DIM // 2
    inv_freq = ROPE_BASE ** (-jnp.arange(half, dtype=F32) / half)
    ang = jnp.arange(S, dtype=F32)[:, None] * inv_freq[None, :]
    cos, sin = jnp.cos(ang), jnp.sin(ang)
    return jnp.concatenate([cos, cos], axis=-1), jnp.concatenate([-sin, sin], axis=-1)


def _rope(x, c, s):
    return x * c + pltpu.roll(x, HEAD_DIM // 2, 1) * s


def _rope_t(dx, c, s):
    return dx * c + pltpu.roll(dx * s, HEAD_DIM // 2, 1)


def ret_fwd(proj, gn, cos, sin, *, name):
    S = proj.shape[0]
    T = min(SEQ_BLOCK, S)
    nb = S // T
    dmat, qdec, kdec, cdec = _ret_tables(T)

    def body(q_ref, k_ref, v_ref, g_ref, c_ref, s_ref, dm_ref, qd_ref, kd_ref, cd_ref, gn_ref, yo_ref, yp_ref, st_ref, state):
        @pl.when(pl.program_id(0) == 0)
        def _():
            state[...] = jnp.zeros_like(state)

        c, s = c_ref[...], s_ref[...]
        for h in range(RET_HEADS):
            cols = slice(LANE * h, LANE * (h + 1))
            qr = _rope(q_ref[:, cols], c, s)
            kr = _rope(k_ref[:, cols], c, s) * (HEAD_DIM ** -0.5)
            v = _bf(v_ref[:, cols])
            sp = state[h]
            st_ref[h] = sp
            a = _dot(_bf(qr), _bf(kr), NT) * dm_ref[h]
            y = _dot(_bf(a), v, NN) + _dot(_bf(qr * qd_ref[h]), _bf(sp), NN)
            state[h] = cd_ref[h] * sp + _dot(_bf(kr * kd_ref[h]), v, TN)
            yp_ref[:, cols] = y
            yn = y * lax.rsqrt(jnp.mean(y * y, axis=-1, keepdims=True) + NORM_EPS) * gn_ref[:, cols]
            g = g_ref[:, cols]
            yo_ref[:, cols] = _bf(yn * (g * _sigmoid(g)))

    col = lambda cb: pl.BlockSpec((T, 512), lambda b: (b, cb // 4))
    tok = pl.BlockSpec((T, LANE), lambda b: (b, 0))
    const = lambda shape: pl.BlockSpec(shape, lambda b: (0,) * len(shape))
    out_tok = pl.BlockSpec((T, 512), lambda b: (b, 0))
    return pl.pallas_call(
        body, grid=(nb,),
        in_specs=[col(CB_RQ), col(CB_RK), col(CB_RV), col(CB_RG), tok, tok, const(dmat.shape), const(qdec.shape), const(kdec.shape),
                  const(cdec.shape), const((1, 512))],
        out_specs=[out_tok, out_tok, pl.BlockSpec((None, RET_HEADS, LANE, LANE), lambda b: (b, 0, 0, 0))],
        out_shape=[_sds((S, 512), BF16), _sds((S, 512), F32), _sds((nb, RET_HEADS, LANE, LANE), F32)],
        scratch_shapes=[pltpu.VMEM((RET_HEADS, LANE, LANE), F32)], name=name,
        compiler_params=_cparams("arbitrary"))(proj, proj, proj, proj, cos, sin, dmat, qdec, kdec, cdec, gn)


def ret_bwd(proj, gn, cos, sin, ypre, states, dycat, *, name):
    S = proj.shape[0]
    T = min(SEQ_BLOCK, S)
    nb = S // T
    dmat, qdec, kdec, cdec = _ret_tables(T)

    def body(q_ref, k_ref, v_ref, g_ref, c_ref, s_ref, dm_ref, qd_ref, kd_ref, cd_ref, gn_ref, yp_ref, st_ref, dy_ref,
             d_ref, stat_ref, gstate):
        @pl.when(pl.program_id(0) == 0)
        def _():
            gstate[...] = jnp.zeros_like(gstate)
            stat_ref[...] = jnp.zeros_like(stat_ref)

        c, s = c_ref[...], s_ref[...]
        scale = HEAD_DIM ** -0.5
        for h in range(RET_HEADS):
            cols = slice(LANE * h, LANE * (h + 1))
            qr = _rope(q_ref[:, cols], c, s)
            kr = _rope(k_ref[:, cols], c, s) * scale
            v = _bf(v_ref[:, cols])
            qd, kd, dm = qd_ref[h], kd_ref[h], dm_ref[h]
            sp = _bf(st_ref[h])
            gs = gstate[h]
            gsb = _bf(gs)
            g = g_ref[:, cols]
            sg = _sigmoid(g)
            y = yp_ref[:, cols]
            gn_row = gn_ref[:, cols]
            r = lax.rsqrt(jnp.mean(y * y, axis=-1, keepdims=True) + NORM_EPS)
            yhat = y * r
            dyo = dy_ref[:, cols]
            d_ref[:, 1536 + LANE * h:1536 + LANE * (h + 1)] = _bf(dyo * (yhat * gn_row) * (sg * (1.0 + g * (1.0 - sg))))
            dyn = dyo * (g * sg)
            stat_ref[0:1, cols] += jnp.sum(dyn * yhat, axis=0, keepdims=True)
            dyhat = dyn * gn_row
            dy = _bf(r * (dyhat - yhat * jnp.mean(dyhat * yhat, axis=-1, keepdims=True)))
            qrb, krb = _bf(qr), _bf(kr)
            qdb = _bf(qr * qd)
            kdb = _bf(kr * kd)
            a = _bf(_dot(qrb, krb, NT) * dm)
            da = _bf(_dot(dy, v, NT) * dm)
            d_ref[:, 1024 + LANE * h:1024 + LANE * (h + 1)] = _bf(_dot(a, dy, TN) + _dot(kdb, gsb, NN))
            dqr = _dot(da, krb, NN) + qd * _dot(dy, sp, NT)
            dkr = _dot(da, qrb, TN) + kd * _dot(v, gsb, NT)
            gstate[h] = cd_ref[h] * gs + _dot(qdb, dy, TN)
            d_ref[:, cols] = _bf(_rope_t(dqr, c, s))
            d_ref[:, 512 + LANE * h:512 + LANE * (h + 1)] = _bf(_rope_t(dkr * scale, c, s))

    rb = lambda b: nb - 1 - b
    col = lambda cb: pl.BlockSpec((T, 512), lambda b: (rb(b), cb // 4))
    tok = pl.BlockSpec((T, LANE), lambda b: (rb(b), 0))
    const = lambda shape: pl.BlockSpec(shape, lambda b: (0,) * len(shape))
    tok512 = pl.BlockSpec((T, 512), lambda b: (rb(b), 0))
    return pl.pallas_call(
        body, grid=(nb,),
        in_specs=[col(CB_RQ), col(CB_RK), col(CB_RV), col(CB_RG), tok, tok, const(dmat.shape), const(qdec.shape), const(kdec.shape),
                  const(cdec.shape), const((1, 512)), tok512,
                  pl.BlockSpec((None, RET_HEADS, LANE, LANE), lambda b: (rb(b), 0, 0, 0)), tok512],
        out_specs=[pl.BlockSpec((T, 2048), lambda b: (rb(b), 0)), const((8, 512))],
        out_shape=[_sds((S, 2048), BF16), _sds((8, 512), F32)],
        scratch_shapes=[pltpu.VMEM((RET_HEADS, LANE, LANE), F32)], name=name,
        compiler_params=_cparams("arbitrary"))(proj, proj, proj, proj, cos, sin, dmat, qdec, kdec, cdec, gn, ypre, states, dycat)


def _sb_cast_kv(k_ref, v_ref, kb, vb, S):
    step = min(TOKEN_TILE, S)
    for r in range(0, S, step):
        kb[r:r + step, :] = _bf(k_ref[r:r + step, :])
        vb[r:r + step, :] = _bf(v_ref[r:r + step, :])


def _sb_logits(q, kblk, vis):
    z = _dot(q, kblk, NT) * (HEAD_DIM ** -0.5)
    l = jnp.log1p(jnp.exp(-jnp.abs(z)))
    lb = jnp.minimum(z, 0.0) - l
    lk = jnp.minimum(-z, 0.0) - l
    if vis is not None:
        lk = jnp.where(vis, lk, 0.0)
    return lb, lk


def _tri(T, cmp):
    r = lax.broadcasted_iota(jnp.int32, (T, T), 0)
    c = lax.broadcasted_iota(jnp.int32, (T, T), 1)
    return cmp(r, c)


def _dot_split2(x, m):
    hi, lo = _split2(x)
    return _dot(hi, m, NN) + _dot(lo, m, NN)


def _carried(carry_fn, refs, *grid):
    ids = [pl.program_id(d) for d in range(len(grid))]
    first, last = ids[0] == 0, ids[0] == grid[0] - 1
    for d in range(1, len(grid)):
        first, last = first & (ids[d] == 0), last & (ids[d] == grid[d] - 1)

    @pl.when(first)
    def _():
        carry_fn(*refs, start=True, finish=False)

    def finish():
        @pl.when(last)
        def _():
            carry_fn(*refs, start=False, finish=True)

    return finish


def sb_fwd(proj, *, name, gather=None):
    S = proj.shape[0]
    T = min(SEQ_BLOCK, S)
    nq = S // T

    assert nq <= LANE

    def body(*refs):
        if gather is None:
            q_ref, k_ref, v_ref, o_ref, cin_ref, kb, vb = refs
            finish_comm = lambda: None
        else:
            q_ref, k_ref, v_ref, src_ref, o_ref, cin_ref, dst_ref, kb, vb, send_sems, recv_sems, local_sem = refs
            finish_comm = _carried(_gather_two_level, (src_ref, dst_ref, send_sems, recv_sems, local_sem), RET_HEADS, nq)
        qi = pl.program_id(1)

        @pl.when(qi == 0)
        def _():
            _sb_cast_kv(k_ref, v_ref, kb, vb, S)

        q = _bf(q_ref[...])
        vis = _tri(T, lambda t, s: s < t)
        after = _tri(T, lambda j, s: j > s).astype(BF16)
        lane = lax.broadcasted_iota(jnp.int32, (T, LANE), 1)

        def block(jb, carry, acc, cin, mask):
            rows = pl.ds(pl.multiple_of(jb * T, T), T)
            lb, lk = _sb_logits(q, kb[rows, :], mask)
            tail = _dot_split2(lk, after) + carry
            w = jnp.exp(lb + tail)
            if mask is not None:
                w = jnp.where(mask, w, 0.0)
            return (carry + jnp.sum(lk, axis=1, keepdims=True), acc + _dot(_bf(w), vb[rows, :], NN),
                    jnp.where(lane == jb, carry, cin))

        st = block(qi, jnp.zeros((T, 1), F32), jnp.zeros((T, LANE), F32), jnp.full((T, LANE), SB_UNVISITED, F32), vis)

        def more(c):
            return (c[0] < qi) & (jnp.max(c[1]) > -SB_SKIP)

        def step(c):
            return (c[0] + 1,) + block(qi - 1 - c[0], c[1], c[2], c[3], None)

        st = lax.while_loop(more, step, (jnp.int32(0),) + st)
        o_ref[...] = st[2]
        cin_ref[...] = st[3]
        finish_comm()

    whole = lambda cb: pl.BlockSpec((S, LANE), lambda h, i: (0, cb + h))
    tok = pl.BlockSpec((T, LANE), lambda h, i: (i, h))
    hbm = pl.BlockSpec(memory_space=pltpu.HBM)
    extra = gather is not None
    return pl.pallas_call(
        body, grid=(RET_HEADS, nq),
        in_specs=[pl.BlockSpec((T, LANE), lambda h, i: (i, CB_SQ + h)), whole(CB_SK), whole(CB_SV)] + [hbm] * extra,
        out_specs=[tok, tok] + [hbm] * extra,
        out_shape=[_sds((S, 512), F32), _sds((S, 512), F32)] + ([_sds((N_DEV,) + gather.shape, gather.dtype)] if extra else []),
        scratch_shapes=[pltpu.VMEM((S, LANE), BF16), pltpu.VMEM((S, LANE), BF16)] + COMM_SCRATCH * extra, name=name,
        compiler_params=_cparams("arbitrary", "arbitrary"))(*((proj, proj, proj) + ((gather,) if extra else ())))


def sb_bwd(proj, cin, dycat, *, name, exchange=None):
    S = proj.shape[0]
    T = min(SEQ_BLOCK, S)
    nq = S // T
    scale = HEAD_DIM ** -0.5

    def body(*refs):
        if exchange is None:
            q_ref, k_ref, v_ref, cin_ref, do_ref, dq_ref, dk_ref, dv_ref, kb, vb = refs
            finish_comm = lambda: None
        else:
            (q_ref, k_ref, v_ref, cin_ref, do_ref, src_ref, dq_ref, dk_ref, dv_ref, dst_ref, kb, vb,
             send_sems, recv_sems, local_sem) = refs
            finish_comm = _carried(_chip_exchange, (src_ref, dst_ref, send_sems, recv_sems, local_sem), RET_HEADS, nq)
        qi = pl.program_id(1)

        @pl.when(qi == 0)
        def _():
            _sb_cast_kv(k_ref, v_ref, kb, vb, S)
            dk_ref[...] = jnp.zeros_like(dk_ref)
            dv_ref[...] = jnp.zeros_like(dv_ref)

        q = _bf(q_ref[...])
        dob = _bf(do_ref[...])
        cin = cin_ref[...]
        vis = _tri(T, lambda t, s: s < t)
        after = _tri(T, lambda j, s: j > s).astype(BF16)
        before = _tri(T, lambda s, j: s < j).astype(BF16)
        lane = lax.broadcasted_iota(jnp.int32, (T, LANE), 1)

        def block(jb, ecarry, dq, mask):
            rows = pl.ds(pl.multiple_of(jb * T, T), T)
            kblk, vblk = kb[rows, :], vb[rows, :]
            lb, lk = _sb_logits(q, kblk, mask)
            carry = jnp.sum(jnp.where(lane == jb, cin, 0.0), axis=1, keepdims=True)
            w = jnp.exp(lb + _dot_split2(lk, after) + carry)
            if mask is not None:
                w = jnp.where(mask, w, 0.0)
            e = w * _dot(dob, vblk, NT)
            dv_ref[rows, :] += _dot(_bf(w), dob, TN)
            dlk = _dot_split2(e, before) + ecarry
            beta = jnp.exp(lb)
            dz = e * (1.0 - beta) - beta * dlk
            if mask is not None:
                dz = jnp.where(mask, dz, 0.0)
            dzb = _bf(dz * scale)
            dk_ref[rows, :] += _dot(dzb, q, TN)
            return ecarry + jnp.sum(e, axis=1, keepdims=True), dq + _dot(dzb, kblk, NN)

        lane1 = lane[0:1, :]
        skipped = (jnp.max(cin, axis=0, keepdims=True) <= -SB_SKIP) & (lane1 < qi)
        first = jnp.sum(jnp.where(skipped, 1, 0))
        st = lax.fori_loop(first, qi, lambda jb, c: block(jb, c[0], c[1], None), (jnp.zeros((T, 1), F32), jnp.zeros((T, LANE), F32)))
        st = block(qi, st[0], st[1], vis)
        dq_ref[...] = _bf(st[1])
        finish_comm()

    whole = lambda cb: pl.BlockSpec((S, LANE), lambda h, i: (0, cb + h))
    tok = pl.BlockSpec((T, LANE), lambda h, i: (i, h))
    acc = pl.BlockSpec((S, LANE), lambda h, i: (0, h))
    hbm = pl.BlockSpec(memory_space=pltpu.HBM)
    extra = exchange is not None
    return pl.pallas_call(
        body, grid=(RET_HEADS, nq),
        in_specs=[pl.BlockSpec((T, LANE), lambda h, i: (i, CB_SQ + h)), whole(CB_SK), whole(CB_SV), tok,
                  pl.BlockSpec((T, LANE), lambda h, i: (i, 4 + h))] + [hbm] * extra,
        out_specs=[tok, acc, acc] + [hbm] * extra,
        out_shape=[_sds((S, 512), BF16), _sds((S, 512), F32), _sds((S, 512), F32)] + ([_sds(exchange.shape, exchange.dtype)] if extra else []),
        scratch_shapes=[pltpu.VMEM((S, LANE), BF16), pltpu.VMEM((S, LANE), BF16)] + COMM_SCRATCH * extra, name=name,
        compiler_params=_cparams("arbitrary", "arbitrary"))(*((proj, proj, proj, cin, dycat) + ((exchange,) if extra else ())))


def _shift_down(x, d, row):
    return jnp.where(row >= d, pltpu.roll(x, d, 0), 0.0)


def _shift_up(x, d, row, S):
    return jnp.where(row < S - d, pltpu.roll(x, S - d, 0), 0.0)


def conv_fwd(proj, conv_w, conv_b, *, name):
    S = proj.shape[0]

    def body(x_ref, w_ref, b_ref, pre_ref, act_ref):
        x = x_ref[...]
        row = lax.broadcasted_iota(jnp.int32, x.shape, 0)
        pre = b_ref[...] + w_ref[3:4, :] * x
        for d in range(1, SSM_CONV):
            pre = pre + w_ref[3 - d:4 - d, :] * _shift_down(x, d, row)
        pre_ref[...] = pre
        act_ref[...] = pre * _sigmoid(pre)

    blk = pl.BlockSpec((S, LANE), lambda c: (0, c))
    return pl.pallas_call(
        body, grid=(8,),
        in_specs=[pl.BlockSpec((S, LANE), lambda c: (0, CB_XS + c)), pl.BlockSpec((SSM_CONV, LANE), lambda c: (0, c)),
                  pl.BlockSpec((1, LANE), lambda c: (0, c))],
        out_specs=[blk, blk], out_shape=[_sds((S, 1024), F32), _sds((S, 1024), F32)], name=name,
        compiler_params=_cparams("parallel"))(proj, conv_w, conv_b)


def conv_bwd(proj, pre, dact, conv_w, *, name):
    S = proj.shape[0]

    def body(x_ref, pre_ref, da_ref, w_ref, dx_ref, st_ref):
        x = x_ref[...]
        p = pre_ref[...]
        row = lax.broadcasted_iota(jnp.int32, x.shape, 0)
        sg = _sigmoid(p)
        dpre = da_ref[...] * (sg * (1.0 + p * (1.0 - sg)))
        dx = w_ref[3:4, :] * dpre
        st_ref[3:4, :] = jnp.sum(dpre * x, axis=0, keepdims=True)
        for d in range(1, SSM_CONV):
            dx = dx + w_ref[3 - d:4 - d, :] * _shift_up(dpre, d, row, S)
            st_ref[3 - d:4 - d, :] = jnp.sum(dpre * _shift_down(x, d, row), axis=0, keepdims=True)
        st_ref[4:5, :] = jnp.sum(dpre, axis=0, keepdims=True)
        st_ref[5:8, :] = jnp.zeros((3, LANE), F32)
        dx_ref[...] = _bf(dx)

    blk = pl.BlockSpec((S, LANE), lambda c: (0, c))
    return pl.pallas_call(
        body, grid=(8,),
        in_specs=[pl.BlockSpec((S, LANE), lambda c: (0, CB_XS + c)), blk, blk, pl.BlockSpec((SSM_CONV, LANE), lambda c: (0, c))],
        out_specs=[blk, pl.BlockSpec((8, LANE), lambda c: (0, c))],
        out_shape=[_sds((S, 1024), BF16), _sds((8, 1024), F32)], name=name,
        compiler_params=_cparams("parallel"))(proj, pre, dact, conv_w)


def _softplus(x):
    return jnp.maximum(x, 0.0) + jnp.log1p(jnp.exp(-jnp.abs(x)))


def _pair(lane, v0, v1):
    return jnp.where(lane < SSM_HEAD_DIM, v0, v1)


def _ssd_pair_common(raw, dtb, alog, xs, cm, hprev, T):
    lane = lax.broadcasted_iota(jnp.int32, (T, LANE), 1)
    dt = _softplus(raw + dtb)
    a = -jnp.exp(alog)
    incl = _tri(T, lambda l, s: s <= l).astype(BF16)
    h1, h2, h3 = _split3(dt * a)
    acum = _dot(incl, h1, NN) + _dot(incl, h2, NN) + _dot(incl, h3, NN)
    acum_t = acum.T
    causal = _tri(T, lambda l, s: s <= l)
    decay = [jnp.where(causal, jnp.exp(jnp.minimum(acum[:, j:j + 1] - acum_t[j:j + 1, :], 0.0)), 0.0) for j in (0, 1)]
    dtc = _pair(lane, dt[:, 0:1], dt[:, 1:2])
    ac = _pair(lane, acum[:, 0:1], acum[:, 1:2])
    xdt = xs * dtc
    ea = jnp.exp(ac)
    e_end = jnp.exp(ac[T - 1:T, :] - ac)
    sub = lax.broadcasted_iota(jnp.int32, (LANE, LANE), 0)
    cd = jnp.where(sub < SSM_HEAD_DIM, jnp.exp(acum[T - 1:T, 0:1]), jnp.exp(acum[T - 1:T, 1:2]))
    r = _dot(cm, _bf(hprev), NT)
    return lane, dt, a, acum, decay, dtc, xdt, ea, e_end, cd, r


def ssd_fwd(xact, proj, dtb, alog, dskip, *, name):
    S = xact.shape[0]
    T = min(SEQ_BLOCK, S)
    nb = S // T

    def body(xs_ref, bm_ref, cm_ref, dt_ref, dtb_ref, al_ref, ds_ref, y_ref, st_ref, state):
        @pl.when(pl.program_id(0) == 0)
        def _():
            state[...] = jnp.zeros_like(state)

        for g in range(2):
            bm, cm = _bf(bm_ref[:, LANE * g:LANE * (g + 1)]), _bf(cm_ref[:, LANE * g:LANE * (g + 1)])
            gm = _dot(cm, bm, NT)
            for i in range(2):
                p = 2 * g + i
                cols = slice(LANE * p, LANE * (p + 1))
                xs = xs_ref[:, cols]
                hprev = state[p]
                st_ref[g, i] = hprev
                lane, dt, a, acum, decay, dtc, xdt, ea, e_end, cd, r = _ssd_pair_common(
                    dt_ref[:, cols], dtb_ref[p], al_ref[p], xs, cm, hprev, T)
                xdtb = _bf(xdt)
                y_intra = _pair(lane, _dot(_bf(gm * decay[0]), xdtb, NN), _dot(_bf(gm * decay[1]), xdtb, NN))
                state[p] = cd * hprev + _dot(_bf(xdt * e_end), bm, TN)
                dsk = ds_ref[p]
                lane1 = lane[0:1, :]
                y_ref[:, cols] = y_intra + ea * r + _pair(lane1, dsk[:, 0:1], dsk[:, 1:2]) * xs

    rows = pl.BlockSpec((4, 1, LANE), lambda b: (0, 0, 0))
    return pl.pallas_call(
        body, grid=(nb,),
        in_specs=[pl.BlockSpec((T, 512), lambda b: (b, 0)), pl.BlockSpec((T, 256), lambda b: (b, 2)), pl.BlockSpec((T, 256), lambda b: (b, 3)),
                  pl.BlockSpec((T, 512), lambda b: (b, CB_DT // 4)), rows, rows, rows],
        out_specs=[pl.BlockSpec((T, 512), lambda b: (b, 0)),
                   pl.BlockSpec((2, None, 2, LANE, LANE), lambda b: (0, b, 0, 0, 0))],
        out_shape=[_sds((S, 512), F32), _sds((2, nb, 2, LANE, LANE), F32)],
        scratch_shapes=[pltpu.VMEM((4, LANE, LANE), F32)], name=name,
        compiler_params=_cparams("arbitrary"))(xact, xact, xact, proj, dtb, alog, dskip)


def ssd_bwd(xact, proj, dtb, alog, dskip, states, dy, *, name):
    S = xact.shape[0]
    T = min(SEQ_BLOCK, S)
    nb = S // T

    def body(xs_ref, bm_ref, cm_ref, dt_ref, dtb_ref, al_ref, ds_ref, st_ref, dy_ref,
             dxs_ref, dbm_ref, dcm_ref, ddt_ref, stat_ref, dstate):
        @pl.when(pl.program_id(0) == 0)
        def _():
            dstate[...] = jnp.zeros_like(dstate)
            stat_ref[...] = jnp.zeros_like(stat_ref)

        for g in range(2):
            wide = (slice(None), slice(256 * g, 256 * (g + 1)))
            narrow = (slice(None), slice(LANE * g, LANE * (g + 1)))
            pair = slice(2 * g, 2 * g + 2)
            group(xs_ref.at[wide], bm_ref.at[narrow], cm_ref.at[narrow], dt_ref.at[:, LANE * 2 * g:LANE * (2 * g + 1)],
                  dt_ref.at[:, LANE * (2 * g + 1):LANE * (2 * g + 2)], dtb_ref.at[pair], al_ref.at[pair], ds_ref.at[pair], st_ref.at[g],
                  dy_ref.at[wide], dxs_ref.at[wide], dbm_ref.at[narrow], dcm_ref.at[narrow], ddt_ref.at[wide], stat_ref.at[pair],
                  dstate.at[pair])

    def group(xs_ref, bm_ref, cm_ref, dt0_ref, dt1_ref, dtb_ref, al_ref, ds_ref, st_ref, dy_ref,
              dxs_ref, dbm_ref, dcm_ref, ddt_ref, stat_ref, dstate):
        bm, cm = _bf(bm_ref[...]), _bf(cm_ref[...])
        gm = _dot(cm, bm, NT)
        dbm = jnp.zeros((T, LANE), F32)
        dcm = jnp.zeros((T, LANE), F32)
        after_eq = _tri(T, lambda i, l: l >= i).astype(BF16)
        rowi = lax.broadcasted_iota(jnp.int32, (T, 1), 0)
        for i, dt_ref in enumerate((dt0_ref, dt1_ref)):
            xs = xs_ref[:, LANE * i:LANE * (i + 1)]
            dyp = dy_ref[:, LANE * i:LANE * (i + 1)]
            hprev = st_ref[i]
            dh = dstate[i]
            raw = dt_ref[...]
            lane, dt, a, acum, decay, dtc, xdt, ea, e_end, cd, r = _ssd_pair_common(
                raw, dtb_ref[i], al_ref[i], xs, cm, hprev, T)
            lane1 = lane[0:1, :]
            dsk = ds_ref[i]
            dskp = _pair(lane1, dsk[:, 0:1], dsk[:, 1:2])
            head = [lane < SSM_HEAD_DIM, lane >= SSM_HEAD_DIM]
            hsum = lambda v, j: jnp.sum(jnp.where(head[j], v, 0.0), axis=1, keepdims=True)
            dhb = _bf(dh)
            xdtb = _bf(xdt)
            dyb = _bf(dyp)
            z = xdt * e_end
            dz = _dot(bm, dhb, NT)
            dbm = dbm + _dot(_bf(z), dhb, NN)
            dxdt = dz * e_end
            de_e = dz * z
            drr = dyp * ea
            dea_ea = drr * r
            dcm = dcm + _dot(_bf(drr), _bf(hprev), NN)
            dstate[i] = cd * dh + _dot(_bf(drr), cm, TN)
            dcd_cd = cd * dh * hprev
            dgs = jnp.zeros((T, T), F32)
            da_cols = []
            for j in (0, 1):
                w = gm * decay[j]
                dw = _dot(_bf(jnp.where(head[j], dyp, 0.0)), xdtb, NT)
                dxdt = dxdt + jnp.where(head[j], _dot(_bf(w), dyb, TN), 0.0)
                dgs = dgs + dw * decay[j]
                dseg = dw * w
                col = jnp.sum(dseg, axis=1, keepdims=True) - jnp.sum(dseg.T, axis=1, keepdims=True)
                col = col + hsum(dea_ea, j) - hsum(de_e, j)
                sub = lax.broadcasted_iota(jnp.int32, (LANE, LANE), 0)
                in_head = (sub < SSM_HEAD_DIM) if j == 0 else (sub >= SSM_HEAD_DIM)
                end = jnp.sum(hsum(de_e, j), axis=0, keepdims=True) + jnp.sum(
                    jnp.sum(jnp.where(in_head, dcd_cd, 0.0), axis=1, keepdims=True), axis=0, keepdims=True)
                da_cols.append(col + jnp.where(rowi == T - 1, end, 0.0))
            dgb = _bf(dgs)
            dcm = dcm + _dot(dgb, bm, NN)
            dbm = dbm + _dot(dgb, cm, TN)
            dacum = jnp.where(lane == 0, da_cols[0], jnp.where(lane == 1, da_cols[1], 0.0))
            h1, h2, h3 = _split3(dacum)
            ddta = _dot(after_eq, h1, NN) + _dot(after_eq, h2, NN) + _dot(after_eq, h3, NN)
            dxs_ref[:, LANE * i:LANE * (i + 1)] = dskp * dyp + dxdt * dtc
            dx_x = dxdt * xs
            ddt = ddta * a + jnp.where(lane == 0, hsum(dx_x, 0), jnp.where(lane == 1, hsum(dx_x, 1), 0.0))
            ddraw = jnp.where(lane < 2, ddt * _sigmoid(raw + dtb_ref[i]), 0.0)
            ddt_ref[:, LANE * i:LANE * (i + 1)] = _bf(ddraw)
            dsum = jnp.sum(dyp * xs, axis=0, keepdims=True)
            d0 = jnp.sum(jnp.where(lane1 < SSM_HEAD_DIM, dsum, 0.0), axis=1, keepdims=True)
            d1 = jnp.sum(jnp.where(lane1 >= SSM_HEAD_DIM, dsum, 0.0), axis=1, keepdims=True)
            dd = jnp.where(lane1 == 0, d0, jnp.where(lane1 == 1, d1, 0.0))
            stat_ref[i, 0:1, :] += jnp.sum(ddraw, axis=0, keepdims=True)
            stat_ref[i, 1:2, :] += jnp.where(lane1 < 2, jnp.sum(ddta * dt, axis=0, keepdims=True) * a, 0.0)
            stat_ref[i, 2:3, :] += dd
        dbm_ref[...] = dbm
        dcm_ref[...] = dcm

    rb = lambda b: nb - 1 - b
    rows = pl.BlockSpec((4, 1, LANE), lambda b: (0, 0, 0))
    tok512 = pl.BlockSpec((T, 512), lambda b: (rb(b), 0))
    tok256 = pl.BlockSpec((T, 256), lambda b: (rb(b), 0))
    return pl.pallas_call(
        body, grid=(nb,),
        in_specs=[tok512, pl.BlockSpec((T, 256), lambda b: (rb(b), 2)), pl.BlockSpec((T, 256), lambda b: (rb(b), 3)),
                  pl.BlockSpec((T, 512), lambda b: (rb(b), CB_DT // 4)), rows, rows, rows,
                  pl.BlockSpec((2, None, 2, LANE, LANE), lambda b: (0, rb(b), 0, 0, 0)), tok512],
        out_specs=[tok512, tok256, tok256, tok512, pl.BlockSpec((4, 8, LANE), lambda b: (0, 0, 0))],
        out_shape=[_sds((S, 512), F32), _sds((S, 256), F32), _sds((S, 256), F32), _sds((S, 512), BF16), _sds((4, 8, LANE), F32)],
        scratch_shapes=[pltpu.VMEM((4, LANE, LANE), F32)], name=name,
        compiler_params=_cparams("arbitrary"))(xact, xact, xact, proj, dtb, alog, dskip, states, dy)


def gated_norm(ypre, proj, gain, *, name):
    S, W = ypre.shape
    tm = min(TOKEN_TILE, S)

    def body(y_ref, z_ref, g_ref, o_ref):
        z = z_ref[...]
        yg = y_ref[...] * (z * _sigmoid(z))
        o_ref[...] = _bf(yg * lax.rsqrt(jnp.mean(yg * yg, axis=-1, keepdims=True) + NORM_EPS) * g_ref[...])

    tile = pl.BlockSpec((tm, W), lambda i: (i, 0))
    return pl.pallas_call(
        body, grid=(S // tm,), in_specs=[tile, pl.BlockSpec((tm, W), lambda i: (i, CB_MZ // 4)), pl.BlockSpec((1, W), lambda i: (0, 0))],
        out_specs=tile, out_shape=_sds((S, W), BF16), name=name, compiler_params=_cparams("parallel"))(ypre, proj, gain)


def gated_norm_bwd(ypre, proj, gain, dycat, *, name):
    S, W = ypre.shape
    tm = min(TOKEN_TILE, S)

    def body(y_ref, z_ref, g_ref, dy_ref, dyp_ref, dz_ref, st_ref):
        z = z_ref[...]
        y = y_ref[...]
        sg = _sigmoid(z)
        sz = z * sg
        yg = y * sz
        r = lax.rsqrt(jnp.mean(yg * yg, axis=-1, keepdims=True) + NORM_EPS)
        yhat = yg * r
        dyo = dy_ref[...]

        @pl.when(pl.program_id(0) == 0)
        def _():
            st_ref[...] = jnp.zeros_like(st_ref)

        st_ref[0:1, :] += jnp.sum(dyo * yhat, axis=0, keepdims=True)
        dyhat = dyo * g_ref[...]
        dyg = r * (dyhat - yhat * jnp.mean(dyhat * yhat, axis=-1, keepdims=True))
        dyp_ref[...] = dyg * sz
        dz_ref[...] = _bf(dyg * y * (sg * (1.0 + z * (1.0 - sg))))

    tile = pl.BlockSpec((tm, W), lambda i: (i, 0))
    return pl.pallas_call(
        body, grid=(S // tm,),
        in_specs=[tile, pl.BlockSpec((tm, W), lambda i: (i, CB_MZ // 4)), pl.BlockSpec((1, W), lambda i: (0, 0)),
                  pl.BlockSpec((tm, W), lambda i: (i, 2))],
        out_specs=[tile, tile, pl.BlockSpec((8, W), lambda i: (0, 0))],
        out_shape=[_sds((S, W), F32), _sds((S, W), BF16), _sds((8, W), F32)], name=name,
        compiler_params=_cparams("arbitrary"))(ypre, proj, gain, dycat)


def ada_mod(c_all, w, bias, *, name):
    M, K = c_all.shape
    N = w.shape[1]
    tn = _tile(N, 512)

    def body(c_ref, w_ref, b_ref, o_ref, cond_ref):
        cv = c_ref[...]
        cond = cv * _sigmoid(cv)
        cond_ref[...] = cond
        o_ref[...] = _dot(_bf(cond), _bf(w_ref[...]), NN) + b_ref[...]

    return pl.pallas_call(
        body, grid=(N // tn,),
        in_specs=[pl.BlockSpec((M, K), lambda j: (0, 0)), pl.BlockSpec((K, tn), lambda j: (0, j)), pl.BlockSpec((1, tn), lambda j: (0, j))],
        out_specs=[pl.BlockSpec((M, tn), lambda j: (0, j)), pl.BlockSpec((M, K), lambda j: (0, 0))],
        out_shape=[_sds((M, N), F32), _sds((M, K), F32)], name=name, compiler_params=_cparams("arbitrary"))(c_all, w, bias)


def _adamw(g, w, m, v):
    m = ADAM_B1 * m + (1.0 - ADAM_B1) * g
    v = ADAM_B2 * v + (1.0 - ADAM_B2) * (g * g)
    m_hat = m / (1.0 - ADAM_B1 ** ADAM_STEP)
    v_hat = v / (1.0 - ADAM_B2 ** ADAM_STEP)
    return -ADAM_LR * (m_hat / (jnp.sqrt(v_hat) + ADAM_EPS) + ADAM_WD * w), m, v


def adamw_parts(parts, w, m, v, *, name):
    P, R, C = parts.shape
    tr = _tile(R, 592, 512, 352, 272, 256, 160, 128, 80, 64, 32, 16)

    def body(p_ref, w_ref, m_ref, v_ref, g_ref, d_ref, mo_ref, vo_ref):
        g = p_ref[0].astype(F32)
        for j in range(1, P):
            g = g + p_ref[j].astype(F32)
        g_ref[...] = g
        d_ref[...], mo_ref[...], vo_ref[...] = _adamw(g, w_ref[...], m_ref[...], v_ref[...])

    tile = pl.BlockSpec((tr, C), lambda i: (i, 0))
    return pl.pallas_call(
        body, grid=(R // tr,), in_specs=[pl.BlockSpec((P, tr, C), lambda i: (0, i, 0)), tile, tile, tile],
        out_specs=[tile] * 4, out_shape=[_sds((R, C), F32)] * 4, name=name, compiler_params=_cparams("parallel"))(parts, w, m, v)


def ada_adamw(cond_t, dmod, w, m, v, *, name):
    D, N = w.shape
    tr = _tile(D, 256)

    def body(c_ref, d_ref, w_ref, m_ref, v_ref, g_ref, dl_ref, mo_ref, vo_ref):
        cc = c_ref[...]
        dd = d_ref[...]
        g = cc[:, 0:1] * dd[0:1, :]
        for b in range(1, N_DEV):
            g = g + cc[:, b:b + 1] * dd[b:b + 1, :]
        g_ref[...] = g
        dl_ref[...], mo_ref[...], vo_ref[...] = _adamw(g, w_ref[...], m_ref[...], v_ref[...])

    tile = pl.BlockSpec((tr, N), lambda i: (i, 0))
    return pl.pallas_call(
        body, grid=(D // tr,), in_specs=[pl.BlockSpec((tr, N_DEV), lambda i: (i, 0)), pl.BlockSpec((N_DEV, N), lambda i: (0, 0)), tile, tile, tile],
        out_specs=[tile] * 4, out_shape=[_sds((D, N), F32)] * 4, name=name, compiler_params=_cparams("parallel"))(cond_t, dmod, w, m, v)


def _my_place():
    mx, my, mc = lax.axis_index("x"), lax.axis_index("y"), lax.axis_index("c")
    return mx, my, mc, 4 * mx + 2 * my + mc


def _peer(mx, my, mc, k):
    px = 1 - mx if (k >> 2) & 1 else mx
    py = 1 - my if (k >> 1) & 1 else my
    pc = 1 - mc if k & 1 else mc
    return (px, py, pc), 4 * px + 2 * py + pc


def _comm_call(body, x, out_shape, space, name):
    spec = pl.BlockSpec(memory_space=space)
    return pl.pallas_call(
        body, in_specs=[spec], out_specs=spec, out_shape=out_shape,
        scratch_shapes=[pltpu.SemaphoreType.DMA((N_DEV - 1,)), pltpu.SemaphoreType.DMA((N_DEV - 1,)), pltpu.SemaphoreType.DMA(())],
        name=name, compiler_params=pltpu.CompilerParams(has_side_effects=True, vmem_limit_bytes=VMEM_LIMIT))(x)


def allgather(x, *, name):
    def body(x_ref, out_ref, send_sems, recv_sems, local_sem):
        mx, my, mc, me = _my_place()
        mine = pltpu.make_async_copy(x_ref, out_ref.at[me], local_sem)
        mine.start()
        copies = []
        for k in range(1, N_DEV):
            peer, _ = _peer(mx, my, mc, k)
            cp = pltpu.make_async_remote_copy(src_ref=x_ref, dst_ref=out_ref.at[me], send_sem=send_sems.at[k - 1],
                                              recv_sem=recv_sems.at[k - 1], device_id=peer, device_id_type=pl.DeviceIdType.MESH)
            cp.start()
            copies.append(cp)
        for cp in copies:
            cp.wait()
        mine.wait()

    return _comm_call(body, x, _sds((N_DEV,) + x.shape, x.dtype), pltpu.VMEM, name)


def allgather_two_level(x, *, name):
    def body(x_ref, out_ref, send_sems, recv_sems, local_sem):
        _gather_two_level(x_ref, out_ref, send_sems, recv_sems, local_sem, start=True, finish=True)

    return _comm_call(body, x, _sds((N_DEV,) + x.shape, x.dtype), pltpu.HBM, name)


def _gather_two_level(x_ref, out_ref, send_sems, recv_sems, local_sem, *, start, finish):
    mx, my, mc, _ = _my_place()
    me, sibling = (mx, my, mc), (mx, my, 1 - mc)
    chips = [(1 - mx, my), (mx, 1 - my), (1 - mx, 1 - my)]

    def copy(k, block, to, src=None):
        slot = out_ref.at[4 * block[0] + 2 * block[1] + block[2]]
        return pltpu.make_async_remote_copy(src_ref=slot if src is None else src, dst_ref=slot, send_sem=send_sems.at[k],
                                            recv_sem=recv_sems.at[k], device_id=to, device_id_type=pl.DeviceIdType.MESH)

    mine = pltpu.make_async_copy(x_ref, out_ref.at[4 * mx + 2 * my + mc], local_sem)
    first = [copy(0, me, sibling, src=x_ref)] + [copy(1 + j, me, (*chip, mc), src=x_ref) for j, chip in enumerate(chips)]
    if start:
        mine.start()
        for cp in first:
            cp.start()
    if finish:
        passed = [copy(4 + j, (*chip, mc), sibling) for j, chip in enumerate(chips)]
        for j, chip in enumerate(chips):
            copy(1 + j, (*chip, mc), me).wait_recv()
            passed[j].start()
        copy(0, sibling, me).wait_recv()
        for j, chip in enumerate(chips):
            copy(4 + j, (*chip, 1 - mc), me).wait_recv()
        for cp in first + passed:
            cp.wait_send()
        mine.wait()


def _chip_exchange(s_ref, r_ref, send_sems, recv_sems, local_sem, *, start, finish):
    mx, my, mc, _ = _my_place()
    mine = pltpu.make_async_copy(s_ref.at[2 * mx + my], r_ref.at[2 * mx + my], local_sem)
    copies = []
    for k in range(1, 4):
        px = 1 - mx if (k >> 1) & 1 else mx
        py = 1 - my if k & 1 else my
        copies.append(pltpu.make_async_remote_copy(
            src_ref=s_ref.at[2 * px + py], dst_ref=r_ref.at[2 * mx + my], send_sem=send_sems.at[k - 1], recv_sem=recv_sems.at[k - 1],
            device_id=(px, py, mc), device_id_type=pl.DeviceIdType.MESH))
    if start:
        mine.start()
        for cp in copies:
            cp.start()
    if finish:
        for cp in copies:
            cp.wait()
        mine.wait()


COMM_SCRATCH = [pltpu.SemaphoreType.DMA((N_DEV - 1,)), pltpu.SemaphoreType.DMA((N_DEV - 1,)), pltpu.SemaphoreType.DMA(())]


def sibling_exchange(send, *, name):
    n_chip, _, R, C = send.shape

    def body(s_ref, r_ref, send_sems, recv_sems, local_sem):
        mx, my, mc, _ = _my_place()
        copies = []
        for q in range(n_chip):
            cp = pltpu.make_async_remote_copy(src_ref=s_ref.at[q, 1 - mc], dst_ref=r_ref.at[q], send_sem=send_sems.at[q],
                                              recv_sem=recv_sems.at[q], device_id=(mx, my, 1 - mc), device_id_type=pl.DeviceIdType.MESH)
            cp.start()
            copies.append(cp)
        for cp in copies:
            cp.wait()

    return _comm_call(body, send, _sds((n_chip, R, C), send.dtype), pltpu.HBM, name)


def chip_exchange(send, *, name):
    def body(s_ref, r_ref, send_sems, recv_sems, local_sem):
        _chip_exchange(s_ref, r_ref, send_sems, recv_sems, local_sem, start=True, finish=True)

    return _comm_call(body, send, _sds(send.shape, send.dtype), pltpu.HBM, name)


def add_partials(a, b, *, name):
    P, R, C = a.shape
    tr = _tile(R, 592, 512, 352, 272, 256, 160, 128, 80, 64, 32, 16)

    def body(a_ref, b_ref, o_ref):
        o_ref[...] = _bf(a_ref[...].astype(F32) + b_ref[...].astype(F32))

    tile = pl.BlockSpec((P, tr, C), lambda i: (0, i, 0))
    return pl.pallas_call(body, grid=(R // tr,), in_specs=[tile, tile], out_specs=tile, out_shape=_sds((P, R, C), BF16), name=name,
                          compiler_params=_cparams("parallel"))(a, b)


def _rows128(a):
    f = a.reshape(-1)
    n = -(-f.shape[0] // (16 * LANE)) * (16 * LANE)
    return jnp.pad(f, (0, n - f.shape[0])).reshape(-1, LANE)


PACK_ROWS = 512


def _pad_rows(buf):
    r = buf.shape[-2]
    pad = -r % PACK_ROWS
    return jnp.pad(buf, [(0, 0)] * (buf.ndim - 2) + [(0, pad), (0, 0)])


def _pack(arrays):
    parts = [_rows128(a) for a in arrays]
    offs = np.cumsum([0] + [p.shape[0] for p in parts])
    return _pad_rows(jnp.concatenate(parts, axis=0)), [int(o) for o in offs]


def _unpack(buf, offs, shapes):
    lead = buf.shape[:-2]
    out = []
    for o, shp in zip(offs, shapes):
        n = int(np.prod(shp))
        rows = -(-n // LANE)
        seg = buf[..., o:o + rows, :].reshape(lead + (rows * LANE,))[..., :n]
        out.append(seg.reshape(lead + tuple(shp)))
    return out


def _pad_w_in_t(w_t):
    D = w_t.shape[1]
    dt = jnp.pad(w_t[IN_MAIN:].reshape(4, 2, D), ((0, 0), (0, LANE - 2), (0, 0)))
    return jnp.concatenate([w_t[:IN_MAIN], dt.reshape(4 * LANE, D)], axis=0)


def _unpad_w_in_t(g_t):
    D = g_t.shape[1]
    dt = g_t[IN_MAIN:].reshape(4, LANE, D)[:, :2].reshape(SSM_HEADS, D)
    return jnp.concatenate([g_t[:IN_MAIN], dt], axis=0)


def _piece_rows(n, shard_shape):
    r = shard_shape[0] if n in ROW_SHARDED else shard_shape[1]
    return r, -(-r // 16) * 16


def _to_piece(n, shard):
    t = shard if n in ROW_SHARDED else shard.T
    return jnp.pad(t, ((0, -t.shape[0] % 16), (0, 0)))


def _from_piece(n, piece, shard_shape):
    r, _ = _piece_rows(n, shard_shape)
    return piece[:r] if n in ROW_SHARDED else piece[:r].T


def _pair_rows(p):
    return jnp.pad(p.reshape(4, 1, 2), ((0, 0), (0, 0), (0, LANE - 2)))


def _row(v):
    return v.reshape(1, -1)


def _ffn_fwd(h, gain, mod3, wg, wu, get_wd, tag, gather=None, arrived=None):
    shift, scale, gate = mod3
    u = norm_mod(h, gain, shift, scale, name=tag + "_norm")
    a, b, act, *gathered = ffn_up(u, wg, wu, name=tag + "_up", gather=gather)
    if gathered:
        arrived(gathered[0])
    hn, out = matmul_resid(act, get_wd(), h, gate, 0.5, name=tag + "_down")
    return hn, (h, u, a, b, act, out)


def _wgrad(a, b, name):
    return matmul(a, b, ta=True, tm=_tile(a.shape[1], 1408, 1536, 1024, 512), tn=b.shape[1], tk=WGRAD_TOKENS, out_dtype=BF16, name=name)


def _ffn_bwd(dh, saved, gain, mod3, wg, wu, wd, tag, exchange=None):
    h, u, a, b, act, out = saved
    _, scale, gate = mod3
    dout, gst = gate_bwd(dh, out, gate, 0.5, name=tag + "_gate_bwd")
    da, db = ffn_dact(dout, wd, a, b, name=tag + "_dact")
    dh_prev, nst, *exchanged = dgrad_norm_bwd([da, db], [wg, wu], [(0, 0), (1, 0)], h, gain, scale, dh, name=tag + "_dgrad",
                                              exchange=exchange)
    grads = (_wgrad(da, u, tag + "_dwg"), _wgrad(db, u, tag + "_dwu"), _wgrad(act, dout, tag + "_dwd"))
    return dh_prev, grads, nst[0], [nst[1], nst[2], gst[0]], (exchanged[0] if exchanged else None)


def _mix_fwd(h, p, mod3, w_in, w_out, cos, sin, tag, gather=None):
    shift, scale, gate = mod3
    u = norm_mod(h, p["norm_mix"], shift, scale, name=tag + "_norm")
    proj = matmul(u, w_in, tb=True, tm=BIG_TOKEN_TILE, tn=512, tk=D_MODEL, name=tag + "_proj")
    y_ret, ypre_ret, st_ret = ret_fwd(proj, p["ret_gn"], cos, sin, name=tag + "_ret")
    y_sb, sb_cin, *gathered = sb_fwd(proj, name=tag + "_sb", gather=gather)
    pre, xact = conv_fwd(proj, p["conv_w"], p["conv_b"], name=tag + "_conv")
    ypre_ssm, st_ssm = ssd_fwd(xact, proj, p["dt_bias"], p["a_log"], p["d_skip"], name=tag + "_ssd")
    y_ssm = gated_norm(ypre_ssm, proj, p["ssm_norm"], name=tag + "_gnorm")
    ycat = jnp.concatenate([y_ret, y_sb.astype(BF16), y_ssm], axis=1)
    hn, mixed = matmul_resid(ycat, w_out, h, gate, 1.0, name=tag + "_out")
    return hn, (h, u, proj, ypre_ret, st_ret, sb_cin, pre, xact, ypre_ssm, st_ssm, ycat, mixed), (gathered[0] if gathered else None)


def _mix_bwd(dh, saved, p, mod3, w_in, w_out, cos, sin, tag, exchange=None):
    h, u, proj, ypre_ret, st_ret, sb_cin, pre, xact, ypre_ssm, st_ssm, ycat, mixed = saved
    _, scale, gate = mod3
    dmixed, gst = gate_bwd(dh, mixed, gate, 1.0, name=tag + "_gate_bwd")
    dycat = matmul(dmixed, w_out, tb=True, tm=BIG_TOKEN_TILE, tn=512, tk=D_MODEL, name=tag + "_dycat")
    dw_out = _wgrad(ycat, dmixed, tag + "_dw_out")
    dret, rst = ret_bwd(proj, p["ret_gn"], cos, sin, ypre_ret, st_ret, dycat, name=tag + "_ret_bwd")
    dsq, dsk, dsv, *exchanged = sb_bwd(proj, sb_cin, dycat, name=tag + "_sb_bwd", exchange=exchange)
    dypre, dz, nst2 = gated_norm_bwd(ypre_ssm, proj, p["ssm_norm"], dycat, name=tag + "_gnorm_bwd")
    dxs, dbm, dcm, ddt, sst = ssd_bwd(xact, proj, p["dt_bias"], p["a_log"], p["d_skip"], st_ssm, dypre, name=tag + "_ssd_bwd")
    dact = jnp.concatenate([dxs, dbm, dcm], axis=1)
    dxbc, cst = conv_bwd(proj, pre, dact, p["conv_w"], name=tag + "_conv_bwd")
    pieces = [dret, dsq, dsk, dsv, dz, dxbc, ddt]
    starts = np.cumsum([0] + [pc.shape[1] for pc in pieces])
    assert starts[-1] == IN_PAD
    dh_prev, nst = dgrad_norm_bwd(pieces, [w_in], [(0, int(r0)) for r0 in starts[:-1]], h, p["norm_mix"], scale, dh, name=tag + "_dgrad")
    dw_in = jnp.concatenate([_wgrad(pc, u, f"{tag}_dw_in{i}") for i, pc in enumerate(pieces)], axis=0)
    small = dict(norm_mix=nst[0], ret_gn=rst[0], ssm_norm=nst2[0], conv_w=cst[0:4], conv_b=cst[4],
                 dt_bias=sst[:, 0, :2].reshape(SSM_HEADS), a_log=sst[:, 1, :2].reshape(SSM_HEADS), d_skip=sst[:, 2, :2].reshape(SSM_HEADS))
    return dh_prev, dw_in, dw_out, small, [nst[1], nst[2], gst[0]], (exchanged[0] if exchanged else None)


BIG = ("ffn1_wg", "ffn1_wu", "ffn1_wd", "w_in", "w_out", "ffn2_wg", "ffn2_wu", "ffn2_wd")
FIRST, REST = BIG[:3], BIG[3:]
ROW_SHARDED = ("ffn1_wd", "w_out", "ffn2_wd")
SMALL = ("ada_b", "norm_ffn1", "norm_mix", "conv_b", "dt_bias", "a_log", "d_skip", "ret_gn", "ssm_norm", "norm_ffn2",
         "final_ada_b", "final_norm")
NAMES = ("ada_w", "ada_b", "norm_ffn1", "ffn1_wg", "ffn1_wu", "ffn1_wd", "norm_mix", "w_in", "conv_w", "conv_b", "dt_bias", "a_log",
         "d_skip", "ret_gn", "ssm_norm", "w_out", "norm_ffn2", "ffn2_wg", "ffn2_wu", "ffn2_wd", "final_ada_w", "final_ada_b", "final_norm")


def kernel(x, c, ada_w, ada_b, norm_ffn1, ffn1_wg, ffn1_wu, ffn1_wd, norm_mix, w_in, conv_w, conv_b, dt_bias, a_log, d_skip, ret_gn, ssm_norm, w_out, norm_ffn2, ffn2_wg, ffn2_wu, ffn2_wd, final_ada_w, final_ada_b, final_norm, loss_target, m_ada_w, m_ada_b, m_norm_ffn1, m_ffn1_wg, m_ffn1_wu, m_ffn1_wd, m_norm_mix, m_w_in, m_conv_w, m_conv_b, m_dt_bias, m_a_log, m_d_skip, m_ret_gn, m_ssm_norm, m_w_out, m_norm_ffn2, m_ffn2_wg, m_ffn2_wu, m_ffn2_wd, m_final_ada_w, m_final_ada_b, m_final_norm, v_ada_w, v_ada_b, v_norm_ffn1, v_ffn1_wg, v_ffn1_wu, v_ffn1_wd, v_norm_mix, v_w_in, v_conv_w, v_conv_b, v_dt_bias, v_a_log, v_d_skip, v_ret_gn, v_ssm_norm, v_w_out, v_norm_ffn2, v_ffn2_wg, v_ffn2_wu, v_ffn2_wd, v_final_ada_w, v_final_ada_b, v_final_norm):
    W = dict(ada_w=ada_w, ada_b=ada_b, norm_ffn1=norm_ffn1, ffn1_wg=ffn1_wg, ffn1_wu=ffn1_wu, ffn1_wd=ffn1_wd, norm_mix=norm_mix,
             w_in=w_in, conv_w=conv_w, conv_b=conv_b, dt_bias=dt_bias, a_log=a_log, d_skip=d_skip, ret_gn=ret_gn, ssm_norm=ssm_norm,
             w_out=w_out, norm_ffn2=norm_ffn2, ffn2_wg=ffn2_wg, ffn2_wu=ffn2_wu, ffn2_wd=ffn2_wd, final_ada_w=final_ada_w,
             final_ada_b=final_ada_b, final_norm=final_norm)
    M1 = dict(ada_w=m_ada_w, ada_b=m_ada_b, norm_ffn1=m_norm_ffn1, ffn1_wg=m_ffn1_wg, ffn1_wu=m_ffn1_wu, ffn1_wd=m_ffn1_wd,
              norm_mix=m_norm_mix, w_in=m_w_in, conv_w=m_conv_w, conv_b=m_conv_b, dt_bias=m_dt_bias, a_log=m_a_log, d_skip=m_d_skip,
              ret_gn=m_ret_gn, ssm_norm=m_ssm_norm, w_out=m_w_out, norm_ffn2=m_norm_ffn2, ffn2_wg=m_ffn2_wg, ffn2_wu=m_ffn2_wu,
              ffn2_wd=m_ffn2_wd, final_ada_w=m_final_ada_w, final_ada_b=m_final_ada_b, final_norm=m_final_norm)
    V2 = dict(ada_w=v_ada_w, ada_b=v_ada_b, norm_ffn1=v_norm_ffn1, ffn1_wg=v_ffn1_wg, ffn1_wu=v_ffn1_wu, ffn1_wd=v_ffn1_wd,
              norm_mix=v_norm_mix, w_in=v_w_in, conv_w=v_conv_w, conv_b=v_conv_b, dt_bias=v_dt_bias, a_log=v_a_log, d_skip=v_d_skip,
              ret_gn=v_ret_gn, ssm_norm=v_ssm_norm, w_out=v_w_out, norm_ffn2=v_norm_ffn2, ffn2_wg=v_ffn2_wg, ffn2_wu=v_ffn2_wu,
              ffn2_wd=v_ffn2_wd, final_ada_w=v_final_ada_w, final_ada_b=v_final_ada_b, final_norm=v_final_norm)
    D = D_MODEL
    S = x.shape[1]
    me = 4 * lax.axis_index("x") + 2 * lax.axis_index("y") + lax.axis_index("c")
    n_mod = ada_w.shape[2]
    n_fmod = final_ada_w.shape[1]

    c_all = allgather(jnp.broadcast_to(c, (8, D)), name="gather_c")[:, 0, :]
    ada_cols = jnp.concatenate([ada_w[0], ada_w[1], final_ada_w], axis=1)
    ada_bias = jnp.concatenate([lax.dynamic_slice(ada_b, (0, me * n_mod), (DEPTH, n_mod)).reshape(1, -1),
                                lax.dynamic_slice(final_ada_b, (me * n_fmod,), (n_fmod,)).reshape(1, -1)], axis=1)
    mod_sh, cond = ada_mod(jnp.pad(c_all, ((0, 8), (0, 0))), ada_cols, ada_bias, name="ada_mod")
    n_cols = mod_sh.shape[1]
    small_in = jnp.concatenate([mod_sh[:8], jnp.pad(conv_w.reshape(8, LANE), ((0, 0), (0, n_cols - LANE)))], axis=0)
    small_g = allgather(small_in, name="gather_mod")
    mod_rows = lax.dynamic_index_in_dim(small_g[:, :8, :], me, axis=1, keepdims=False)
    mod = [mod_rows[:, l * n_mod:(l + 1) * n_mod].reshape(9, D) for l in range(DEPTH)]
    fmod = mod_rows[:, DEPTH * n_mod:].reshape(2, D)
    conv_w_full = small_g[:, 8:, :LANE].reshape(N_DEV, DEPTH, SSM_CONV, LANE).transpose(1, 2, 0, 3).reshape(DEPTH, SSM_CONV, 8 * LANE)

    rows = {n: _piece_rows(n, W[n].shape[1:]) for n in BIG}

    def offsets(names):
        offs, o = {}, 0
        for n in names:
            offs[n] = o
            o += rows[n][1]
        return offs

    pack_of = lambda src, dtype, l, names: jnp.concatenate([_to_piece(n, src[n][l]).astype(dtype) for n in names], axis=0)
    full = {}

    def file_weights(wgath, items):
        o = 0
        for l, n in items:
            w = wgath[:, o:o + rows[n][0], :].reshape(N_DEV * rows[n][0], D)
            full[l, n] = _pad_w_in_t(w) if n == "w_in" else w
            o += rows[n][1]

    first_items = [(0, "ffn1_wg"), (0, "ffn1_wu")]
    riding = {(0, "ffn1_up"): [(0, "ffn1_wd"), (0, "w_in"), (0, "w_out")],
              (0, "sb"): [(0, n) for n in BIG[5:]] + ([(1, n) for n in FIRST] if DEPTH > 1 else []),
              (0, "ffn2_up"): [(1, "w_in"), (1, "w_out")] if DEPTH > 1 else [],
              (1, "ffn1_up"): [(1, n) for n in BIG[5:]]}
    pack_items = lambda items: jnp.concatenate([_to_piece(n, W[n][l]).astype(BF16) for l, n in items], axis=0)
    file_weights(allgather_two_level(pack_items(first_items), name="gather_weights0"), first_items)

    def ride(l, kernel):
        items = riding.get((l, kernel))
        if not items:
            return None, None
        return pack_items(items), lambda wgath: file_weights(wgath, items)

    cos, sin = _rope_tables(S)
    h = x[0]
    target = loss_target[0]
    layer_p = []
    for l in range(DEPTH):
        layer_p.append(dict(norm_ffn1=_row(norm_ffn1[l]), norm_mix=_row(norm_mix[l]), norm_ffn2=_row(norm_ffn2[l]),
                            ret_gn=_row(ret_gn[l]), ssm_norm=_row(ssm_norm[l]), conv_w=conv_w_full[l], conv_b=_row(conv_b[l]),
                            dt_bias=_pair_rows(dt_bias[l]), a_log=_pair_rows(a_log[l]), d_skip=_pair_rows(d_skip[l])))
    mods = [[[_row(mod[l][3 * s + k]) for k in range(3)] for s in range(3)] for l in range(DEPTH)]

    saved = []
    for l in range(DEPTH):
        p = layer_p[l]
        pack, arrived = ride(l, "ffn1_up")
        h, s1 = _ffn_fwd(h, p["norm_ffn1"], mods[l][0], full[l, "ffn1_wg"], full[l, "ffn1_wu"], lambda: full[l, "ffn1_wd"],
                         f"l{l}_ffn1", gather=pack, arrived=arrived)
        pack, arrived = ride(l, "sb")
        h, s2, gathered = _mix_fwd(h, p, mods[l][1], full[l, "w_in"], full[l, "w_out"], cos, sin, f"l{l}_mix", gather=pack)
        if pack is not None:
            arrived(gathered)
        pack, arrived = ride(l, "ffn2_up")
        h, s3 = _ffn_fwd(h, p["norm_ffn2"], mods[l][2], full[l, "ffn2_wg"], full[l, "ffn2_wu"], lambda: full[l, "ffn2_wd"],
                         f"l{l}_ffn2", gather=pack, arrived=arrived)
        saved.append((s1, s2, s3))

    def chip_sums(grads, names, tag):
        def send_piece(n):
            g = _unpad_w_in_t(grads[n]) if n == "w_in" else grads[n]
            r, rp = rows[n]
            return jnp.pad(g.reshape(N_DEV, r, D), ((0, 0), (0, rp - r), (0, 0)))

        spack = jnp.concatenate([send_piece(n) for n in names], axis=1)
        by_core = spack.reshape((N_DEV // 2, 2) + spack.shape[1:])
        from_sibling = sibling_exchange(by_core, name="exchange_sibling" + tag)
        own = lax.dynamic_index_in_dim(by_core, lax.axis_index("c"), axis=1, keepdims=False)
        return add_partials(own, from_sibling, name="add_sibling" + tag)

    dh, fst = final_loss_bwd(h, _row(final_norm), _row(fmod[0]), _row(fmod[1]), target, name="final")
    reduced = []
    small_g_l = [None] * DEPTH
    dmod = [None] * DEPTH
    pending = None
    for l in reversed(range(DEPTH)):
        p, f = layer_p[l], {n: full[l, n] for n in BIG}
        s1, s2, s3 = saved[l]
        dh, g2, gn2, dm2, _ = _ffn_bwd(dh, s3, p["norm_ffn2"], mods[l][2], f["ffn2_wg"], f["ffn2_wu"], f["ffn2_wd"], f"l{l}_ffn2")
        dh, gw_in, gw_out, sm, dm1, exchanged = _mix_bwd(dh, s2, p, mods[l][1], f["w_in"], f["w_out"], cos, sin, f"l{l}_mix",
                                                         exchange=pending)
        if pending is not None:
            reduced.append((l + 1, BIG, exchanged))
        grads = dict(zip(REST, (gw_in, gw_out) + g2))
        early = chip_sums(grads, REST, f"{l}r") if l == 0 else None
        dh, g1, gn1, dm0, exchanged = _ffn_bwd(dh, s1, p["norm_ffn1"], mods[l][0], f["ffn1_wg"], f["ffn1_wu"], f["ffn1_wd"], f"l{l}_ffn1",
                                               exchange=early)
        grads.update(zip(FIRST, g1))
        if l == 0:
            reduced.append((0, REST, exchanged))
            reduced.append((0, FIRST, chip_exchange(chip_sums(grads, FIRST, "0f"), name="exchange_chips0")))
        else:
            pending = chip_sums(grads, BIG, str(l))
        sm["norm_ffn1"], sm["norm_ffn2"] = gn1, gn2
        small_g_l[l] = sm
        dmod[l] = jnp.concatenate(dm0 + dm1 + dm2, axis=0)
    grad_x = dh[None]

    big_res = {}
    for l, names, rp in reduced:
        outs = adamw_parts(rp, *[pack_of(src, F32, l, names) for src in (W, M1, V2)], name=f"adamw_big{l}_{names[0]}")
        offs = offsets(names)
        for n in names:
            big_res[l, n] = [_from_piece(n, o[offs[n]:offs[n] + rows[n][1]], W[n].shape[1:]) for o in outs]
    big_out = [{n: jnp.stack([big_res[l, n][k] for l in range(DEPTH)]) for n in BIG} for k in range(4)]

    stack2 = lambda key: jnp.stack([small_g_l[l][key] for l in range(DEPTH)])
    pieces = [("loss", fst[3, 0:1]), ("ada_b", jnp.stack(dmod)), ("final_ada_b", jnp.concatenate([fst[1], fst[2]])),
              ("norm_ffn1", stack2("norm_ffn1")), ("norm_mix", stack2("norm_mix")), ("norm_ffn2", stack2("norm_ffn2")),
              ("conv_w", stack2("conv_w")), ("conv_b", stack2("conv_b")), ("dt_bias", stack2("dt_bias")), ("a_log", stack2("a_log")),
              ("d_skip", stack2("d_skip")), ("ret_gn", stack2("ret_gn")), ("ssm_norm", stack2("ssm_norm")), ("final_norm", fst[0])]
    names = [n for n, _ in pieces]
    shapes = [a.shape for _, a in pieces]
    ppack, poffs = _pack([a for _, a in pieces])
    pg = allgather(ppack, name="gather_small")
    zero_like = lambda n, a: jnp.zeros(a.shape, F32)
    spacks = [_pack([(src[n] if n in SMALL else zero_like(n, a)) for n, a in pieces])[0] for src in (W, M1, V2)]
    souts = adamw_parts(pg, *spacks, name="adamw_small")
    small_out = [dict(zip(names, _unpack(o, poffs, shapes))) for o in souts]
    loss = small_out[0]["loss"][0]

    gathered = dict(zip(names, _unpack(pg, poffs, shapes)))
    conv_parts = lax.dynamic_slice_in_dim(gathered["conv_w"], me * LANE, LANE, axis=3).reshape(N_DEV, DEPTH * SSM_CONV, LANE)
    conv_out = [o.reshape(conv_w.shape) for o in adamw_parts(conv_parts, conv_w.reshape(-1, LANE), m_conv_w.reshape(-1, LANE),
                                                              v_conv_w.reshape(-1, LANE), name="adamw_conv_w")]
    cond_t = cond[:8].T
    ada_out = []
    for l in range(DEPTH):
        dsel = lax.dynamic_slice_in_dim(gathered["ada_b"][:, l, :], me * n_mod, n_mod, axis=1)
        ada_out.append(ada_adamw(cond_t, dsel, ada_w[l], m_ada_w[l], v_ada_w[l], name=f"adamw_ada_w{l}"))
    ada_out = [jnp.stack([ada_out[l][k] for l in range(DEPTH)]) for k in range(4)]
    fsel = lax.dynamic_slice_in_dim(gathered["final_ada_b"].reshape(N_DEV, 2 * D), me * n_fmod, n_fmod, axis=1)
    fada_out = ada_adamw(cond_t, fsel, final_ada_w, m_final_ada_w, v_final_ada_w, name="adamw_final_ada_w")

    def pick(k, n):
        if n in BIG:
            return big_out[k][n]
        if n == "ada_w":
            return ada_out[k]
        if n == "final_ada_w":
            return fada_out[k]
        if n == "conv_w":
            return conv_out[k]
        return small_out[k][n]

    return (loss, grad_x) + tuple(pick(k, n) for k in range(4) for n in NAMES)
```
